```python
import jax, jax.numpy as jnp
from jax import lax
import numpy as np

D_MODEL = 1024
BATCH = 2
SEQ = 8192
DEPTH = 2
DEC_BATCH = 128
DEC_SEQ = 8
PAST_LEN = 16384
PAGE_SIZE = 128

N_A_LAYERS = DEPTH // 2
N_B_LAYERS = DEPTH - N_A_LAYERS
HEAD_DIM = 64
N_HEADS = D_MODEL // HEAD_DIM
N_KV_HEADS = N_HEADS // 4
GROUP = N_HEADS // N_KV_HEADS
WINDOW = 128
Q_BLOCK = 128
ROPE_THETA = 500000.0
ROT_DIM = HEAD_DIM // 4
CONV_WIDTH = 3
D_FF = 2816
N_SUB = 3
EPS = 1e-6
NEG = -1e30

kernel_name = "yoco_shortconv_swa_sink_macaron_adaln_step"


def _rmsnorm(x, g):
    x32 = x.astype(jnp.float32)
    y = x32 * lax.rsqrt(jnp.mean(x32 * x32, axis=-1, keepdims=True) + EPS)
    return (y * g.astype(jnp.float32)).astype(x.dtype)


def _modulate(h, shift, scale):
    return h * (1 + scale[:, None, :]) + shift[:, None, :]


def _adaln(c, w, b, n):
    mod = jax.nn.silu(c) @ w + b
    return jnp.split(mod, n, axis=-1)


def _swiglu(h, wg, wu, wd):
    return (jax.nn.silu(h @ wg) * (h @ wu)) @ wd


def _rope(x, pos):
    half = ROT_DIM // 2
    inv_freq = ROPE_THETA ** (-jnp.arange(0, ROT_DIM, 2, dtype=jnp.float32) / ROT_DIM)
    ang = pos.astype(jnp.float32)[:, None] * inv_freq[None, :]
    cos = jnp.cos(ang)[:, None, :].astype(x.dtype)
    sin = jnp.sin(ang)[:, None, :].astype(x.dtype)
    x1 = x[..., :half]
    x2 = x[..., half:ROT_DIM]
    xp = x[..., ROT_DIM:]
    return jnp.concatenate([x1 * cos - x2 * sin, x2 * cos + x1 * sin, xp], axis=-1)


def _short_conv_mixer(h, conv_state, w_in, conv_w, w_out):
    b_g, c_g, v = jnp.split(h @ w_in, 3, axis=-1)
    u = c_g * v
    t = u.shape[1]
    padded = jnp.concatenate([conv_state.astype(u.dtype), u], axis=1)
    conv = padded[:, 0:t] * conv_w[0]
    for j in range(1, CONV_WIDTH):
        conv = conv + padded[:, j:j + t] * conv_w[j]
    y = (b_g * conv) @ w_out
    return y, padded[:, t:]


def _banded(k):
    bt, t = k.shape[:2]
    nb = t // Q_BLOCK
    pad = jnp.concatenate([jnp.zeros_like(k[:, :Q_BLOCK]), k], axis=1)
    prev = pad[:, :t].reshape(bt, nb, Q_BLOCK, N_KV_HEADS, HEAD_DIM)
    cur = k.reshape(bt, nb, Q_BLOCK, N_KV_HEADS, HEAD_DIM)
    return jnp.concatenate([prev, cur], axis=2)


def _sink_attention(q, k, v, q_pos, k_pos, sinks):
    bt, nb, qb = q.shape[:3]
    qg = q.reshape(bt, nb, qb, N_KV_HEADS, GROUP, HEAD_DIM)
    s = jnp.einsum('bnqkgd,bnskd->bnkgqs', qg, k,
                   preferred_element_type=jnp.float32) * (HEAD_DIM ** -0.5)
    rel = q_pos[:, :, None] - k_pos[:, None, :]
    mask = (rel >= 0) & (rel < WINDOW) & (k_pos[:, None, :] >= 0)
    s = jnp.where(mask[None, :, None, None], s, NEG)
    sink = sinks.astype(jnp.float32).reshape(1, 1, N_KV_HEADS, GROUP, 1, 1)
    m = jnp.maximum(jnp.max(s, axis=-1, keepdims=True), sink)
    p = jnp.exp(s - m)
    denom = jnp.sum(p, axis=-1, keepdims=True) + jnp.exp(sink - m)
    p = (p / denom).astype(v.dtype)
    o = jnp.einsum('bnkgqs,bnskd->bnqkgd', p, v)
    return o.reshape(bt, nb * qb, N_HEADS * HEAD_DIM)


def _window_attention(q, k, v, pos, k_buf, v_buf, sinks):
    bt, t = q.shape[:2]
    if k_buf is None:
        nb = t // Q_BLOCK
        qb = q.reshape(bt, nb, Q_BLOCK, N_HEADS, HEAD_DIM)
        kb = _banded(k)
        vb = _banded(v)
        q_pos = pos.reshape(nb, Q_BLOCK)
        k_pos = jnp.concatenate([q_pos - Q_BLOCK, q_pos], axis=1)
    else:
        w = k_buf.shape[1]
        qb = q[:, None]
        kb = jnp.concatenate([k_buf.astype(k.dtype), k], axis=1)[:, None]
        vb = jnp.concatenate([v_buf.astype(v.dtype), v], axis=1)[:, None]
        q_pos = pos[None]
        k_pos = jnp.concatenate([pos[0] - w + jnp.arange(w, dtype=jnp.int32), pos])[None]
    return _sink_attention(qb, kb, vb, q_pos, k_pos, sinks)


def _shared_kv(x, c, pos, p):
    sh, sc = _adaln(c, p['w_ada_kv'], p['b_ada_kv'], 2)
    h = _modulate(_rmsnorm(x, p['kv_norm_g']), sh, sc)
    bt, t = x.shape[:2]
    k = _rope((h @ p['w_k']).reshape(bt, t, N_KV_HEADS, HEAD_DIM), pos)
    v = (h @ p['w_v']).reshape(bt, t, N_KV_HEADS, HEAD_DIM)
    return k, v


def _trunk(x, c, pos, conv_states, k_buf, v_buf, p):
    bt, t = x.shape[:2]
    new_conv = []
    k = v = None
    for layer in range(DEPTH):
        if layer == N_A_LAYERS:
            k, v = _shared_kv(x, c, pos, p)
        sh1, sc1, g1, sh2, sc2, g2, sh3, sc3, g3 = _adaln(c, p['w_ada'][layer], p['b_ada'][layer], 3 * N_SUB)
        h = _modulate(_rmsnorm(x, p['norm_g'][layer, 0]), sh1, sc1)
        x = x + 0.5 * g1[:, None, :] * _swiglu(h, p['w_ffn_gate'][layer, 0], p['w_ffn_up'][layer, 0], p['w_ffn_down'][layer, 0])
        h = _modulate(_rmsnorm(x, p['norm_g'][layer, 1]), sh2, sc2)
        if layer < N_A_LAYERS:
            y, st = _short_conv_mixer(h, conv_states[layer], p['conv_w_in'][layer], p['conv_w'][layer], p['conv_w_out'][layer])
            new_conv.append(st)
        else:
            j = layer - N_A_LAYERS
            q = _rope((h @ p['attn_w_q'][j]).reshape(bt, t, N_HEADS, HEAD_DIM), pos)
            y = _window_attention(q, k, v, pos, k_buf, v_buf, p['attn_sinks'][j]) @ p['attn_w_o'][j]
        x = x + g2[:, None, :] * y
        h = _modulate(_rmsnorm(x, p['norm_g'][layer, 2]), sh3, sc3)
        x = x + 0.5 * g3[:, None, :] * _swiglu(h, p['w_ffn_gate'][layer, 1], p['w_ffn_up'][layer, 1], p['w_ffn_down'][layer, 1])
    y = _rmsnorm(x, p['final_norm_g'])
    if k_buf is None:
        k_state = k[:, -WINDOW:]
        v_state = v[:, -WINDOW:]
    else:
        w = k_buf.shape[1]
        k_state = jnp.concatenate([k_buf.astype(k.dtype), k], axis=1)[:, -w:]
        v_state = jnp.concatenate([v_buf.astype(v.dtype), v], axis=1)[:, -w:]
    return y, jnp.stack(new_conv, axis=0), k_state, v_state


def setup_inputs(seed: int = 0) -> dict:
    key = jax.random.key(seed)
    ks = jax.random.split(key, 26)
    f32 = jnp.float32
    D = D_MODEL
    w_buf = min(WINDOW, PAST_LEN)

    def nrm(k, shape, scale):
        return jax.random.normal(k, shape, f32) * scale

    return {
        'x_prompt': nrm(ks[0], (BATCH, SEQ, D), 1.0),
        'x_sample': nrm(ks[1], (DEC_BATCH, DEC_SEQ, D), 1.0),
        'state_conv': nrm(ks[2], (N_A_LAYERS, DEC_BATCH, CONV_WIDTH - 1, D), 1.0),
        'cache_k_win': nrm(ks[3], (DEC_BATCH, w_buf, N_KV_HEADS, HEAD_DIM), 1.0),
        'cache_v_win': nrm(ks[4], (DEC_BATCH, w_buf, N_KV_HEADS, HEAD_DIM), 1.0),
        'c_prompt': nrm(ks[5], (BATCH, D), 1.0),
        'c_sample': nrm(ks[6], (DEC_BATCH, D), 1.0),
        'norm_g': 1.0 + nrm(ks[7], (DEPTH, N_SUB, D), 0.02),
        'w_ada': nrm(ks[8], (DEPTH, D, 3 * N_SUB * D), 0.5 * D ** -0.5),
        'b_ada': nrm(ks[9], (DEPTH, 3 * N_SUB * D), 0.02),
        'w_ffn_gate': nrm(ks[10], (DEPTH, 2, D, D_FF), D ** -0.5),
        'w_ffn_up': nrm(ks[11], (DEPTH, 2, D, D_FF), D ** -0.5),
        'w_ffn_down': nrm(ks[12], (DEPTH, 2, D_FF, D), D_FF ** -0.5),
        'conv_w_in': nrm(ks[13], (N_A_LAYERS, D, 3 * D), D ** -0.5),
        'conv_w': nrm(ks[14], (N_A_LAYERS, CONV_WIDTH, D), CONV_WIDTH ** -0.5),
        'conv_w_out': nrm(ks[15], (N_A_LAYERS, D, D), D ** -0.5),
        'kv_norm_g': 1.0 + nrm(ks[16], (D,), 0.02),
        'w_ada_kv': nrm(ks[17], (D, 2 * D), 0.5 * D ** -0.5),
        'b_ada_kv': nrm(ks[18], (2 * D,), 0.02),
        'w_k': nrm(ks[19], (D, N_KV_HEADS * HEAD_DIM), D ** -0.5),
        'w_v': nrm(ks[20], (D, N_KV_HEADS * HEAD_DIM), D ** -0.5),
        'attn_w_q': nrm(ks[21], (N_B_LAYERS, D, N_HEADS * HEAD_DIM), D ** -0.5),
        'attn_sinks': nrm(ks[22], (N_B_LAYERS, N_HEADS), 1.0),
        'attn_w_o': nrm(ks[23], (N_B_LAYERS, N_HEADS * HEAD_DIM, D), (N_HEADS * HEAD_DIM) ** -0.5),
        'final_norm_g': 1.0 + nrm(ks[24], (D,), 0.02),
    }


def reference(x_prompt, x_sample, state_conv, cache_k_win, cache_v_win, c_prompt, c_sample,
              norm_g, w_ada, b_ada, w_ffn_gate, w_ffn_up, w_ffn_down,
              conv_w_in, conv_w, conv_w_out, kv_norm_g, w_ada_kv, b_ada_kv, w_k, w_v,
              attn_w_q, attn_sinks, attn_w_o, final_norm_g):
    p = {
        'norm_g': norm_g, 'w_ada': w_ada, 'b_ada': b_ada,
        'w_ffn_gate': w_ffn_gate, 'w_ffn_up': w_ffn_up, 'w_ffn_down': w_ffn_down,
        'conv_w_in': conv_w_in, 'conv_w': conv_w, 'conv_w_out': conv_w_out,
        'kv_norm_g': kv_norm_g, 'w_ada_kv': w_ada_kv, 'b_ada_kv': b_ada_kv,
        'w_k': w_k, 'w_v': w_v,
        'attn_w_q': attn_w_q, 'attn_sinks': attn_sinks, 'attn_w_o': attn_w_o,
        'final_norm_g': final_norm_g,
    }
    conv0 = jnp.zeros((N_A_LAYERS, x_prompt.shape[0], CONV_WIDTH - 1, D_MODEL), x_prompt.dtype)
    pos_p = jnp.arange(x_prompt.shape[1], dtype=jnp.int32)
    y_prompt, conv_p, k_p, v_p = _trunk(x_prompt, c_prompt, pos_p, conv0, None, None, p)
    pos_s = PAST_LEN + jnp.arange(x_sample.shape[1], dtype=jnp.int32)
    y_sample, conv_s, k_s, v_s = _trunk(x_sample, c_sample, pos_s, state_conv, cache_k_win, cache_v_win, p)
    return (y_prompt, y_sample, conv_p, conv_s, k_p, v_p, k_s, v_s)
```

```python
import functools

import jax
import jax.numpy as jnp
from jax import lax
from jax.experimental import pallas as pl
from jax.experimental.pallas import tpu as pltpu

F32 = jnp.float32
BF16 = jnp.bfloat16

D_MODEL = 1024
D_FF = 2816
HEAD_DIM = 64
N_HEADS = 16
N_KV_HEADS = 4
GROUP = N_HEADS // N_KV_HEADS
KV_DIM = N_KV_HEADS * HEAD_DIM
WINDOW = 128
ROT_DIM = 16
ROPE_THETA = 500000.0
CONV_WIDTH = 3
N_MOD = 9
EPS = 1e-6
NEG = -1e30

SUBLANES = 8
LANES = 128
VMEM_LIMIT_BYTES = 56 * 1024 * 1024

ROW_TILE = 512
SAMPLE_ATTN_SEQS = 16
FF_CHUNKS = ((0, 1536), (1536, D_FF))


def _dot(a, b):
    return jnp.dot(a, b, preferred_element_type=F32)


def _dot_nt(a, b):
    return lax.dot_general(a, b, (((1,), (1,)), ((), ())), preferred_element_type=F32)


def _log2(n):
    assert n & (n - 1) == 0, n
    return n.bit_length() - 1


def _silu(x):
    return x * jax.nn.sigmoid(x)


def _unit_rms(x):
    return x * lax.rsqrt(jnp.mean(x * x, axis=-1, keepdims=True) + EPS)


def _repeat_rows(ref, reps):
    n = ref.shape[1]
    return jnp.concatenate(
        [jnp.broadcast_to(ref[i:i + 1, :], (reps, n)) for i in range(ref.shape[0])], axis=0)


def _mod_rows(ref, per_seq_rows):
    if per_seq_rows is None:
        return ref[pl.ds(pl.program_id(0), 1), :]
    return _repeat_rows(ref, per_seq_rows)


def _rope(x, cos_t, sin_up_t, sin_dn_t):
    n = x.shape[1]
    reps = n // LANES
    half = ROT_DIM // 2
    cos = jnp.concatenate([cos_t] * reps, axis=1)
    sin_up = jnp.concatenate([sin_up_t] * reps, axis=1)
    sin_dn = jnp.concatenate([sin_dn_t] * reps, axis=1)
    x_up = pltpu.roll(x, n - half, axis=1)
    x_dn = pltpu.roll(x, half, axis=1)
    return x * cos + x_up * sin_up + x_dn * sin_dn


def _ada_kernel(c_ref, w_ref, b_ref, o_ref):
    a = _silu(c_ref[...]).astype(BF16)
    o_ref[...] = _dot(a, w_ref[...].astype(BF16)) + b_ref[...]


def _ada(c_all, w, b):
    n_layers, _, n = w.shape
    m = c_all.shape[0]
    tn = D_MODEL
    return pl.pallas_call(
        _ada_kernel,
        grid=(n_layers, n // tn),
        in_specs=[
            pl.BlockSpec((m, D_MODEL), lambda l, j: (0, 0)),
            pl.BlockSpec((None, D_MODEL, tn), lambda l, j: (l, 0, j)),
            pl.BlockSpec((None, 1, tn), lambda l, j: (l, 0, j)),
        ],
        out_specs=pl.BlockSpec((None, m, tn), lambda l, j: (l, 0, j)),
        out_shape=jax.ShapeDtypeStruct((n_layers, m, n), F32),
        compiler_params=pltpu.CompilerParams(
            dimension_semantics=("arbitrary", "arbitrary"),
            vmem_limit_bytes=VMEM_LIMIT_BYTES),
        name="adaln_tables",
    )(c_all, w, b.reshape(n_layers, 1, n))


def _ffn_kernel(*refs, per_seq_rows, with_kv, final):
    it = iter(refs)
    x_ref, sh_ref, sc_ref, gt_ref, ng_ref, wg_ref, wu_ref, wd_ref = (next(it) for _ in range(8))
    if with_kv:
        (kvg_ref, kvsh_ref, kvsc_ref, wk_ref, wv_ref,
         cos_ref, sup_ref, sdn_ref) = (next(it) for _ in range(8))
    if final:
        fg_ref = next(it)
    o_ref = next(it)
    if with_kv:
        k_ref, v_ref = next(it), next(it)

    x = x_ref[...]
    xn = _unit_rms(x)
    h = (xn * ng_ref[...]) * (1.0 + _mod_rows(sc_ref, per_seq_rows)) + _mod_rows(sh_ref, per_seq_rows)
    hb = h.astype(BF16)
    y = None
    for lo, hi in FF_CHUNKS:
        g = _dot(hb, wg_ref[:, lo:hi])
        u = _dot(hb, wu_ref[:, lo:hi])
        a = (_silu(g) * u).astype(BF16)
        part = _dot(a, wd_ref[lo:hi, :])
        y = part if y is None else y + part
    x_new = x + (0.5 * _mod_rows(gt_ref, per_seq_rows)) * y

    if with_kv:
        hk = ((xn * kvg_ref[...]) * (1.0 + _mod_rows(kvsc_ref, per_seq_rows))
              + _mod_rows(kvsh_ref, per_seq_rows)).astype(BF16)
        k_ref[...] = _rope(_dot(hk, wk_ref[...]), cos_ref[...], sup_ref[...], sdn_ref[...])
        v_ref[...] = _dot(hk, wv_ref[...])

    if final:
        o_ref[...] = _unit_rms(x_new) * fg_ref[...]
    else:
        o_ref[...] = x_new


def _resident(shape, index_map):
    return pl.BlockSpec(shape, index_map, pipeline_mode=pl.Buffered(1))


def _mod_spec(layer, col, group):
    if group.per_seq_rows is None:
        return pl.BlockSpec((None, SUBLANES, D_MODEL),
                            lambda b, t: (layer, group.prompt_mod_block, col))
    return pl.BlockSpec((None, group.tile // group.per_seq_rows, D_MODEL),
                        lambda b, t: (layer, t, col))


class _Group:
    def __init__(self, n_seq, seq_len, per_seq_rows, prompt_mod_block):
        self.tile = ROW_TILE
        self.rows = n_seq * seq_len
        self.per_seq_rows = per_seq_rows
        self.prompt_mod_block = prompt_mod_block
        if per_seq_rows is None:
            self.grid = (n_seq, seq_len // self.tile)
        else:
            self.grid = (1, self.rows // self.tile)
        self.tiles_per_seq = self.grid[1]

    def row_spec(self, width):
        nt = self.grid[1]
        return pl.BlockSpec((self.tile, width), lambda b, t: (b * nt + t, 0))

    def table_spec(self):
        if self.per_seq_rows is None:
            return pl.BlockSpec((self.tile, LANES), lambda b, t: (t, 0))
        return pl.BlockSpec((self.tile, LANES), lambda b, t: (0, 0))


def _ffn(x, group, mods, layer, sub, norm_g, wg, wu, wd, slot, kv=None, final_g=None):
    with_kv = kv is not None
    final = final_g is not None
    vec = lambda i: _resident((None, 1, D_MODEL), lambda b, t: (i, 0, 0))
    in_specs = [
        group.row_spec(D_MODEL),
        _mod_spec(layer, 3 * sub + 0, group),
        _mod_spec(layer, 3 * sub + 1, group),
        _mod_spec(layer, 3 * sub + 2, group),
        vec(layer * 3 + sub),
        _resident((None, None, D_MODEL, D_FF), lambda b, t: (layer, slot, 0, 0)),
        _resident((None, None, D_MODEL, D_FF), lambda b, t: (layer, slot, 0, 0)),
        _resident((None, None, D_FF, D_MODEL), lambda b, t: (layer, slot, 0, 0)),
    ]
    args = [x, mods, mods, mods, norm_g, wg, wu, wd]
    out_specs = [group.row_spec(D_MODEL)]
    out_shape = [jax.ShapeDtypeStruct((group.rows, D_MODEL), F32)]
    if with_kv:
        in_specs += [
            _resident((1, D_MODEL), lambda b, t: (0, 0)),
            _mod_spec(0, 0, group),
            _mod_spec(0, 1, group),
            _resident((D_MODEL, KV_DIM), lambda b, t: (0, 0)),
            _resident((D_MODEL, KV_DIM), lambda b, t: (0, 0)),
            group.table_spec(), group.table_spec(), group.table_spec(),
        ]
        args += [kv["norm_g"], kv["mods"], kv["mods"], kv["w_k"], kv["w_v"], *kv["tables"]]
        out_specs += [group.row_spec(KV_DIM), group.row_spec(KV_DIM)]
        out_shape += [jax.ShapeDtypeStruct((group.rows, KV_DIM), F32)] * 2
    if final:
        in_specs.append(_resident((1, D_MODEL), lambda b, t: (0, 0)))
        args.append(final_g)
    return pl.pallas_call(
        functools.partial(_ffn_kernel, per_seq_rows=group.per_seq_rows,
                          with_kv=with_kv, final=final),
        grid=group.grid,
        in_specs=in_specs,
        out_specs=out_specs,
        out_shape=out_shape,
        compiler_params=pltpu.CompilerParams(
            dimension_semantics=("arbitrary", "arbitrary"),
            vmem_limit_bytes=VMEM_LIMIT_BYTES),
        name=f"ffn_l{layer}_s{sub}",
    )(*args)


def _conv_kernel(*refs, per_seq_rows):
    it = iter(refs)
    x_ref, sh_ref, sc_ref, gt_ref, ng_ref, win_ref, cw_ref, wout_ref = (next(it) for _ in range(8))
    if per_seq_rows is not None:
        st0_ref, st1_ref = next(it), next(it)
    o_ref, u_ref = next(it), next(it)
    if per_seq_rows is None:
        carry_ref = next(it)

    tile = x_ref.shape[0]
    x = x_ref[...]
    h = ((_unit_rms(x) * ng_ref[...]) * (1.0 + _mod_rows(sc_ref, per_seq_rows))
         + _mod_rows(sh_ref, per_seq_rows)).astype(BF16)
    bcv = _dot(h, win_ref[...])
    b_g = bcv[:, :D_MODEL]
    u = bcv[:, D_MODEL:2 * D_MODEL] * bcv[:, 2 * D_MODEL:]

    row = lax.broadcasted_iota(jnp.int32, (tile, 1), 0)
    if per_seq_rows is None:
        @pl.when(pl.program_id(1) == 0)
        def _():
            carry_ref[...] = jnp.zeros_like(carry_ref)
        prev2 = carry_ref[SUBLANES - 2:SUBLANES - 1, :]
        prev1 = carry_ref[SUBLANES - 1:SUBLANES, :]
        t = row
    else:
        prev2 = _repeat_rows(st0_ref, per_seq_rows)
        prev1 = _repeat_rows(st1_ref, per_seq_rows)
        t = row & (per_seq_rows - 1)
    u1 = jnp.where(t == 0, prev1, pltpu.roll(u, 1, axis=0))
    u2 = jnp.where(t == 0, prev2, jnp.where(t == 1, prev1, pltpu.roll(u, 2, axis=0)))
    conv = u2 * cw_ref[0:1, :] + u1 * cw_ref[1:2, :] + u * cw_ref[2:3, :]
    y = _dot((b_g * conv).astype(BF16), wout_ref[...])
    o_ref[...] = x + _mod_rows(gt_ref, per_seq_rows) * y

    if per_seq_rows is None:
        tail = u[tile - SUBLANES:, :]
        carry_ref[...] = tail
        u_ref[...] = tail
    else:
        u_ref[...] = u


def _conv_mixer(x, group, mods, norm_g, w_in, conv_w, w_out, state=None):
    prompt = group.per_seq_rows is None
    in_specs = [
        group.row_spec(D_MODEL),
        _mod_spec(0, 3, group), _mod_spec(0, 4, group), _mod_spec(0, 5, group),
        _resident((None, 1, D_MODEL), lambda b, t: (1, 0, 0)),
        _resident((D_MODEL, 3 * D_MODEL), lambda b, t: (0, 0)),
        _resident((CONV_WIDTH, D_MODEL), lambda b, t: (0, 0)),
        _resident((D_MODEL, D_MODEL), lambda b, t: (0, 0)),
    ]
    args = [x, mods, mods, mods, norm_g, w_in, conv_w, w_out]
    scratch = []
    if prompt:
        u_spec = pl.BlockSpec((SUBLANES, D_MODEL), lambda b, t: (b, 0))
        u_shape = jax.ShapeDtypeStruct((group.grid[0] * SUBLANES, D_MODEL), F32)
        scratch.append(pltpu.VMEM((SUBLANES, D_MODEL), F32))
    else:
        n_seq = group.tile // group.per_seq_rows
        st_spec = pl.BlockSpec((n_seq, D_MODEL), lambda b, t: (t, 0))
        in_specs += [st_spec, st_spec]
        args += [state[0], state[1]]
        u_spec = group.row_spec(D_MODEL)
        u_shape = jax.ShapeDtypeStruct((group.rows, D_MODEL), F32)
    return pl.pallas_call(
        functools.partial(_conv_kernel, per_seq_rows=group.per_seq_rows),
        grid=group.grid,
        in_specs=in_specs,
        out_specs=[group.row_spec(D_MODEL), u_spec],
        out_shape=[jax.ShapeDtypeStruct((group.rows, D_MODEL), F32), u_shape],
        scratch_shapes=scratch,
        compiler_params=pltpu.CompilerParams(
            dimension_semantics=("arbitrary", "arbitrary"),
            vmem_limit_bytes=VMEM_LIMIT_BYTES),
        name="conv_mixer",
    )(*args)


def _sink_column(sink_ref, kv_head, q_rows):
    grp = lax.broadcasted_iota(jnp.int32, (GROUP * q_rows, 1), 0) >> _log2(q_rows)
    col = jnp.full((GROUP * q_rows, 1), sink_ref[kv_head * GROUP + GROUP - 1], F32)
    for g in range(GROUP - 2, -1, -1):
        col = jnp.where(grp == g, sink_ref[kv_head * GROUP + g], col)
    return col


def _stack_heads(q_ref, rows, kv_head):
    return jnp.concatenate(
        [q_ref[rows, (kv_head * GROUP + g) * HEAD_DIM:(kv_head * GROUP + g + 1) * HEAD_DIM]
         for g in range(GROUP)], axis=0)


def _softmax_parts(scores, sink):
    m = sink
    for s in scores:
        m = jnp.maximum(m, jnp.max(s, axis=-1, keepdims=True))
    ps = [jnp.exp(s - m) for s in scores]
    denom = jnp.exp(sink - m)
    for p in ps:
        denom = denom + jnp.sum(p, axis=-1, keepdims=True)
    return ps, 1.0 / denom


def _project_q(x_ref, sh_ref, sc_ref, ng_ref, wq_ref, tables, per_seq_rows):
    h = ((_unit_rms(x_ref[...]) * ng_ref[...]) * (1.0 + _mod_rows(sc_ref, per_seq_rows))
         + _mod_rows(sh_ref, per_seq_rows)).astype(BF16)
    q = _rope(_dot(h, wq_ref[...]), *tables)
    return (q * (HEAD_DIM ** -0.5)).astype(BF16)


def _attn_prompt_kernel(x_ref, sh_ref, sc_ref, gt_ref, ng_ref, wq_ref, wo_ref,
                        kp_ref, kc_ref, vp_ref, vc_ref, cos_ref, sup_ref, sdn_ref, sink_ref,
                        o_ref, q_scr, k_scr, v_scr, a_scr):
    tile = x_ref.shape[0]
    q_scr[...] = _project_q(x_ref, sh_ref, sc_ref, ng_ref, wq_ref,
                            (cos_ref[...], sup_ref[...], sdn_ref[...]), None)
    k_scr[:WINDOW, :] = kp_ref[...].astype(BF16)
    k_scr[WINDOW:, :] = kc_ref[...].astype(BF16)
    v_scr[:WINDOW, :] = vp_ref[...].astype(BF16)
    v_scr[WINDOW:, :] = vc_ref[...].astype(BF16)

    qi = lax.broadcasted_iota(jnp.int32, (GROUP * WINDOW, 2 * WINDOW), 0) & (WINDOW - 1)
    kj = lax.broadcasted_iota(jnp.int32, (GROUP * WINDOW, 2 * WINDOW), 1)
    in_band = (kj > qi) & (kj <= qi + WINDOW)
    first_tile = pl.program_id(1) == 0

    for blk in range(tile // WINDOW):
        rows = slice(blk * WINDOW, (blk + 1) * WINDOW)
        keys = slice(blk * WINDOW, (blk + 2) * WINDOW)
        if blk == 0:
            valid = in_band & ((kj >= WINDOW) | jnp.logical_not(first_tile))
        else:
            valid = in_band
        heads = []
        for kvh in range(N_KV_HEADS):
            lanes = slice(kvh * HEAD_DIM, (kvh + 1) * HEAD_DIM)
            s = _dot_nt(_stack_heads(q_scr, rows, kvh), k_scr[keys, lanes])
            s = jnp.where(valid, s, NEG)
            (p,), inv = _softmax_parts([s], _sink_column(sink_ref, kvh, WINDOW))
            o = _dot(p.astype(BF16), v_scr[keys, lanes]) * inv
            heads += [o[g * WINDOW:(g + 1) * WINDOW, :] for g in range(GROUP)]
        a_scr[rows, :] = jnp.concatenate(heads, axis=1).astype(BF16)

    y = _dot(a_scr[...], wo_ref[...])
    o_ref[...] = x_ref[...] + _mod_rows(gt_ref, None) * y


def _attn_prompt(x, group, mods, norm_g, w_q, w_o, k, v, tables, sinks):
    tile = group.tile
    nt = group.grid[1]
    blocks_per_tile = tile // WINDOW

    def prev_block(b, t):
        return (jnp.maximum((b * nt + t) * blocks_per_tile - 1, 0), 0)

    in_specs = [
        group.row_spec(D_MODEL),
        _mod_spec(1, 3, group), _mod_spec(1, 4, group), _mod_spec(1, 5, group),
        _resident((None, 1, D_MODEL), lambda b, t: (4, 0, 0)),
        _resident((D_MODEL, D_MODEL), lambda b, t: (0, 0)),
        _resident((D_MODEL, D_MODEL), lambda b, t: (0, 0)),
        pl.BlockSpec((WINDOW, KV_DIM), prev_block), group.row_spec(KV_DIM),
        pl.BlockSpec((WINDOW, KV_DIM), prev_block), group.row_spec(KV_DIM),
        group.table_spec(), group.table_spec(), group.table_spec(),
        pl.BlockSpec(memory_space=pltpu.SMEM),
    ]
    return pl.pallas_call(
        _attn_prompt_kernel,
        grid=group.grid,
        in_specs=in_specs,
        out_specs=group.row_spec(D_MODEL),
        out_shape=jax.ShapeDtypeStruct((group.rows, D_MODEL), F32),
        scratch_shapes=[
            pltpu.VMEM((tile, D_MODEL), BF16),
            pltpu.VMEM((tile + WINDOW, KV_DIM), BF16),
            pltpu.VMEM((tile + WINDOW, KV_DIM), BF16),
            pltpu.VMEM((tile, D_MODEL), BF16),
        ],
        compiler_params=pltpu.CompilerParams(
            dimension_semantics=("arbitrary", "arbitrary"),
            vmem_limit_bytes=VMEM_LIMIT_BYTES),
        name="attn_prompt",
    )(x, mods, mods, mods, norm_g, w_q, w_o, k, k, v, v, *tables, sinks)


def _attn_sample_kernel(x_ref, sh_ref, sc_ref, gt_ref, ng_ref, wq_ref, wo_ref,
                        kc_ref, vc_ref, kn_ref, vn_ref, cos_ref, sup_ref, sdn_ref, sink_ref,
                        o_ref, q_scr, kc_scr, vc_scr, kn_scr, vn_scr, *, dec_seq):
    rows = x_ref.shape[0]
    n_cache = kc_ref.shape[0]
    q_scr[...] = _project_q(x_ref, sh_ref, sc_ref, ng_ref, wq_ref,
                            (cos_ref[...], sup_ref[...], sdn_ref[...]), dec_seq)
    kc_scr[...] = kc_ref[...].astype(BF16)
    vc_scr[...] = vc_ref[...].astype(BF16)
    kn_scr[...] = kn_ref[...].astype(BF16)
    vn_scr[...] = vn_ref[...].astype(BF16)

    q_row = lax.broadcasted_iota(jnp.int32, (GROUP * rows, 1), 0) & (rows - 1)
    q_seq, q_i = q_row >> _log2(dec_seq), q_row & (dec_seq - 1)
    c_col = lax.broadcasted_iota(jnp.int32, (1, n_cache), 1)
    valid_c = (q_seq == (c_col >> _log2(WINDOW))) & ((c_col & (WINDOW - 1)) > q_i)
    n_col = lax.broadcasted_iota(jnp.int32, (1, rows), 1)
    valid_n = (q_seq == (n_col >> _log2(dec_seq))) & ((n_col & (dec_seq - 1)) <= q_i)

    heads = []
    for kvh in range(N_KV_HEADS):
        lanes = slice(kvh * HEAD_DIM, (kvh + 1) * HEAD_DIM)
        qg = _stack_heads(q_scr, slice(None), kvh)
        s_c = jnp.where(valid_c, _dot_nt(qg, kc_scr[:, lanes]), NEG)
        s_n = jnp.where(valid_n, _dot_nt(qg, kn_scr[:, lanes]), NEG)
        (p_c, p_n), inv = _softmax_parts([s_c, s_n], _sink_column(sink_ref, kvh, rows))
        o = (_dot(p_c.astype(BF16), vc_scr[:, lanes])
             + _dot(p_n.astype(BF16), vn_scr[:, lanes])) * inv
        heads += [o[g * rows:(g + 1) * rows, :] for g in range(GROUP)]
    attn = jnp.concatenate(heads, axis=1).astype(BF16)
    o_ref[...] = x_ref[...] + _mod_rows(gt_ref, dec_seq) * _dot(attn, wo_ref[...])


def _attn_sample(x, n_seq_total, dec_seq, mods, norm_g, w_q, w_o, k_cache, v_cache,
                 k_new, v_new, tables, sinks):
    n_seq = SAMPLE_ATTN_SEQS
    rows = n_seq * dec_seq
    grid = (n_seq_total // n_seq,)
    row_spec = lambda width: pl.BlockSpec((rows, width), lambda t: (t, 0))
    mod_spec = lambda col: pl.BlockSpec((None, n_seq, D_MODEL), lambda t: (1, t, col))
    cache_spec = pl.BlockSpec((n_seq * WINDOW, KV_DIM), lambda t: (t, 0))
    table_spec = pl.BlockSpec((rows, LANES), lambda t: (0, 0))
    in_specs = [
        row_spec(D_MODEL), mod_spec(3), mod_spec(4), mod_spec(5),
        _resident((None, 1, D_MODEL), lambda t: (4, 0, 0)),
        _resident((D_MODEL, D_MODEL), lambda t: (0, 0)),
        _resident((D_MODEL, D_MODEL), lambda t: (0, 0)),
        cache_spec, cache_spec, row_spec(KV_DIM), row_spec(KV_DIM),
        table_spec, table_spec, table_spec,
        pl.BlockSpec(memory_space=pltpu.SMEM),
    ]
    return pl.pallas_call(
        functools.partial(_attn_sample_kernel, dec_seq=dec_seq),
        grid=grid,
        in_specs=in_specs,
        out_specs=row_spec(D_MODEL),
        out_shape=jax.ShapeDtypeStruct((n_seq_total * dec_seq, D_MODEL), F32),
        scratch_shapes=[
            pltpu.VMEM((rows, D_MODEL), BF16),
            pltpu.VMEM((n_seq * WINDOW, KV_DIM), BF16),
            pltpu.VMEM((n_seq * WINDOW, KV_DIM), BF16),
            pltpu.VMEM((rows, KV_DIM), BF16),
            pltpu.VMEM((rows, KV_DIM), BF16),
        ],
        compiler_params=pltpu.CompilerParams(
            dimension_semantics=("arbitrary",),
            vmem_limit_bytes=VMEM_LIMIT_BYTES),
        name="attn_sample",
    )(x, mods, mods, mods, norm_g, w_q, w_o, k_cache, v_cache, k_new, v_new, *tables, sinks)


def _rope_tables(pos):
    half = ROT_DIM // 2
    inv_freq = ROPE_THETA ** (-jnp.arange(0, ROT_DIM, 2, dtype=F32) / ROT_DIM)
    ang = pos.astype(F32)[:, None] * inv_freq[None, :]
    cos, sin = jnp.cos(ang), jnp.sin(ang)
    n = pos.shape[0]
    rest = HEAD_DIM - ROT_DIM
    zeros = jnp.zeros((n, half), F32)
    cos_t = jnp.concatenate([cos, cos, jnp.ones((n, rest), F32)], axis=1)
    sin_up = jnp.concatenate([-sin, zeros, jnp.zeros((n, rest), F32)], axis=1)
    sin_dn = jnp.concatenate([zeros, sin, jnp.zeros((n, rest), F32)], axis=1)
    reps = LANES // HEAD_DIM
    return tuple(jnp.tile(t, (1, reps)) for t in (cos_t, sin_up, sin_dn))


def kernel(x_prompt, x_sample, state_conv, cache_k_win, cache_v_win, c_prompt, c_sample, norm_g, w_ada, b_ada, w_ffn_gate, w_ffn_up, w_ffn_down, conv_w_in, conv_w, conv_w_out, kv_norm_g, w_ada_kv, b_ada_kv, w_k, w_v, attn_w_q, attn_sinks, attn_w_o, final_norm_g):
    batch, seq, d = x_prompt.shape
    dec_batch, dec_seq, _ = x_sample.shape
    w_buf = cache_k_win.shape[1]
    past_len = 16384
    assert d == D_MODEL and w_buf == WINDOW and dec_seq == SUBLANES
    assert w_ada.shape[0] == 2 and conv_w_in.shape[0] == 1 and attn_w_q.shape[0] == 1

    pad = (-(dec_batch + batch)) % SUBLANES
    c_all = jnp.concatenate([c_sample, c_prompt, jnp.zeros((pad, d), F32)], axis=0)
    mods = _ada(c_all, w_ada, b_ada)
    mods_kv = _ada(c_all, w_ada_kv[None], b_ada_kv[None])
    prompt_mod_block = dec_batch // SUBLANES

    groups = {
        "prompt": _Group(batch, seq, None, prompt_mod_block),
        "sample": _Group(dec_batch, dec_seq, dec_seq, prompt_mod_block),
    }
    xs = {"prompt": x_prompt.reshape(batch * seq, d), "sample": x_sample.reshape(dec_batch * dec_seq, d)}

    norm_g3 = norm_g.reshape(-1, 1, d)
    wg, wu, wd = (w.astype(BF16) for w in (w_ffn_gate, w_ffn_up, w_ffn_down))
    w_in, w_out = conv_w_in[0].astype(BF16), conv_w_out[0].astype(BF16)
    w_q, w_o = attn_w_q[0].astype(BF16), attn_w_o[0].astype(BF16)
    sinks = attn_sinks[0]

    tab_prompt = _rope_tables(jnp.arange(seq, dtype=jnp.int32))
    tab_dec = _rope_tables(past_len + jnp.arange(dec_seq, dtype=jnp.int32))
    tables = {
        "prompt": tab_prompt,
        "sample": tuple(jnp.tile(t, (ROW_TILE // dec_seq, 1)) for t in tab_dec),
    }
    kv_common = dict(norm_g=kv_norm_g.reshape(1, d), mods=mods_kv,
                     w_k=w_k.astype(BF16), w_v=w_v.astype(BF16))

    out = {}
    for name, g in groups.items():
        x = xs[name]
        (x,) = _ffn(x, g, mods, 0, 0, norm_g3, wg, wu, wd, 0)
        state = None if name == "prompt" else (state_conv[0, :, 0, :], state_conv[0, :, 1, :])
        x, u = _conv_mixer(x, g, mods, norm_g3, w_in, conv_w[0], w_out, state)
        (x,) = _ffn(x, g, mods, 0, 2, norm_g3, wg, wu, wd, 1)
        x, k, v = _ffn(x, g, mods, 1, 0, norm_g3, wg, wu, wd, 0,
                       kv=dict(kv_common, tables=tables[name]))
        if name == "prompt":
            x = _attn_prompt(x, g, mods, norm_g3, w_q, w_o, k, v, tables[name], sinks)
        else:
            tabs = tuple(t[:SAMPLE_ATTN_SEQS * dec_seq] for t in tables[name])
            x = _attn_sample(x, dec_batch, dec_seq, mods, norm_g3, w_q, w_o,
                             cache_k_win.reshape(dec_batch * w_buf, KV_DIM),
                             cache_v_win.reshape(dec_batch * w_buf, KV_DIM),
                             k, v, tabs, sinks)
        (y,) = _ffn(x, g, mods, 1, 2, norm_g3, wg, wu, wd, 1, final_g=final_norm_g.reshape(1, d))
        out[name] = (y, u, k, v)

    y_p, u_p, k_p, v_p = out["prompt"]
    y_s, u_s, k_s, v_s = out["sample"]
    tail = CONV_WIDTH - 1
    conv_p = u_p.reshape(batch, SUBLANES, d)[:, SUBLANES - tail:][None]
    conv_s = u_s.reshape(dec_batch, dec_seq, d)[:, dec_seq - tail:][None]
    heads = lambda a, n, t: a.reshape(n, t, N_KV_HEADS, HEAD_DIM)
    k_state_p = heads(k_p, batch, seq)[:, seq - WINDOW:]
    v_state_p = heads(v_p, batch, seq)[:, seq - WINDOW:]
    k_state_s = jnp.concatenate([cache_k_win[:, dec_seq:], heads(k_s, dec_batch, dec_seq)], axis=1)
    v_state_s = jnp.concatenate([cache_v_win[:, dec_seq:], heads(v_s, dec_batch, dec_seq)], axis=1)
    return (y_p.reshape(batch, seq, d), y_s.reshape(dec_batch, dec_seq, d), conv_p, conv_s,
            k_state_p, v_state_p, k_state_s, v_state_s)
```

```python
import functools

import jax
import jax.numpy as jnp
from jax import lax
from jax.experimental import pallas as pl
from jax.experimental.pallas import tpu as pltpu

F32 = jnp.float32
BF16 = jnp.bfloat16

D_MODEL = 1024
D_FF = 2816
HEAD_DIM = 64
N_HEADS = 16
N_KV_HEADS = 4
GROUP = N_HEADS // N_KV_HEADS
KV_DIM = N_KV_HEADS * HEAD_DIM
WINDOW = 128
ROT_DIM = 16
ROPE_THETA = 500000.0
CONV_WIDTH = 3
N_MOD = 9
EPS = 1e-6
NEG = -1e30

SUBLANES = 8
LANES = 128
VMEM_LIMIT_BYTES = 56 * 1024 * 1024

ROW_TILE = 512
SAMPLE_ATTN_SEQS = 16
FF_CHUNKS = ((0, 1536), (1536, D_FF))


def _dot(a, b):
    return jnp.dot(a, b, preferred_element_type=F32)


def _dot_nt(a, b):
    return lax.dot_general(a, b, (((1,), (1,)), ((), ())), preferred_element_type=F32)


def _log2(n):
    assert n & (n - 1) == 0, n
    return n.bit_length() - 1


def _silu(x):
    return x * jax.nn.sigmoid(x)


def _unit_rms(x):
    return x * lax.rsqrt(jnp.mean(x * x, axis=-1, keepdims=True) + EPS)


def _repeat_rows(ref, reps):
    n = ref.shape[1]
    return jnp.concatenate(
        [jnp.broadcast_to(ref[i:i + 1, :], (reps, n)) for i in range(ref.shape[0])], axis=0)


def _mod_rows(ref, per_seq_rows):
    if per_seq_rows is None:
        return ref[pl.ds(pl.program_id(0), 1), :]
    return _repeat_rows(ref, per_seq_rows)


def _rope(x, cos_t, sin_up_t, sin_dn_t):
    n = x.shape[1]
    reps = n // LANES
    half = ROT_DIM // 2
    cos = jnp.concatenate([cos_t] * reps, axis=1)
    sin_up = jnp.concatenate([sin_up_t] * reps, axis=1)
    sin_dn = jnp.concatenate([sin_dn_t] * reps, axis=1)
    x_up = pltpu.roll(x, n - half, axis=1)
    x_dn = pltpu.roll(x, half, axis=1)
    return x * cos + x_up * sin_up + x_dn * sin_dn


def _ada_kernel(c_ref, w_ref, b_ref, o_ref):
    a = _silu(c_ref[...]).astype(BF16)
    o_ref[...] = _dot(a, w_ref[...].astype(BF16)) + b_ref[...]


def _ada(c_all, w, b):
    n_layers, _, n = w.shape
    m = c_all.shape[0]
    tn = D_MODEL
    return pl.pallas_call(
        _ada_kernel,
        grid=(n_layers, n // tn),
        in_specs=[
            pl.BlockSpec((m, D_MODEL), lambda l, j: (0, 0)),
            pl.BlockSpec((None, D_MODEL, tn), lambda l, j: (l, 0, j)),
            pl.BlockSpec((None, 1, tn), lambda l, j: (l, 0, j)),
        ],
        out_specs=pl.BlockSpec((None, m, tn), lambda l, j: (l, 0, j)),
        out_shape=jax.ShapeDtypeStruct((n_layers, m, n), F32),
        compiler_params=pltpu.CompilerParams(
            dimension_semantics=("arbitrary", "arbitrary"),
            vmem_limit_bytes=VMEM_LIMIT_BYTES),
        name="adaln_tables",
    )(c_all, w, b.reshape(n_layers, 1, n))


def _rope_t(xt, cos_t, sin_t):
    half = ROT_DIM // 2
    out = []
    for h in range(xt.shape[0] // HEAD_DIM):
        base = h * HEAD_DIM
        x1 = xt[base:base + half, :]
        x2 = xt[base + half:base + ROT_DIM, :]
        out += [x1 * cos_t - x2 * sin_t, x2 * cos_t + x1 * sin_t,
                xt[base + ROT_DIM:base + HEAD_DIM, :]]
    return jnp.concatenate(out, axis=0)


def _ffn_kernel(*refs, per_seq_rows, with_kv, kv_transposed, final):
    it = iter(refs)
    x_ref, sh_ref, sc_ref, gt_ref, ng_ref, wg_ref, wu_ref, wd_ref = (next(it) for _ in range(8))
    if with_kv:
        kvg_ref, kvsh_ref, kvsc_ref, wk_ref, wv_ref, cos_ref, sup_ref = (next(it) for _ in range(7))
        if not kv_transposed:
            sdn_ref = next(it)
    if final:
        fg_ref = next(it)
    o_ref = next(it)
    if with_kv:
        k_ref, v_ref = next(it), next(it)

    x = x_ref[...]
    xn = _unit_rms(x)
    h = (xn * ng_ref[...]) * (1.0 + _mod_rows(sc_ref, per_seq_rows)) + _mod_rows(sh_ref, per_seq_rows)
    hb = h.astype(BF16)
    y = None
    for lo, hi in FF_CHUNKS:
        g = _dot(hb, wg_ref[:, lo:hi])
        u = _dot(hb, wu_ref[:, lo:hi])
        a = (_silu(g) * u).astype(BF16)
        part = _dot(a, wd_ref[lo:hi, :])
        y = part if y is None else y + part
    x_new = x + (0.5 * _mod_rows(gt_ref, per_seq_rows)) * y

    if with_kv:
        hk = ((xn * kvg_ref[...]) * (1.0 + _mod_rows(kvsc_ref, per_seq_rows))
              + _mod_rows(kvsh_ref, per_seq_rows)).astype(BF16)
        if kv_transposed:
            k_ref[...] = _rope_t(_dot_nt(wk_ref[...], hk), cos_ref[...], sup_ref[...])
            v_ref[...] = _dot_nt(wv_ref[...], hk)
        else:
            k_ref[...] = _rope(_dot(hk, wk_ref[...]), cos_ref[...], sup_ref[...], sdn_ref[...])
            v_ref[...] = _dot(hk, wv_ref[...])

    if final:
        o_ref[...] = _unit_rms(x_new) * fg_ref[...]
    else:
        o_ref[...] = x_new


def _resident(shape, index_map):
    return pl.BlockSpec(shape, index_map, pipeline_mode=pl.Buffered(1))


def _mod_spec(layer, col, group):
    if group.per_seq_rows is None:
        return pl.BlockSpec((None, SUBLANES, D_MODEL),
                            lambda b, t: (layer, group.prompt_mod_block, col))
    return pl.BlockSpec((None, group.tile // group.per_seq_rows, D_MODEL),
                        lambda b, t: (layer, t, col))


class _Group:
    def __init__(self, n_seq, seq_len, per_seq_rows, prompt_mod_block):
        self.tile = ROW_TILE
        self.rows = n_seq * seq_len
        self.per_seq_rows = per_seq_rows
        self.prompt_mod_block = prompt_mod_block
        if per_seq_rows is None:
            self.grid = (n_seq, seq_len // self.tile)
        else:
            self.grid = (1, self.rows // self.tile)
        self.tiles_per_seq = self.grid[1]

    def row_spec(self, width):
        nt = self.grid[1]
        return pl.BlockSpec((self.tile, width), lambda b, t: (b * nt + t, 0))

    def col_spec(self, height):
        nt = self.grid[1]
        return pl.BlockSpec((height, self.tile), lambda b, t: (0, b * nt + t))

    def table_t_spec(self):
        if self.per_seq_rows is None:
            return pl.BlockSpec((ROT_DIM // 2, self.tile), lambda b, t: (0, t))
        return pl.BlockSpec((ROT_DIM // 2, self.tile), lambda b, t: (0, 0))

    def table_spec(self):
        if self.per_seq_rows is None:
            return pl.BlockSpec((self.tile, LANES), lambda b, t: (t, 0))
        return pl.BlockSpec((self.tile, LANES), lambda b, t: (0, 0))


def _ffn(x, group, mods, layer, sub, norm_g, wg, wu, wd, slot, kv=None, final_g=None):
    with_kv = kv is not None
    final = final_g is not None
    vec = lambda i: _resident((None, 1, D_MODEL), lambda b, t: (i, 0, 0))
    in_specs = [
        group.row_spec(D_MODEL),
        _mod_spec(layer, 3 * sub + 0, group),
        _mod_spec(layer, 3 * sub + 1, group),
        _mod_spec(layer, 3 * sub + 2, group),
        vec(layer * 3 + sub),
        _resident((None, None, D_MODEL, D_FF), lambda b, t: (layer, slot, 0, 0)),
        _resident((None, None, D_MODEL, D_FF), lambda b, t: (layer, slot, 0, 0)),
        _resident((None, None, D_FF, D_MODEL), lambda b, t: (layer, slot, 0, 0)),
    ]
    args = [x, mods, mods, mods, norm_g, wg, wu, wd]
    out_specs = [group.row_spec(D_MODEL)]
    out_shape = [jax.ShapeDtypeStruct((group.rows, D_MODEL), F32)]
    kv_transposed = with_kv and kv["transposed"]
    if with_kv:
        w_shape = (KV_DIM, D_MODEL) if kv_transposed else (D_MODEL, KV_DIM)
        in_specs += [
            _resident((1, D_MODEL), lambda b, t: (0, 0)),
            _mod_spec(0, 0, group),
            _mod_spec(0, 1, group),
            _resident(w_shape, lambda b, t: (0, 0)),
            _resident(w_shape, lambda b, t: (0, 0)),
        ]
        args += [kv["norm_g"], kv["mods"], kv["mods"], kv["w_k"], kv["w_v"], *kv["tables"]]
        if kv_transposed:
            in_specs += [group.table_t_spec()] * 2
            out_specs += [group.col_spec(KV_DIM)] * 2
            out_shape += [jax.ShapeDtypeStruct((KV_DIM, group.rows), F32)] * 2
        else:
            in_specs += [group.table_spec()] * 3
            out_specs += [group.row_spec(KV_DIM)] * 2
            out_shape += [jax.ShapeDtypeStruct((group.rows, KV_DIM), F32)] * 2
    if final:
        in_specs.append(_resident((1, D_MODEL), lambda b, t: (0, 0)))
        args.append(final_g)
    return pl.pallas_call(
        functools.partial(_ffn_kernel, per_seq_rows=group.per_seq_rows,
                          with_kv=with_kv, kv_transposed=kv_transposed, final=final),
        grid=group.grid,
        in_specs=in_specs,
        out_specs=out_specs,
        out_shape=out_shape,
        compiler_params=pltpu.CompilerParams(
            dimension_semantics=("arbitrary", "arbitrary"),
            vmem_limit_bytes=VMEM_LIMIT_BYTES),
        name=f"ffn_l{layer}_s{sub}",
    )(*args)


def _conv_kernel(*refs, per_seq_rows):
    it = iter(refs)
    x_ref, sh_ref, sc_ref, gt_ref, ng_ref, win_ref, cw_ref, wout_ref = (next(it) for _ in range(8))
    if per_seq_rows is not None:
        st0_ref, st1_ref = next(it), next(it)
    o_ref, u_ref = next(it), next(it)
    if per_seq_rows is None:
        carry_ref = next(it)

    tile = x_ref.shape[0]
    x = x_ref[...]
    h = ((_unit_rms(x) * ng_ref[...]) * (1.0 + _mod_rows(sc_ref, per_seq_rows))
         + _mod_rows(sh_ref, per_seq_rows)).astype(BF16)
    bcv = _dot(h, win_ref[...])
    b_g = bcv[:, :D_MODEL]
    u = bcv[:, D_MODEL:2 * D_MODEL] * bcv[:, 2 * D_MODEL:]

    row = lax.broadcasted_iota(jnp.int32, (tile, 1), 0)
    if per_seq_rows is None:
        @pl.when(pl.program_id(1) == 0)
        def _():
            carry_ref[...] = jnp.zeros_like(carry_ref)
        prev2 = carry_ref[SUBLANES - 2:SUBLANES - 1, :]
        prev1 = carry_ref[SUBLANES - 1:SUBLANES, :]
        t = row
    else:
        prev2 = _repeat_rows(st0_ref, per_seq_rows)
        prev1 = _repeat_rows(st1_ref, per_seq_rows)
        t = row & (per_seq_rows - 1)
    u1 = jnp.where(t == 0, prev1, pltpu.roll(u, 1, axis=0))
    u2 = jnp.where(t == 0, prev2, jnp.where(t == 1, prev1, pltpu.roll(u, 2, axis=0)))
    conv = u2 * cw_ref[0:1, :] + u1 * cw_ref[1:2, :] + u * cw_ref[2:3, :]
    y = _dot((b_g * conv).astype(BF16), wout_ref[...])
    o_ref[...] = x + _mod_rows(gt_ref, per_seq_rows) * y

    if per_seq_rows is None:
        tail = u[tile - SUBLANES:, :]
        carry_ref[...] = tail
        u_ref[...] = tail
    else:
        u_ref[...] = u


def _conv_mixer(x, group, mods, norm_g, w_in, conv_w, w_out, state=None):
    prompt = group.per_seq_rows is None
    in_specs = [
        group.row_spec(D_MODEL),
        _mod_spec(0, 3, group), _mod_spec(0, 4, group), _mod_spec(0, 5, group),
        _resident((None, 1, D_MODEL), lambda b, t: (1, 0, 0)),
        _resident((D_MODEL, 3 * D_MODEL), lambda b, t: (0, 0)),
        _resident((CONV_WIDTH, D_MODEL), lambda b, t: (0, 0)),
        _resident((D_MODEL, D_MODEL), lambda b, t: (0, 0)),
    ]
    args = [x, mods, mods, mods, norm_g, w_in, conv_w, w_out]
    scratch = []
    if prompt:
        u_spec = pl.BlockSpec((SUBLANES, D_MODEL), lambda b, t: (b, 0))
        u_shape = jax.ShapeDtypeStruct((group.grid[0] * SUBLANES, D_MODEL), F32)
        scratch.append(pltpu.VMEM((SUBLANES, D_MODEL), F32))
    else:
        n_seq = group.tile // group.per_seq_rows
        st_spec = pl.BlockSpec((n_seq, D_MODEL), lambda b, t: (t, 0))
        in_specs += [st_spec, st_spec]
        args += [state[0], state[1]]
        u_spec = group.row_spec(D_MODEL)
        u_shape = jax.ShapeDtypeStruct((group.rows, D_MODEL), F32)
    return pl.pallas_call(
        functools.partial(_conv_kernel, per_seq_rows=group.per_seq_rows),
        grid=group.grid,
        in_specs=in_specs,
        out_specs=[group.row_spec(D_MODEL), u_spec],
        out_shape=[jax.ShapeDtypeStruct((group.rows, D_MODEL), F32), u_shape],
        scratch_shapes=scratch,
        compiler_params=pltpu.CompilerParams(
            dimension_semantics=("arbitrary", "arbitrary"),
            vmem_limit_bytes=VMEM_LIMIT_BYTES),
        name="conv_mixer",
    )(*args)


def _sink_column(sink_ref, kv_head, q_rows):
    grp = lax.broadcasted_iota(jnp.int32, (GROUP * q_rows, 1), 0) >> _log2(q_rows)
    col = jnp.full((GROUP * q_rows, 1), sink_ref[kv_head * GROUP + GROUP - 1], F32)
    for g in range(GROUP - 2, -1, -1):
        col = jnp.where(grp == g, sink_ref[kv_head * GROUP + g], col)
    return col


def _stack_heads(q_ref, rows, kv_head):
    return jnp.concatenate(
        [q_ref[rows, (kv_head * GROUP + g) * HEAD_DIM:(kv_head * GROUP + g + 1) * HEAD_DIM]
         for g in range(GROUP)], axis=0)


def _softmax_parts(scores, sink):
    m = sink
    for s in scores:
        m = jnp.maximum(m, jnp.max(s, axis=-1, keepdims=True))
    ps = [jnp.exp(s - m) for s in scores]
    denom = jnp.exp(sink - m)
    for p in ps:
        denom = denom + jnp.sum(p, axis=-1, keepdims=True)
    return ps, 1.0 / denom


def _project_q(x_ref, sh_ref, sc_ref, ng_ref, wq_ref, tables, per_seq_rows):
    h = ((_unit_rms(x_ref[...]) * ng_ref[...]) * (1.0 + _mod_rows(sc_ref, per_seq_rows))
         + _mod_rows(sh_ref, per_seq_rows)).astype(BF16)
    q = _rope(_dot(h, wq_ref[...]), *tables)
    return (q * (HEAD_DIM ** -0.5)).astype(BF16)


def _attn_prompt_kernel(x_ref, sh_ref, sc_ref, gt_ref, ng_ref, wq_ref, wo_ref,
                        kp_ref, kc_ref, vp_ref, vc_ref, cos_ref, sup_ref, sdn_ref, sink_ref,
                        o_ref, q_scr, k_scr, v_scr, a_scr):
    tile = x_ref.shape[0]
    q_scr[...] = _project_q(x_ref, sh_ref, sc_ref, ng_ref, wq_ref,
                            (cos_ref[...], sup_ref[...], sdn_ref[...]), None)
    k_scr[:WINDOW, :] = kp_ref[...].astype(BF16)
    k_scr[WINDOW:, :] = kc_ref[...].astype(BF16)
    v_scr[:WINDOW, :] = vp_ref[...].astype(BF16)
    v_scr[WINDOW:, :] = vc_ref[...].astype(BF16)

    qi = lax.broadcasted_iota(jnp.int32, (GROUP * WINDOW, 2 * WINDOW), 0) & (WINDOW - 1)
    kj = lax.broadcasted_iota(jnp.int32, (GROUP * WINDOW, 2 * WINDOW), 1)
    in_band = (kj > qi) & (kj <= qi + WINDOW)
    first_tile = pl.program_id(1) == 0

    for blk in range(tile // WINDOW):
        rows = slice(blk * WINDOW, (blk + 1) * WINDOW)
        keys = slice(blk * WINDOW, (blk + 2) * WINDOW)
        if blk == 0:
            valid = in_band & ((kj >= WINDOW) | jnp.logical_not(first_tile))
        else:
            valid = in_band
        heads = []
        for kvh in range(N_KV_HEADS):
            lanes = slice(kvh * HEAD_DIM, (kvh + 1) * HEAD_DIM)
            s = _dot_nt(_stack_heads(q_scr, rows, kvh), k_scr[keys, lanes])
            s = jnp.where(valid, s, NEG)
            (p,), inv = _softmax_parts([s], _sink_column(sink_ref, kvh, WINDOW))
            o = _dot(p.astype(BF16), v_scr[keys, lanes]) * inv
            heads += [o[g * WINDOW:(g + 1) * WINDOW, :] for g in range(GROUP)]
        a_scr[rows, :] = jnp.concatenate(heads, axis=1).astype(BF16)

    y = _dot(a_scr[...], wo_ref[...])
    o_ref[...] = x_ref[...] + _mod_rows(gt_ref, None) * y


def _attn_prompt(x, group, mods, norm_g, w_q, w_o, k, v, tables, sinks):
    tile = group.tile
    nt = group.grid[1]
    blocks_per_tile = tile // WINDOW

    def prev_block(b, t):
        return (jnp.maximum((b * nt + t) * blocks_per_tile - 1, 0), 0)

    in_specs = [
        group.row_spec(D_MODEL),
        _mod_spec(1, 3, group), _mod_spec(1, 4, group), _mod_spec(1, 5, group),
        _resident((None, 1, D_MODEL), lambda b, t: (4, 0, 0)),
        _resident((D_MODEL, D_MODEL), lambda b, t: (0, 0)),
        _resident((D_MODEL, D_MODEL), lambda b, t: (0, 0)),
        pl.BlockSpec((WINDOW, KV_DIM), prev_block), group.row_spec(KV_DIM),
        pl.BlockSpec((WINDOW, KV_DIM), prev_block), group.row_spec(KV_DIM),
        group.table_spec(), group.table_spec(), group.table_spec(),
        pl.BlockSpec(memory_space=pltpu.SMEM),
    ]
    return pl.pallas_call(
        _attn_prompt_kernel,
        grid=group.grid,
        in_specs=in_specs,
        out_specs=group.row_spec(D_MODEL),
        out_shape=jax.ShapeDtypeStruct((group.rows, D_MODEL), F32),
        scratch_shapes=[
            pltpu.VMEM((tile, D_MODEL), BF16),
            pltpu.VMEM((tile + WINDOW, KV_DIM), BF16),
            pltpu.VMEM((tile + WINDOW, KV_DIM), BF16),
            pltpu.VMEM((tile, D_MODEL), BF16),
        ],
        compiler_params=pltpu.CompilerParams(
            dimension_semantics=("arbitrary", "arbitrary"),
            vmem_limit_bytes=VMEM_LIMIT_BYTES),
        name="attn_prompt",
    )(x, mods, mods, mods, norm_g, w_q, w_o, k, k, v, v, *tables, sinks)


def _attn_sample_kernel(x_ref, sh_ref, sc_ref, gt_ref, ng_ref, wq_ref, wo_ref,
                        kc_ref, vc_ref, kn_ref, vn_ref, cos_ref, sup_ref, sdn_ref, sink_ref,
                        o_ref, kout_ref, vout_ref, q_scr, *, dec_seq):
    rows = x_ref.shape[0]
    n_seq = rows // dec_seq
    n_cache = n_seq * WINDOW
    q_scr[...] = _project_q(x_ref, sh_ref, sc_ref, ng_ref, wq_ref,
                            (cos_ref[...], sup_ref[...], sdn_ref[...]), dec_seq)

    def all_windows(ref, kvh):
        return jnp.concatenate(
            [ref[(s * N_KV_HEADS + kvh) * HEAD_DIM:(s * N_KV_HEADS + kvh + 1) * HEAD_DIM, :].astype(BF16)
             for s in range(n_seq)], axis=1)

    q_row = lax.broadcasted_iota(jnp.int32, (GROUP * rows, 1), 0) & (rows - 1)
    q_seq, q_i = q_row >> _log2(dec_seq), q_row & (dec_seq - 1)
    c_col = lax.broadcasted_iota(jnp.int32, (1, n_cache), 1)
    valid_c = (q_seq == (c_col >> _log2(WINDOW))) & ((c_col & (WINDOW - 1)) > q_i)
    n_col = lax.broadcasted_iota(jnp.int32, (1, rows), 1)
    valid_n = (q_seq == (n_col >> _log2(dec_seq))) & ((n_col & (dec_seq - 1)) <= q_i)

    heads = []
    for kvh in range(N_KV_HEADS):
        dims = slice(kvh * HEAD_DIM, (kvh + 1) * HEAD_DIM)
        qg = _stack_heads(q_scr, slice(None), kvh)
        s_c = jnp.where(valid_c, _dot(qg, all_windows(kc_ref, kvh)), NEG)
        s_n = jnp.where(valid_n, _dot(qg, kn_ref[dims, :].astype(BF16)), NEG)
        (p_c, p_n), inv = _softmax_parts([s_c, s_n], _sink_column(sink_ref, kvh, rows))
        o = (_dot_nt(p_c.astype(BF16), all_windows(vc_ref, kvh))
             + _dot_nt(p_n.astype(BF16), vn_ref[dims, :].astype(BF16))) * inv
        heads += [o[g * rows:(g + 1) * rows, :] for g in range(GROUP)]
    attn = jnp.concatenate(heads, axis=1).astype(BF16)
    o_ref[...] = x_ref[...] + _mod_rows(gt_ref, dec_seq) * _dot(attn, wo_ref[...])

    keep = lax.broadcasted_iota(jnp.int32, (1, WINDOW), 1) < WINDOW - dec_seq
    for src_ref, new_ref, dst_ref in ((kc_ref, kn_ref, kout_ref), (vc_ref, vn_ref, vout_ref)):
        new = new_ref[...]
        for s in range(n_seq):
            blk = slice(s * KV_DIM, (s + 1) * KV_DIM)
            old = pltpu.roll(src_ref[blk, :], WINDOW - dec_seq, axis=1)
            app = pltpu.roll(new, (WINDOW - dec_seq - s * dec_seq) % rows, axis=1)
            dst_ref[blk, :] = jnp.where(keep, old, app)


def _attn_sample(x, n_seq_total, dec_seq, mods, norm_g, w_q, w_o, k_cache, v_cache,
                 k_new, v_new, tables, sinks):
    n_seq = SAMPLE_ATTN_SEQS
    rows = n_seq * dec_seq
    assert rows == WINDOW == LANES
    grid = (n_seq_total // n_seq,)
    row_spec = lambda width: pl.BlockSpec((rows, width), lambda t: (t, 0))
    mod_spec = lambda col: pl.BlockSpec((None, n_seq, D_MODEL), lambda t: (1, t, col))
    cache_spec = pl.BlockSpec((n_seq * KV_DIM, WINDOW), lambda t: (t, 0))
    new_spec = pl.BlockSpec((KV_DIM, rows), lambda t: (0, t))
    table_spec = pl.BlockSpec((rows, LANES), lambda t: (0, 0))
    in_specs = [
        row_spec(D_MODEL), mod_spec(3), mod_spec(4), mod_spec(5),
        _resident((None, 1, D_MODEL), lambda t: (4, 0, 0)),
        _resident((D_MODEL, D_MODEL), lambda t: (0, 0)),
        _resident((D_MODEL, D_MODEL), lambda t: (0, 0)),
        cache_spec, cache_spec, new_spec, new_spec,
        table_spec, table_spec, table_spec,
        pl.BlockSpec(memory_space=pltpu.SMEM),
    ]
    cache_shape = jax.ShapeDtypeStruct((n_seq_total * KV_DIM, WINDOW), F32)
    return pl.pallas_call(
        functools.partial(_attn_sample_kernel, dec_seq=dec_seq),
        grid=grid,
        in_specs=in_specs,
        out_specs=[row_spec(D_MODEL), cache_spec, cache_spec],
        out_shape=[jax.ShapeDtypeStruct((n_seq_total * dec_seq, D_MODEL), F32),
                   cache_shape, cache_shape],
        scratch_shapes=[pltpu.VMEM((rows, D_MODEL), BF16)],
        compiler_params=pltpu.CompilerParams(
            dimension_semantics=("arbitrary",),
            vmem_limit_bytes=VMEM_LIMIT_BYTES),
        name="attn_sample",
    )(x, mods, mods, mods, norm_g, w_q, w_o, k_cache, v_cache, k_new, v_new, *tables, sinks)


def _rope_angles(pos):
    inv_freq = ROPE_THETA ** (-jnp.arange(0, ROT_DIM, 2, dtype=F32) / ROT_DIM)
    ang = pos.astype(F32)[:, None] * inv_freq[None, :]
    return jnp.cos(ang), jnp.sin(ang)


def _rope_tables(pos):
    half = ROT_DIM // 2
    cos, sin = _rope_angles(pos)
    n = pos.shape[0]
    rest = HEAD_DIM - ROT_DIM
    zeros = jnp.zeros((n, half), F32)
    cos_t = jnp.concatenate([cos, cos, jnp.ones((n, rest), F32)], axis=1)
    sin_up = jnp.concatenate([-sin, zeros, jnp.zeros((n, rest), F32)], axis=1)
    sin_dn = jnp.concatenate([zeros, sin, jnp.zeros((n, rest), F32)], axis=1)
    reps = LANES // HEAD_DIM
    return tuple(jnp.tile(t, (1, reps)) for t in (cos_t, sin_up, sin_dn))


def kernel(x_prompt, x_sample, state_conv, cache_k_win, cache_v_win, c_prompt, c_sample, norm_g, w_ada, b_ada, w_ffn_gate, w_ffn_up, w_ffn_down, conv_w_in, conv_w, conv_w_out, kv_norm_g, w_ada_kv, b_ada_kv, w_k, w_v, attn_w_q, attn_sinks, attn_w_o, final_norm_g):
    batch, seq, d = x_prompt.shape
    dec_batch, dec_seq, _ = x_sample.shape
    w_buf = cache_k_win.shape[1]
    past_len = 16384
    assert d == D_MODEL and w_buf == WINDOW and dec_seq == SUBLANES
    assert w_ada.shape[0] == 2 and conv_w_in.shape[0] == 1 and attn_w_q.shape[0] == 1

    pad = (-(dec_batch + batch)) % SUBLANES
    c_all = jnp.concatenate([c_sample, c_prompt, jnp.zeros((pad, d), F32)], axis=0)
    mods = _ada(c_all, w_ada, b_ada)
    mods_kv = _ada(c_all, w_ada_kv[None], b_ada_kv[None])
    prompt_mod_block = dec_batch // SUBLANES

    groups = {
        "prompt": _Group(batch, seq, None, prompt_mod_block),
        "sample": _Group(dec_batch, dec_seq, dec_seq, prompt_mod_block),
    }
    xs = {"prompt": x_prompt.reshape(batch * seq, d), "sample": x_sample.reshape(dec_batch * dec_seq, d)}

    norm_g3 = norm_g.reshape(-1, 1, d)
    wg, wu, wd = (w.astype(BF16) for w in (w_ffn_gate, w_ffn_up, w_ffn_down))
    w_in, w_out = conv_w_in[0].astype(BF16), conv_w_out[0].astype(BF16)
    w_q, w_o = attn_w_q[0].astype(BF16), attn_w_o[0].astype(BF16)
    sinks = attn_sinks[0]

    pos_dec = past_len + jnp.arange(dec_seq, dtype=jnp.int32)
    tab_prompt = _rope_tables(jnp.arange(seq, dtype=jnp.int32))
    tab_dec = tuple(jnp.tile(t, (ROW_TILE // dec_seq, 1)) for t in _rope_tables(pos_dec))
    tab_dec_t = tuple(jnp.tile(t.T, (1, ROW_TILE // dec_seq)) for t in _rope_angles(pos_dec))
    kv_common = dict(norm_g=kv_norm_g.reshape(1, d), mods=mods_kv)
    kv_args = {
        "prompt": dict(kv_common, transposed=False, tables=tab_prompt,
                       w_k=w_k.astype(BF16), w_v=w_v.astype(BF16)),
        "sample": dict(kv_common, transposed=True, tables=tab_dec_t,
                       w_k=w_k.T.astype(BF16), w_v=w_v.T.astype(BF16)),
    }
    to_rows = lambda c: jnp.transpose(c, (0, 2, 3, 1)).reshape(dec_batch * KV_DIM, w_buf)
    from_rows = lambda c: jnp.transpose(c.reshape(dec_batch, N_KV_HEADS, HEAD_DIM, w_buf), (0, 3, 1, 2))

    out = {}
    for name, g in groups.items():
        x = xs[name]
        (x,) = _ffn(x, g, mods, 0, 0, norm_g3, wg, wu, wd, 0)
        state = None if name == "prompt" else (state_conv[0, :, 0, :], state_conv[0, :, 1, :])
        x, u = _conv_mixer(x, g, mods, norm_g3, w_in, conv_w[0], w_out, state)
        (x,) = _ffn(x, g, mods, 0, 2, norm_g3, wg, wu, wd, 1)
        x, k, v = _ffn(x, g, mods, 1, 0, norm_g3, wg, wu, wd, 0, kv=kv_args[name])
        if name == "prompt":
            x = _attn_prompt(x, g, mods, norm_g3, w_q, w_o, k, v, tab_prompt, sinks)
            last = lambda a: a.reshape(batch, seq, KV_DIM)[:, seq - WINDOW:].reshape(
                batch, WINDOW, N_KV_HEADS, HEAD_DIM)
            k, v = last(k), last(v)
        else:
            tabs = tuple(t[:SAMPLE_ATTN_SEQS * dec_seq] for t in tab_dec)
            x, k, v = _attn_sample(x, dec_batch, dec_seq, mods, norm_g3, w_q, w_o,
                                   to_rows(cache_k_win), to_rows(cache_v_win), k, v, tabs, sinks)
            k, v = from_rows(k), from_rows(v)
        (y,) = _ffn(x, g, mods, 1, 2, norm_g3, wg, wu, wd, 1, final_g=final_norm_g.reshape(1, d))
        out[name] = (y, u, k, v)

    y_p, u_p, k_state_p, v_state_p = out["prompt"]
    y_s, u_s, k_state_s, v_state_s = out["sample"]
    tail = CONV_WIDTH - 1
    conv_p = u_p.reshape(batch, SUBLANES, d)[:, SUBLANES - tail:][None]
    conv_s = u_s.reshape(dec_batch, dec_seq, d)[:, dec_seq - tail:][None]
    return (y_p.reshape(batch, seq, d), y_s.reshape(dec_batch, dec_seq, d), conv_p, conv_s,
            k_state_p, v_state_p, k_state_s, v_state_s)
```

```python
import functools

import jax
import jax.numpy as jnp
from jax import lax
from jax.experimental import pallas as pl
from jax.experimental.pallas import tpu as pltpu

F32 = jnp.float32
BF16 = jnp.bfloat16

D_MODEL = 1024
D_FF = 2816
HEAD_DIM = 64
N_HEADS = 16
N_KV_HEADS = 4
GROUP = N_HEADS // N_KV_HEADS
KV_DIM = N_KV_HEADS * HEAD_DIM
WINDOW = 128
ROT_DIM = 16
ROPE_THETA = 500000.0
CONV_WIDTH = 3
N_MOD = 9
EPS = 1e-6
NEG = -1e30
FMAX = float(jnp.finfo(jnp.float32).max)

SUBLANES = 8
LANES = 128
VMEM_LIMIT_BYTES = 56 * 1024 * 1024

ROW_TILE = 512
SAMPLE_ATTN_SEQS = 16
FF_CHUNKS = ((0, 1536), (1536, D_FF))


def _dot(a, b):
    return jnp.dot(a, b, preferred_element_type=F32)


def _dot_nt(a, b):
    return lax.dot_general(a, b, (((1,), (1,)), ((), ())), preferred_element_type=F32)


def _log2(n):
    assert n & (n - 1) == 0, n
    return n.bit_length() - 1


def _silu(x):
    return x * jax.nn.sigmoid(x)


def _unit_rms(x):
    return x * lax.rsqrt(jnp.mean(x * x, axis=-1, keepdims=True) + EPS)


def _repeat_rows(ref, reps):
    n = ref.shape[1]
    return jnp.concatenate(
        [jnp.broadcast_to(ref[i:i + 1, :], (reps, n)) for i in range(ref.shape[0])], axis=0)


def _mod_rows(ref, per_seq_rows):
    if per_seq_rows is None:
        return ref[pl.ds(pl.program_id(0), 1), :]
    return _repeat_rows(ref, per_seq_rows)


def _rope(x, cos_t, sin_up_t, sin_dn_t):
    n = x.shape[1]
    reps = n // LANES
    half = ROT_DIM // 2
    cos = jnp.concatenate([cos_t] * reps, axis=1)
    sin_up = jnp.concatenate([sin_up_t] * reps, axis=1)
    sin_dn = jnp.concatenate([sin_dn_t] * reps, axis=1)
    x_up = pltpu.roll(x, n - half, axis=1)
    x_dn = pltpu.roll(x, half, axis=1)
    return x * cos + x_up * sin_up + x_dn * sin_dn


def _ada_kernel(c_ref, w_ref, b_ref, o_ref):
    a = _silu(c_ref[...]).astype(BF16)
    o_ref[...] = _dot(a, w_ref[...].astype(BF16)) + b_ref[...]


def _ada(c_all, w, b):
    n_layers, _, n = w.shape
    m = c_all.shape[0]
    tn = D_MODEL
    return pl.pallas_call(
        _ada_kernel,
        grid=(n_layers, n // tn),
        in_specs=[
            pl.BlockSpec((m, D_MODEL), lambda l, j: (0, 0)),
            pl.BlockSpec((None, D_MODEL, tn), lambda l, j: (l, 0, j)),
            pl.BlockSpec((None, 1, tn), lambda l, j: (l, 0, j)),
        ],
        out_specs=pl.BlockSpec((None, m, tn), lambda l, j: (l, 0, j)),
        out_shape=jax.ShapeDtypeStruct((n_layers, m, n), F32),
        compiler_params=pltpu.CompilerParams(
            dimension_semantics=("arbitrary", "arbitrary"),
            vmem_limit_bytes=VMEM_LIMIT_BYTES),
        name="adaln_tables",
    )(c_all, w, b.reshape(n_layers, 1, n))


def _rope_t(xt, cos_t, sin_t):
    half = ROT_DIM // 2
    out = []
    for h in range(xt.shape[0] // HEAD_DIM):
        base = h * HEAD_DIM
        x1 = xt[base:base + half, :]
        x2 = xt[base + half:base + ROT_DIM, :]
        out += [x1 * cos_t - x2 * sin_t, x2 * cos_t + x1 * sin_t,
                xt[base + ROT_DIM:base + HEAD_DIM, :]]
    return jnp.concatenate(out, axis=0)


def _ffn_kernel(*refs, per_seq_rows, with_kv, kv_transposed, final):
    it = iter(refs)
    x_ref, sh_ref, sc_ref, gt_ref, ng_ref, wg_ref, wu_ref, wd_ref = (next(it) for _ in range(8))
    if with_kv:
        kvg_ref, kvsh_ref, kvsc_ref, wk_ref, wv_ref, cos_ref, sup_ref = (next(it) for _ in range(7))
        if not kv_transposed:
            sdn_ref = next(it)
    if final:
        fg_ref = next(it)
    o_ref = next(it)
    if with_kv:
        k_ref, v_ref = next(it), next(it)

    x = x_ref[...]
    xn = _unit_rms(x)
    h = (xn * ng_ref[...]) * (1.0 + _mod_rows(sc_ref, per_seq_rows)) + _mod_rows(sh_ref, per_seq_rows)
    hb = h.astype(BF16)
    y = None
    for lo, hi in FF_CHUNKS:
        g = _dot(hb, wg_ref[:, lo:hi])
        u = _dot(hb, wu_ref[:, lo:hi])
        a = (_silu(g) * u).astype(BF16)
        part = _dot(a, wd_ref[lo:hi, :])
        y = part if y is None else y + part
    x_new = x + (0.5 * _mod_rows(gt_ref, per_seq_rows)) * y

    if with_kv:
        hk = ((xn * kvg_ref[...]) * (1.0 + _mod_rows(kvsc_ref, per_seq_rows))
              + _mod_rows(kvsh_ref, per_seq_rows)).astype(BF16)
        if kv_transposed:
            k_ref[...] = _rope_t(_dot_nt(wk_ref[...], hk), cos_ref[...], sup_ref[...])
        else:
            k_ref[...] = _rope(_dot(hk, wk_ref[...]), cos_ref[...], sup_ref[...], sdn_ref[...])
        v_ref[...] = _dot_nt(wv_ref[...], hk)

    if final:
        o_ref[...] = _unit_rms(x_new) * fg_ref[...]
    else:
        o_ref[...] = x_new


def _resident(shape, index_map):
    return pl.BlockSpec(shape, index_map, pipeline_mode=pl.Buffered(1))


def _mod_spec(layer, col, group):
    if group.per_seq_rows is None:
        return pl.BlockSpec((None, SUBLANES, D_MODEL),
                            lambda b, t: (layer, group.prompt_mod_block, col))
    return pl.BlockSpec((None, group.tile // group.per_seq_rows, D_MODEL),
                        lambda b, t: (layer, t, col))


class _Group:
    def __init__(self, n_seq, seq_len, per_seq_rows, prompt_mod_block):
        self.tile = ROW_TILE
        self.rows = n_seq * seq_len
        self.per_seq_rows = per_seq_rows
        self.prompt_mod_block = prompt_mod_block
        if per_seq_rows is None:
            self.grid = (n_seq, seq_len // self.tile)
        else:
            self.grid = (1, self.rows // self.tile)
        self.tiles_per_seq = self.grid[1]

    def row_spec(self, width):
        nt = self.grid[1]
        return pl.BlockSpec((self.tile, width), lambda b, t: (b * nt + t, 0))

    def col_spec(self, height):
        nt = self.grid[1]
        return pl.BlockSpec((height, self.tile), lambda b, t: (0, b * nt + t))

    def table_t_spec(self):
        if self.per_seq_rows is None:
            return pl.BlockSpec((ROT_DIM // 2, self.tile), lambda b, t: (0, t))
        return pl.BlockSpec((ROT_DIM // 2, self.tile), lambda b, t: (0, 0))

    def table_spec(self):
        if self.per_seq_rows is None:
            return pl.BlockSpec((self.tile, LANES), lambda b, t: (t, 0))
        return pl.BlockSpec((self.tile, LANES), lambda b, t: (0, 0))


def _ffn(x, group, mods, layer, sub, norm_g, wg, wu, wd, slot, kv=None, final_g=None):
    with_kv = kv is not None
    final = final_g is not None
    vec = lambda i: _resident((None, 1, D_MODEL), lambda b, t: (i, 0, 0))
    in_specs = [
        group.row_spec(D_MODEL),
        _mod_spec(layer, 3 * sub + 0, group),
        _mod_spec(layer, 3 * sub + 1, group),
        _mod_spec(layer, 3 * sub + 2, group),
        vec(layer * 3 + sub),
        _resident((None, None, D_MODEL, D_FF), lambda b, t: (layer, slot, 0, 0)),
        _resident((None, None, D_MODEL, D_FF), lambda b, t: (layer, slot, 0, 0)),
        _resident((None, None, D_FF, D_MODEL), lambda b, t: (layer, slot, 0, 0)),
    ]
    args = [x, mods, mods, mods, norm_g, wg, wu, wd]
    out_specs = [group.row_spec(D_MODEL)]
    out_shape = [jax.ShapeDtypeStruct((group.rows, D_MODEL), F32)]
    kv_transposed = with_kv and kv["transposed"]
    if with_kv:
        wk_shape = (KV_DIM, D_MODEL) if kv_transposed else (D_MODEL, KV_DIM)
        in_specs += [
            _resident((1, D_MODEL), lambda b, t: (0, 0)),
            _mod_spec(0, 0, group),
            _mod_spec(0, 1, group),
            _resident(wk_shape, lambda b, t: (0, 0)),
            _resident((KV_DIM, D_MODEL), lambda b, t: (0, 0)),
        ]
        args += [kv["norm_g"], kv["mods"], kv["mods"], kv["w_k"], kv["w_v"], *kv["tables"]]
        t_shape = jax.ShapeDtypeStruct((KV_DIM, group.rows), F32)
        if kv_transposed:
            in_specs += [group.table_t_spec()] * 2
            out_specs += [group.col_spec(KV_DIM)] * 2
            out_shape += [t_shape] * 2
        else:
            in_specs += [group.table_spec()] * 3
            out_specs += [group.row_spec(KV_DIM), group.col_spec(KV_DIM)]
            out_shape += [jax.ShapeDtypeStruct((group.rows, KV_DIM), F32), t_shape]
    if final:
        in_specs.append(_resident((1, D_MODEL), lambda b, t: (0, 0)))
        args.append(final_g)
    return pl.pallas_call(
        functools.partial(_ffn_kernel, per_seq_rows=group.per_seq_rows,
                          with_kv=with_kv, kv_transposed=kv_transposed, final=final),
        grid=group.grid,
        in_specs=in_specs,
        out_specs=out_specs,
        out_shape=out_shape,
        compiler_params=pltpu.CompilerParams(
            dimension_semantics=("arbitrary", "arbitrary"),
            vmem_limit_bytes=VMEM_LIMIT_BYTES),
        name=f"ffn_l{layer}_s{sub}",
    )(*args)


def _conv_kernel(*refs, per_seq_rows):
    it = iter(refs)
    x_ref, sh_ref, sc_ref, gt_ref, ng_ref, win_ref, cw_ref, wout_ref = (next(it) for _ in range(8))
    if per_seq_rows is not None:
        st0_ref, st1_ref = next(it), next(it)
    o_ref, u_ref = next(it), next(it)
    if per_seq_rows is None:
        carry_ref = next(it)

    tile = x_ref.shape[0]
    x = x_ref[...]
    h = ((_unit_rms(x) * ng_ref[...]) * (1.0 + _mod_rows(sc_ref, per_seq_rows))
         + _mod_rows(sh_ref, per_seq_rows)).astype(BF16)
    bcv = _dot(h, win_ref[...])
    b_g = bcv[:, :D_MODEL]
    u = bcv[:, D_MODEL:2 * D_MODEL] * bcv[:, 2 * D_MODEL:]

    row = lax.broadcasted_iota(jnp.int32, (tile, 1), 0)
    if per_seq_rows is None:
        @pl.when(pl.program_id(1) == 0)
        def _():
            carry_ref[...] = jnp.zeros_like(carry_ref)
        prev2 = carry_ref[SUBLANES - 2:SUBLANES - 1, :]
        prev1 = carry_ref[SUBLANES - 1:SUBLANES, :]
        t = row
    else:
        prev2 = _repeat_rows(st0_ref, per_seq_rows)
        prev1 = _repeat_rows(st1_ref, per_seq_rows)
        t = row & (per_seq_rows - 1)
    u1 = jnp.where(t == 0, prev1, pltpu.roll(u, 1, axis=0))
    u2 = jnp.where(t == 0, prev2, jnp.where(t == 1, prev1, pltpu.roll(u, 2, axis=0)))
    conv = u2 * cw_ref[0:1, :] + u1 * cw_ref[1:2, :] + u * cw_ref[2:3, :]
    y = _dot((b_g * conv).astype(BF16), wout_ref[...])
    o_ref[...] = x + _mod_rows(gt_ref, per_seq_rows) * y

    if per_seq_rows is None:
        tail = u[tile - SUBLANES:, :]
        carry_ref[...] = tail
        u_ref[...] = tail
    else:
        u_ref[...] = u


def _conv_mixer(x, group, mods, norm_g, w_in, conv_w, w_out, state=None):
    prompt = group.per_seq_rows is None
    in_specs = [
        group.row_spec(D_MODEL),
        _mod_spec(0, 3, group), _mod_spec(0, 4, group), _mod_spec(0, 5, group),
        _resident((None, 1, D_MODEL), lambda b, t: (1, 0, 0)),
        _resident((D_MODEL, 3 * D_MODEL), lambda b, t: (0, 0)),
        _resident((CONV_WIDTH, D_MODEL), lambda b, t: (0, 0)),
        _resident((D_MODEL, D_MODEL), lambda b, t: (0, 0)),
    ]
    args = [x, mods, mods, mods, norm_g, w_in, conv_w, w_out]
    scratch = []
    if prompt:
        u_spec = pl.BlockSpec((SUBLANES, D_MODEL), lambda b, t: (b, 0))
        u_shape = jax.ShapeDtypeStruct((group.grid[0] * SUBLANES, D_MODEL), F32)
        scratch.append(pltpu.VMEM((SUBLANES, D_MODEL), F32))
    else:
        n_seq = group.tile // group.per_seq_rows
        st_spec = pl.BlockSpec((n_seq, D_MODEL), lambda b, t: (t, 0))
        in_specs += [st_spec, st_spec]
        args += [state[0], state[1]]
        u_spec = group.row_spec(D_MODEL)
        u_shape = jax.ShapeDtypeStruct((group.rows, D_MODEL), F32)
    return pl.pallas_call(
        functools.partial(_conv_kernel, per_seq_rows=group.per_seq_rows),
        grid=group.grid,
        in_specs=in_specs,
        out_specs=[group.row_spec(D_MODEL), u_spec],
        out_shape=[jax.ShapeDtypeStruct((group.rows, D_MODEL), F32), u_shape],
        scratch_shapes=scratch,
        compiler_params=pltpu.CompilerParams(
            dimension_semantics=("arbitrary", "arbitrary"),
            vmem_limit_bytes=VMEM_LIMIT_BYTES),
        name="conv_mixer",
    )(*args)


def _sink_column(sink_ref, kv_head, q_rows):
    grp = lax.broadcasted_iota(jnp.int32, (GROUP * q_rows, 1), 0) >> _log2(q_rows)
    col = jnp.full((GROUP * q_rows, 1), sink_ref[kv_head * GROUP + GROUP - 1], F32)
    for g in range(GROUP - 2, -1, -1):
        col = jnp.where(grp == g, sink_ref[kv_head * GROUP + g], col)
    return col


def _stack_heads(q_ref, rows, kv_head):
    return jnp.concatenate(
        [q_ref[rows, (kv_head * GROUP + g) * HEAD_DIM:(kv_head * GROUP + g + 1) * HEAD_DIM]
         for g in range(GROUP)], axis=0)


def _softmax_parts(scores, sink):
    m = sink
    for s in scores:
        m = jnp.maximum(m, jnp.max(s, axis=-1, keepdims=True))
    ps = [jnp.exp(s - m) for s in scores]
    denom = jnp.exp(sink - m)
    for p in ps:
        denom = denom + jnp.sum(p, axis=-1, keepdims=True)
    return ps, 1.0 / denom


def _project_q(x_ref, sh_ref, sc_ref, ng_ref, wq_ref, tables, per_seq_rows):
    h = ((_unit_rms(x_ref[...]) * ng_ref[...]) * (1.0 + _mod_rows(sc_ref, per_seq_rows))
         + _mod_rows(sh_ref, per_seq_rows)).astype(BF16)
    q = _rope(_dot(h, wq_ref[...]), *tables)
    return (q * (HEAD_DIM ** -0.5)).astype(BF16)


def _band_caps():
    shape = (2 * WINDOW, GROUP * WINDOW)
    kj = lax.broadcasted_iota(jnp.int32, shape, 0)
    qi = lax.broadcasted_iota(jnp.int32, shape, 1) & (WINDOW - 1)
    in_band = (kj > qi) & (kj <= qi + WINDOW)
    return (jnp.where(in_band, FMAX, NEG), jnp.where(in_band & (kj >= WINDOW), FMAX, NEG))


def _attn_prompt_kernel(x_ref, sh_ref, sc_ref, gt_ref, ng_ref, wq_ref, wo_ref,
                        kp_ref, kc_ref, vp_ref, vc_ref, cos_ref, sin_ref, sink_ref,
                        o_ref, q_scr, k_scr, v_scr, a_scr, cap_scr):
    tile = x_ref.shape[0]

    @pl.when((pl.program_id(0) == 0) & (pl.program_id(1) == 0))
    def _():
        general, first = _band_caps()
        cap_scr[0] = general
        cap_scr[1] = first

    h = ((_unit_rms(x_ref[...]) * ng_ref[...]) * (1.0 + _mod_rows(sc_ref, None))
         + _mod_rows(sh_ref, None)).astype(BF16)
    qt = _rope_t(_dot_nt(wq_ref[...], h), cos_ref[...], sin_ref[...])
    q_scr[...] = (qt * (HEAD_DIM ** -0.5)).astype(BF16)
    k_scr[:WINDOW, :] = kp_ref[...].astype(BF16)
    k_scr[WINDOW:, :] = kc_ref[...].astype(BF16)
    v_scr[:, :WINDOW] = vp_ref[...].astype(BF16)
    v_scr[:, WINDOW:] = vc_ref[...].astype(BF16)
    first_tile = pl.program_id(1) == 0

    for blk in range(tile // WINDOW):
        cols = slice(blk * WINDOW, (blk + 1) * WINDOW)
        keys = slice(blk * WINDOW, (blk + 2) * WINDOW)
        cap = cap_scr[jnp.where(first_tile, 1, 0)] if blk == 0 else cap_scr[0]
        for kvh in range(N_KV_HEADS):
            dims = slice(kvh * HEAD_DIM, (kvh + 1) * HEAD_DIM)
            heads = [slice((kvh * GROUP + g) * HEAD_DIM, (kvh * GROUP + g + 1) * HEAD_DIM)
                     for g in range(GROUP)]
            qg = jnp.concatenate([q_scr[hd, cols] for hd in heads], axis=1)
            s = jnp.minimum(_dot(k_scr[keys, dims], qg), cap)
            sink = jnp.concatenate(
                [jnp.full((1, WINDOW), sink_ref[kvh * GROUP + g], F32) for g in range(GROUP)], axis=1)
            m = jnp.maximum(jnp.max(s, axis=0, keepdims=True), sink)
            p = jnp.exp(s - m)
            denom = jnp.sum(p, axis=0, keepdims=True) + jnp.exp(sink - m)
            o = _dot(v_scr[dims, keys], p.astype(BF16)) * (1.0 / denom)
            for g, hd in enumerate(heads):
                a_scr[hd, cols] = o[:, g * WINDOW:(g + 1) * WINDOW]

    y = _dot(a_scr[...].T.astype(BF16), wo_ref[...])
    o_ref[...] = x_ref[...] + _mod_rows(gt_ref, None) * y


def _attn_prompt(x, group, mods, norm_g, w_q_t, w_o, k, v_t, tables_t, sinks):
    tile = group.tile
    nt = group.grid[1]
    blocks_per_tile = tile // WINDOW

    def prev_block(b, t):
        return jnp.maximum((b * nt + t) * blocks_per_tile - 1, 0)

    in_specs = [
        group.row_spec(D_MODEL),
        _mod_spec(1, 3, group), _mod_spec(1, 4, group), _mod_spec(1, 5, group),
        _resident((None, 1, D_MODEL), lambda b, t: (4, 0, 0)),
        _resident((D_MODEL, D_MODEL), lambda b, t: (0, 0)),
        _resident((D_MODEL, D_MODEL), lambda b, t: (0, 0)),
        pl.BlockSpec((WINDOW, KV_DIM), lambda b, t: (prev_block(b, t), 0)), group.row_spec(KV_DIM),
        pl.BlockSpec((KV_DIM, WINDOW), lambda b, t: (0, prev_block(b, t))), group.col_spec(KV_DIM),
        group.table_t_spec(), group.table_t_spec(),
        pl.BlockSpec(memory_space=pltpu.SMEM),
    ]
    return pl.pallas_call(
        _attn_prompt_kernel,
        grid=group.grid,
        in_specs=in_specs,
        out_specs=group.row_spec(D_MODEL),
        out_shape=jax.ShapeDtypeStruct((group.rows, D_MODEL), F32),
        scratch_shapes=[
            pltpu.VMEM((D_MODEL, tile), BF16),
            pltpu.VMEM((tile + WINDOW, KV_DIM), BF16),
            pltpu.VMEM((KV_DIM, tile + WINDOW), BF16),
            pltpu.VMEM((D_MODEL, tile), F32),
            pltpu.VMEM((2, 2 * WINDOW, GROUP * WINDOW), F32),
        ],
        compiler_params=pltpu.CompilerParams(
            dimension_semantics=("arbitrary", "arbitrary"),
            vmem_limit_bytes=VMEM_LIMIT_BYTES),
        name="attn_prompt",
    )(x, mods, mods, mods, norm_g, w_q_t, w_o, k, k, v_t, v_t, *tables_t, sinks)


def _attn_sample_kernel(x_ref, sh_ref, sc_ref, gt_ref, ng_ref, wq_ref, wo_ref,
                        kc_ref, vc_ref, kn_ref, vn_ref, cos_ref, sup_ref, sdn_ref, sink_ref,
                        o_ref, kout_ref, vout_ref, q_scr, *, dec_seq):
    rows = x_ref.shape[0]
    n_seq = rows // dec_seq
    n_cache = n_seq * WINDOW
    q_scr[...] = _project_q(x_ref, sh_ref, sc_ref, ng_ref, wq_ref,
                            (cos_ref[...], sup_ref[...], sdn_ref[...]), dec_seq)

    def all_windows(ref, kvh):
        return jnp.concatenate(
            [ref[(s * N_KV_HEADS + kvh) * HEAD_DIM:(s * N_KV_HEADS + kvh + 1) * HEAD_DIM, :].astype(BF16)
             for s in range(n_seq)], axis=1)

    q_row = lax.broadcasted_iota(jnp.int32, (GROUP * rows, 1), 0) & (rows - 1)
    q_seq, q_i = q_row >> _log2(dec_seq), q_row & (dec_seq - 1)
    c_col = lax.broadcasted_iota(jnp.int32, (1, n_cache), 1)
    valid_c = (q_seq == (c_col >> _log2(WINDOW))) & ((c_col & (WINDOW - 1)) > q_i)
    n_col = lax.broadcasted_iota(jnp.int32, (1, rows), 1)
    valid_n = (q_seq == (n_col >> _log2(dec_seq))) & ((n_col & (dec_seq - 1)) <= q_i)

    heads = []
    for kvh in range(N_KV_HEADS):
        dims = slice(kvh * HEAD_DIM, (kvh + 1) * HEAD_DIM)
        qg = _stack_heads(q_scr, slice(None), kvh)
        s_c = jnp.where(valid_c, _dot(qg, all_windows(kc_ref, kvh)), NEG)
        s_n = jnp.where(valid_n, _dot(qg, kn_ref[dims, :].astype(BF16)), NEG)
        (p_c, p_n), inv = _softmax_parts([s_c, s_n], _sink_column(sink_ref, kvh, rows))
        o = (_dot_nt(p_c.astype(BF16), all_windows(vc_ref, kvh))
             + _dot_nt(p_n.astype(BF16), vn_ref[dims, :].astype(BF16))) * inv
        heads += [o[g * rows:(g + 1) * rows, :] for g in range(GROUP)]
    attn = jnp.concatenate(heads, axis=1).astype(BF16)
    o_ref[...] = x_ref[...] + _mod_rows(gt_ref, dec_seq) * _dot(attn, wo_ref[...])

    keep = lax.broadcasted_iota(jnp.int32, (1, WINDOW), 1) < WINDOW - dec_seq
    for src_ref, new_ref, dst_ref in ((kc_ref, kn_ref, kout_ref), (vc_ref, vn_ref, vout_ref)):
        new = new_ref[...]
        for s in range(n_seq):
            blk = slice(s * KV_DIM, (s + 1) * KV_DIM)
            old = pltpu.roll(src_ref[blk, :], WINDOW - dec_seq, axis=1)
            app = pltpu.roll(new, (WINDOW - dec_seq - s * dec_seq) % rows, axis=1)
            dst_ref[blk, :] = jnp.where(keep, old, app)


def _attn_sample(x, n_seq_total, dec_seq, mods, norm_g, w_q, w_o, k_cache, v_cache,
                 k_new, v_new, tables, sinks):
    n_seq = SAMPLE_ATTN_SEQS
    rows = n_seq * dec_seq
    assert rows == WINDOW == LANES
    grid = (n_seq_total // n_seq,)
    row_spec = lambda width: pl.BlockSpec((rows, width), lambda t: (t, 0))
    mod_spec = lambda col: pl.BlockSpec((None, n_seq, D_MODEL), lambda t: (1, t, col))
    cache_spec = pl.BlockSpec((n_seq * KV_DIM, WINDOW), lambda t: (t, 0))
    new_spec = pl.BlockSpec((KV_DIM, rows), lambda t: (0, t))
    table_spec = pl.BlockSpec((rows, LANES), lambda t: (0, 0))
    in_specs = [
        row_spec(D_MODEL), mod_spec(3), mod_spec(4), mod_spec(5),
        _resident((None, 1, D_MODEL), lambda t: (4, 0, 0)),
        _resident((D_MODEL, D_MODEL), lambda t: (0, 0)),
        _resident((D_MODEL, D_MODEL), lambda t: (0, 0)),
        cache_spec, cache_spec, new_spec, new_spec,
        table_spec, table_spec, table_spec,
        pl.BlockSpec(memory_space=pltpu.SMEM),
    ]
    cache_shape = jax.ShapeDtypeStruct((n_seq_total * KV_DIM, WINDOW), F32)
    return pl.pallas_call(
        functools.partial(_attn_sample_kernel, dec_seq=dec_seq),
        grid=grid,
        in_specs=in_specs,
        out_specs=[row_spec(D_MODEL), cache_spec, cache_spec],
        out_shape=[jax.ShapeDtypeStruct((n_seq_total * dec_seq, D_MODEL), F32),
                   cache_shape, cache_shape],
        scratch_shapes=[pltpu.VMEM((rows, D_MODEL), BF16)],
        compiler_params=pltpu.CompilerParams(
            dimension_semantics=("arbitrary",),
            vmem_limit_bytes=VMEM_LIMIT_BYTES),
        name="attn_sample",
    )(x, mods, mods, mods, norm_g, w_q, w_o, k_cache, v_cache, k_new, v_new, *tables, sinks)


def _rope_angles(pos):
    inv_freq = ROPE_THETA ** (-jnp.arange(0, ROT_DIM, 2, dtype=F32) / ROT_DIM)
    ang = pos.astype(F32)[:, None] * inv_freq[None, :]
    return jnp.cos(ang), jnp.sin(ang)


def _rope_tables(pos):
    half = ROT_DIM // 2
    cos, sin = _rope_angles(pos)
    n = pos.shape[0]
    rest = HEAD_DIM - ROT_DIM
    zeros = jnp.zeros((n, half), F32)
    cos_t = jnp.concatenate([cos, cos, jnp.ones((n, rest), F32)], axis=1)
    sin_up = jnp.concatenate([-sin, zeros, jnp.zeros((n, rest), F32)], axis=1)
    sin_dn = jnp.concatenate([zeros, sin, jnp.zeros((n, rest), F32)], axis=1)
    reps = LANES // HEAD_DIM
    return tuple(jnp.tile(t, (1, reps)) for t in (cos_t, sin_up, sin_dn))


def kernel(x_prompt, x_sample, state_conv, cache_k_win, cache_v_win, c_prompt, c_sample, norm_g, w_ada, b_ada, w_ffn_gate, w_ffn_up, w_ffn_down, conv_w_in, conv_w, conv_w_out, kv_norm_g, w_ada_kv, b_ada_kv, w_k, w_v, attn_w_q, attn_sinks, attn_w_o, final_norm_g):
    batch, seq, d = x_prompt.shape
    dec_batch, dec_seq, _ = x_sample.shape
    w_buf = cache_k_win.shape[1]
    past_len = 16384
    assert d == D_MODEL and w_buf == WINDOW and dec_seq == SUBLANES
    assert w_ada.shape[0] == 2 and conv_w_in.shape[0] == 1 and attn_w_q.shape[0] == 1

    pad = (-(dec_batch + batch)) % SUBLANES
    c_all = jnp.concatenate([c_sample, c_prompt, jnp.zeros((pad, d), F32)], axis=0)
    mods = _ada(c_all, w_ada, b_ada)
    mods_kv = _ada(c_all, w_ada_kv[None], b_ada_kv[None])
    prompt_mod_block = dec_batch // SUBLANES

    groups = {
        "prompt": _Group(batch, seq, None, prompt_mod_block),
        "sample": _Group(dec_batch, dec_seq, dec_seq, prompt_mod_block),
    }
    xs = {"prompt": x_prompt.reshape(batch * seq, d), "sample": x_sample.reshape(dec_batch * dec_seq, d)}

    norm_g3 = norm_g.reshape(-1, 1, d)
    wg, wu, wd = (w.astype(BF16) for w in (w_ffn_gate, w_ffn_up, w_ffn_down))
    w_in, w_out = conv_w_in[0].astype(BF16), conv_w_out[0].astype(BF16)
    w_q, w_o = attn_w_q[0].astype(BF16), attn_w_o[0].astype(BF16)
    sinks = attn_sinks[0]

    pos_dec = past_len + jnp.arange(dec_seq, dtype=jnp.int32)
    pos_prompt = jnp.arange(seq, dtype=jnp.int32)
    tab_prompt = _rope_tables(pos_prompt)
    tab_dec = tuple(jnp.tile(t, (ROW_TILE // dec_seq, 1)) for t in _rope_tables(pos_dec))
    tab_prompt_t = tuple(t.T for t in _rope_angles(pos_prompt))
    tab_dec_t = tuple(jnp.tile(t.T, (1, ROW_TILE // dec_seq)) for t in _rope_angles(pos_dec))
    kv_common = dict(norm_g=kv_norm_g.reshape(1, d), mods=mods_kv, w_v=w_v.T.astype(BF16))
    kv_args = {
        "prompt": dict(kv_common, transposed=False, tables=tab_prompt, w_k=w_k.astype(BF16)),
        "sample": dict(kv_common, transposed=True, tables=tab_dec_t, w_k=w_k.T.astype(BF16)),
    }
    w_q_t = attn_w_q[0].T.astype(BF16)
    to_rows = lambda c: jnp.transpose(c, (0, 2, 3, 1)).reshape(dec_batch * KV_DIM, w_buf)
    from_rows = lambda c: jnp.transpose(c.reshape(dec_batch, N_KV_HEADS, HEAD_DIM, w_buf), (0, 3, 1, 2))

    out = {}
    for name, g in groups.items():
        x = xs[name]
        (x,) = _ffn(x, g, mods, 0, 0, norm_g3, wg, wu, wd, 0)
        state = None if name == "prompt" else (state_conv[0, :, 0, :], state_conv[0, :, 1, :])
        x, u = _conv_mixer(x, g, mods, norm_g3, w_in, conv_w[0], w_out, state)
        (x,) = _ffn(x, g, mods, 0, 2, norm_g3, wg, wu, wd, 1)
        x, k, v = _ffn(x, g, mods, 1, 0, norm_g3, wg, wu, wd, 0, kv=kv_args[name])
        if name == "prompt":
            x = _attn_prompt(x, g, mods, norm_g3, w_q_t, w_o, k, v, tab_prompt_t, sinks)
            k = k.reshape(batch, seq, KV_DIM)[:, seq - WINDOW:].reshape(
                batch, WINDOW, N_KV_HEADS, HEAD_DIM)
            v = jnp.transpose(
                v.reshape(N_KV_HEADS, HEAD_DIM, batch, seq)[..., seq - WINDOW:], (2, 3, 0, 1))
        else:
            tabs = tuple(t[:SAMPLE_ATTN_SEQS * dec_seq] for t in tab_dec)
            x, k, v = _attn_sample(x, dec_batch, dec_seq, mods, norm_g3, w_q, w_o,
                                   to_rows(cache_k_win), to_rows(cache_v_win), k, v, tabs, sinks)
            k, v = from_rows(k), from_rows(v)
        (y,) = _ffn(x, g, mods, 1, 2, norm_g3, wg, wu, wd, 1, final_g=final_norm_g.reshape(1, d))
        out[name] = (y, u, k, v)

    y_p, u_p, k_state_p, v_state_p = out["prompt"]
    y_s, u_s, k_state_s, v_state_s = out["sample"]
    tail = CONV_WIDTH - 1
    conv_p = u_p.reshape(batch, SUBLANES, d)[:, SUBLANES - tail:][None]
    conv_s = u_s.reshape(dec_batch, dec_seq, d)[:, dec_seq - tail:][None]
    return (y_p.reshape(batch, seq, d), y_s.reshape(dec_batch, dec_seq, d), conv_p, conv_s,
            k_state_p, v_state_p, k_state_s, v_state_s)
```

```python
import functools

import jax
import jax.numpy as jnp
from jax import lax
from jax.experimental import pallas as pl
from jax.experimental.pallas import tpu as pltpu

F32 = jnp.float32
BF16 = jnp.bfloat16

D_MODEL = 1024
D_FF = 2816
HEAD_DIM = 64
N_HEADS = 16
N_KV_HEADS = 4
GROUP = N_HEADS // N_KV_HEADS
KV_DIM = N_KV_HEADS * HEAD_DIM
WINDOW = 128
ROT_DIM = 16
ROPE_THETA = 500000.0
CONV_WIDTH = 3
N_MOD = 9
EPS = 1e-6
NEG = -1e30
FMAX = float(jnp.finfo(jnp.float32).max)

SUBLANES = 8
LANES = 128
VMEM_LIMIT_BYTES = 56 * 1024 * 1024

FFN_TILE = 1024
MIXER_TILE = 512
SAMPLE_ATTN_SEQS = 16
FF_CHUNKS = ((0, 1024), (1024, 2048), (2048, D_FF))


def _dot(a, b):
    return jnp.dot(a, b, preferred_element_type=F32)


def _dot_nt(a, b):
    return lax.dot_general(a, b, (((1,), (1,)), ((), ())), preferred_element_type=F32)


def _log2(n):
    assert n & (n - 1) == 0, n
    return n.bit_length() - 1


def _silu(x):
    return x * jax.nn.sigmoid(x)


def _unit_rms(x):
    return x * lax.rsqrt(jnp.mean(x * x, axis=-1, keepdims=True) + EPS)


def _repeat_rows(ref, reps):
    n = ref.shape[1]
    return jnp.concatenate(
        [jnp.broadcast_to(ref[i:i + 1, :], (reps, n)) for i in range(ref.shape[0])], axis=0)


def _mod_rows(ref, per_seq_rows):
    if per_seq_rows is None:
        return ref[pl.ds(pl.program_id(0), 1), :]
    return _repeat_rows(ref, per_seq_rows)


def _rope(x, cos_t, sin_up_t, sin_dn_t):
    n = x.shape[1]
    reps = n // LANES
    half = ROT_DIM // 2
    cos = jnp.concatenate([cos_t] * reps, axis=1)
    sin_up = jnp.concatenate([sin_up_t] * reps, axis=1)
    sin_dn = jnp.concatenate([sin_dn_t] * reps, axis=1)
    x_up = pltpu.roll(x, n - half, axis=1)
    x_dn = pltpu.roll(x, half, axis=1)
    return x * cos + x_up * sin_up + x_dn * sin_dn


def _ada_kernel(c_ref, w_ref, b_ref, o_ref):
    a = _silu(c_ref[...]).astype(BF16)
    o_ref[...] = _dot(a, w_ref[...].astype(BF16)) + b_ref[...]


def _ada(c_all, w, b):
    n_layers, _, n = w.shape
    m = c_all.shape[0]
    tn = D_MODEL
    return pl.pallas_call(
        _ada_kernel,
        grid=(n_layers, n // tn),
        in_specs=[
            pl.BlockSpec((m, D_MODEL), lambda l, j: (0, 0)),
            pl.BlockSpec((None, D_MODEL, tn), lambda l, j: (l, 0, j)),
            pl.BlockSpec((None, 1, tn), lambda l, j: (l, 0, j)),
        ],
        out_specs=pl.BlockSpec((None, m, tn), lambda l, j: (l, 0, j)),
        out_shape=jax.ShapeDtypeStruct((n_layers, m, n), F32),
        compiler_params=pltpu.CompilerParams(
            dimension_semantics=("arbitrary", "arbitrary"),
            vmem_limit_bytes=VMEM_LIMIT_BYTES),
        name="adaln_tables",
    )(c_all, w, b.reshape(n_layers, 1, n))


def _rope_t(xt, cos_t, sin_t):
    half = ROT_DIM // 2
    out = []
    for h in range(xt.shape[0] // HEAD_DIM):
        base = h * HEAD_DIM
        x1 = xt[base:base + half, :]
        x2 = xt[base + half:base + ROT_DIM, :]
        out += [x1 * cos_t - x2 * sin_t, x2 * cos_t + x1 * sin_t,
                xt[base + ROT_DIM:base + HEAD_DIM, :]]
    return jnp.concatenate(out, axis=0)


def _ffn_kernel(*refs, per_seq_rows, with_kv, kv_transposed, final):
    it = iter(refs)
    x_ref, sh_ref, sc_ref, gt_ref, ng_ref, wg_ref, wu_ref, wd_ref = (next(it) for _ in range(8))
    if with_kv:
        kvg_ref, kvsh_ref, kvsc_ref, wk_ref, wv_ref, cos_ref, sup_ref = (next(it) for _ in range(7))
        if not kv_transposed:
            sdn_ref = next(it)
    if final:
        fg_ref = next(it)
    o_ref = next(it)
    if with_kv:
        k_ref, v_ref = next(it), next(it)

    x = x_ref[...]
    xn = _unit_rms(x)
    h = (xn * ng_ref[...]) * (1.0 + _mod_rows(sc_ref, per_seq_rows)) + _mod_rows(sh_ref, per_seq_rows)
    hb = h.astype(BF16)
    y = None
    for lo, hi in FF_CHUNKS:
        g = _dot(hb, wg_ref[:, lo:hi])
        u = _dot(hb, wu_ref[:, lo:hi])
        a = (_silu(g) * u).astype(BF16)
        part = _dot(a, wd_ref[lo:hi, :])
        y = part if y is None else y + part
    x_new = x + (0.5 * _mod_rows(gt_ref, per_seq_rows)) * y

    if with_kv:
        hk = ((xn * kvg_ref[...]) * (1.0 + _mod_rows(kvsc_ref, per_seq_rows))
              + _mod_rows(kvsh_ref, per_seq_rows)).astype(BF16)
        if kv_transposed:
            k_ref[...] = _rope_t(_dot_nt(wk_ref[...], hk), cos_ref[...], sup_ref[...])
        else:
            k_ref[...] = _rope(_dot(hk, wk_ref[...]), cos_ref[...], sup_ref[...], sdn_ref[...])
        v_ref[...] = _dot_nt(wv_ref[...], hk)

    if final:
        o_ref[...] = _unit_rms(x_new) * fg_ref[...]
    else:
        o_ref[...] = x_new


def _resident(shape, index_map):
    return pl.BlockSpec(shape, index_map, pipeline_mode=pl.Buffered(1))


def _mod_spec(layer, col, group):
    if group.per_seq_rows is None:
        return pl.BlockSpec((None, SUBLANES, D_MODEL),
                            lambda b, t: (layer, group.prompt_mod_block, col))
    return pl.BlockSpec((None, group.tile // group.per_seq_rows, D_MODEL),
                        lambda b, t: (layer, t, col))


class _Group:
    def __init__(self, n_seq, seq_len, per_seq_rows, prompt_mod_block, tile):
        self.tile = tile
        self.rows = n_seq * seq_len
        self.per_seq_rows = per_seq_rows
        self.prompt_mod_block = prompt_mod_block
        if per_seq_rows is None:
            self.grid = (n_seq, seq_len // self.tile)
        else:
            self.grid = (1, self.rows // self.tile)
        self.tiles_per_seq = self.grid[1]

    def row_spec(self, width):
        nt = self.grid[1]
        return pl.BlockSpec((self.tile, width), lambda b, t: (b * nt + t, 0))

    def col_spec(self, height):
        nt = self.grid[1]
        return pl.BlockSpec((height, self.tile), lambda b, t: (0, b * nt + t))

    def table_t_spec(self):
        if self.per_seq_rows is None:
            return pl.BlockSpec((ROT_DIM // 2, self.tile), lambda b, t: (0, t))
        return pl.BlockSpec((ROT_DIM // 2, self.tile), lambda b, t: (0, 0))

    def table_spec(self):
        if self.per_seq_rows is None:
            return pl.BlockSpec((self.tile, LANES), lambda b, t: (t, 0))
        return pl.BlockSpec((self.tile, LANES), lambda b, t: (0, 0))


def _ffn(x, group, mods, layer, sub, norm_g, wg, wu, wd, slot, kv=None, final_g=None):
    with_kv = kv is not None
    final = final_g is not None
    vec = lambda i: _resident((None, 1, D_MODEL), lambda b, t: (i, 0, 0))
    in_specs = [
        group.row_spec(D_MODEL),
        _mod_spec(layer, 3 * sub + 0, group),
        _mod_spec(layer, 3 * sub + 1, group),
        _mod_spec(layer, 3 * sub + 2, group),
        vec(layer * 3 + sub),
        _resident((None, None, D_MODEL, D_FF), lambda b, t: (layer, slot, 0, 0)),
        _resident((None, None, D_MODEL, D_FF), lambda b, t: (layer, slot, 0, 0)),
        _resident((None, None, D_FF, D_MODEL), lambda b, t: (layer, slot, 0, 0)),
    ]
    args = [x, mods, mods, mods, norm_g, wg, wu, wd]
    out_specs = [group.row_spec(D_MODEL)]
    out_shape = [jax.ShapeDtypeStruct((group.rows, D_MODEL), F32)]
    kv_transposed = with_kv and kv["transposed"]
    if with_kv:
        wk_shape = (KV_DIM, D_MODEL) if kv_transposed else (D_MODEL, KV_DIM)
        in_specs += [
            _resident((1, D_MODEL), lambda b, t: (0, 0)),
            _mod_spec(0, 0, group),
            _mod_spec(0, 1, group),
            _resident(wk_shape, lambda b, t: (0, 0)),
            _resident((KV_DIM, D_MODEL), lambda b, t: (0, 0)),
        ]
        args += [kv["norm_g"], kv["mods"], kv["mods"], kv["w_k"], kv["w_v"], *kv["tables"]]
        t_shape = jax.ShapeDtypeStruct((KV_DIM, group.rows), F32)
        if kv_transposed:
            in_specs += [group.table_t_spec()] * 2
            out_specs += [group.col_spec(KV_DIM)] * 2
            out_shape += [t_shape] * 2
        else:
            in_specs += [group.table_spec()] * 3
            out_specs += [group.row_spec(KV_DIM), group.col_spec(KV_DIM)]
            out_shape += [jax.ShapeDtypeStruct((group.rows, KV_DIM), F32), t_shape]
    if final:
        in_specs.append(_resident((1, D_MODEL), lambda b, t: (0, 0)))
        args.append(final_g)
    return pl.pallas_call(
        functools.partial(_ffn_kernel, per_seq_rows=group.per_seq_rows,
                          with_kv=with_kv, kv_transposed=kv_transposed, final=final),
        grid=group.grid,
        in_specs=in_specs,
        out_specs=out_specs,
        out_shape=out_shape,
        compiler_params=pltpu.CompilerParams(
            dimension_semantics=("arbitrary", "arbitrary"),
            vmem_limit_bytes=VMEM_LIMIT_BYTES),
        name=f"ffn_l{layer}_s{sub}",
    )(*args)


def _conv_kernel(*refs, per_seq_rows):
    it = iter(refs)
    x_ref, sh_ref, sc_ref, gt_ref, ng_ref, win_ref, cw_ref, wout_ref = (next(it) for _ in range(8))
    if per_seq_rows is not None:
        st0_ref, st1_ref = next(it), next(it)
    o_ref, u_ref = next(it), next(it)
    if per_seq_rows is None:
        carry_ref = next(it)

    tile = x_ref.shape[0]
    x = x_ref[...]
    h = ((_unit_rms(x) * ng_ref[...]) * (1.0 + _mod_rows(sc_ref, per_seq_rows))
         + _mod_rows(sh_ref, per_seq_rows)).astype(BF16)
    bcv = _dot(h, win_ref[...])
    b_g = bcv[:, :D_MODEL]
    u = bcv[:, D_MODEL:2 * D_MODEL] * bcv[:, 2 * D_MODEL:]

    row = lax.broadcasted_iota(jnp.int32, (tile, 1), 0)
    if per_seq_rows is None:
        @pl.when(pl.program_id(1) == 0)
        def _():
            carry_ref[...] = jnp.zeros_like(carry_ref)
        prev2 = carry_ref[SUBLANES - 2:SUBLANES - 1, :]
        prev1 = carry_ref[SUBLANES - 1:SUBLANES, :]
        t = row
    else:
        prev2 = _repeat_rows(st0_ref, per_seq_rows)
        prev1 = _repeat_rows(st1_ref, per_seq_rows)
        t = row & (per_seq_rows - 1)
    u1 = jnp.where(t == 0, prev1, pltpu.roll(u, 1, axis=0))
    u2 = jnp.where(t == 0, prev2, jnp.where(t == 1, prev1, pltpu.roll(u, 2, axis=0)))
    conv = u2 * cw_ref[0:1, :] + u1 * cw_ref[1:2, :] + u * cw_ref[2:3, :]
    y = _dot((b_g * conv).astype(BF16), wout_ref[...])
    o_ref[...] = x + _mod_rows(gt_ref, per_seq_rows) * y

    if per_seq_rows is None:
        tail = u[tile - SUBLANES:, :]
        carry_ref[...] = tail
        u_ref[...] = tail
    else:
        u_ref[...] = u


def _conv_mixer(x, group, mods, norm_g, w_in, conv_w, w_out, state=None):
    prompt = group.per_seq_rows is None
    in_specs = [
        group.row_spec(D_MODEL),
        _mod_spec(0, 3, group), _mod_spec(0, 4, group), _mod_spec(0, 5, group),
        _resident((None, 1, D_MODEL), lambda b, t: (1, 0, 0)),
        _resident((D_MODEL, 3 * D_MODEL), lambda b, t: (0, 0)),
        _resident((CONV_WIDTH, D_MODEL), lambda b, t: (0, 0)),
        _resident((D_MODEL, D_MODEL), lambda b, t: (0, 0)),
    ]
    args = [x, mods, mods, mods, norm_g, w_in, conv_w, w_out]
    scratch = []
    if prompt:
        u_spec = pl.BlockSpec((SUBLANES, D_MODEL), lambda b, t: (b, 0))
        u_shape = jax.ShapeDtypeStruct((group.grid[0] * SUBLANES, D_MODEL), F32)
        scratch.append(pltpu.VMEM((SUBLANES, D_MODEL), F32))
    else:
        n_seq = group.tile // group.per_seq_rows
        st_spec = pl.BlockSpec((n_seq, D_MODEL), lambda b, t: (t, 0))
        in_specs += [st_spec, st_spec]
        args += [state[0], state[1]]
        u_spec = group.row_spec(D_MODEL)
        u_shape = jax.ShapeDtypeStruct((group.rows, D_MODEL), F32)
    return pl.pallas_call(
        functools.partial(_conv_kernel, per_seq_rows=group.per_seq_rows),
        grid=group.grid,
        in_specs=in_specs,
        out_specs=[group.row_spec(D_MODEL), u_spec],
        out_shape=[jax.ShapeDtypeStruct((group.rows, D_MODEL), F32), u_shape],
        scratch_shapes=scratch,
        compiler_params=pltpu.CompilerParams(
            dimension_semantics=("arbitrary", "arbitrary"),
            vmem_limit_bytes=VMEM_LIMIT_BYTES),
        name="conv_mixer",
    )(*args)


def _sink_column(sink_ref, kv_head, q_rows):
    grp = lax.broadcasted_iota(jnp.int32, (GROUP * q_rows, 1), 0) >> _log2(q_rows)
    col = jnp.full((GROUP * q_rows, 1), sink_ref[kv_head * GROUP + GROUP - 1], F32)
    for g in range(GROUP - 2, -1, -1):
        col = jnp.where(grp == g, sink_ref[kv_head * GROUP + g], col)
    return col


def _stack_heads(q_ref, rows, kv_head):
    return jnp.concatenate(
        [q_ref[rows, (kv_head * GROUP + g) * HEAD_DIM:(kv_head * GROUP + g + 1) * HEAD_DIM]
         for g in range(GROUP)], axis=0)


def _softmax_parts(scores, sink):
    m = sink
    for s in scores:
        m = jnp.maximum(m, jnp.max(s, axis=-1, keepdims=True))
    ps = [jnp.exp(s - m) for s in scores]
    denom = jnp.exp(sink - m)
    for p in ps:
        denom = denom + jnp.sum(p, axis=-1, keepdims=True)
    return ps, 1.0 / denom


def _project_q(x_ref, sh_ref, sc_ref, ng_ref, wq_ref, tables, per_seq_rows):
    h = ((_unit_rms(x_ref[...]) * ng_ref[...]) * (1.0 + _mod_rows(sc_ref, per_seq_rows))
         + _mod_rows(sh_ref, per_seq_rows)).astype(BF16)
    q = _rope(_dot(h, wq_ref[...]), *tables)
    return (q * (HEAD_DIM ** -0.5)).astype(BF16)


def _band_caps():
    shape = (2 * WINDOW, GROUP * WINDOW)
    kj = lax.broadcasted_iota(jnp.int32, shape, 0)
    qi = lax.broadcasted_iota(jnp.int32, shape, 1) & (WINDOW - 1)
    in_band = (kj > qi) & (kj <= qi + WINDOW)
    return (jnp.where(in_band, FMAX, NEG), jnp.where(in_band & (kj >= WINDOW), FMAX, NEG))


def _attn_prompt_kernel(x_ref, sh_ref, sc_ref, gt_ref, ng_ref, wq_ref, wo_ref,
                        kp_ref, kc_ref, vp_ref, vc_ref, cos_ref, sin_ref, sink_ref,
                        o_ref, q_scr, k_scr, v_scr, a_scr, cap_scr):
    tile = x_ref.shape[0]

    @pl.when((pl.program_id(0) == 0) & (pl.program_id(1) == 0))
    def _():
        general, first = _band_caps()
        cap_scr[0] = general
        cap_scr[1] = first

    h = ((_unit_rms(x_ref[...]) * ng_ref[...]) * (1.0 + _mod_rows(sc_ref, None))
         + _mod_rows(sh_ref, None)).astype(BF16)
    qt = _rope_t(_dot_nt(wq_ref[...], h), cos_ref[...], sin_ref[...])
    q_scr[...] = (qt * (HEAD_DIM ** -0.5)).astype(BF16)
    k_scr[:WINDOW, :] = kp_ref[...].astype(BF16)
    k_scr[WINDOW:, :] = kc_ref[...].astype(BF16)
    v_scr[:, :WINDOW] = vp_ref[...].astype(BF16)
    v_scr[:, WINDOW:] = vc_ref[...].astype(BF16)
    first_tile = pl.program_id(1) == 0

    for blk in range(tile // WINDOW):
        cols = slice(blk * WINDOW, (blk + 1) * WINDOW)
        keys = slice(blk * WINDOW, (blk + 2) * WINDOW)
        cap = cap_scr[jnp.where(first_tile, 1, 0)] if blk == 0 else cap_scr[0]
        for kvh in range(N_KV_HEADS):
            dims = slice(kvh * HEAD_DIM, (kvh + 1) * HEAD_DIM)
            heads = [slice((kvh * GROUP + g) * HEAD_DIM, (kvh * GROUP + g + 1) * HEAD_DIM)
                     for g in range(GROUP)]
            qg = jnp.concatenate([q_scr[hd, cols] for hd in heads], axis=1)
            s = jnp.minimum(_dot(k_scr[keys, dims], qg), cap)
            sink = jnp.concatenate(
                [jnp.full((1, WINDOW), sink_ref[kvh * GROUP + g], F32) for g in range(GROUP)], axis=1)
            m = jnp.maximum(jnp.max(s, axis=0, keepdims=True), sink)
            p = jnp.exp(s - m)
            denom = jnp.sum(p, axis=0, keepdims=True) + jnp.exp(sink - m)
            o = _dot(v_scr[dims, keys], p.astype(BF16)) * (1.0 / denom)
            for g, hd in enumerate(heads):
                a_scr[hd, cols] = o[:, g * WINDOW:(g + 1) * WINDOW]

    y = _dot(a_scr[...].T.astype(BF16), wo_ref[...])
    o_ref[...] = x_ref[...] + _mod_rows(gt_ref, None) * y


def _attn_prompt(x, group, mods, norm_g, w_q_t, w_o, k, v_t, tables_t, sinks):
    tile = group.tile
    nt = group.grid[1]
    blocks_per_tile = tile // WINDOW

    def prev_block(b, t):
        return jnp.maximum((b * nt + t) * blocks_per_tile - 1, 0)

    in_specs = [
        group.row_spec(D_MODEL),
        _mod_spec(1, 3, group), _mod_spec(1, 4, group), _mod_spec(1, 5, group),
        _resident((None, 1, D_MODEL), lambda b, t: (4, 0, 0)),
        _resident((D_MODEL, D_MODEL), lambda b, t: (0, 0)),
        _resident((D_MODEL, D_MODEL), lambda b, t: (0, 0)),
        pl.BlockSpec((WINDOW, KV_DIM), lambda b, t: (prev_block(b, t), 0)), group.row_spec(KV_DIM),
        pl.BlockSpec((KV_DIM, WINDOW), lambda b, t: (0, prev_block(b, t))), group.col_spec(KV_DIM),
        group.table_t_spec(), group.table_t_spec(),
        pl.BlockSpec(memory_space=pltpu.SMEM),
    ]
    return pl.pallas_call(
        _attn_prompt_kernel,
        grid=group.grid,
        in_specs=in_specs,
        out_specs=group.row_spec(D_MODEL),
        out_shape=jax.ShapeDtypeStruct((group.rows, D_MODEL), F32),
        scratch_shapes=[
            pltpu.VMEM((D_MODEL, tile), BF16),
            pltpu.VMEM((tile + WINDOW, KV_DIM), BF16),
            pltpu.VMEM((KV_DIM, tile + WINDOW), BF16),
            pltpu.VMEM((D_MODEL, tile), F32),
            pltpu.VMEM((2, 2 * WINDOW, GROUP * WINDOW), F32),
        ],
        compiler_params=pltpu.CompilerParams(
            dimension_semantics=("arbitrary", "arbitrary"),
            vmem_limit_bytes=VMEM_LIMIT_BYTES),
        name="attn_prompt",
    )(x, mods, mods, mods, norm_g, w_q_t, w_o, k, k, v_t, v_t, *tables_t, sinks)


def _attn_sample_kernel(x_ref, sh_ref, sc_ref, gt_ref, ng_ref, wq_ref, wo_ref,
                        kc_ref, vc_ref, kn_ref, vn_ref, cos_ref, sup_ref, sdn_ref, sink_ref,
                        o_ref, kout_ref, vout_ref, q_scr, *, dec_seq):
    rows = x_ref.shape[0]
    n_seq = rows // dec_seq
    n_cache = n_seq * WINDOW
    q_scr[...] = _project_q(x_ref, sh_ref, sc_ref, ng_ref, wq_ref,
                            (cos_ref[...], sup_ref[...], sdn_ref[...]), dec_seq)

    def all_windows(ref, kvh):
        return jnp.concatenate(
            [ref[(s * N_KV_HEADS + kvh) * HEAD_DIM:(s * N_KV_HEADS + kvh + 1) * HEAD_DIM, :].astype(BF16)
             for s in range(n_seq)], axis=1)

    q_row = lax.broadcasted_iota(jnp.int32, (GROUP * rows, 1), 0) & (rows - 1)
    q_seq, q_i = q_row >> _log2(dec_seq), q_row & (dec_seq - 1)
    c_col = lax.broadcasted_iota(jnp.int32, (1, n_cache), 1)
    valid_c = (q_seq == (c_col >> _log2(WINDOW))) & ((c_col & (WINDOW - 1)) > q_i)
    n_col = lax.broadcasted_iota(jnp.int32, (1, rows), 1)
    valid_n = (q_seq == (n_col >> _log2(dec_seq))) & ((n_col & (dec_seq - 1)) <= q_i)

    heads = []
    for kvh in range(N_KV_HEADS):
        dims = slice(kvh * HEAD_DIM, (kvh + 1) * HEAD_DIM)
        qg = _stack_heads(q_scr, slice(None), kvh)
        s_c = jnp.where(valid_c, _dot(qg, all_windows(kc_ref, kvh)), NEG)
        s_n = jnp.where(valid_n, _dot(qg, kn_ref[dims, :].astype(BF16)), NEG)
        (p_c, p_n), inv = _softmax_parts([s_c, s_n], _sink_column(sink_ref, kvh, rows))
        o = (_dot_nt(p_c.astype(BF16), all_windows(vc_ref, kvh))
             + _dot_nt(p_n.astype(BF16), vn_ref[dims, :].astype(BF16))) * inv
        heads += [o[g * rows:(g + 1) * rows, :] for g in range(GROUP)]
    attn = jnp.concatenate(heads, axis=1).astype(BF16)
    o_ref[...] = x_ref[...] + _mod_rows(gt_ref, dec_seq) * _dot(attn, wo_ref[...])

    keep = lax.broadcasted_iota(jnp.int32, (1, WINDOW), 1) < WINDOW - dec_seq
    for src_ref, new_ref, dst_ref in ((kc_ref, kn_ref, kout_ref), (vc_ref, vn_ref, vout_ref)):
        new = new_ref[...]
        for s in range(n_seq):
            blk = slice(s * KV_DIM, (s + 1) * KV_DIM)
            old = pltpu.roll(src_ref[blk, :], WINDOW - dec_seq, axis=1)
            app = pltpu.roll(new, (WINDOW - dec_seq - s * dec_seq) % rows, axis=1)
            dst_ref[blk, :] = jnp.where(keep, old, app)


def _attn_sample(x, n_seq_total, dec_seq, mods, norm_g, w_q, w_o, k_cache, v_cache,
                 k_new, v_new, tables, sinks):
    n_seq = SAMPLE_ATTN_SEQS
    rows = n_seq * dec_seq
    assert rows == WINDOW == LANES
    grid = (n_seq_total // n_seq,)
    row_spec = lambda width: pl.BlockSpec((rows, width), lambda t: (t, 0))
    mod_spec = lambda col: pl.BlockSpec((None, n_seq, D_MODEL), lambda t: (1, t, col))
    cache_spec = pl.BlockSpec((n_seq * KV_DIM, WINDOW), lambda t: (t, 0))
    new_spec = pl.BlockSpec((KV_DIM, rows), lambda t: (0, t))
    table_spec = pl.BlockSpec((rows, LANES), lambda t: (0, 0))
    in_specs = [
        row_spec(D_MODEL), mod_spec(3), mod_spec(4), mod_spec(5),
        _resident((None, 1, D_MODEL), lambda t: (4, 0, 0)),
        _resident((D_MODEL, D_MODEL), lambda t: (0, 0)),
        _resident((D_MODEL, D_MODEL), lambda t: (0, 0)),
        cache_spec, cache_spec, new_spec, new_spec,
        table_spec, table_spec, table_spec,
        pl.BlockSpec(memory_space=pltpu.SMEM),
    ]
    cache_shape = jax.ShapeDtypeStruct((n_seq_total * KV_DIM, WINDOW), F32)
    return pl.pallas_call(
        functools.partial(_attn_sample_kernel, dec_seq=dec_seq),
        grid=grid,
        in_specs=in_specs,
        out_specs=[row_spec(D_MODEL), cache_spec, cache_spec],
        out_shape=[jax.ShapeDtypeStruct((n_seq_total * dec_seq, D_MODEL), F32),
                   cache_shape, cache_shape],
        scratch_shapes=[pltpu.VMEM((rows, D_MODEL), BF16)],
        compiler_params=pltpu.CompilerParams(
            dimension_semantics=("arbitrary",),
            vmem_limit_bytes=VMEM_LIMIT_BYTES),
        name="attn_sample",
    )(x, mods, mods, mods, norm_g, w_q, w_o, k_cache, v_cache, k_new, v_new, *tables, sinks)


def _rope_angles(pos):
    inv_freq = ROPE_THETA ** (-jnp.arange(0, ROT_DIM, 2, dtype=F32) / ROT_DIM)
    ang = pos.astype(F32)[:, None] * inv_freq[None, :]
    return jnp.cos(ang), jnp.sin(ang)


def _rope_tables(pos):
    half = ROT_DIM // 2
    cos, sin = _rope_angles(pos)
    n = pos.shape[0]
    rest = HEAD_DIM - ROT_DIM
    zeros = jnp.zeros((n, half), F32)
    cos_t = jnp.concatenate([cos, cos, jnp.ones((n, rest), F32)], axis=1)
    sin_up = jnp.concatenate([-sin, zeros, jnp.zeros((n, rest), F32)], axis=1)
    sin_dn = jnp.concatenate([zeros, sin, jnp.zeros((n, rest), F32)], axis=1)
    reps = LANES // HEAD_DIM
    return tuple(jnp.tile(t, (1, reps)) for t in (cos_t, sin_up, sin_dn))


def kernel(x_prompt, x_sample, state_conv, cache_k_win, cache_v_win, c_prompt, c_sample, norm_g, w_ada, b_ada, w_ffn_gate, w_ffn_up, w_ffn_down, conv_w_in, conv_w, conv_w_out, kv_norm_g, w_ada_kv, b_ada_kv, w_k, w_v, attn_w_q, attn_sinks, attn_w_o, final_norm_g):
    batch, seq, d = x_prompt.shape
    dec_batch, dec_seq, _ = x_sample.shape
    w_buf = cache_k_win.shape[1]
    past_len = 16384
    assert d == D_MODEL and w_buf == WINDOW and dec_seq == SUBLANES
    assert w_ada.shape[0] == 2 and conv_w_in.shape[0] == 1 and attn_w_q.shape[0] == 1

    pad = (-(dec_batch + batch)) % SUBLANES
    c_all = jnp.concatenate([c_sample, c_prompt, jnp.zeros((pad, d), F32)], axis=0)
    mods = _ada(c_all, w_ada, b_ada)
    mods_kv = _ada(c_all, w_ada_kv[None], b_ada_kv[None])
    prompt_mod_block = dec_batch // SUBLANES

    make_groups = lambda tile: {
        "prompt": _Group(batch, seq, None, prompt_mod_block, tile),
        "sample": _Group(dec_batch, dec_seq, dec_seq, prompt_mod_block, tile),
    }
    ffn_groups, mixer_groups = make_groups(FFN_TILE), make_groups(MIXER_TILE)
    xs = {"prompt": x_prompt.reshape(batch * seq, d), "sample": x_sample.reshape(dec_batch * dec_seq, d)}

    norm_g3 = norm_g.reshape(-1, 1, d)
    wg, wu, wd = (w.astype(BF16) for w in (w_ffn_gate, w_ffn_up, w_ffn_down))
    w_in, w_out = conv_w_in[0].astype(BF16), conv_w_out[0].astype(BF16)
    w_q, w_o = attn_w_q[0].astype(BF16), attn_w_o[0].astype(BF16)
    sinks = attn_sinks[0]

    pos_dec = past_len + jnp.arange(dec_seq, dtype=jnp.int32)
    pos_prompt = jnp.arange(seq, dtype=jnp.int32)
    tab_prompt = _rope_tables(pos_prompt)
    tab_dec = tuple(jnp.tile(t, (SAMPLE_ATTN_SEQS, 1)) for t in _rope_tables(pos_dec))
    tab_prompt_t = tuple(t.T for t in _rope_angles(pos_prompt))
    tab_dec_t = tuple(jnp.tile(t.T, (1, FFN_TILE // dec_seq)) for t in _rope_angles(pos_dec))
    kv_common = dict(norm_g=kv_norm_g.reshape(1, d), mods=mods_kv, w_v=w_v.T.astype(BF16))
    kv_args = {
        "prompt": dict(kv_common, transposed=False, tables=tab_prompt, w_k=w_k.astype(BF16)),
        "sample": dict(kv_common, transposed=True, tables=tab_dec_t, w_k=w_k.T.astype(BF16)),
    }
    w_q_t = attn_w_q[0].T.astype(BF16)
    to_rows = lambda c: jnp.transpose(c, (0, 2, 3, 1)).reshape(dec_batch * KV_DIM, w_buf)
    from_rows = lambda c: jnp.transpose(c.reshape(dec_batch, N_KV_HEADS, HEAD_DIM, w_buf), (0, 3, 1, 2))

    out = {}
    for name, g in ffn_groups.items():
        gm = mixer_groups[name]
        x = xs[name]
        (x,) = _ffn(x, g, mods, 0, 0, norm_g3, wg, wu, wd, 0)
        state = None if name == "prompt" else (state_conv[0, :, 0, :], state_conv[0, :, 1, :])
        x, u = _conv_mixer(x, gm, mods, norm_g3, w_in, conv_w[0], w_out, state)
        (x,) = _ffn(x, g, mods, 0, 2, norm_g3, wg, wu, wd, 1)
        x, k, v = _ffn(x, g, mods, 1, 0, norm_g3, wg, wu, wd, 0, kv=kv_args[name])
        if name == "prompt":
            x = _attn_prompt(x, gm, mods, norm_g3, w_q_t, w_o, k, v, tab_prompt_t, sinks)
            k = k.reshape(batch, seq, KV_DIM)[:, seq - WINDOW:].reshape(
                batch, WINDOW, N_KV_HEADS, HEAD_DIM)
            v = jnp.stack([v[:, (b + 1) * seq - WINDOW:(b + 1) * seq] for b in range(batch)])
            v = jnp.transpose(v.reshape(batch, N_KV_HEADS, HEAD_DIM, WINDOW), (0, 3, 1, 2))
        else:
            x, k, v = _attn_sample(x, dec_batch, dec_seq, mods, norm_g3, w_q, w_o,
                                   to_rows(cache_k_win), to_rows(cache_v_win), k, v, tab_dec, sinks)
            k, v = from_rows(k), from_rows(v)
        (y,) = _ffn(x, g, mods, 1, 2, norm_g3, wg, wu, wd, 1, final_g=final_norm_g.reshape(1, d))
        out[name] = (y, u, k, v)

    y_p, u_p, k_state_p, v_state_p = out["prompt"]
    y_s, u_s, k_state_s, v_state_s = out["sample"]
    tail = CONV_WIDTH - 1
    conv_p = u_p.reshape(batch, SUBLANES, d)[:, SUBLANES - tail:][None]
    conv_s = u_s.reshape(dec_batch, dec_seq, d)[:, dec_seq - tail:][None]
    return (y_p.reshape(batch, seq, d), y_s.reshape(dec_batch, dec_seq, d), conv_p, conv_s,
            k_state_p, v_state_p, k_state_s, v_state_s)
```

```python
import functools

import jax
import jax.numpy as jnp
from jax import lax
from jax.experimental import pallas as pl
from jax.experimental.pallas import tpu as pltpu

F32 = jnp.float32
BF16 = jnp.bfloat16

D_MODEL = 1024
D_FF = 2816
HEAD_DIM = 64
N_HEADS = 16
N_KV_HEADS = 4
GROUP = N_HEADS // N_KV_HEADS
KV_DIM = N_KV_HEADS * HEAD_DIM
WINDOW = 128
ROT_DIM = 16
ROPE_THETA = 500000.0
CONV_WIDTH = 3
N_MOD = 9
EPS = 1e-6
NEG = -1e30
FMAX = float(jnp.finfo(jnp.float32).max)

SUBLANES = 8
LANES = 128
VMEM_LIMIT_BYTES = 56 * 1024 * 1024

FFN_TILE = 1024
MIXER_TILE = 512
ATTN_CHUNK = 256
LOG2E = 1.4426950408889634
SAMPLE_ATTN_SEQS = 16
FF_CHUNKS = ((0, 1024), (1024, 2048), (2048, D_FF))


def _dot(a, b):
    return jnp.dot(a, b, preferred_element_type=F32)


def _dot_nt(a, b):
    return lax.dot_general(a, b, (((1,), (1,)), ((), ())), preferred_element_type=F32)


def _log2(n):
    assert n & (n - 1) == 0, n
    return n.bit_length() - 1


def _silu(x):
    return x * jax.nn.sigmoid(x)


def _unit_rms(x):
    return x * lax.rsqrt(jnp.mean(x * x, axis=-1, keepdims=True) + EPS)


def _repeat_rows(ref, reps):
    n = ref.shape[1]
    return jnp.concatenate(
        [jnp.broadcast_to(ref[i:i + 1, :], (reps, n)) for i in range(ref.shape[0])], axis=0)


def _mod_rows(ref, per_seq_rows):
    if per_seq_rows is None:
        return ref[pl.ds(pl.program_id(0), 1), :]
    return _repeat_rows(ref, per_seq_rows)


def _rope(x, cos_t, sin_up_t, sin_dn_t):
    n = x.shape[1]
    reps = n // LANES
    half = ROT_DIM // 2
    cos = jnp.concatenate([cos_t] * reps, axis=1)
    sin_up = jnp.concatenate([sin_up_t] * reps, axis=1)
    sin_dn = jnp.concatenate([sin_dn_t] * reps, axis=1)
    x_up = pltpu.roll(x, n - half, axis=1)
    x_dn = pltpu.roll(x, half, axis=1)
    return x * cos + x_up * sin_up + x_dn * sin_dn


def _ada_kernel(c_ref, w_ref, b_ref, o_ref):
    a = _silu(c_ref[...]).astype(BF16)
    o_ref[...] = _dot(a, w_ref[...].astype(BF16)) + b_ref[...]


def _ada(c_all, w, b):
    n_layers, _, n = w.shape
    m = c_all.shape[0]
    tn = D_MODEL
    return pl.pallas_call(
        _ada_kernel,
        grid=(n_layers, n // tn),
        in_specs=[
            pl.BlockSpec((m, D_MODEL), lambda l, j: (0, 0)),
            pl.BlockSpec((None, D_MODEL, tn), lambda l, j: (l, 0, j)),
            pl.BlockSpec((None, 1, tn), lambda l, j: (l, 0, j)),
        ],
        out_specs=pl.BlockSpec((None, m, tn), lambda l, j: (l, 0, j)),
        out_shape=jax.ShapeDtypeStruct((n_layers, m, n), F32),
        compiler_params=pltpu.CompilerParams(
            dimension_semantics=("arbitrary", "arbitrary"),
            vmem_limit_bytes=VMEM_LIMIT_BYTES),
        name="adaln_tables",
    )(c_all, w, b.reshape(n_layers, 1, n))


def _rope_t(xt, cos_t, sin_t):
    half = ROT_DIM // 2
    out = []
    for h in range(xt.shape[0] // HEAD_DIM):
        base = h * HEAD_DIM
        x1 = xt[base:base + half, :]
        x2 = xt[base + half:base + ROT_DIM, :]
        out += [x1 * cos_t - x2 * sin_t, x2 * cos_t + x1 * sin_t,
                xt[base + ROT_DIM:base + HEAD_DIM, :]]
    return jnp.concatenate(out, axis=0)


def _ffn_kernel(*refs, per_seq_rows, with_kv, kv_transposed, final):
    it = iter(refs)
    x_ref, sh_ref, sc_ref, gt_ref, ng_ref, wg_ref, wu_ref, wd_ref = (next(it) for _ in range(8))
    if with_kv:
        kvg_ref, kvsh_ref, kvsc_ref, wk_ref, wv_ref, cos_ref, sup_ref = (next(it) for _ in range(7))
        if not kv_transposed:
            sdn_ref = next(it)
    if final:
        fg_ref = next(it)
    o_ref = next(it)
    if with_kv:
        k_ref, v_ref = next(it), next(it)

    x = x_ref[...]
    xn = _unit_rms(x)
    h = (xn * ng_ref[...]) * (1.0 + _mod_rows(sc_ref, per_seq_rows)) + _mod_rows(sh_ref, per_seq_rows)
    hb = h.astype(BF16)
    y = None
    for lo, hi in FF_CHUNKS:
        g = _dot(hb, wg_ref[:, lo:hi])
        u = _dot(hb, wu_ref[:, lo:hi])
        a = (_silu(g) * u).astype(BF16)
        part = _dot(a, wd_ref[lo:hi, :])
        y = part if y is None else y + part
    x_new = x + (0.5 * _mod_rows(gt_ref, per_seq_rows)) * y

    if with_kv:
        hk = ((xn * kvg_ref[...]) * (1.0 + _mod_rows(kvsc_ref, per_seq_rows))
              + _mod_rows(kvsh_ref, per_seq_rows)).astype(BF16)
        if kv_transposed:
            k_ref[...] = _rope_t(_dot_nt(wk_ref[...], hk), cos_ref[...], sup_ref[...])
        else:
            k_ref[...] = _rope(_dot(hk, wk_ref[...]), cos_ref[...], sup_ref[...], sdn_ref[...])
        v_ref[...] = _dot_nt(wv_ref[...], hk)

    if final:
        o_ref[...] = _unit_rms(x_new) * fg_ref[...]
    else:
        o_ref[...] = x_new


def _resident(shape, index_map):
    return pl.BlockSpec(shape, index_map, pipeline_mode=pl.Buffered(1))


def _mod_spec(layer, col, group):
    if group.per_seq_rows is None:
        return pl.BlockSpec((None, SUBLANES, D_MODEL),
                            lambda b, t: (layer, group.prompt_mod_block, col))
    return pl.BlockSpec((None, group.tile // group.per_seq_rows, D_MODEL),
                        lambda b, t: (layer, t, col))


class _Group:
    def __init__(self, n_seq, seq_len, per_seq_rows, prompt_mod_block, tile):
        self.tile = tile
        self.rows = n_seq * seq_len
        self.per_seq_rows = per_seq_rows
        self.prompt_mod_block = prompt_mod_block
        if per_seq_rows is None:
            self.grid = (n_seq, seq_len // self.tile)
        else:
            self.grid = (1, self.rows // self.tile)
        self.tiles_per_seq = self.grid[1]

    def row_spec(self, width):
        nt = self.grid[1]
        return pl.BlockSpec((self.tile, width), lambda b, t: (b * nt + t, 0))

    def col_spec(self, height):
        nt = self.grid[1]
        return pl.BlockSpec((height, self.tile), lambda b, t: (0, b * nt + t))

    def table_t_spec(self):
        if self.per_seq_rows is None:
            return pl.BlockSpec((ROT_DIM // 2, self.tile), lambda b, t: (0, t))
        return pl.BlockSpec((ROT_DIM // 2, self.tile), lambda b, t: (0, 0))

    def table_spec(self):
        if self.per_seq_rows is None:
            return pl.BlockSpec((self.tile, LANES), lambda b, t: (t, 0))
        return pl.BlockSpec((self.tile, LANES), lambda b, t: (0, 0))


def _ffn(x, group, mods, layer, sub, norm_g, wg, wu, wd, slot, kv=None, final_g=None):
    with_kv = kv is not None
    final = final_g is not None
    vec = lambda i: _resident((None, 1, D_MODEL), lambda b, t: (i, 0, 0))
    in_specs = [
        group.row_spec(D_MODEL),
        _mod_spec(layer, 3 * sub + 0, group),
        _mod_spec(layer, 3 * sub + 1, group),
        _mod_spec(layer, 3 * sub + 2, group),
        vec(layer * 3 + sub),
        _resident((None, None, D_MODEL, D_FF), lambda b, t: (layer, slot, 0, 0)),
        _resident((None, None, D_MODEL, D_FF), lambda b, t: (layer, slot, 0, 0)),
        _resident((None, None, D_FF, D_MODEL), lambda b, t: (layer, slot, 0, 0)),
    ]
    args = [x, mods, mods, mods, norm_g, wg, wu, wd]
    out_specs = [group.row_spec(D_MODEL)]
    out_shape = [jax.ShapeDtypeStruct((group.rows, D_MODEL), F32)]
    kv_transposed = with_kv and kv["transposed"]
    if with_kv:
        wk_shape = (KV_DIM, D_MODEL) if kv_transposed else (D_MODEL, KV_DIM)
        in_specs += [
            _resident((1, D_MODEL), lambda b, t: (0, 0)),
            _mod_spec(0, 0, group),
            _mod_spec(0, 1, group),
            _resident(wk_shape, lambda b, t: (0, 0)),
            _resident((KV_DIM, D_MODEL), lambda b, t: (0, 0)),
        ]
        args += [kv["norm_g"], kv["mods"], kv["mods"], kv["w_k"], kv["w_v"], *kv["tables"]]
        t_shape = jax.ShapeDtypeStruct((KV_DIM, group.rows), F32)
        if kv_transposed:
            in_specs += [group.table_t_spec()] * 2
            out_specs += [group.col_spec(KV_DIM)] * 2
            out_shape += [t_shape] * 2
        else:
            in_specs += [group.table_spec()] * 3
            out_specs += [group.row_spec(KV_DIM), group.col_spec(KV_DIM)]
            out_shape += [jax.ShapeDtypeStruct((group.rows, KV_DIM), F32), t_shape]
    if final:
        in_specs.append(_resident((1, D_MODEL), lambda b, t: (0, 0)))
        args.append(final_g)
    return pl.pallas_call(
        functools.partial(_ffn_kernel, per_seq_rows=group.per_seq_rows,
                          with_kv=with_kv, kv_transposed=kv_transposed, final=final),
        grid=group.grid,
        in_specs=in_specs,
        out_specs=out_specs,
        out_shape=out_shape,
        compiler_params=pltpu.CompilerParams(
            dimension_semantics=("arbitrary", "arbitrary"),
            vmem_limit_bytes=VMEM_LIMIT_BYTES),
        name=f"ffn_l{layer}_s{sub}",
    )(*args)


def _conv_kernel(*refs, per_seq_rows):
    it = iter(refs)
    x_ref, sh_ref, sc_ref, gt_ref, ng_ref, win_ref, cw_ref, wout_ref = (next(it) for _ in range(8))
    if per_seq_rows is not None:
        st0_ref, st1_ref = next(it), next(it)
    o_ref, u_ref = next(it), next(it)
    if per_seq_rows is None:
        carry_ref = next(it)

    tile = x_ref.shape[0]
    x = x_ref[...]
    h = ((_unit_rms(x) * ng_ref[...]) * (1.0 + _mod_rows(sc_ref, per_seq_rows))
         + _mod_rows(sh_ref, per_seq_rows)).astype(BF16)
    bcv = _dot(h, win_ref[...])
    b_g = bcv[:, :D_MODEL]
    u = bcv[:, D_MODEL:2 * D_MODEL] * bcv[:, 2 * D_MODEL:]

    row = lax.broadcasted_iota(jnp.int32, (tile, 1), 0)
    if per_seq_rows is None:
        @pl.when(pl.program_id(1) == 0)
        def _():
            carry_ref[...] = jnp.zeros_like(carry_ref)
        prev2 = carry_ref[SUBLANES - 2:SUBLANES - 1, :]
        prev1 = carry_ref[SUBLANES - 1:SUBLANES, :]
        t = row
    else:
        prev2 = _repeat_rows(st0_ref, per_seq_rows)
        prev1 = _repeat_rows(st1_ref, per_seq_rows)
        t = row & (per_seq_rows - 1)
    u1 = jnp.where(t == 0, prev1, pltpu.roll(u, 1, axis=0))
    u2 = jnp.where(t == 0, prev2, jnp.where(t == 1, prev1, pltpu.roll(u, 2, axis=0)))
    conv = u2 * cw_ref[0:1, :] + u1 * cw_ref[1:2, :] + u * cw_ref[2:3, :]
    y = _dot((b_g * conv).astype(BF16), wout_ref[...])
    o_ref[...] = x + _mod_rows(gt_ref, per_seq_rows) * y

    if per_seq_rows is None:
        tail = u[tile - SUBLANES:, :]
        carry_ref[...] = tail
        u_ref[...] = tail
    else:
        u_ref[...] = u


def _conv_mixer(x, group, mods, norm_g, w_in, conv_w, w_out, state=None):
    prompt = group.per_seq_rows is None
    in_specs = [
        group.row_spec(D_MODEL),
        _mod_spec(0, 3, group), _mod_spec(0, 4, group), _mod_spec(0, 5, group),
        _resident((None, 1, D_MODEL), lambda b, t: (1, 0, 0)),
        _resident((D_MODEL, 3 * D_MODEL), lambda b, t: (0, 0)),
        _resident((CONV_WIDTH, D_MODEL), lambda b, t: (0, 0)),
        _resident((D_MODEL, D_MODEL), lambda b, t: (0, 0)),
    ]
    args = [x, mods, mods, mods, norm_g, w_in, conv_w, w_out]
    scratch = []
    if prompt:
        u_spec = pl.BlockSpec((SUBLANES, D_MODEL), lambda b, t: (b, 0))
        u_shape = jax.ShapeDtypeStruct((group.grid[0] * SUBLANES, D_MODEL), F32)
        scratch.append(pltpu.VMEM((SUBLANES, D_MODEL), F32))
    else:
        n_seq = group.tile // group.per_seq_rows
        st_spec = pl.BlockSpec((n_seq, D_MODEL), lambda b, t: (t, 0))
        in_specs += [st_spec, st_spec]
        args += [state[0], state[1]]
        u_spec = group.row_spec(D_MODEL)
        u_shape = jax.ShapeDtypeStruct((group.rows, D_MODEL), F32)
    return pl.pallas_call(
        functools.partial(_conv_kernel, per_seq_rows=group.per_seq_rows),
        grid=group.grid,
        in_specs=in_specs,
        out_specs=[group.row_spec(D_MODEL), u_spec],
        out_shape=[jax.ShapeDtypeStruct((group.rows, D_MODEL), F32), u_shape],
        scratch_shapes=scratch,
        compiler_params=pltpu.CompilerParams(
            dimension_semantics=("arbitrary", "arbitrary"),
            vmem_limit_bytes=VMEM_LIMIT_BYTES),
        name="conv_mixer",
    )(*args)


def _sink_column(sink_ref, kv_head, q_rows):
    grp = lax.broadcasted_iota(jnp.int32, (GROUP * q_rows, 1), 0) >> _log2(q_rows)
    col = jnp.full((GROUP * q_rows, 1), sink_ref[kv_head * GROUP + GROUP - 1], F32)
    for g in range(GROUP - 2, -1, -1):
        col = jnp.where(grp == g, sink_ref[kv_head * GROUP + g], col)
    return col


def _stack_heads(q_ref, rows, kv_head):
    return jnp.concatenate(
        [q_ref[rows, (kv_head * GROUP + g) * HEAD_DIM:(kv_head * GROUP + g + 1) * HEAD_DIM]
         for g in range(GROUP)], axis=0)


def _softmax_parts(scores, sink):
    m = sink
    for s in scores:
        m = jnp.maximum(m, jnp.max(s, axis=-1, keepdims=True))
    ps = [jnp.exp(s - m) for s in scores]
    denom = jnp.exp(sink - m)
    for p in ps:
        denom = denom + jnp.sum(p, axis=-1, keepdims=True)
    return ps, 1.0 / denom


def _project_q(x_ref, sh_ref, sc_ref, ng_ref, wq_ref, tables, per_seq_rows):
    h = ((_unit_rms(x_ref[...]) * ng_ref[...]) * (1.0 + _mod_rows(sc_ref, per_seq_rows))
         + _mod_rows(sh_ref, per_seq_rows)).astype(BF16)
    q = _rope(_dot(h, wq_ref[...]), *tables)
    return (q * (HEAD_DIM ** -0.5)).astype(BF16)


def _band_caps():
    shape = (2 * WINDOW, GROUP * WINDOW)
    kj = lax.broadcasted_iota(jnp.int32, shape, 0)
    qi = lax.broadcasted_iota(jnp.int32, shape, 1) & (WINDOW - 1)
    in_band = (kj > qi) & (kj <= qi + WINDOW)
    return (jnp.where(in_band, FMAX, NEG), jnp.where(in_band & (kj >= WINDOW), FMAX, NEG))


def _attn_prompt_kernel(x_ref, sh_ref, sc_ref, gt_ref, ng_ref, wq_ref, wo_ref,
                        kp_ref, kc_ref, vp_ref, vc_ref, cos_ref, sin_ref, sink_ref,
                        o_ref, q_scr, k_scr, v_scr, a_scr, cap_scr, h_scr, at_scr):
    tile = x_ref.shape[0]

    @pl.when((pl.program_id(0) == 0) & (pl.program_id(1) == 0))
    def _():
        general, first = _band_caps()
        cap_scr[0] = general
        cap_scr[1] = first

    k_scr[:WINDOW, :] = kp_ref[...].astype(BF16)
    k_scr[WINDOW:, :] = kc_ref[...].astype(BF16)
    v_scr[:, :WINDOW] = vp_ref[...].astype(BF16)
    v_scr[:, WINDOW:] = vc_ref[...].astype(BF16)
    first_tile = pl.program_id(1) == 0
    norm_gain = ng_ref[...]
    scale1 = 1.0 + _mod_rows(sc_ref, None)
    shift = _mod_rows(sh_ref, None)
    gate = _mod_rows(gt_ref, None)

    group_dims = GROUP * HEAD_DIM

    def chunk_rows(c):
        return slice(c * ATTN_CHUNK, (c + 1) * ATTN_CHUNK)

    def normalize(c):
        rows = chunk_rows(c)
        h_scr[rows, :] = ((_unit_rms(x_ref[rows, :]) * norm_gain) * scale1 + shift).astype(BF16)

    def project_piece(c, kvh):
        rows = chunk_rows(c)
        dims = slice(kvh * group_dims, (kvh + 1) * group_dims)
        qt = _rope_t(_dot(h_scr[rows, :], wq_ref[:, dims]).T, cos_ref[:, rows], sin_ref[:, rows])
        q_scr[dims, rows] = (qt * (HEAD_DIM ** -0.5 * LOG2E)).astype(BF16)

    def attend(blk, kvh):
        cols = slice(blk * WINDOW, (blk + 1) * WINDOW)
        keys = slice(blk * WINDOW, (blk + 2) * WINDOW)
        cap = cap_scr[jnp.where(first_tile, 1, 0)] if blk == 0 else cap_scr[0]
        dims = slice(kvh * HEAD_DIM, (kvh + 1) * HEAD_DIM)
        heads = [slice((kvh * GROUP + g) * HEAD_DIM, (kvh * GROUP + g + 1) * HEAD_DIM)
                 for g in range(GROUP)]
        qg = jnp.concatenate([q_scr[hd, cols] for hd in heads], axis=1)
        s = jnp.minimum(_dot(k_scr[keys, dims], qg), cap)
        sink = LOG2E * jnp.concatenate(
            [jnp.full((1, WINDOW), sink_ref[kvh * GROUP + g], F32) for g in range(GROUP)], axis=1)
        m = jnp.maximum(jnp.max(s, axis=0, keepdims=True), sink)
        p = jnp.exp2(s - m)
        denom = jnp.sum(p, axis=0, keepdims=True) + jnp.exp2(sink - m)
        o = _dot(v_scr[dims, keys], p.astype(BF16)) * (1.0 / denom)
        for g, hd in enumerate(heads):
            a_scr[hd, cols] = o[:, g * WINDOW:(g + 1) * WINDOW]

    def transpose_piece(c, kvh):
        rows = chunk_rows(c)
        dims = slice(kvh * group_dims, (kvh + 1) * group_dims)
        at_scr[rows, dims] = a_scr[dims, rows].T.astype(BF16)

    def emit_piece(c, n):
        rows = chunk_rows(c)
        cols = slice(n * group_dims, (n + 1) * group_dims)
        y = _dot(at_scr[rows, :], wo_ref[:, cols])
        o_ref[rows, cols] = x_ref[rows, cols] + gate[:, cols] * y

    n_chunks = tile // ATTN_CHUNK
    blocks_per_chunk = ATTN_CHUNK // WINDOW
    n_pieces = D_MODEL // group_dims
    normalize(0)
    for kvh in range(N_KV_HEADS):
        project_piece(0, kvh)
    for c in range(n_chunks):
        if c + 1 < n_chunks:
            normalize(c + 1)
        step = 0
        for b in range(blocks_per_chunk):
            for kvh in range(N_KV_HEADS):
                attend(c * blocks_per_chunk + b, kvh)
                if b == blocks_per_chunk - 1:
                    transpose_piece(c, kvh)
                if step % 2 == 0 and c + 1 < n_chunks and step // 2 < N_KV_HEADS:
                    project_piece(c + 1, step // 2)
                if step % 2 == 1 and c > 0 and step // 2 < n_pieces:
                    emit_piece(c - 1, step // 2)
                step += 1
    for n in range(n_pieces):
        emit_piece(n_chunks - 1, n)


def _attn_prompt(x, group, mods, norm_g, w_q, w_o, k, v_t, tables_t, sinks):
    tile = group.tile
    nt = group.grid[1]
    blocks_per_tile = tile // WINDOW

    def prev_block(b, t):
        return jnp.maximum((b * nt + t) * blocks_per_tile - 1, 0)

    in_specs = [
        group.row_spec(D_MODEL),
        _mod_spec(1, 3, group), _mod_spec(1, 4, group), _mod_spec(1, 5, group),
        _resident((None, 1, D_MODEL), lambda b, t: (4, 0, 0)),
        _resident((D_MODEL, D_MODEL), lambda b, t: (0, 0)),
        _resident((D_MODEL, D_MODEL), lambda b, t: (0, 0)),
        pl.BlockSpec((WINDOW, KV_DIM), lambda b, t: (prev_block(b, t), 0)), group.row_spec(KV_DIM),
        pl.BlockSpec((KV_DIM, WINDOW), lambda b, t: (0, prev_block(b, t))), group.col_spec(KV_DIM),
        group.table_t_spec(), group.table_t_spec(),
        pl.BlockSpec(memory_space=pltpu.SMEM),
    ]
    return pl.pallas_call(
        _attn_prompt_kernel,
        grid=group.grid,
        in_specs=in_specs,
        out_specs=group.row_spec(D_MODEL),
        out_shape=jax.ShapeDtypeStruct((group.rows, D_MODEL), F32),
        scratch_shapes=[
            pltpu.VMEM((D_MODEL, tile), BF16),
            pltpu.VMEM((tile + WINDOW, KV_DIM), BF16),
            pltpu.VMEM((KV_DIM, tile + WINDOW), BF16),
            pltpu.VMEM((D_MODEL, tile), F32),
            pltpu.VMEM((2, 2 * WINDOW, GROUP * WINDOW), F32),
            pltpu.VMEM((tile, D_MODEL), BF16),
            pltpu.VMEM((tile, D_MODEL), BF16),
        ],
        compiler_params=pltpu.CompilerParams(
            dimension_semantics=("arbitrary", "arbitrary"),
            vmem_limit_bytes=VMEM_LIMIT_BYTES),
        name="attn_prompt",
    )(x, mods, mods, mods, norm_g, w_q, w_o, k, k, v_t, v_t, *tables_t, sinks)


def _attn_sample_kernel(x_ref, sh_ref, sc_ref, gt_ref, ng_ref, wq_ref, wo_ref,
                        kc_ref, vc_ref, kn_ref, vn_ref, cos_ref, sup_ref, sdn_ref, sink_ref,
                        o_ref, kout_ref, vout_ref, q_scr, *, dec_seq):
    rows = x_ref.shape[0]
    n_seq = rows // dec_seq
    n_cache = n_seq * WINDOW
    q_scr[...] = _project_q(x_ref, sh_ref, sc_ref, ng_ref, wq_ref,
                            (cos_ref[...], sup_ref[...], sdn_ref[...]), dec_seq)

    def all_windows(ref, kvh):
        return jnp.concatenate(
            [ref[(s * N_KV_HEADS + kvh) * HEAD_DIM:(s * N_KV_HEADS + kvh + 1) * HEAD_DIM, :].astype(BF16)
             for s in range(n_seq)], axis=1)

    q_row = lax.broadcasted_iota(jnp.int32, (GROUP * rows, 1), 0) & (rows - 1)
    q_seq, q_i = q_row >> _log2(dec_seq), q_row & (dec_seq - 1)
    c_col = lax.broadcasted_iota(jnp.int32, (1, n_cache), 1)
    valid_c = (q_seq == (c_col >> _log2(WINDOW))) & ((c_col & (WINDOW - 1)) > q_i)
    n_col = lax.broadcasted_iota(jnp.int32, (1, rows), 1)
    valid_n = (q_seq == (n_col >> _log2(dec_seq))) & ((n_col & (dec_seq - 1)) <= q_i)

    heads = []
    for kvh in range(N_KV_HEADS):
        dims = slice(kvh * HEAD_DIM, (kvh + 1) * HEAD_DIM)
        qg = _stack_heads(q_scr, slice(None), kvh)
        s_c = jnp.where(valid_c, _dot(qg, all_windows(kc_ref, kvh)), NEG)
        s_n = jnp.where(valid_n, _dot(qg, kn_ref[dims, :].astype(BF16)), NEG)
        (p_c, p_n), inv = _softmax_parts([s_c, s_n], _sink_column(sink_ref, kvh, rows))
        o = (_dot_nt(p_c.astype(BF16), all_windows(vc_ref, kvh))
             + _dot_nt(p_n.astype(BF16), vn_ref[dims, :].astype(BF16))) * inv
        heads += [o[g * rows:(g + 1) * rows, :] for g in range(GROUP)]
    attn = jnp.concatenate(heads, axis=1).astype(BF16)
    o_ref[...] = x_ref[...] + _mod_rows(gt_ref, dec_seq) * _dot(attn, wo_ref[...])

    keep = lax.broadcasted_iota(jnp.int32, (1, WINDOW), 1) < WINDOW - dec_seq
    for src_ref, new_ref, dst_ref in ((kc_ref, kn_ref, kout_ref), (vc_ref, vn_ref, vout_ref)):
        new = new_ref[...]
        for s in range(n_seq):
            blk = slice(s * KV_DIM, (s + 1) * KV_DIM)
            old = pltpu.roll(src_ref[blk, :], WINDOW - dec_seq, axis=1)
            app = pltpu.roll(new, (WINDOW - dec_seq - s * dec_seq) % rows, axis=1)
            dst_ref[blk, :] = jnp.where(keep, old, app)


def _attn_sample(x, n_seq_total, dec_seq, mods, norm_g, w_q, w_o, k_cache, v_cache,
                 k_new, v_new, tables, sinks):
    n_seq = SAMPLE_ATTN_SEQS
    rows = n_seq * dec_seq
    assert rows == WINDOW == LANES
    grid = (n_seq_total // n_seq,)
    row_spec = lambda width: pl.BlockSpec((rows, width), lambda t: (t, 0))
    mod_spec = lambda col: pl.BlockSpec((None, n_seq, D_MODEL), lambda t: (1, t, col))
    cache_spec = pl.BlockSpec((n_seq * KV_DIM, WINDOW), lambda t: (t, 0))
    new_spec = pl.BlockSpec((KV_DIM, rows), lambda t: (0, t))
    table_spec = pl.BlockSpec((rows, LANES), lambda t: (0, 0))
    in_specs = [
        row_spec(D_MODEL), mod_spec(3), mod_spec(4), mod_spec(5),
        _resident((None, 1, D_MODEL), lambda t: (4, 0, 0)),
        _resident((D_MODEL, D_MODEL), lambda t: (0, 0)),
        _resident((D_MODEL, D_MODEL), lambda t: (0, 0)),
        cache_spec, cache_spec, new_spec, new_spec,
        table_spec, table_spec, table_spec,
        pl.BlockSpec(memory_space=pltpu.SMEM),
    ]
    cache_shape = jax.ShapeDtypeStruct((n_seq_total * KV_DIM, WINDOW), F32)
    return pl.pallas_call(
        functools.partial(_attn_sample_kernel, dec_seq=dec_seq),
        grid=grid,
        in_specs=in_specs,
        out_specs=[row_spec(D_MODEL), cache_spec, cache_spec],
        out_shape=[jax.ShapeDtypeStruct((n_seq_total * dec_seq, D_MODEL), F32),
                   cache_shape, cache_shape],
        scratch_shapes=[pltpu.VMEM((rows, D_MODEL), BF16)],
        compiler_params=pltpu.CompilerParams(
            dimension_semantics=("arbitrary",),
            vmem_limit_bytes=VMEM_LIMIT_BYTES),
        name="attn_sample",
    )(x, mods, mods, mods, norm_g, w_q, w_o, k_cache, v_cache, k_new, v_new, *tables, sinks)


def _rope_angles(pos):
    inv_freq = ROPE_THETA ** (-jnp.arange(0, ROT_DIM, 2, dtype=F32) / ROT_DIM)
    ang = pos.astype(F32)[:, None] * inv_freq[None, :]
    return jnp.cos(ang), jnp.sin(ang)


def _rope_tables(pos):
    half = ROT_DIM // 2
    cos, sin = _rope_angles(pos)
    n = pos.shape[0]
    rest = HEAD_DIM - ROT_DIM
    zeros = jnp.zeros((n, half), F32)
    cos_t = jnp.concatenate([cos, cos, jnp.ones((n, rest), F32)], axis=1)
    sin_up = jnp.concatenate([-sin, zeros, jnp.zeros((n, rest), F32)], axis=1)
    sin_dn = jnp.concatenate([zeros, sin, jnp.zeros((n, rest), F32)], axis=1)
    reps = LANES // HEAD_DIM
    return tuple(jnp.tile(t, (1, reps)) for t in (cos_t, sin_up, sin_dn))


def kernel(x_prompt, x_sample, state_conv, cache_k_win, cache_v_win, c_prompt, c_sample, norm_g, w_ada, b_ada, w_ffn_gate, w_ffn_up, w_ffn_down, conv_w_in, conv_w, conv_w_out, kv_norm_g, w_ada_kv, b_ada_kv, w_k, w_v, attn_w_q, attn_sinks, attn_w_o, final_norm_g):
    batch, seq, d = x_prompt.shape
    dec_batch, dec_seq, _ = x_sample.shape
    w_buf = cache_k_win.shape[1]
    past_len = 16384
    assert d == D_MODEL and w_buf == WINDOW and dec_seq == SUBLANES
    assert w_ada.shape[0] == 2 and conv_w_in.shape[0] == 1 and attn_w_q.shape[0] == 1

    pad = (-(dec_batch + batch)) % SUBLANES
    c_all = jnp.concatenate([c_sample, c_prompt, jnp.zeros((pad, d), F32)], axis=0)
    mods = _ada(c_all, w_ada, b_ada)
    mods_kv = _ada(c_all, w_ada_kv[None], b_ada_kv[None])
    prompt_mod_block = dec_batch // SUBLANES

    make_groups = lambda tile: {
        "prompt": _Group(batch, seq, None, prompt_mod_block, tile),
        "sample": _Group(dec_batch, dec_seq, dec_seq, prompt_mod_block, tile),
    }
    ffn_groups, mixer_groups = make_groups(FFN_TILE), make_groups(MIXER_TILE)
    xs = {"prompt": x_prompt.reshape(batch * seq, d), "sample": x_sample.reshape(dec_batch * dec_seq, d)}

    norm_g3 = norm_g.reshape(-1, 1, d)
    wg, wu, wd = (w.astype(BF16) for w in (w_ffn_gate, w_ffn_up, w_ffn_down))
    w_in, w_out = conv_w_in[0].astype(BF16), conv_w_out[0].astype(BF16)
    w_q, w_o = attn_w_q[0].astype(BF16), attn_w_o[0].astype(BF16)
    sinks = attn_sinks[0]

    pos_dec = past_len + jnp.arange(dec_seq, dtype=jnp.int32)
    pos_prompt = jnp.arange(seq, dtype=jnp.int32)
    tab_prompt = _rope_tables(pos_prompt)
    tab_dec = tuple(jnp.tile(t, (SAMPLE_ATTN_SEQS, 1)) for t in _rope_tables(pos_dec))
    tab_prompt_t = tuple(t.T for t in _rope_angles(pos_prompt))
    tab_dec_t = tuple(jnp.tile(t.T, (1, FFN_TILE // dec_seq)) for t in _rope_angles(pos_dec))
    kv_common = dict(norm_g=kv_norm_g.reshape(1, d), mods=mods_kv, w_v=w_v.T.astype(BF16))
    kv_args = {
        "prompt": dict(kv_common, transposed=False, tables=tab_prompt, w_k=w_k.astype(BF16)),
        "sample": dict(kv_common, transposed=True, tables=tab_dec_t, w_k=w_k.T.astype(BF16)),
    }
    to_rows = lambda c: jnp.transpose(c, (0, 2, 3, 1)).reshape(dec_batch * KV_DIM, w_buf)
    from_rows = lambda c: jnp.transpose(c.reshape(dec_batch, N_KV_HEADS, HEAD_DIM, w_buf), (0, 3, 1, 2))

    out = {}
    for name, g in ffn_groups.items():
        gm = mixer_groups[name]
        x = xs[name]
        (x,) = _ffn(x, g, mods, 0, 0, norm_g3, wg, wu, wd, 0)
        state = None if name == "prompt" else (state_conv[0, :, 0, :], state_conv[0, :, 1, :])
        x, u = _conv_mixer(x, gm, mods, norm_g3, w_in, conv_w[0], w_out, state)
        (x,) = _ffn(x, g, mods, 0, 2, norm_g3, wg, wu, wd, 1)
        x, k, v = _ffn(x, g, mods, 1, 0, norm_g3, wg, wu, wd, 0, kv=kv_args[name])
        if name == "prompt":
            x = _attn_prompt(x, gm, mods, norm_g3, w_q, w_o, k, v, tab_prompt_t, sinks)
            k = k.reshape(batch, seq, KV_DIM)[:, seq - WINDOW:].reshape(
                batch, WINDOW, N_KV_HEADS, HEAD_DIM)
            v = jnp.stack([v[:, (b + 1) * seq - WINDOW:(b + 1) * seq] for b in range(batch)])
            v = jnp.transpose(v.reshape(batch, N_KV_HEADS, HEAD_DIM, WINDOW), (0, 3, 1, 2))
        else:
            x, k, v = _attn_sample(x, dec_batch, dec_seq, mods, norm_g3, w_q, w_o,
                                   to_rows(cache_k_win), to_rows(cache_v_win), k, v, tab_dec, sinks)
            k, v = from_rows(k), from_rows(v)
        (y,) = _ffn(x, g, mods, 1, 2, norm_g3, wg, wu, wd, 1, final_g=final_norm_g.reshape(1, d))
        out[name] = (y, u, k, v)

    y_p, u_p, k_state_p, v_state_p = out["prompt"]
    y_s, u_s, k_state_s, v_state_s = out["sample"]
    tail = CONV_WIDTH - 1
    conv_p = u_p.reshape(batch, SUBLANES, d)[:, SUBLANES - tail:][None]
    conv_s = u_s.reshape(dec_batch, dec_seq, d)[:, dec_seq - tail:][None]
    return (y_p.reshape(batch, seq, d), y_s.reshape(dec_batch, dec_seq, d), conv_p, conv_s,
            k_state_p, v_state_p, k_state_s, v_state_s)
```

```python
import functools

import jax
import jax.numpy as jnp
from jax import lax
from jax.experimental import pallas as pl
from jax.experimental.pallas import tpu as pltpu

F32 = jnp.float32
BF16 = jnp.bfloat16

D_MODEL = 1024
D_FF = 2816
HEAD_DIM = 64
N_HEADS = 16
N_KV_HEADS = 4
GROUP = N_HEADS // N_KV_HEADS
KV_DIM = N_KV_HEADS * HEAD_DIM
WINDOW = 128
ROT_DIM = 16
ROPE_THETA = 500000.0
CONV_WIDTH = 3
N_MOD = 9
EPS = 1e-6
NEG = -1e30
FMAX = float(jnp.finfo(jnp.float32).max)

SUBLANES = 8
LANES = 128
VMEM_LIMIT_BYTES = 60 * 1024 * 1024

FFN_TILE = 512
WEIGHT_STAGE_CHUNKS = 16
MIXER_TILE = 512
ATTN_CHUNK = 256
LOG2E = 1.4426950408889634
SAMPLE_ATTN_SEQS = 16
FF_CHUNKS = ((0, 1024), (1024, 2048), (2048, D_FF))


def _dot(a, b):
    return jnp.dot(a, b, preferred_element_type=F32)


def _dot_nt(a, b):
    return lax.dot_general(a, b, (((1,), (1,)), ((), ())), preferred_element_type=F32)


def _log2(n):
    assert n & (n - 1) == 0, n
    return n.bit_length() - 1


def _silu(x):
    return x * jax.nn.sigmoid(x)


def _unit_rms(x):
    return x * lax.rsqrt(jnp.mean(x * x, axis=-1, keepdims=True) + EPS)


def _repeat_rows(ref, reps):
    n = ref.shape[1]
    return jnp.concatenate(
        [jnp.broadcast_to(ref[i:i + 1, :], (reps, n)) for i in range(ref.shape[0])], axis=0)


def _mod_rows(ref, per_seq_rows):
    if per_seq_rows is None:
        return ref[pl.ds(pl.program_id(0), 1), :]
    return _repeat_rows(ref, per_seq_rows)


def _rope(x, cos_t, sin_up_t, sin_dn_t):
    n = x.shape[1]
    reps = n // LANES
    half = ROT_DIM // 2
    cos = jnp.concatenate([cos_t] * reps, axis=1)
    sin_up = jnp.concatenate([sin_up_t] * reps, axis=1)
    sin_dn = jnp.concatenate([sin_dn_t] * reps, axis=1)
    x_up = pltpu.roll(x, n - half, axis=1)
    x_dn = pltpu.roll(x, half, axis=1)
    return x * cos + x_up * sin_up + x_dn * sin_dn


def _ada_kernel(c_ref, w_ref, b_ref, o_ref):
    a = _silu(c_ref[...]).astype(BF16)
    o_ref[...] = _dot(a, w_ref[...].astype(BF16)) + b_ref[...]


def _ada(c_all, w, b):
    n_layers, _, n = w.shape
    m = c_all.shape[0]
    tn = D_MODEL
    return pl.pallas_call(
        _ada_kernel,
        grid=(n_layers, n // tn),
        in_specs=[
            pl.BlockSpec((m, D_MODEL), lambda l, j: (0, 0)),
            pl.BlockSpec((None, D_MODEL, tn), lambda l, j: (l, 0, j)),
            pl.BlockSpec((None, 1, tn), lambda l, j: (l, 0, j)),
        ],
        out_specs=pl.BlockSpec((None, m, tn), lambda l, j: (l, 0, j)),
        out_shape=jax.ShapeDtypeStruct((n_layers, m, n), F32),
        compiler_params=pltpu.CompilerParams(
            dimension_semantics=("arbitrary", "arbitrary"),
            vmem_limit_bytes=VMEM_LIMIT_BYTES),
        name="adaln_tables",
    )(c_all, w, b.reshape(n_layers, 1, n))


def _rope_t(xt, cos_t, sin_t):
    half = ROT_DIM // 2
    out = []
    for h in range(xt.shape[0] // HEAD_DIM):
        base = h * HEAD_DIM
        x1 = xt[base:base + half, :]
        x2 = xt[base + half:base + ROT_DIM, :]
        out += [x1 * cos_t - x2 * sin_t, x2 * cos_t + x1 * sin_t,
                xt[base + ROT_DIM:base + HEAD_DIM, :]]
    return jnp.concatenate(out, axis=0)


def _stage_bf16(src_hbm, dst_ref, stage_ref, sem, chunk_rows):
    n_chunks = src_hbm.shape[0] // chunk_rows

    def copy(n, slot):
        return pltpu.make_async_copy(
            src_hbm.at[pl.ds(n * chunk_rows, chunk_rows)], stage_ref.at[slot], sem.at[slot])

    copy(0, 0).start()

    def body(n, carry):
        slot = n & 1

        @pl.when(n + 1 < n_chunks)
        def _():
            copy(n + 1, 1 - slot).start()

        copy(n, slot).wait()
        dst_ref[pl.ds(pl.multiple_of(n * chunk_rows, chunk_rows), chunk_rows), :] = (
            stage_ref[slot].astype(BF16))
        return carry

    lax.fori_loop(0, n_chunks, body, 0)


def _ffn_kernel(*refs, n_prompt_tiles, tiles_per_seq, dec_seq, layer, slot, with_kv, final):
    it = iter(refs)
    take = lambda n: [next(it) for _ in range(n)]
    xp_ref, xs_ref = take(2)
    p_mods, s_mods = take(3), take(3)
    ng_ref, wg_hbm, wu_hbm, wd_hbm = take(4)
    if with_kv:
        (kvg_ref,), p_kvmods, s_kvmods = take(1), take(2), take(2)
        wk_ref, wkt_ref, wvt_ref = take(3)
        p_tabs, s_tabs = take(3), take(2)
    if final:
        (fg_ref,) = take(1)
    op_ref, os_ref = take(2)
    if with_kv:
        kp_ref, vtp_ref, kts_ref, vts_ref = take(4)
    wg_s, wu_s, wd_s, stage_in, stage_out, sem, h_scr = take(7)
    if with_kv:
        (hk_scr,) = take(1)

    i = pl.program_id(0)

    @pl.when(i == 0)
    def _():
        _stage_bf16(wg_hbm.at[layer, slot], wg_s, stage_in, sem, stage_in.shape[1])
        _stage_bf16(wu_hbm.at[layer, slot], wu_s, stage_in, sem, stage_in.shape[1])
        _stage_bf16(wd_hbm.at[layer, slot], wd_s, stage_out, sem, stage_out.shape[1])

    is_prompt = i < n_prompt_tiles
    prompt_rows = lambda ref: ref[pl.ds(i // tiles_per_seq, 1), :]
    decode_rows = lambda ref: _repeat_rows(ref, dec_seq)

    def prepare(x_ref, mod_refs, kvmod_refs, rows_of):
        xn = _unit_rms(x_ref[...])
        shift, scale = rows_of(mod_refs[0]), rows_of(mod_refs[1])
        h_scr[...] = ((xn * ng_ref[...]) * (1.0 + scale) + shift).astype(BF16)
        if with_kv:
            shift, scale = rows_of(kvmod_refs[0]), rows_of(kvmod_refs[1])
            hk_scr[...] = ((xn * kvg_ref[...]) * (1.0 + scale) + shift).astype(BF16)

    @pl.when(is_prompt)
    def _():
        prepare(xp_ref, p_mods, p_kvmods if with_kv else None, prompt_rows)

    @pl.when(jnp.logical_not(is_prompt))
    def _():
        prepare(xs_ref, s_mods, s_kvmods if with_kv else None, decode_rows)

    hb = h_scr[...]
    y = None
    for lo, hi in FF_CHUNKS:
        g = _dot(hb, wg_s[:, lo:hi])
        u = _dot(hb, wu_s[:, lo:hi])
        a = (_silu(g) * u).astype(BF16)
        part = _dot(a, wd_s[lo:hi, :])
        y = part if y is None else y + part

    def finish(x_ref, o_ref, gate):
        x_new = x_ref[...] + (0.5 * gate) * y
        o_ref[...] = _unit_rms(x_new) * fg_ref[...] if final else x_new

    @pl.when(is_prompt)
    def _():
        finish(xp_ref, op_ref, prompt_rows(p_mods[2]))
        if with_kv:
            hk = hk_scr[...]
            kp_ref[...] = _rope(_dot(hk, wk_ref[...]), *(t[...] for t in p_tabs))
            vtp_ref[...] = _dot_nt(wvt_ref[...], hk)

    @pl.when(jnp.logical_not(is_prompt))
    def _():
        finish(xs_ref, os_ref, decode_rows(s_mods[2]))
        if with_kv:
            hk = hk_scr[...]
            kts_ref[...] = _rope_t(_dot_nt(wkt_ref[...], hk), *(t[...] for t in s_tabs))
            vts_ref[...] = _dot_nt(wvt_ref[...], hk)


def _resident(shape, index_map):
    return pl.BlockSpec(shape, index_map, pipeline_mode=pl.Buffered(1))


def _mod_spec(layer, col, group):
    if group.per_seq_rows is None:
        return pl.BlockSpec((None, SUBLANES, D_MODEL),
                            lambda b, t: (layer, group.prompt_mod_block, col))
    return pl.BlockSpec((None, group.tile // group.per_seq_rows, D_MODEL),
                        lambda b, t: (layer, t, col))


class _Group:
    def __init__(self, n_seq, seq_len, per_seq_rows, prompt_mod_block, tile):
        self.tile = tile
        self.rows = n_seq * seq_len
        self.per_seq_rows = per_seq_rows
        self.prompt_mod_block = prompt_mod_block
        if per_seq_rows is None:
            self.grid = (n_seq, seq_len // self.tile)
        else:
            self.grid = (1, self.rows // self.tile)
        self.tiles_per_seq = self.grid[1]

    def row_spec(self, width):
        nt = self.grid[1]
        return pl.BlockSpec((self.tile, width), lambda b, t: (b * nt + t, 0))

    def col_spec(self, height):
        nt = self.grid[1]
        return pl.BlockSpec((height, self.tile), lambda b, t: (0, b * nt + t))

    def table_t_spec(self):
        if self.per_seq_rows is None:
            return pl.BlockSpec((ROT_DIM // 2, self.tile), lambda b, t: (0, t))
        return pl.BlockSpec((ROT_DIM // 2, self.tile), lambda b, t: (0, 0))

    def table_spec(self):
        if self.per_seq_rows is None:
            return pl.BlockSpec((self.tile, LANES), lambda b, t: (t, 0))
        return pl.BlockSpec((self.tile, LANES), lambda b, t: (0, 0))


def _ffn(xp, xs, prompt, decode, mods, layer, sub, norm_g, wg, wu, wd, slot, kv=None, final_g=None):
    with_kv = kv is not None
    final = final_g is not None
    tile = prompt.tile
    assert decode.tile == tile
    n_p, n_s = prompt.rows // tile, decode.rows // tile
    tiles_per_seq = prompt.tiles_per_seq
    p_tile = lambda i: jnp.minimum(i, n_p - 1)
    s_tile = lambda i: jnp.maximum(i - n_p, 0)
    n_dec_seq = tile // decode.per_seq_rows

    def mod_specs(table_layer, cols):
        p = [pl.BlockSpec((None, SUBLANES, D_MODEL),
                          functools.partial(lambda i, c: (table_layer, prompt.prompt_mod_block, c), c=c))
             for c in cols]
        s = [pl.BlockSpec((None, n_dec_seq, D_MODEL),
                          functools.partial(lambda i, c: (table_layer, s_tile(i), c), c=c))
             for c in cols]
        return p + s

    p_rows = lambda width: pl.BlockSpec((tile, width), lambda i: (p_tile(i), 0))
    s_rows = lambda width: pl.BlockSpec((tile, width), lambda i: (s_tile(i), 0))
    p_cols = lambda height: pl.BlockSpec((height, tile), lambda i: (0, p_tile(i)))
    s_cols = lambda height: pl.BlockSpec((height, tile), lambda i: (0, s_tile(i)))
    const = lambda shape: _resident(shape, lambda i: (0,) * len(shape))
    hbm = pl.BlockSpec(memory_space=pl.ANY)

    in_specs = [p_rows(D_MODEL), s_rows(D_MODEL)]
    in_specs += mod_specs(layer, [3 * sub + 0, 3 * sub + 1, 3 * sub + 2])
    in_specs += [_resident((None, 1, D_MODEL), lambda i: (layer * 3 + sub, 0, 0)), hbm, hbm, hbm]
    args = [xp, xs] + [mods] * 6 + [norm_g, wg, wu, wd]
    out_specs = [p_rows(D_MODEL), s_rows(D_MODEL)]
    out_shape = [jax.ShapeDtypeStruct((prompt.rows, D_MODEL), F32),
                 jax.ShapeDtypeStruct((decode.rows, D_MODEL), F32)]
    if with_kv:
        in_specs += [const((1, D_MODEL))] + mod_specs(0, [0, 1])
        in_specs += [const((D_MODEL, KV_DIM)), const((KV_DIM, D_MODEL)), const((KV_DIM, D_MODEL))]
        in_specs += [pl.BlockSpec((tile, LANES), lambda i: (p_tile(i) % tiles_per_seq, 0))] * 3
        in_specs += [const((ROT_DIM // 2, tile))] * 2
        args += [kv["norm_g"]] + [kv["mods"]] * 4 + [kv["w_k"], kv["w_k_t"], kv["w_v_t"]]
        args += [*kv["prompt_tables"], *kv["decode_tables_t"]]
        out_specs += [p_rows(KV_DIM), p_cols(KV_DIM), s_cols(KV_DIM), s_cols(KV_DIM)]
        out_shape += [jax.ShapeDtypeStruct((prompt.rows, KV_DIM), F32),
                      jax.ShapeDtypeStruct((KV_DIM, prompt.rows), F32),
                      jax.ShapeDtypeStruct((KV_DIM, decode.rows), F32),
                      jax.ShapeDtypeStruct((KV_DIM, decode.rows), F32)]
    if final:
        in_specs.append(const((1, D_MODEL)))
        args.append(final_g)
    scratch = [
        pltpu.VMEM((D_MODEL, D_FF), BF16), pltpu.VMEM((D_MODEL, D_FF), BF16),
        pltpu.VMEM((D_FF, D_MODEL), BF16),
        pltpu.VMEM((2, D_MODEL // WEIGHT_STAGE_CHUNKS, D_FF), F32),
        pltpu.VMEM((2, D_FF // WEIGHT_STAGE_CHUNKS, D_MODEL), F32),
        pltpu.SemaphoreType.DMA((2,)),
        pltpu.VMEM((tile, D_MODEL), BF16),
    ]
    if with_kv:
        scratch.append(pltpu.VMEM((tile, D_MODEL), BF16))
    return pl.pallas_call(
        functools.partial(_ffn_kernel, n_prompt_tiles=n_p, tiles_per_seq=tiles_per_seq,
                          dec_seq=decode.per_seq_rows, layer=layer, slot=slot,
                          with_kv=with_kv, final=final),
        grid=(n_p + n_s,),
        in_specs=in_specs,
        out_specs=out_specs,
        out_shape=out_shape,
        scratch_shapes=scratch,
        compiler_params=pltpu.CompilerParams(
            dimension_semantics=("arbitrary",),
            vmem_limit_bytes=VMEM_LIMIT_BYTES),
        name=f"ffn_l{layer}_s{sub}",
    )(*args)


def _conv_kernel(*refs, per_seq_rows):
    it = iter(refs)
    x_ref, sh_ref, sc_ref, gt_ref, ng_ref, win_ref, cw_ref, wout_ref = (next(it) for _ in range(8))
    if per_seq_rows is not None:
        st0_ref, st1_ref = next(it), next(it)
    o_ref, u_ref = next(it), next(it)
    if per_seq_rows is None:
        carry_ref = next(it)

    tile = x_ref.shape[0]
    x = x_ref[...]
    h = ((_unit_rms(x) * ng_ref[...]) * (1.0 + _mod_rows(sc_ref, per_seq_rows))
         + _mod_rows(sh_ref, per_seq_rows)).astype(BF16)
    bcv = _dot(h, win_ref[...])
    b_g = bcv[:, :D_MODEL]
    u = bcv[:, D_MODEL:2 * D_MODEL] * bcv[:, 2 * D_MODEL:]

    row = lax.broadcasted_iota(jnp.int32, (tile, 1), 0)
    if per_seq_rows is None:
        @pl.when(pl.program_id(1) == 0)
        def _():
            carry_ref[...] = jnp.zeros_like(carry_ref)
        prev2 = carry_ref[SUBLANES - 2:SUBLANES - 1, :]
        prev1 = carry_ref[SUBLANES - 1:SUBLANES, :]
        t = row
    else:
        prev2 = _repeat_rows(st0_ref, per_seq_rows)
        prev1 = _repeat_rows(st1_ref, per_seq_rows)
        t = row & (per_seq_rows - 1)
    u1 = jnp.where(t == 0, prev1, pltpu.roll(u, 1, axis=0))
    u2 = jnp.where(t == 0, prev2, jnp.where(t == 1, prev1, pltpu.roll(u, 2, axis=0)))
    conv = u2 * cw_ref[0:1, :] + u1 * cw_ref[1:2, :] + u * cw_ref[2:3, :]
    y = _dot((b_g * conv).astype(BF16), wout_ref[...])
    o_ref[...] = x + _mod_rows(gt_ref, per_seq_rows) * y

    if per_seq_rows is None:
        tail = u[tile - SUBLANES:, :]
        carry_ref[...] = tail
        u_ref[...] = tail
    else:
        u_ref[...] = u


def _conv_mixer(x, group, mods, norm_g, w_in, conv_w, w_out, state=None):
    prompt = group.per_seq_rows is None
    in_specs = [
        group.row_spec(D_MODEL),
        _mod_spec(0, 3, group), _mod_spec(0, 4, group), _mod_spec(0, 5, group),
        _resident((None, 1, D_MODEL), lambda b, t: (1, 0, 0)),
        _resident((D_MODEL, 3 * D_MODEL), lambda b, t: (0, 0)),
        _resident((CONV_WIDTH, D_MODEL), lambda b, t: (0, 0)),
        _resident((D_MODEL, D_MODEL), lambda b, t: (0, 0)),
    ]
    args = [x, mods, mods, mods, norm_g, w_in, conv_w, w_out]
    scratch = []
    if prompt:
        u_spec = pl.BlockSpec((SUBLANES, D_MODEL), lambda b, t: (b, 0))
        u_shape = jax.ShapeDtypeStruct((group.grid[0] * SUBLANES, D_MODEL), F32)
        scratch.append(pltpu.VMEM((SUBLANES, D_MODEL), F32))
    else:
        n_seq = group.tile // group.per_seq_rows
        st_spec = pl.BlockSpec((n_seq, D_MODEL), lambda b, t: (t, 0))
        in_specs += [st_spec, st_spec]
        args += [state[0], state[1]]
        u_spec = group.row_spec(D_MODEL)
        u_shape = jax.ShapeDtypeStruct((group.rows, D_MODEL), F32)
    return pl.pallas_call(
        functools.partial(_conv_kernel, per_seq_rows=group.per_seq_rows),
        grid=group.grid,
        in_specs=in_specs,
        out_specs=[group.row_spec(D_MODEL), u_spec],
        out_shape=[jax.ShapeDtypeStruct((group.rows, D_MODEL), F32), u_shape],
        scratch_shapes=scratch,
        compiler_params=pltpu.CompilerParams(
            dimension_semantics=("arbitrary", "arbitrary"),
            vmem_limit_bytes=VMEM_LIMIT_BYTES),
        name="conv_mixer",
    )(*args)


def _sink_column(sink_ref, kv_head, q_rows):
    grp = lax.broadcasted_iota(jnp.int32, (GROUP * q_rows, 1), 0) >> _log2(q_rows)
    col = jnp.full((GROUP * q_rows, 1), sink_ref[kv_head * GROUP + GROUP - 1], F32)
    for g in range(GROUP - 2, -1, -1):
        col = jnp.where(grp == g, sink_ref[kv_head * GROUP + g], col)
    return col


def _stack_heads(q_ref, rows, kv_head):
    return jnp.concatenate(
        [q_ref[rows, (kv_head * GROUP + g) * HEAD_DIM:(kv_head * GROUP + g + 1) * HEAD_DIM]
         for g in range(GROUP)], axis=0)


def _softmax_parts(scores, sink):
    m = sink
    for s in scores:
        m = jnp.maximum(m, jnp.max(s, axis=-1, keepdims=True))
    ps = [jnp.exp(s - m) for s in scores]
    denom = jnp.exp(sink - m)
    for p in ps:
        denom = denom + jnp.sum(p, axis=-1, keepdims=True)
    return ps, 1.0 / denom


def _project_q(x_ref, sh_ref, sc_ref, ng_ref, wq_ref, tables, per_seq_rows):
    h = ((_unit_rms(x_ref[...]) * ng_ref[...]) * (1.0 + _mod_rows(sc_ref, per_seq_rows))
         + _mod_rows(sh_ref, per_seq_rows)).astype(BF16)
    q = _rope(_dot(h, wq_ref[...]), *tables)
    return (q * (HEAD_DIM ** -0.5)).astype(BF16)


def _band_caps():
    shape = (2 * WINDOW, GROUP * WINDOW)
    kj = lax.broadcasted_iota(jnp.int32, shape, 0)
    qi = lax.broadcasted_iota(jnp.int32, shape, 1) & (WINDOW - 1)
    in_band = (kj > qi) & (kj <= qi + WINDOW)
    return (jnp.where(in_band, FMAX, NEG), jnp.where(in_band & (kj >= WINDOW), FMAX, NEG))


def _attn_prompt_kernel(x_ref, sh_ref, sc_ref, gt_ref, ng_ref, wq_ref, wo_ref,
                        kp_ref, kc_ref, vp_ref, vc_ref, cos_ref, sin_ref, sink_ref,
                        o_ref, q_scr, k_scr, v_scr, a_scr, cap_scr, h_scr, at_scr):
    tile = x_ref.shape[0]

    @pl.when((pl.program_id(0) == 0) & (pl.program_id(1) == 0))
    def _():
        general, first = _band_caps()
        cap_scr[0] = general
        cap_scr[1] = first

    k_scr[:WINDOW, :] = kp_ref[...].astype(BF16)
    k_scr[WINDOW:, :] = kc_ref[...].astype(BF16)
    v_scr[:, :WINDOW] = vp_ref[...].astype(BF16)
    v_scr[:, WINDOW:] = vc_ref[...].astype(BF16)
    first_tile = pl.program_id(1) == 0
    norm_gain = ng_ref[...]
    scale1 = 1.0 + _mod_rows(sc_ref, None)
    shift = _mod_rows(sh_ref, None)
    gate = _mod_rows(gt_ref, None)

    group_dims = GROUP * HEAD_DIM

    def chunk_rows(c):
        return slice(c * ATTN_CHUNK, (c + 1) * ATTN_CHUNK)

    def normalize(c):
        rows = chunk_rows(c)
        h_scr[rows, :] = ((_unit_rms(x_ref[rows, :]) * norm_gain) * scale1 + shift).astype(BF16)

    def project_piece(c, kvh):
        rows = chunk_rows(c)
        dims = slice(kvh * group_dims, (kvh + 1) * group_dims)
        qt = _rope_t(_dot(h_scr[rows, :], wq_ref[:, dims]).T, cos_ref[:, rows], sin_ref[:, rows])
        q_scr[dims, rows] = (qt * (HEAD_DIM ** -0.5 * LOG2E)).astype(BF16)

    def attend(blk, kvh):
        cols = slice(blk * WINDOW, (blk + 1) * WINDOW)
        keys = slice(blk * WINDOW, (blk + 2) * WINDOW)
        cap = cap_scr[jnp.where(first_tile, 1, 0)] if blk == 0 else cap_scr[0]
        dims = slice(kvh * HEAD_DIM, (kvh + 1) * HEAD_DIM)
        heads = [slice((kvh * GROUP + g) * HEAD_DIM, (kvh * GROUP + g + 1) * HEAD_DIM)
                 for g in range(GROUP)]
        qg = jnp.concatenate([q_scr[hd, cols] for hd in heads], axis=1)
        s = jnp.minimum(_dot(k_scr[keys, dims], qg), cap)
        sink = LOG2E * jnp.concatenate(
            [jnp.full((1, WINDOW), sink_ref[kvh * GROUP + g], F32) for g in range(GROUP)], axis=1)
        m = jnp.maximum(jnp.max(s, axis=0, keepdims=True), sink)
        p = jnp.exp2(s - m)
        denom = jnp.sum(p, axis=0, keepdims=True) + jnp.exp2(sink - m)
        o = _dot(v_scr[dims, keys], p.astype(BF16)) * (1.0 / denom)
        for g, hd in enumerate(heads):
            a_scr[hd, cols] = o[:, g * WINDOW:(g + 1) * WINDOW]

    def transpose_piece(c, kvh):
        rows = chunk_rows(c)
        dims = slice(kvh * group_dims, (kvh + 1) * group_dims)
        at_scr[rows, dims] = a_scr[dims, rows].T.astype(BF16)

    def emit_piece(c, n):
        rows = chunk_rows(c)
        cols = slice(n * group_dims, (n + 1) * group_dims)
        y = _dot(at_scr[rows, :], wo_ref[:, cols])
        o_ref[rows, cols] = x_ref[rows, cols] + gate[:, cols] * y

    n_chunks = tile // ATTN_CHUNK
    blocks_per_chunk = ATTN_CHUNK // WINDOW
    n_pieces = D_MODEL // group_dims
    normalize(0)
    for kvh in range(N_KV_HEADS):
        project_piece(0, kvh)
    for c in range(n_chunks):
        if c + 1 < n_chunks:
            normalize(c + 1)
        step = 0
        for b in range(blocks_per_chunk):
            for kvh in range(N_KV_HEADS):
                attend(c * blocks_per_chunk + b, kvh)
                if b == blocks_per_chunk - 1:
                    transpose_piece(c, kvh)
                if step % 2 == 0 and c + 1 < n_chunks and step // 2 < N_KV_HEADS:
                    project_piece(c + 1, step // 2)
                if step % 2 == 1 and c > 0 and step // 2 < n_pieces:
                    emit_piece(c - 1, step // 2)
                step += 1
    for n in range(n_pieces):
        emit_piece(n_chunks - 1, n)


def _attn_prompt(x, group, mods, norm_g, w_q, w_o, k, v_t, tables_t, sinks):
    tile = group.tile
    nt = group.grid[1]
    blocks_per_tile = tile // WINDOW

    def prev_block(b, t):
        return jnp.maximum((b * nt + t) * blocks_per_tile - 1, 0)

    in_specs = [
        group.row_spec(D_MODEL),
        _mod_spec(1, 3, group), _mod_spec(1, 4, group), _mod_spec(1, 5, group),
        _resident((None, 1, D_MODEL), lambda b, t: (4, 0, 0)),
        _resident((D_MODEL, D_MODEL), lambda b, t: (0, 0)),
        _resident((D_MODEL, D_MODEL), lambda b, t: (0, 0)),
        pl.BlockSpec((WINDOW, KV_DIM), lambda b, t: (prev_block(b, t), 0)), group.row_spec(KV_DIM),
        pl.BlockSpec((KV_DIM, WINDOW), lambda b, t: (0, prev_block(b, t))), group.col_spec(KV_DIM),
        group.table_t_spec(), group.table_t_spec(),
        pl.BlockSpec(memory_space=pltpu.SMEM),
    ]
    return pl.pallas_call(
        _attn_prompt_kernel,
        grid=group.grid,
        in_specs=in_specs,
        out_specs=group.row_spec(D_MODEL),
        out_shape=jax.ShapeDtypeStruct((group.rows, D_MODEL), F32),
        scratch_shapes=[
            pltpu.VMEM((D_MODEL, tile), BF16),
            pltpu.VMEM((tile + WINDOW, KV_DIM), BF16),
            pltpu.VMEM((KV_DIM, tile + WINDOW), BF16),
            pltpu.VMEM((D_MODEL, tile), F32),
            pltpu.VMEM((2, 2 * WINDOW, GROUP * WINDOW), F32),
            pltpu.VMEM((tile, D_MODEL), BF16),
            pltpu.VMEM((tile, D_MODEL), BF16),
        ],
        compiler_params=pltpu.CompilerParams(
            dimension_semantics=("arbitrary", "arbitrary"),
            vmem_limit_bytes=VMEM_LIMIT_BYTES),
        name="attn_prompt",
    )(x, mods, mods, mods, norm_g, w_q, w_o, k, k, v_t, v_t, *tables_t, sinks)


def _attn_sample_kernel(x_ref, sh_ref, sc_ref, gt_ref, ng_ref, wq_ref, wo_ref,
                        kc_ref, vc_ref, kn_ref, vn_ref, cos_ref, sup_ref, sdn_ref, sink_ref,
                        o_ref, kout_ref, vout_ref, q_scr, *, dec_seq):
    rows = x_ref.shape[0]
    n_seq = rows // dec_seq
    n_cache = n_seq * WINDOW
    q_scr[...] = _project_q(x_ref, sh_ref, sc_ref, ng_ref, wq_ref,
                            (cos_ref[...], sup_ref[...], sdn_ref[...]), dec_seq)

    def all_windows(ref, kvh):
        return jnp.concatenate(
            [ref[(s * N_KV_HEADS + kvh) * HEAD_DIM:(s * N_KV_HEADS + kvh + 1) * HEAD_DIM, :].astype(BF16)
             for s in range(n_seq)], axis=1)

    q_row = lax.broadcasted_iota(jnp.int32, (GROUP * rows, 1), 0) & (rows - 1)
    q_seq, q_i = q_row >> _log2(dec_seq), q_row & (dec_seq - 1)
    c_col = lax.broadcasted_iota(jnp.int32, (1, n_cache), 1)
    valid_c = (q_seq == (c_col >> _log2(WINDOW))) & ((c_col & (WINDOW - 1)) > q_i)
    n_col = lax.broadcasted_iota(jnp.int32, (1, rows), 1)
    valid_n = (q_seq == (n_col >> _log2(dec_seq))) & ((n_col & (dec_seq - 1)) <= q_i)

    heads = []
    for kvh in range(N_KV_HEADS):
        dims = slice(kvh * HEAD_DIM, (kvh + 1) * HEAD_DIM)
        qg = _stack_heads(q_scr, slice(None), kvh)
        s_c = jnp.where(valid_c, _dot(qg, all_windows(kc_ref, kvh)), NEG)
        s_n = jnp.where(valid_n, _dot(qg, kn_ref[dims, :].astype(BF16)), NEG)
        (p_c, p_n), inv = _softmax_parts([s_c, s_n], _sink_column(sink_ref, kvh, rows))
        o = (_dot_nt(p_c.astype(BF16), all_windows(vc_ref, kvh))
             + _dot_nt(p_n.astype(BF16), vn_ref[dims, :].astype(BF16))) * inv
        heads += [o[g * rows:(g + 1) * rows, :] for g in range(GROUP)]
    attn = jnp.concatenate(heads, axis=1).astype(BF16)
    o_ref[...] = x_ref[...] + _mod_rows(gt_ref, dec_seq) * _dot(attn, wo_ref[...])

    keep = lax.broadcasted_iota(jnp.int32, (1, WINDOW), 1) < WINDOW - dec_seq
    for src_ref, new_ref, dst_ref in ((kc_ref, kn_ref, kout_ref), (vc_ref, vn_ref, vout_ref)):
        new = new_ref[...]
        for s in range(n_seq):
            blk = slice(s * KV_DIM, (s + 1) * KV_DIM)
            old = pltpu.roll(src_ref[blk, :], WINDOW - dec_seq, axis=1)
            app = pltpu.roll(new, (WINDOW - dec_seq - s * dec_seq) % rows, axis=1)
            dst_ref[blk, :] = jnp.where(keep, old, app)


def _attn_sample(x, n_seq_total, dec_seq, mods, norm_g, w_q, w_o, k_cache, v_cache,
                 k_new, v_new, tables, sinks):
    n_seq = SAMPLE_ATTN_SEQS
    rows = n_seq * dec_seq
    assert rows == WINDOW == LANES
    grid = (n_seq_total // n_seq,)
    row_spec = lambda width: pl.BlockSpec((rows, width), lambda t: (t, 0))
    mod_spec = lambda col: pl.BlockSpec((None, n_seq, D_MODEL), lambda t: (1, t, col))
    cache_spec = pl.BlockSpec((n_seq * KV_DIM, WINDOW), lambda t: (t, 0))
    new_spec = pl.BlockSpec((KV_DIM, rows), lambda t: (0, t))
    table_spec = pl.BlockSpec((rows, LANES), lambda t: (0, 0))
    in_specs = [
        row_spec(D_MODEL), mod_spec(3), mod_spec(4), mod_spec(5),
        _resident((None, 1, D_MODEL), lambda t: (4, 0, 0)),
        _resident((D_MODEL, D_MODEL), lambda t: (0, 0)),
        _resident((D_MODEL, D_MODEL), lambda t: (0, 0)),
        cache_spec, cache_spec, new_spec, new_spec,
        table_spec, table_spec, table_spec,
        pl.BlockSpec(memory_space=pltpu.SMEM),
    ]
    cache_shape = jax.ShapeDtypeStruct((n_seq_total * KV_DIM, WINDOW), F32)
    return pl.pallas_call(
        functools.partial(_attn_sample_kernel, dec_seq=dec_seq),
        grid=grid,
        in_specs=in_specs,
        out_specs=[row_spec(D_MODEL), cache_spec, cache_spec],
        out_shape=[jax.ShapeDtypeStruct((n_seq_total * dec_seq, D_MODEL), F32),
                   cache_shape, cache_shape],
        scratch_shapes=[pltpu.VMEM((rows, D_MODEL), BF16)],
        compiler_params=pltpu.CompilerParams(
            dimension_semantics=("arbitrary",),
            vmem_limit_bytes=VMEM_LIMIT_BYTES),
        name="attn_sample",
    )(x, mods, mods, mods, norm_g, w_q, w_o, k_cache, v_cache, k_new, v_new, *tables, sinks)


def _rope_angles(pos):
    inv_freq = ROPE_THETA ** (-jnp.arange(0, ROT_DIM, 2, dtype=F32) / ROT_DIM)
    ang = pos.astype(F32)[:, None] * inv_freq[None, :]
    return jnp.cos(ang), jnp.sin(ang)


def _rope_tables(pos):
    half = ROT_DIM // 2
    cos, sin = _rope_angles(pos)
    n = pos.shape[0]
    rest = HEAD_DIM - ROT_DIM
    zeros = jnp.zeros((n, half), F32)
    cos_t = jnp.concatenate([cos, cos, jnp.ones((n, rest), F32)], axis=1)
    sin_up = jnp.concatenate([-sin, zeros, jnp.zeros((n, rest), F32)], axis=1)
    sin_dn = jnp.concatenate([zeros, sin, jnp.zeros((n, rest), F32)], axis=1)
    reps = LANES // HEAD_DIM
    return tuple(jnp.tile(t, (1, reps)) for t in (cos_t, sin_up, sin_dn))


def kernel(x_prompt, x_sample, state_conv, cache_k_win, cache_v_win, c_prompt, c_sample, norm_g, w_ada, b_ada, w_ffn_gate, w_ffn_up, w_ffn_down, conv_w_in, conv_w, conv_w_out, kv_norm_g, w_ada_kv, b_ada_kv, w_k, w_v, attn_w_q, attn_sinks, attn_w_o, final_norm_g):
    batch, seq, d = x_prompt.shape
    dec_batch, dec_seq, _ = x_sample.shape
    w_buf = cache_k_win.shape[1]
    past_len = 16384
    assert d == D_MODEL and w_buf == WINDOW and dec_seq == SUBLANES
    assert w_ada.shape[0] == 2 and conv_w_in.shape[0] == 1 and attn_w_q.shape[0] == 1

    pad = (-(dec_batch + batch)) % SUBLANES
    c_all = jnp.concatenate([c_sample, c_prompt, jnp.zeros((pad, d), F32)], axis=0)
    mods = _ada(c_all, w_ada, b_ada)
    mods_kv = _ada(c_all, w_ada_kv[None], b_ada_kv[None])
    prompt_mod_block = dec_batch // SUBLANES

    make_groups = lambda tile: {
        "prompt": _Group(batch, seq, None, prompt_mod_block, tile),
        "sample": _Group(dec_batch, dec_seq, dec_seq, prompt_mod_block, tile),
    }
    ffn_groups, mixer_groups = make_groups(FFN_TILE), make_groups(MIXER_TILE)
    xp, xs = x_prompt.reshape(batch * seq, d), x_sample.reshape(dec_batch * dec_seq, d)

    norm_g3 = norm_g.reshape(-1, 1, d)
    w_in, w_out = conv_w_in[0].astype(BF16), conv_w_out[0].astype(BF16)
    w_q, w_o = attn_w_q[0].astype(BF16), attn_w_o[0].astype(BF16)
    sinks = attn_sinks[0]

    pos_dec = past_len + jnp.arange(dec_seq, dtype=jnp.int32)
    pos_prompt = jnp.arange(seq, dtype=jnp.int32)
    tab_prompt = _rope_tables(pos_prompt)
    tab_dec = tuple(jnp.tile(t, (SAMPLE_ATTN_SEQS, 1)) for t in _rope_tables(pos_dec))
    tab_prompt_t = tuple(t.T for t in _rope_angles(pos_prompt))
    tab_dec_t = tuple(jnp.tile(t.T, (1, FFN_TILE // dec_seq)) for t in _rope_angles(pos_dec))
    kv_args = dict(norm_g=kv_norm_g.reshape(1, d), mods=mods_kv, w_k=w_k.astype(BF16),
                   w_k_t=w_k.T.astype(BF16), w_v_t=w_v.T.astype(BF16),
                   prompt_tables=tab_prompt, decode_tables_t=tab_dec_t)
    to_rows = lambda c: jnp.transpose(c, (0, 2, 3, 1)).reshape(dec_batch * KV_DIM, w_buf)
    from_rows = lambda c: jnp.transpose(c.reshape(dec_batch, N_KV_HEADS, HEAD_DIM, w_buf), (0, 3, 1, 2))

    ffn = functools.partial(_ffn, prompt=ffn_groups["prompt"], decode=ffn_groups["sample"], mods=mods,
                            norm_g=norm_g3, wg=w_ffn_gate, wu=w_ffn_up, wd=w_ffn_down)
    xp, xs = ffn(xp, xs, layer=0, sub=0, slot=0)
    xp, u_p = _conv_mixer(xp, mixer_groups["prompt"], mods, norm_g3, w_in, conv_w[0], w_out)
    xs, u_s = _conv_mixer(xs, mixer_groups["sample"], mods, norm_g3, w_in, conv_w[0], w_out,
                          (state_conv[0, :, 0, :], state_conv[0, :, 1, :]))
    xp, xs = ffn(xp, xs, layer=0, sub=2, slot=1)
    xp, xs, k_p, vt_p, kt_s, vt_s = ffn(xp, xs, layer=1, sub=0, slot=0, kv=kv_args)
    xp = _attn_prompt(xp, mixer_groups["prompt"], mods, norm_g3, w_q, w_o, k_p, vt_p, tab_prompt_t, sinks)
    xs, k_win, v_win = _attn_sample(xs, dec_batch, dec_seq, mods, norm_g3, w_q, w_o,
                                    to_rows(cache_k_win), to_rows(cache_v_win), kt_s, vt_s, tab_dec, sinks)
    y_p, y_s = ffn(xp, xs, layer=1, sub=2, slot=1, final_g=final_norm_g.reshape(1, d))

    k_state_p = k_p.reshape(batch, seq, KV_DIM)[:, seq - WINDOW:].reshape(
        batch, WINDOW, N_KV_HEADS, HEAD_DIM)
    v_state_p = jnp.stack([vt_p[:, (b + 1) * seq - WINDOW:(b + 1) * seq] for b in range(batch)])
    v_state_p = jnp.transpose(v_state_p.reshape(batch, N_KV_HEADS, HEAD_DIM, WINDOW), (0, 3, 1, 2))
    k_state_s, v_state_s = from_rows(k_win), from_rows(v_win)
    tail = CONV_WIDTH - 1
    conv_p = u_p.reshape(batch, SUBLANES, d)[:, SUBLANES - tail:][None]
    conv_s = u_s.reshape(dec_batch, dec_seq, d)[:, dec_seq - tail:][None]
    return (y_p.reshape(batch, seq, d), y_s.reshape(dec_batch, dec_seq, d), conv_p, conv_s,
            k_state_p, v_state_p, k_state_s, v_state_s)
```

```python
import functools

import jax
import jax.numpy as jnp
from jax import lax
from jax.experimental import pallas as pl
from jax.experimental.pallas import tpu as pltpu

F32 = jnp.float32
BF16 = jnp.bfloat16

D_MODEL = 1024
D_FF = 2816
HEAD_DIM = 64
N_HEADS = 16
N_KV_HEADS = 4
GROUP = N_HEADS // N_KV_HEADS
KV_DIM = N_KV_HEADS * HEAD_DIM
WINDOW = 128
ROT_DIM = 16
ROPE_THETA = 500000.0
CONV_WIDTH = 3
N_MOD = 9
EPS = 1e-6
NEG = -1e30
FMAX = float(jnp.finfo(jnp.float32).max)

SUBLANES = 8
LANES = 128
VMEM_LIMIT_BYTES = 60 * 1024 * 1024

FFN_TILE = 512
PROMPT_MOD_COPIES = FFN_TILE // SUBLANES
WEIGHT_STAGE_CHUNKS = 16
MIXER_TILE = 512
ATTN_CHUNK = 256
LOG2E = 1.4426950408889634
SAMPLE_ATTN_SEQS = 16
FF_CHUNKS = ((0, 1024), (1024, 2048), (2048, D_FF))


def _dot(a, b):
    return jnp.dot(a, b, preferred_element_type=F32)


def _dot_nt(a, b):
    return lax.dot_general(a, b, (((1,), (1,)), ((), ())), preferred_element_type=F32)


def _log2(n):
    assert n & (n - 1) == 0, n
    return n.bit_length() - 1


def _silu(x):
    return x * jax.nn.sigmoid(x)


def _unit_rms(x):
    return x * lax.rsqrt(jnp.mean(x * x, axis=-1, keepdims=True) + EPS)


def _repeat_rows(ref, reps):
    n = ref.shape[1]
    return jnp.concatenate(
        [jnp.broadcast_to(ref[i:i + 1, :], (reps, n)) for i in range(ref.shape[0])], axis=0)


def _mod_rows(ref, per_seq_rows):
    if per_seq_rows is None:
        return ref[0:1, :]
    return _repeat_rows(ref, per_seq_rows)


def _rope(x, cos_t, sin_up_t, sin_dn_t):
    n = x.shape[1]
    reps = n // LANES
    half = ROT_DIM // 2
    cos = jnp.concatenate([cos_t] * reps, axis=1)
    sin_up = jnp.concatenate([sin_up_t] * reps, axis=1)
    sin_dn = jnp.concatenate([sin_dn_t] * reps, axis=1)
    x_up = pltpu.roll(x, n - half, axis=1)
    x_dn = pltpu.roll(x, half, axis=1)
    return x * cos + x_up * sin_up + x_dn * sin_dn


def _ada_kernel(c_ref, w_ref, b_ref, o_ref):
    a = _silu(c_ref[...]).astype(BF16)
    o_ref[...] = _dot(a, w_ref[...].astype(BF16)) + b_ref[...]


def _ada(c_all, w, b):
    n_layers, _, n = w.shape
    m = c_all.shape[0]
    tn = D_MODEL
    return pl.pallas_call(
        _ada_kernel,
        grid=(n_layers, n // tn),
        in_specs=[
            pl.BlockSpec((m, D_MODEL), lambda l, j: (0, 0)),
            pl.BlockSpec((None, D_MODEL, tn), lambda l, j: (l, 0, j)),
            pl.BlockSpec((None, 1, tn), lambda l, j: (l, 0, j)),
        ],
        out_specs=pl.BlockSpec((None, m, tn), lambda l, j: (l, 0, j)),
        out_shape=jax.ShapeDtypeStruct((n_layers, m, n), F32),
        compiler_params=pltpu.CompilerParams(
            dimension_semantics=("arbitrary", "arbitrary"),
            vmem_limit_bytes=VMEM_LIMIT_BYTES),
        name="adaln_tables",
    )(c_all, w, b.reshape(n_layers, 1, n))


def _rope_t(xt, cos_t, sin_t):
    half = ROT_DIM // 2
    out = []
    for h in range(xt.shape[0] // HEAD_DIM):
        base = h * HEAD_DIM
        x1 = xt[base:base + half, :]
        x2 = xt[base + half:base + ROT_DIM, :]
        out += [x1 * cos_t - x2 * sin_t, x2 * cos_t + x1 * sin_t,
                xt[base + ROT_DIM:base + HEAD_DIM, :]]
    return jnp.concatenate(out, axis=0)


def _stage_bf16(src_hbm, dst_ref, stage_ref, sem, chunk_rows):
    n_chunks = src_hbm.shape[0] // chunk_rows

    def copy(n, slot):
        return pltpu.make_async_copy(
            src_hbm.at[pl.ds(n * chunk_rows, chunk_rows)], stage_ref.at[slot], sem.at[slot])

    copy(0, 0).start()

    def body(n, carry):
        slot = n & 1

        @pl.when(n + 1 < n_chunks)
        def _():
            copy(n + 1, 1 - slot).start()

        copy(n, slot).wait()
        dst_ref[pl.ds(pl.multiple_of(n * chunk_rows, chunk_rows), chunk_rows), :] = (
            stage_ref[slot].astype(BF16))
        return carry

    lax.fori_loop(0, n_chunks, body, 0)


def _ffn_kernel(*refs, n_prompt_tiles, tiles_per_seq, dec_seq, layer, slot, with_kv, final):
    it = iter(refs)
    take = lambda n: [next(it) for _ in range(n)]
    xp_ref, xs_ref = take(2)
    sh_ref, sc_ref, gt_ref = take(3)
    ng_ref, wg_hbm, wu_hbm, wd_hbm = take(4)
    if with_kv:
        kvg_ref, kvsh_ref, kvsc_ref, wk_ref, wkt_ref, wvt_ref = take(6)
        p_tabs, s_tabs = take(3), take(2)
    if final:
        (fg_ref,) = take(1)
    op_ref, os_ref = take(2)
    if with_kv:
        kp_ref, vtp_ref, kts_ref, vts_ref = take(4)
    wg_s, wu_s, wd_s, stage_in, stage_out, sem = take(6)

    i = pl.program_id(0)

    @pl.when(i == 0)
    def _():
        _stage_bf16(wg_hbm.at[layer, slot], wg_s, stage_in, sem, stage_in.shape[1])
        _stage_bf16(wu_hbm.at[layer, slot], wu_s, stage_in, sem, stage_in.shape[1])
        _stage_bf16(wd_hbm.at[layer, slot], wd_s, stage_out, sem, stage_out.shape[1])

    is_prompt = i < n_prompt_tiles
    x = jnp.where(is_prompt, xp_ref[...], xs_ref[...])
    xn = _unit_rms(x)
    hb = ((xn * ng_ref[...]) * (1.0 + _repeat_rows(sc_ref, dec_seq))
          + _repeat_rows(sh_ref, dec_seq)).astype(BF16)
    y = None
    for lo, hi in FF_CHUNKS:
        g = _dot(hb, wg_s[:, lo:hi])
        u = _dot(hb, wu_s[:, lo:hi])
        a = (_silu(g) * u).astype(BF16)
        part = _dot(a, wd_s[lo:hi, :])
        y = part if y is None else y + part
    x_new = x + (0.5 * _repeat_rows(gt_ref, dec_seq)) * y
    if final:
        x_new = _unit_rms(x_new) * fg_ref[...]

    if with_kv:
        hk = ((xn * kvg_ref[...]) * (1.0 + _repeat_rows(kvsc_ref, dec_seq))
              + _repeat_rows(kvsh_ref, dec_seq)).astype(BF16)
        k = _rope(_dot(hk, wk_ref[...]), *(t[...] for t in p_tabs))
        k_t = _rope_t(_dot_nt(wkt_ref[...], hk), *(t[...] for t in s_tabs))
        v_t = _dot_nt(wvt_ref[...], hk)

    @pl.when(is_prompt)
    def _():
        op_ref[...] = x_new
        if with_kv:
            kp_ref[...] = k
            vtp_ref[...] = v_t

    @pl.when(jnp.logical_not(is_prompt))
    def _():
        os_ref[...] = x_new
        if with_kv:
            kts_ref[...] = k_t
            vts_ref[...] = v_t


def _resident(shape, index_map):
    return pl.BlockSpec(shape, index_map, pipeline_mode=pl.Buffered(1))


def _mod_spec(layer, col, group):
    if group.per_seq_rows is None:
        return pl.BlockSpec((None, SUBLANES, D_MODEL),
                            lambda b, t: (layer, group.prompt_mod_block(b), col))
    return pl.BlockSpec((None, group.tile // group.per_seq_rows, D_MODEL),
                        lambda b, t: (layer, t, col))


class _Group:
    def __init__(self, n_seq, seq_len, per_seq_rows, prompt_mod_row0, tile):
        self.tile = tile
        self.rows = n_seq * seq_len
        self.per_seq_rows = per_seq_rows
        self.prompt_mod_block = lambda b: (prompt_mod_row0 + b * PROMPT_MOD_COPIES) // SUBLANES
        if per_seq_rows is None:
            self.grid = (n_seq, seq_len // self.tile)
        else:
            self.grid = (1, self.rows // self.tile)
        self.tiles_per_seq = self.grid[1]

    def row_spec(self, width):
        nt = self.grid[1]
        return pl.BlockSpec((self.tile, width), lambda b, t: (b * nt + t, 0))

    def col_spec(self, height):
        nt = self.grid[1]
        return pl.BlockSpec((height, self.tile), lambda b, t: (0, b * nt + t))

    def table_t_spec(self):
        if self.per_seq_rows is None:
            return pl.BlockSpec((ROT_DIM // 2, self.tile), lambda b, t: (0, t))
        return pl.BlockSpec((ROT_DIM // 2, self.tile), lambda b, t: (0, 0))

    def table_spec(self):
        if self.per_seq_rows is None:
            return pl.BlockSpec((self.tile, LANES), lambda b, t: (t, 0))
        return pl.BlockSpec((self.tile, LANES), lambda b, t: (0, 0))


def _ffn(xp, xs, prompt, decode, mods, layer, sub, norm_g, wg, wu, wd, slot, kv=None, final_g=None):
    with_kv = kv is not None
    final = final_g is not None
    tile = prompt.tile
    assert decode.tile == tile
    n_p, n_s = prompt.rows // tile, decode.rows // tile
    tiles_per_seq = prompt.tiles_per_seq
    p_tile = lambda i: jnp.minimum(i, n_p - 1)
    s_tile = lambda i: jnp.maximum(i - n_p, 0)
    n_mod_rows = tile // decode.per_seq_rows
    assert n_mod_rows == PROMPT_MOD_COPIES

    def mod_block(i):
        return jnp.where(i < n_p, n_s + i // tiles_per_seq, i - n_p)

    def mod_specs(table_layer, cols):
        return [pl.BlockSpec((None, n_mod_rows, D_MODEL),
                             functools.partial(lambda i, c: (table_layer, mod_block(i), c), c=c))
                for c in cols]

    p_rows = lambda width: pl.BlockSpec((tile, width), lambda i: (p_tile(i), 0))
    s_rows = lambda width: pl.BlockSpec((tile, width), lambda i: (s_tile(i), 0))
    p_cols = lambda height: pl.BlockSpec((height, tile), lambda i: (0, p_tile(i)))
    s_cols = lambda height: pl.BlockSpec((height, tile), lambda i: (0, s_tile(i)))
    const = lambda shape: _resident(shape, lambda i: (0,) * len(shape))
    hbm = pl.BlockSpec(memory_space=pl.ANY)

    in_specs = [p_rows(D_MODEL), s_rows(D_MODEL)]
    in_specs += mod_specs(layer, [3 * sub + 0, 3 * sub + 1, 3 * sub + 2])
    in_specs += [_resident((None, 1, D_MODEL), lambda i: (layer * 3 + sub, 0, 0)), hbm, hbm, hbm]
    args = [xp, xs] + [mods] * 3 + [norm_g, wg, wu, wd]
    out_specs = [p_rows(D_MODEL), s_rows(D_MODEL)]
    out_shape = [jax.ShapeDtypeStruct((prompt.rows, D_MODEL), F32),
                 jax.ShapeDtypeStruct((decode.rows, D_MODEL), F32)]
    if with_kv:
        in_specs += [const((1, D_MODEL))] + mod_specs(0, [0, 1])
        in_specs += [const((D_MODEL, KV_DIM)), const((KV_DIM, D_MODEL)), const((KV_DIM, D_MODEL))]
        in_specs += [pl.BlockSpec((tile, LANES), lambda i: (p_tile(i) % tiles_per_seq, 0))] * 3
        in_specs += [const((ROT_DIM // 2, tile))] * 2
        args += [kv["norm_g"]] + [kv["mods"]] * 2 + [kv["w_k"], kv["w_k_t"], kv["w_v_t"]]
        args += [*kv["prompt_tables"], *kv["decode_tables_t"]]
        out_specs += [p_rows(KV_DIM), p_cols(KV_DIM), s_cols(KV_DIM), s_cols(KV_DIM)]
        out_shape += [jax.ShapeDtypeStruct((prompt.rows, KV_DIM), F32),
                      jax.ShapeDtypeStruct((KV_DIM, prompt.rows), F32),
                      jax.ShapeDtypeStruct((KV_DIM, decode.rows), F32),
                      jax.ShapeDtypeStruct((KV_DIM, decode.rows), F32)]
    if final:
        in_specs.append(const((1, D_MODEL)))
        args.append(final_g)
    scratch = [
        pltpu.VMEM((D_MODEL, D_FF), BF16), pltpu.VMEM((D_MODEL, D_FF), BF16),
        pltpu.VMEM((D_FF, D_MODEL), BF16),
        pltpu.VMEM((2, D_MODEL // WEIGHT_STAGE_CHUNKS, D_FF), F32),
        pltpu.VMEM((2, D_FF // WEIGHT_STAGE_CHUNKS, D_MODEL), F32),
        pltpu.SemaphoreType.DMA((2,)),
    ]
    return pl.pallas_call(
        functools.partial(_ffn_kernel, n_prompt_tiles=n_p, tiles_per_seq=tiles_per_seq,
                          dec_seq=decode.per_seq_rows, layer=layer, slot=slot,
                          with_kv=with_kv, final=final),
        grid=(n_p + n_s,),
        in_specs=in_specs,
        out_specs=out_specs,
        out_shape=out_shape,
        scratch_shapes=scratch,
        compiler_params=pltpu.CompilerParams(
            dimension_semantics=("arbitrary",),
            vmem_limit_bytes=VMEM_LIMIT_BYTES),
        name=f"ffn_l{layer}_s{sub}",
    )(*args)


def _conv_kernel(*refs, per_seq_rows):
    it = iter(refs)
    x_ref, sh_ref, sc_ref, gt_ref, ng_ref, win_ref, cw_ref, wout_ref = (next(it) for _ in range(8))
    if per_seq_rows is not None:
        st0_ref, st1_ref = next(it), next(it)
    o_ref, u_ref = next(it), next(it)
    if per_seq_rows is None:
        carry_ref = next(it)

    tile = x_ref.shape[0]
    x = x_ref[...]
    h = ((_unit_rms(x) * ng_ref[...]) * (1.0 + _mod_rows(sc_ref, per_seq_rows))
         + _mod_rows(sh_ref, per_seq_rows)).astype(BF16)
    bcv = _dot(h, win_ref[...])
    b_g = bcv[:, :D_MODEL]
    u = bcv[:, D_MODEL:2 * D_MODEL] * bcv[:, 2 * D_MODEL:]

    row = lax.broadcasted_iota(jnp.int32, (tile, 1), 0)
    if per_seq_rows is None:
        @pl.when(pl.program_id(1) == 0)
        def _():
            carry_ref[...] = jnp.zeros_like(carry_ref)
        prev2 = carry_ref[SUBLANES - 2:SUBLANES - 1, :]
        prev1 = carry_ref[SUBLANES - 1:SUBLANES, :]
        t = row
    else:
        prev2 = _repeat_rows(st0_ref, per_seq_rows)
        prev1 = _repeat_rows(st1_ref, per_seq_rows)
        t = row & (per_seq_rows - 1)
    u1 = jnp.where(t == 0, prev1, pltpu.roll(u, 1, axis=0))
    u2 = jnp.where(t == 0, prev2, jnp.where(t == 1, prev1, pltpu.roll(u, 2, axis=0)))
    conv = u2 * cw_ref[0:1, :] + u1 * cw_ref[1:2, :] + u * cw_ref[2:3, :]
    y = _dot((b_g * conv).astype(BF16), wout_ref[...])
    o_ref[...] = x + _mod_rows(gt_ref, per_seq_rows) * y

    if per_seq_rows is None:
        tail = u[tile - SUBLANES:, :]
        carry_ref[...] = tail
        u_ref[...] = tail
    else:
        u_ref[...] = u


def _conv_mixer(x, group, mods, norm_g, w_in, conv_w, w_out, state=None):
    prompt = group.per_seq_rows is None
    in_specs = [
        group.row_spec(D_MODEL),
        _mod_spec(0, 3, group), _mod_spec(0, 4, group), _mod_spec(0, 5, group),
        _resident((None, 1, D_MODEL), lambda b, t: (1, 0, 0)),
        _resident((D_MODEL, 3 * D_MODEL), lambda b, t: (0, 0)),
        _resident((CONV_WIDTH, D_MODEL), lambda b, t: (0, 0)),
        _resident((D_MODEL, D_MODEL), lambda b, t: (0, 0)),
    ]
    args = [x, mods, mods, mods, norm_g, w_in, conv_w, w_out]
    scratch = []
    if prompt:
        u_spec = pl.BlockSpec((SUBLANES, D_MODEL), lambda b, t: (b, 0))
        u_shape = jax.ShapeDtypeStruct((group.grid[0] * SUBLANES, D_MODEL), F32)
        scratch.append(pltpu.VMEM((SUBLANES, D_MODEL), F32))
    else:
        n_seq = group.tile // group.per_seq_rows
        st_spec = pl.BlockSpec((n_seq, D_MODEL), lambda b, t: (t, 0))
        in_specs += [st_spec, st_spec]
        args += [state[0], state[1]]
        u_spec = group.row_spec(D_MODEL)
        u_shape = jax.ShapeDtypeStruct((group.rows, D_MODEL), F32)
    return pl.pallas_call(
        functools.partial(_conv_kernel, per_seq_rows=group.per_seq_rows),
        grid=group.grid,
        in_specs=in_specs,
        out_specs=[group.row_spec(D_MODEL), u_spec],
        out_shape=[jax.ShapeDtypeStruct((group.rows, D_MODEL), F32), u_shape],
        scratch_shapes=scratch,
        compiler_params=pltpu.CompilerParams(
            dimension_semantics=("arbitrary", "arbitrary"),
            vmem_limit_bytes=VMEM_LIMIT_BYTES),
        name="conv_mixer",
    )(*args)


def _sink_column(sink_ref, kv_head, q_rows):
    grp = lax.broadcasted_iota(jnp.int32, (GROUP * q_rows, 1), 0) >> _log2(q_rows)
    col = jnp.full((GROUP * q_rows, 1), sink_ref[kv_head * GROUP + GROUP - 1], F32)
    for g in range(GROUP - 2, -1, -1):
        col = jnp.where(grp == g, sink_ref[kv_head * GROUP + g], col)
    return col


def _stack_heads(q_ref, rows, kv_head):
    return jnp.concatenate(
        [q_ref[rows, (kv_head * GROUP + g) * HEAD_DIM:(kv_head * GROUP + g + 1) * HEAD_DIM]
         for g in range(GROUP)], axis=0)


def _softmax_parts(scores, sink):
    m = sink
    for s in scores:
        m = jnp.maximum(m, jnp.max(s, axis=-1, keepdims=True))
    ps = [jnp.exp(s - m) for s in scores]
    denom = jnp.exp(sink - m)
    for p in ps:
        denom = denom + jnp.sum(p, axis=-1, keepdims=True)
    return ps, 1.0 / denom


def _project_q(x_ref, sh_ref, sc_ref, ng_ref, wq_ref, tables, per_seq_rows):
    h = ((_unit_rms(x_ref[...]) * ng_ref[...]) * (1.0 + _mod_rows(sc_ref, per_seq_rows))
         + _mod_rows(sh_ref, per_seq_rows)).astype(BF16)
    q = _rope(_dot(h, wq_ref[...]), *tables)
    return (q * (HEAD_DIM ** -0.5)).astype(BF16)


def _band_caps():
    shape = (2 * WINDOW, GROUP * WINDOW)
    kj = lax.broadcasted_iota(jnp.int32, shape, 0)
    qi = lax.broadcasted_iota(jnp.int32, shape, 1) & (WINDOW - 1)
    in_band = (kj > qi) & (kj <= qi + WINDOW)
    return (jnp.where(in_band, FMAX, NEG), jnp.where(in_band & (kj >= WINDOW), FMAX, NEG))


def _attn_prompt_kernel(x_ref, sh_ref, sc_ref, gt_ref, ng_ref, wq_ref, wo_ref,
                        kp_ref, kc_ref, vp_ref, vc_ref, cos_ref, sin_ref, sink_ref,
                        o_ref, q_scr, k_scr, v_scr, a_scr, cap_scr, h_scr, at_scr):
    tile = x_ref.shape[0]

    @pl.when((pl.program_id(0) == 0) & (pl.program_id(1) == 0))
    def _():
        general, first = _band_caps()
        cap_scr[0] = general
        cap_scr[1] = first

    k_scr[:WINDOW, :] = kp_ref[...].astype(BF16)
    k_scr[WINDOW:, :] = kc_ref[...].astype(BF16)
    v_scr[:, :WINDOW] = vp_ref[...].astype(BF16)
    v_scr[:, WINDOW:] = vc_ref[...].astype(BF16)
    first_tile = pl.program_id(1) == 0
    norm_gain = ng_ref[...]
    scale1 = 1.0 + _mod_rows(sc_ref, None)
    shift = _mod_rows(sh_ref, None)
    gate = _mod_rows(gt_ref, None)

    group_dims = GROUP * HEAD_DIM

    def chunk_rows(c):
        return slice(c * ATTN_CHUNK, (c + 1) * ATTN_CHUNK)

    def normalize(c):
        rows = chunk_rows(c)
        h_scr[rows, :] = ((_unit_rms(x_ref[rows, :]) * norm_gain) * scale1 + shift).astype(BF16)

    def project_piece(c, kvh):
        rows = chunk_rows(c)
        dims = slice(kvh * group_dims, (kvh + 1) * group_dims)
        qt = _rope_t(_dot(h_scr[rows, :], wq_ref[:, dims]).T, cos_ref[:, rows], sin_ref[:, rows])
        q_scr[dims, rows] = (qt * (HEAD_DIM ** -0.5 * LOG2E)).astype(BF16)

    def attend(blk, kvh):
        cols = slice(blk * WINDOW, (blk + 1) * WINDOW)
        keys = slice(blk * WINDOW, (blk + 2) * WINDOW)
        cap = cap_scr[jnp.where(first_tile, 1, 0)] if blk == 0 else cap_scr[0]
        dims = slice(kvh * HEAD_DIM, (kvh + 1) * HEAD_DIM)
        heads = [slice((kvh * GROUP + g) * HEAD_DIM, (kvh * GROUP + g + 1) * HEAD_DIM)
                 for g in range(GROUP)]
        qg = jnp.concatenate([q_scr[hd, cols] for hd in heads], axis=1)
        s = jnp.minimum(_dot(k_scr[keys, dims], qg), cap)
        sink = LOG2E * jnp.concatenate(
            [jnp.full((1, WINDOW), sink_ref[kvh * GROUP + g], F32) for g in range(GROUP)], axis=1)
        m = jnp.maximum(jnp.max(s, axis=0, keepdims=True), sink)
        p = jnp.exp2(s - m)
        denom = jnp.sum(p, axis=0, keepdims=True) + jnp.exp2(sink - m)
        o = _dot(v_scr[dims, keys], p.astype(BF16)) * (1.0 / denom)
        for g, hd in enumerate(heads):
            a_scr[hd, cols] = o[:, g * WINDOW:(g + 1) * WINDOW]

    def transpose_piece(c, kvh):
        rows = chunk_rows(c)
        dims = slice(kvh * group_dims, (kvh + 1) * group_dims)
        at_scr[rows, dims] = a_scr[dims, rows].T.astype(BF16)

    def emit_piece(c, n):
        rows = chunk_rows(c)
        cols = slice(n * group_dims, (n + 1) * group_dims)
        y = _dot(at_scr[rows, :], wo_ref[:, cols])
        o_ref[rows, cols] = x_ref[rows, cols] + gate[:, cols] * y

    n_chunks = tile // ATTN_CHUNK
    blocks_per_chunk = ATTN_CHUNK // WINDOW
    n_pieces = D_MODEL // group_dims
    normalize(0)
    for kvh in range(N_KV_HEADS):
        project_piece(0, kvh)
    for c in range(n_chunks):
        if c + 1 < n_chunks:
            normalize(c + 1)
        step = 0
        for b in range(blocks_per_chunk):
            for kvh in range(N_KV_HEADS):
                attend(c * blocks_per_chunk + b, kvh)
                if b == blocks_per_chunk - 1:
                    transpose_piece(c, kvh)
                if step % 2 == 0 and c + 1 < n_chunks and step // 2 < N_KV_HEADS:
                    project_piece(c + 1, step // 2)
                if step % 2 == 1 and c > 0 and step // 2 < n_pieces:
                    emit_piece(c - 1, step // 2)
                step += 1
    for n in range(n_pieces):
        emit_piece(n_chunks - 1, n)


def _attn_prompt(x, group, mods, norm_g, w_q, w_o, k, v_t, tables_t, sinks):
    tile = group.tile
    nt = group.grid[1]
    blocks_per_tile = tile // WINDOW

    def prev_block(b, t):
        return jnp.maximum((b * nt + t) * blocks_per_tile - 1, 0)

    in_specs = [
        group.row_spec(D_MODEL),
        _mod_spec(1, 3, group), _mod_spec(1, 4, group), _mod_spec(1, 5, group),
        _resident((None, 1, D_MODEL), lambda b, t: (4, 0, 0)),
        _resident((D_MODEL, D_MODEL), lambda b, t: (0, 0)),
        _resident((D_MODEL, D_MODEL), lambda b, t: (0, 0)),
        pl.BlockSpec((WINDOW, KV_DIM), lambda b, t: (prev_block(b, t), 0)), group.row_spec(KV_DIM),
        pl.BlockSpec((KV_DIM, WINDOW), lambda b, t: (0, prev_block(b, t))), group.col_spec(KV_DIM),
        group.table_t_spec(), group.table_t_spec(),
        pl.BlockSpec(memory_space=pltpu.SMEM),
    ]
    return pl.pallas_call(
        _attn_prompt_kernel,
        grid=group.grid,
        in_specs=in_specs,
        out_specs=group.row_spec(D_MODEL),
        out_shape=jax.ShapeDtypeStruct((group.rows, D_MODEL), F32),
        scratch_shapes=[
            pltpu.VMEM((D_MODEL, tile), BF16),
            pltpu.VMEM((tile + WINDOW, KV_DIM), BF16),
            pltpu.VMEM((KV_DIM, tile + WINDOW), BF16),
            pltpu.VMEM((D_MODEL, tile), F32),
            pltpu.VMEM((2, 2 * WINDOW, GROUP * WINDOW), F32),
            pltpu.VMEM((tile, D_MODEL), BF16),
            pltpu.VMEM((tile, D_MODEL), BF16),
        ],
        compiler_params=pltpu.CompilerParams(
            dimension_semantics=("arbitrary", "arbitrary"),
            vmem_limit_bytes=VMEM_LIMIT_BYTES),
        name="attn_prompt",
    )(x, mods, mods, mods, norm_g, w_q, w_o, k, k, v_t, v_t, *tables_t, sinks)


def _attn_sample_kernel(x_ref, sh_ref, sc_ref, gt_ref, ng_ref, wq_ref, wo_ref,
                        kc_ref, vc_ref, kn_ref, vn_ref, cos_ref, sup_ref, sdn_ref, sink_ref,
                        o_ref, kout_ref, vout_ref, q_scr, *, dec_seq):
    rows = x_ref.shape[0]
    n_seq = rows // dec_seq
    n_cache = n_seq * WINDOW
    q_scr[...] = _project_q(x_ref, sh_ref, sc_ref, ng_ref, wq_ref,
                            (cos_ref[...], sup_ref[...], sdn_ref[...]), dec_seq)

    def all_windows(ref, kvh):
        return jnp.concatenate(
            [ref[(s * N_KV_HEADS + kvh) * HEAD_DIM:(s * N_KV_HEADS + kvh + 1) * HEAD_DIM, :].astype(BF16)
             for s in range(n_seq)], axis=1)

    q_row = lax.broadcasted_iota(jnp.int32, (GROUP * rows, 1), 0) & (rows - 1)
    q_seq, q_i = q_row >> _log2(dec_seq), q_row & (dec_seq - 1)
    c_col = lax.broadcasted_iota(jnp.int32, (1, n_cache), 1)
    valid_c = (q_seq == (c_col >> _log2(WINDOW))) & ((c_col & (WINDOW - 1)) > q_i)
    n_col = lax.broadcasted_iota(jnp.int32, (1, rows), 1)
    valid_n = (q_seq == (n_col >> _log2(dec_seq))) & ((n_col & (dec_seq - 1)) <= q_i)

    heads = []
    for kvh in range(N_KV_HEADS):
        dims = slice(kvh * HEAD_DIM, (kvh + 1) * HEAD_DIM)
        qg = _stack_heads(q_scr, slice(None), kvh)
        s_c = jnp.where(valid_c, _dot(qg, all_windows(kc_ref, kvh)), NEG)
        s_n = jnp.where(valid_n, _dot(qg, kn_ref[dims, :].astype(BF16)), NEG)
        (p_c, p_n), inv = _softmax_parts([s_c, s_n], _sink_column(sink_ref, kvh, rows))
        o = (_dot_nt(p_c.astype(BF16), all_windows(vc_ref, kvh))
             + _dot_nt(p_n.astype(BF16), vn_ref[dims, :].astype(BF16))) * inv
        heads += [o[g * rows:(g + 1) * rows, :] for g in range(GROUP)]
    attn = jnp.concatenate(heads, axis=1).astype(BF16)
    o_ref[...] = x_ref[...] + _mod_rows(gt_ref, dec_seq) * _dot(attn, wo_ref[...])

    keep = lax.broadcasted_iota(jnp.int32, (1, WINDOW), 1) < WINDOW - dec_seq
    for src_ref, new_ref, dst_ref in ((kc_ref, kn_ref, kout_ref), (vc_ref, vn_ref, vout_ref)):
        new = new_ref[...]
        for s in range(n_seq):
            blk = slice(s * KV_DIM, (s + 1) * KV_DIM)
            old = pltpu.roll(src_ref[blk, :], WINDOW - dec_seq, axis=1)
            app = pltpu.roll(new, (WINDOW - dec_seq - s * dec_seq) % rows, axis=1)
            dst_ref[blk, :] = jnp.where(keep, old, app)


def _attn_sample(x, n_seq_total, dec_seq, mods, norm_g, w_q, w_o, k_cache, v_cache,
                 k_new, v_new, tables, sinks):
    n_seq = SAMPLE_ATTN_SEQS
    rows = n_seq * dec_seq
    assert rows == WINDOW == LANES
    grid = (n_seq_total // n_seq,)
    row_spec = lambda width: pl.BlockSpec((rows, width), lambda t: (t, 0))
    mod_spec = lambda col: pl.BlockSpec((None, n_seq, D_MODEL), lambda t: (1, t, col))
    cache_spec = pl.BlockSpec((n_seq * KV_DIM, WINDOW), lambda t: (t, 0))
    new_spec = pl.BlockSpec((KV_DIM, rows), lambda t: (0, t))
    table_spec = pl.BlockSpec((rows, LANES), lambda t: (0, 0))
    in_specs = [
        row_spec(D_MODEL), mod_spec(3), mod_spec(4), mod_spec(5),
        _resident((None, 1, D_MODEL), lambda t: (4, 0, 0)),
        _resident((D_MODEL, D_MODEL), lambda t: (0, 0)),
        _resident((D_MODEL, D_MODEL), lambda t: (0, 0)),
        cache_spec, cache_spec, new_spec, new_spec,
        table_spec, table_spec, table_spec,
        pl.BlockSpec(memory_space=pltpu.SMEM),
    ]
    cache_shape = jax.ShapeDtypeStruct((n_seq_total * KV_DIM, WINDOW), F32)
    return pl.pallas_call(
        functools.partial(_attn_sample_kernel, dec_seq=dec_seq),
        grid=grid,
        in_specs=in_specs,
        out_specs=[row_spec(D_MODEL), cache_spec, cache_spec],
        out_shape=[jax.ShapeDtypeStruct((n_seq_total * dec_seq, D_MODEL), F32),
                   cache_shape, cache_shape],
        scratch_shapes=[pltpu.VMEM((rows, D_MODEL), BF16)],
        compiler_params=pltpu.CompilerParams(
            dimension_semantics=("arbitrary",),
            vmem_limit_bytes=VMEM_LIMIT_BYTES),
        name="attn_sample",
    )(x, mods, mods, mods, norm_g, w_q, w_o, k_cache, v_cache, k_new, v_new, *tables, sinks)


def _rope_angles(pos):
    inv_freq = ROPE_THETA ** (-jnp.arange(0, ROT_DIM, 2, dtype=F32) / ROT_DIM)
    ang = pos.astype(F32)[:, None] * inv_freq[None, :]
    return jnp.cos(ang), jnp.sin(ang)


def _rope_tables(pos):
    half = ROT_DIM // 2
    cos, sin = _rope_angles(pos)
    n = pos.shape[0]
    rest = HEAD_DIM - ROT_DIM
    zeros = jnp.zeros((n, half), F32)
    cos_t = jnp.concatenate([cos, cos, jnp.ones((n, rest), F32)], axis=1)
    sin_up = jnp.concatenate([-sin, zeros, jnp.zeros((n, rest), F32)], axis=1)
    sin_dn = jnp.concatenate([zeros, sin, jnp.zeros((n, rest), F32)], axis=1)
    reps = LANES // HEAD_DIM
    return tuple(jnp.tile(t, (1, reps)) for t in (cos_t, sin_up, sin_dn))


def kernel(x_prompt, x_sample, state_conv, cache_k_win, cache_v_win, c_prompt, c_sample, norm_g, w_ada, b_ada, w_ffn_gate, w_ffn_up, w_ffn_down, conv_w_in, conv_w, conv_w_out, kv_norm_g, w_ada_kv, b_ada_kv, w_k, w_v, attn_w_q, attn_sinks, attn_w_o, final_norm_g):
    batch, seq, d = x_prompt.shape
    dec_batch, dec_seq, _ = x_sample.shape
    w_buf = cache_k_win.shape[1]
    past_len = 16384
    assert d == D_MODEL and w_buf == WINDOW and dec_seq == SUBLANES
    assert w_ada.shape[0] == 2 and conv_w_in.shape[0] == 1 and attn_w_q.shape[0] == 1

    c_all = jnp.concatenate([c_sample, jnp.repeat(c_prompt, PROMPT_MOD_COPIES, axis=0)], axis=0)
    mods = _ada(c_all, w_ada, b_ada)
    mods_kv = _ada(c_all, w_ada_kv[None], b_ada_kv[None])
    prompt_mod_block = dec_batch

    make_groups = lambda tile: {
        "prompt": _Group(batch, seq, None, prompt_mod_block, tile),
        "sample": _Group(dec_batch, dec_seq, dec_seq, prompt_mod_block, tile),
    }
    ffn_groups, mixer_groups = make_groups(FFN_TILE), make_groups(MIXER_TILE)
    xp, xs = x_prompt.reshape(batch * seq, d), x_sample.reshape(dec_batch * dec_seq, d)

    norm_g3 = norm_g.reshape(-1, 1, d)
    w_in, w_out = conv_w_in[0].astype(BF16), conv_w_out[0].astype(BF16)
    w_q, w_o = attn_w_q[0].astype(BF16), attn_w_o[0].astype(BF16)
    sinks = attn_sinks[0]

    pos_dec = past_len + jnp.arange(dec_seq, dtype=jnp.int32)
    pos_prompt = jnp.arange(seq, dtype=jnp.int32)
    tab_prompt = _rope_tables(pos_prompt)
    tab_dec = tuple(jnp.tile(t, (SAMPLE_ATTN_SEQS, 1)) for t in _rope_tables(pos_dec))
    tab_prompt_t = tuple(t.T for t in _rope_angles(pos_prompt))
    tab_dec_t = tuple(jnp.tile(t.T, (1, FFN_TILE // dec_seq)) for t in _rope_angles(pos_dec))
    kv_args = dict(norm_g=kv_norm_g.reshape(1, d), mods=mods_kv, w_k=w_k.astype(BF16),
                   w_k_t=w_k.T.astype(BF16), w_v_t=w_v.T.astype(BF16),
                   prompt_tables=tab_prompt, decode_tables_t=tab_dec_t)
    to_rows = lambda c: jnp.transpose(c, (0, 2, 3, 1)).reshape(dec_batch * KV_DIM, w_buf)
    from_rows = lambda c: jnp.transpose(c.reshape(dec_batch, N_KV_HEADS, HEAD_DIM, w_buf), (0, 3, 1, 2))

    ffn = functools.partial(_ffn, prompt=ffn_groups["prompt"], decode=ffn_groups["sample"], mods=mods,
                            norm_g=norm_g3, wg=w_ffn_gate, wu=w_ffn_up, wd=w_ffn_down)
    xp, xs = ffn(xp, xs, layer=0, sub=0, slot=0)
    xp, u_p = _conv_mixer(xp, mixer_groups["prompt"], mods, norm_g3, w_in, conv_w[0], w_out)
    xs, u_s = _conv_mixer(xs, mixer_groups["sample"], mods, norm_g3, w_in, conv_w[0], w_out,
                          (state_conv[0, :, 0, :], state_conv[0, :, 1, :]))
    xp, xs = ffn(xp, xs, layer=0, sub=2, slot=1)
    xp, xs, k_p, vt_p, kt_s, vt_s = ffn(xp, xs, layer=1, sub=0, slot=0, kv=kv_args)
    xp = _attn_prompt(xp, mixer_groups["prompt"], mods, norm_g3, w_q, w_o, k_p, vt_p, tab_prompt_t, sinks)
    xs, k_win, v_win = _attn_sample(xs, dec_batch, dec_seq, mods, norm_g3, w_q, w_o,
                                    to_rows(cache_k_win), to_rows(cache_v_win), kt_s, vt_s, tab_dec, sinks)
    y_p, y_s = ffn(xp, xs, layer=1, sub=2, slot=1, final_g=final_norm_g.reshape(1, d))

    k_state_p = k_p.reshape(batch, seq, KV_DIM)[:, seq - WINDOW:].reshape(
        batch, WINDOW, N_KV_HEADS, HEAD_DIM)
    v_state_p = jnp.stack([vt_p[:, (b + 1) * seq - WINDOW:(b + 1) * seq] for b in range(batch)])
    v_state_p = jnp.transpose(v_state_p.reshape(batch, N_KV_HEADS, HEAD_DIM, WINDOW), (0, 3, 1, 2))
    k_state_s, v_state_s = from_rows(k_win), from_rows(v_win)
    tail = CONV_WIDTH - 1
    conv_p = u_p.reshape(batch, SUBLANES, d)[:, SUBLANES - tail:][None]
    conv_s = u_s.reshape(dec_batch, dec_seq, d)[:, dec_seq - tail:][None]
    return (y_p.reshape(batch, seq, d), y_s.reshape(dec_batch, dec_seq, d), conv_p, conv_s,
            k_state_p, v_state_p, k_state_s, v_state_s)
```

```python
import functools

import jax
import jax.numpy as jnp
from jax import lax
from jax.experimental import pallas as pl
from jax.experimental.pallas import tpu as pltpu

F32 = jnp.float32
BF16 = jnp.bfloat16

D_MODEL = 1024
D_FF = 2816
HEAD_DIM = 64
N_HEADS = 16
N_KV_HEADS = 4
GROUP = N_HEADS // N_KV_HEADS
KV_DIM = N_KV_HEADS * HEAD_DIM
WINDOW = 128
ROT_DIM = 16
ROPE_THETA = 500000.0
CONV_WIDTH = 3
N_MOD = 9
EPS = 1e-6
NEG = -1e30
FMAX = float(jnp.finfo(jnp.float32).max)

SUBLANES = 8
LANES = 128
VMEM_LIMIT_BYTES = 60 * 1024 * 1024

FFN_TILE = 512
FF_STAGE = 256
MIXER_TILE = 512
ATTN_CHUNK = 256
LOG2E = 1.4426950408889634
SAMPLE_ATTN_SEQS = 16
FF_CHUNKS = ((0, 1024), (1024, 2048), (2048, D_FF))


def _dot(a, b):
    return jnp.dot(a, b, preferred_element_type=F32)


def _dot_nt(a, b):
    return lax.dot_general(a, b, (((1,), (1,)), ((), ())), preferred_element_type=F32)


def _log2(n):
    assert n & (n - 1) == 0, n
    return n.bit_length() - 1


def _silu(x):
    return x * jax.nn.sigmoid(x)


def _unit_rms(x):
    return x * lax.rsqrt(jnp.mean(x * x, axis=-1, keepdims=True) + EPS)


def _repeat_rows(ref, reps):
    n = ref.shape[1]
    return jnp.concatenate(
        [jnp.broadcast_to(ref[i:i + 1, :], (reps, n)) for i in range(ref.shape[0])], axis=0)


def _mod_rows(ref, per_seq_rows):
    if per_seq_rows is None:
        return ref[pl.ds(pl.program_id(0), 1), :]
    return _repeat_rows(ref, per_seq_rows)


def _rope(x, cos_t, sin_up_t, sin_dn_t):
    n = x.shape[1]
    reps = n // LANES
    half = ROT_DIM // 2
    cos = jnp.concatenate([cos_t] * reps, axis=1)
    sin_up = jnp.concatenate([sin_up_t] * reps, axis=1)
    sin_dn = jnp.concatenate([sin_dn_t] * reps, axis=1)
    x_up = pltpu.roll(x, n - half, axis=1)
    x_dn = pltpu.roll(x, half, axis=1)
    return x * cos + x_up * sin_up + x_dn * sin_dn


def _ada_kernel(c_ref, w_ref, b_ref, o_ref):
    a = _silu(c_ref[...]).astype(BF16)
    o_ref[...] = _dot(a, w_ref[...].astype(BF16)) + b_ref[...]


def _ada(c_all, w, b):
    n_layers, _, n = w.shape
    m = c_all.shape[0]
    tn = D_MODEL
    return pl.pallas_call(
        _ada_kernel,
        grid=(n_layers, n // tn),
        in_specs=[
            pl.BlockSpec((m, D_MODEL), lambda l, j: (0, 0)),
            pl.BlockSpec((None, D_MODEL, tn), lambda l, j: (l, 0, j)),
            pl.BlockSpec((None, 1, tn), lambda l, j: (l, 0, j)),
        ],
        out_specs=pl.BlockSpec((None, m, tn), lambda l, j: (l, 0, j)),
        out_shape=jax.ShapeDtypeStruct((n_layers, m, n), F32),
        compiler_params=pltpu.CompilerParams(
            dimension_semantics=("arbitrary", "arbitrary"),
            vmem_limit_bytes=VMEM_LIMIT_BYTES),
        name="adaln_tables",
    )(c_all, w, b.reshape(n_layers, 1, n))


def _rope_t(xt, cos_t, sin_t):
    half = ROT_DIM // 2
    out = []
    for h in range(xt.shape[0] // HEAD_DIM):
        base = h * HEAD_DIM
        x1 = xt[base:base + half, :]
        x2 = xt[base + half:base + ROT_DIM, :]
        out += [x1 * cos_t - x2 * sin_t, x2 * cos_t + x1 * sin_t,
                xt[base + ROT_DIM:base + HEAD_DIM, :]]
    return jnp.concatenate(out, axis=0)


def _ffn_kernel(*refs, per_seq_rows, layer, slot, with_kv, kv_transposed, final):
    it = iter(refs)
    take = lambda n: [next(it) for _ in range(n)]
    x_ref, sh_ref, sc_ref, gt_ref, ng_ref, wg_hbm, wu_hbm, wd_hbm = take(8)
    if with_kv:
        kvg_ref, kvsh_ref, kvsc_ref, wk_ref, wv_ref, cos_ref, sup_ref = take(7)
        if not kv_transposed:
            (sdn_ref,) = take(1)
    if final:
        (fg_ref,) = take(1)
    (o_ref,) = take(1)
    if with_kv:
        k_ref, v_ref = take(2)
    wg_s, wu_s, wd_s, stage_in, stage_out, sem = take(6)

    x = x_ref[...]
    xn = _unit_rms(x)
    hb = ((xn * ng_ref[...]) * (1.0 + _mod_rows(sc_ref, per_seq_rows))
          + _mod_rows(sh_ref, per_seq_rows)).astype(BF16)

    def partial_ffn(lo, hi):
        g = _dot(hb, wg_s[:, lo:hi])
        u = _dot(hb, wu_s[:, lo:hi])
        return _dot((_silu(g) * u).astype(BF16), wd_s[lo:hi, :])

    def finish(y):
        x_new = x + (0.5 * _mod_rows(gt_ref, per_seq_rows)) * y
        o_ref[...] = _unit_rms(x_new) * fg_ref[...] if final else x_new

    first_step = (pl.program_id(0) == 0) & (pl.program_id(1) == 0)

    @pl.when(first_step)
    def _():
        n_slices = D_FF // FF_STAGE

        def copies(c):
            cols = pl.ds(c * FF_STAGE, FF_STAGE)
            buf = c % 2
            return (
                pltpu.make_async_copy(wg_hbm.at[layer, slot, :, cols], stage_in.at[buf, 0], sem.at[buf, 0]),
                pltpu.make_async_copy(wu_hbm.at[layer, slot, :, cols], stage_in.at[buf, 1], sem.at[buf, 1]),
                pltpu.make_async_copy(wd_hbm.at[layer, slot, cols, :], stage_out.at[buf], sem.at[buf, 2]),
            )

        for c in range(min(2, n_slices)):
            for cp in copies(c):
                cp.start()
        y = None
        for c in range(n_slices):
            lo, hi = c * FF_STAGE, (c + 1) * FF_STAGE
            for cp in copies(c):
                cp.wait()
            wg_s[:, lo:hi] = stage_in[c % 2, 0].astype(BF16)
            wu_s[:, lo:hi] = stage_in[c % 2, 1].astype(BF16)
            wd_s[lo:hi, :] = stage_out[c % 2].astype(BF16)
            if c + 2 < n_slices:
                for cp in copies(c + 2):
                    cp.start()
            part = partial_ffn(lo, hi)
            y = part if y is None else y + part
        finish(y)

    @pl.when(jnp.logical_not(first_step))
    def _():
        y = None
        for lo, hi in FF_CHUNKS:
            part = partial_ffn(lo, hi)
            y = part if y is None else y + part
        finish(y)

    if with_kv:
        hk = ((xn * kvg_ref[...]) * (1.0 + _mod_rows(kvsc_ref, per_seq_rows))
              + _mod_rows(kvsh_ref, per_seq_rows)).astype(BF16)
        if kv_transposed:
            k_ref[...] = _rope_t(_dot_nt(wk_ref[...], hk), cos_ref[...], sup_ref[...])
        else:
            k_ref[...] = _rope(_dot(hk, wk_ref[...]), cos_ref[...], sup_ref[...], sdn_ref[...])
        v_ref[...] = _dot_nt(wv_ref[...], hk)


def _resident(shape, index_map):
    return pl.BlockSpec(shape, index_map, pipeline_mode=pl.Buffered(1))


def _mod_spec(layer, col, group):
    if group.per_seq_rows is None:
        return pl.BlockSpec((None, SUBLANES, D_MODEL),
                            lambda b, t: (layer, group.prompt_mod_block, col))
    return pl.BlockSpec((None, group.tile // group.per_seq_rows, D_MODEL),
                        lambda b, t: (layer, t, col))


class _Group:
    def __init__(self, n_seq, seq_len, per_seq_rows, prompt_mod_block, tile):
        self.tile = tile
        self.rows = n_seq * seq_len
        self.per_seq_rows = per_seq_rows
        self.prompt_mod_block = prompt_mod_block
        if per_seq_rows is None:
            self.grid = (n_seq, seq_len // self.tile)
        else:
            self.grid = (1, self.rows // self.tile)
        self.tiles_per_seq = self.grid[1]

    def row_spec(self, width):
        nt = self.grid[1]
        return pl.BlockSpec((self.tile, width), lambda b, t: (b * nt + t, 0))

    def col_spec(self, height):
        nt = self.grid[1]
        return pl.BlockSpec((height, self.tile), lambda b, t: (0, b * nt + t))

    def table_t_spec(self):
        if self.per_seq_rows is None:
            return pl.BlockSpec((ROT_DIM // 2, self.tile), lambda b, t: (0, t))
        return pl.BlockSpec((ROT_DIM // 2, self.tile), lambda b, t: (0, 0))

    def table_spec(self):
        if self.per_seq_rows is None:
            return pl.BlockSpec((self.tile, LANES), lambda b, t: (t, 0))
        return pl.BlockSpec((self.tile, LANES), lambda b, t: (0, 0))


def _ffn(x, group, mods, layer, sub, norm_g, wg, wu, wd, slot, kv=None, final_g=None):
    with_kv = kv is not None
    final = final_g is not None
    hbm = pl.BlockSpec(memory_space=pl.ANY)
    in_specs = [
        group.row_spec(D_MODEL),
        _mod_spec(layer, 3 * sub + 0, group),
        _mod_spec(layer, 3 * sub + 1, group),
        _mod_spec(layer, 3 * sub + 2, group),
        _resident((None, 1, D_MODEL), lambda b, t: (layer * 3 + sub, 0, 0)),
        hbm, hbm, hbm,
    ]
    args = [x, mods, mods, mods, norm_g, wg, wu, wd]
    out_specs = [group.row_spec(D_MODEL)]
    out_shape = [jax.ShapeDtypeStruct((group.rows, D_MODEL), F32)]
    kv_transposed = with_kv and kv["transposed"]
    if with_kv:
        wk_shape = (KV_DIM, D_MODEL) if kv_transposed else (D_MODEL, KV_DIM)
        in_specs += [
            _resident((1, D_MODEL), lambda b, t: (0, 0)),
            _mod_spec(0, 0, group),
            _mod_spec(0, 1, group),
            _resident(wk_shape, lambda b, t: (0, 0)),
            _resident((KV_DIM, D_MODEL), lambda b, t: (0, 0)),
        ]
        args += [kv["norm_g"], kv["mods"], kv["mods"], kv["w_k"], kv["w_v"], *kv["tables"]]
        t_shape = jax.ShapeDtypeStruct((KV_DIM, group.rows), F32)
        if kv_transposed:
            in_specs += [group.table_t_spec()] * 2
            out_specs += [group.col_spec(KV_DIM)] * 2
            out_shape += [t_shape] * 2
        else:
            in_specs += [group.table_spec()] * 3
            out_specs += [group.row_spec(KV_DIM), group.col_spec(KV_DIM)]
            out_shape += [jax.ShapeDtypeStruct((group.rows, KV_DIM), F32), t_shape]
    if final:
        in_specs.append(_resident((1, D_MODEL), lambda b, t: (0, 0)))
        args.append(final_g)
    scratch = [
        pltpu.VMEM((D_MODEL, D_FF), BF16), pltpu.VMEM((D_MODEL, D_FF), BF16),
        pltpu.VMEM((D_FF, D_MODEL), BF16),
        pltpu.VMEM((2, 2, D_MODEL, FF_STAGE), F32),
        pltpu.VMEM((2, FF_STAGE, D_MODEL), F32),
        pltpu.SemaphoreType.DMA((2, 3)),
    ]
    return pl.pallas_call(
        functools.partial(_ffn_kernel, per_seq_rows=group.per_seq_rows, layer=layer, slot=slot,
                          with_kv=with_kv, kv_transposed=kv_transposed, final=final),
        grid=group.grid,
        in_specs=in_specs,
        out_specs=out_specs,
        out_shape=out_shape,
        scratch_shapes=scratch,
        compiler_params=pltpu.CompilerParams(
            dimension_semantics=("arbitrary", "arbitrary"),
            vmem_limit_bytes=VMEM_LIMIT_BYTES),
        name=f"ffn_l{layer}_s{sub}",
    )(*args)


def _conv_kernel(*refs, per_seq_rows):
    it = iter(refs)
    x_ref, sh_ref, sc_ref, gt_ref, ng_ref, win_ref, cw_ref, wout_ref = (next(it) for _ in range(8))
    if per_seq_rows is not None:
        st0_ref, st1_ref = next(it), next(it)
    o_ref, u_ref = next(it), next(it)
    if per_seq_rows is None:
        carry_ref = next(it)

    tile = x_ref.shape[0]
    x = x_ref[...]
    h = ((_unit_rms(x) * ng_ref[...]) * (1.0 + _mod_rows(sc_ref, per_seq_rows))
         + _mod_rows(sh_ref, per_seq_rows)).astype(BF16)
    bcv = _dot(h, win_ref[...])
    b_g = bcv[:, :D_MODEL]
    u = bcv[:, D_MODEL:2 * D_MODEL] * bcv[:, 2 * D_MODEL:]

    row = lax.broadcasted_iota(jnp.int32, (tile, 1), 0)
    if per_seq_rows is None:
        @pl.when(pl.program_id(1) == 0)
        def _():
            carry_ref[...] = jnp.zeros_like(carry_ref)
        prev2 = carry_ref[SUBLANES - 2:SUBLANES - 1, :]
        prev1 = carry_ref[SUBLANES - 1:SUBLANES, :]
        t = row
    else:
        prev2 = _repeat_rows(st0_ref, per_seq_rows)
        prev1 = _repeat_rows(st1_ref, per_seq_rows)
        t = row & (per_seq_rows - 1)
    u1 = jnp.where(t == 0, prev1, pltpu.roll(u, 1, axis=0))
    u2 = jnp.where(t == 0, prev2, jnp.where(t == 1, prev1, pltpu.roll(u, 2, axis=0)))
    conv = u2 * cw_ref[0:1, :] + u1 * cw_ref[1:2, :] + u * cw_ref[2:3, :]
    y = _dot((b_g * conv).astype(BF16), wout_ref[...])
    o_ref[...] = x + _mod_rows(gt_ref, per_seq_rows) * y

    if per_seq_rows is None:
        tail = u[tile - SUBLANES:, :]
        carry_ref[...] = tail
        u_ref[...] = tail
    else:
        u_ref[...] = u


def _conv_mixer(x, group, mods, norm_g, w_in, conv_w, w_out, state=None):
    prompt = group.per_seq_rows is None
    in_specs = [
        group.row_spec(D_MODEL),
        _mod_spec(0, 3, group), _mod_spec(0, 4, group), _mod_spec(0, 5, group),
        _resident((None, 1, D_MODEL), lambda b, t: (1, 0, 0)),
        _resident((D_MODEL, 3 * D_MODEL), lambda b, t: (0, 0)),
        _resident((CONV_WIDTH, D_MODEL), lambda b, t: (0, 0)),
        _resident((D_MODEL, D_MODEL), lambda b, t: (0, 0)),
    ]
    args = [x, mods, mods, mods, norm_g, w_in, conv_w, w_out]
    scratch = []
    if prompt:
        u_spec = pl.BlockSpec((SUBLANES, D_MODEL), lambda b, t: (b, 0))
        u_shape = jax.ShapeDtypeStruct((group.grid[0] * SUBLANES, D_MODEL), F32)
        scratch.append(pltpu.VMEM((SUBLANES, D_MODEL), F32))
    else:
        n_seq = group.tile // group.per_seq_rows
        st_spec = pl.BlockSpec((n_seq, D_MODEL), lambda b, t: (t, 0))
        in_specs += [st_spec, st_spec]
        args += [state[0], state[1]]
        u_spec = group.row_spec(D_MODEL)
        u_shape = jax.ShapeDtypeStruct((group.rows, D_MODEL), F32)
    return pl.pallas_call(
        functools.partial(_conv_kernel, per_seq_rows=group.per_seq_rows),
        grid=group.grid,
        in_specs=in_specs,
        out_specs=[group.row_spec(D_MODEL), u_spec],
        out_shape=[jax.ShapeDtypeStruct((group.rows, D_MODEL), F32), u_shape],
        scratch_shapes=scratch,
        compiler_params=pltpu.CompilerParams(
            dimension_semantics=("arbitrary", "arbitrary"),
            vmem_limit_bytes=VMEM_LIMIT_BYTES),
        name="conv_mixer",
    )(*args)


def _sink_column(sink_ref, kv_head, q_rows):
    grp = lax.broadcasted_iota(jnp.int32, (GROUP * q_rows, 1), 0) >> _log2(q_rows)
    col = jnp.full((GROUP * q_rows, 1), sink_ref[kv_head * GROUP + GROUP - 1], F32)
    for g in range(GROUP - 2, -1, -1):
        col = jnp.where(grp == g, sink_ref[kv_head * GROUP + g], col)
    return col


def _stack_heads(q_ref, rows, kv_head):
    return jnp.concatenate(
        [q_ref[rows, (kv_head * GROUP + g) * HEAD_DIM:(kv_head * GROUP + g + 1) * HEAD_DIM]
         for g in range(GROUP)], axis=0)


def _softmax_parts(scores, sink):
    m = sink
    for s in scores:
        m = jnp.maximum(m, jnp.max(s, axis=-1, keepdims=True))
    ps = [jnp.exp(s - m) for s in scores]
    denom = jnp.exp(sink - m)
    for p in ps:
        denom = denom + jnp.sum(p, axis=-1, keepdims=True)
    return ps, 1.0 / denom


def _project_q(x_ref, sh_ref, sc_ref, ng_ref, wq_ref, tables, per_seq_rows):
    h = ((_unit_rms(x_ref[...]) * ng_ref[...]) * (1.0 + _mod_rows(sc_ref, per_seq_rows))
         + _mod_rows(sh_ref, per_seq_rows)).astype(BF16)
    q = _rope(_dot(h, wq_ref[...]), *tables)
    return (q * (HEAD_DIM ** -0.5)).astype(BF16)


def _band_caps():
    shape = (2 * WINDOW, GROUP * WINDOW)
    kj = lax.broadcasted_iota(jnp.int32, shape, 0)
    qi = lax.broadcasted_iota(jnp.int32, shape, 1) & (WINDOW - 1)
    in_band = (kj > qi) & (kj <= qi + WINDOW)
    return (jnp.where(in_band, FMAX, NEG), jnp.where(in_band & (kj >= WINDOW), FMAX, NEG))


def _attn_prompt_kernel(x_ref, sh_ref, sc_ref, gt_ref, ng_ref, wq_ref, wo_ref,
                        kp_ref, kc_ref, vp_ref, vc_ref, cos_ref, sin_ref, sink_ref,
                        o_ref, q_scr, k_scr, v_scr, a_scr, cap_scr, h_scr, at_scr):
    tile = x_ref.shape[0]

    @pl.when((pl.program_id(0) == 0) & (pl.program_id(1) == 0))
    def _():
        general, first = _band_caps()
        cap_scr[0] = general
        cap_scr[1] = first

    k_scr[:WINDOW, :] = kp_ref[...].astype(BF16)
    k_scr[WINDOW:, :] = kc_ref[...].astype(BF16)
    v_scr[:, :WINDOW] = vp_ref[...].astype(BF16)
    v_scr[:, WINDOW:] = vc_ref[...].astype(BF16)
    first_tile = pl.program_id(1) == 0
    norm_gain = ng_ref[...]
    scale1 = 1.0 + _mod_rows(sc_ref, None)
    shift = _mod_rows(sh_ref, None)
    gate = _mod_rows(gt_ref, None)

    group_dims = GROUP * HEAD_DIM

    def chunk_rows(c):
        return slice(c * ATTN_CHUNK, (c + 1) * ATTN_CHUNK)

    def normalize(c):
        rows = chunk_rows(c)
        h_scr[rows, :] = ((_unit_rms(x_ref[rows, :]) * norm_gain) * scale1 + shift).astype(BF16)

    def project_piece(c, kvh):
        rows = chunk_rows(c)
        dims = slice(kvh * group_dims, (kvh + 1) * group_dims)
        qt = _rope_t(_dot(h_scr[rows, :], wq_ref[:, dims]).T, cos_ref[:, rows], sin_ref[:, rows])
        q_scr[dims, rows] = (qt * (HEAD_DIM ** -0.5 * LOG2E)).astype(BF16)

    def attend(blk, kvh):
        cols = slice(blk * WINDOW, (blk + 1) * WINDOW)
        keys = slice(blk * WINDOW, (blk + 2) * WINDOW)
        cap = cap_scr[jnp.where(first_tile, 1, 0)] if blk == 0 else cap_scr[0]
        dims = slice(kvh * HEAD_DIM, (kvh + 1) * HEAD_DIM)
        heads = [slice((kvh * GROUP + g) * HEAD_DIM, (kvh * GROUP + g + 1) * HEAD_DIM)
                 for g in range(GROUP)]
        qg = jnp.concatenate([q_scr[hd, cols] for hd in heads], axis=1)
        s = jnp.minimum(_dot(k_scr[keys, dims], qg), cap)
        sink = LOG2E * jnp.concatenate(
            [jnp.full((1, WINDOW), sink_ref[kvh * GROUP + g], F32) for g in range(GROUP)], axis=1)
        m = jnp.maximum(jnp.max(s, axis=0, keepdims=True), sink)
        p = jnp.exp2(s - m)
        denom = jnp.sum(p, axis=0, keepdims=True) + jnp.exp2(sink - m)
        o = _dot(v_scr[dims, keys], p.astype(BF16)) * (1.0 / denom)
        for g, hd in enumerate(heads):
            a_scr[hd, cols] = o[:, g * WINDOW:(g + 1) * WINDOW]

    def transpose_piece(c, kvh):
        rows = chunk_rows(c)
        dims = slice(kvh * group_dims, (kvh + 1) * group_dims)
        at_scr[rows, dims] = a_scr[dims, rows].T.astype(BF16)

    def emit_piece(c, n):
        rows = chunk_rows(c)
        cols = slice(n * group_dims, (n + 1) * group_dims)
        y = _dot(at_scr[rows, :], wo_ref[:, cols])
        o_ref[rows, cols] = x_ref[rows, cols] + gate[:, cols] * y

    n_chunks = tile // ATTN_CHUNK
    blocks_per_chunk = ATTN_CHUNK // WINDOW
    n_pieces = D_MODEL // group_dims
    normalize(0)
    for kvh in range(N_KV_HEADS):
        project_piece(0, kvh)
    for c in range(n_chunks):
        if c + 1 < n_chunks:
            normalize(c + 1)
        step = 0
        for b in range(blocks_per_chunk):
            for kvh in range(N_KV_HEADS):
                attend(c * blocks_per_chunk + b, kvh)
                if b == blocks_per_chunk - 1:
                    transpose_piece(c, kvh)
                if step % 2 == 0 and c + 1 < n_chunks and step // 2 < N_KV_HEADS:
                    project_piece(c + 1, step // 2)
                if step % 2 == 1 and c > 0 and step // 2 < n_pieces:
                    emit_piece(c - 1, step // 2)
                step += 1
    for n in range(n_pieces):
        emit_piece(n_chunks - 1, n)


def _attn_prompt(x, group, mods, norm_g, w_q, w_o, k, v_t, tables_t, sinks):
    tile = group.tile
    nt = group.grid[1]
    blocks_per_tile = tile // WINDOW

    def prev_block(b, t):
        return jnp.maximum((b * nt + t) * blocks_per_tile - 1, 0)

    in_specs = [
        group.row_spec(D_MODEL),
        _mod_spec(1, 3, group), _mod_spec(1, 4, group), _mod_spec(1, 5, group),
        _resident((None, 1, D_MODEL), lambda b, t: (4, 0, 0)),
        _resident((D_MODEL, D_MODEL), lambda b, t: (0, 0)),
        _resident((D_MODEL, D_MODEL), lambda b, t: (0, 0)),
        pl.BlockSpec((WINDOW, KV_DIM), lambda b, t: (prev_block(b, t), 0)), group.row_spec(KV_DIM),
        pl.BlockSpec((KV_DIM, WINDOW), lambda b, t: (0, prev_block(b, t))), group.col_spec(KV_DIM),
        group.table_t_spec(), group.table_t_spec(),
        pl.BlockSpec(memory_space=pltpu.SMEM),
    ]
    return pl.pallas_call(
        _attn_prompt_kernel,
        grid=group.grid,
        in_specs=in_specs,
        out_specs=group.row_spec(D_MODEL),
        out_shape=jax.ShapeDtypeStruct((group.rows, D_MODEL), F32),
        scratch_shapes=[
            pltpu.VMEM((D_MODEL, tile), BF16),
            pltpu.VMEM((tile + WINDOW, KV_DIM), BF16),
            pltpu.VMEM((KV_DIM, tile + WINDOW), BF16),
            pltpu.VMEM((D_MODEL, tile), F32),
            pltpu.VMEM((2, 2 * WINDOW, GROUP * WINDOW), F32),
            pltpu.VMEM((tile, D_MODEL), BF16),
            pltpu.VMEM((tile, D_MODEL), BF16),
        ],
        compiler_params=pltpu.CompilerParams(
            dimension_semantics=("arbitrary", "arbitrary"),
            vmem_limit_bytes=VMEM_LIMIT_BYTES),
        name="attn_prompt",
    )(x, mods, mods, mods, norm_g, w_q, w_o, k, k, v_t, v_t, *tables_t, sinks)


def _attn_sample_kernel(x_ref, sh_ref, sc_ref, gt_ref, ng_ref, wq_ref, wo_ref,
                        kc_ref, vc_ref, kn_ref, vn_ref, cos_ref, sup_ref, sdn_ref, sink_ref,
                        o_ref, kout_ref, vout_ref, q_scr, *, dec_seq):
    rows = x_ref.shape[0]
    n_seq = rows // dec_seq
    n_cache = n_seq * WINDOW
    q_scr[...] = _project_q(x_ref, sh_ref, sc_ref, ng_ref, wq_ref,
                            (cos_ref[...], sup_ref[...], sdn_ref[...]), dec_seq)

    def all_windows(ref, kvh):
        return jnp.concatenate(
            [ref[(s * N_KV_HEADS + kvh) * HEAD_DIM:(s * N_KV_HEADS + kvh + 1) * HEAD_DIM, :].astype(BF16)
             for s in range(n_seq)], axis=1)

    q_row = lax.broadcasted_iota(jnp.int32, (GROUP * rows, 1), 0) & (rows - 1)
    q_seq, q_i = q_row >> _log2(dec_seq), q_row & (dec_seq - 1)
    c_col = lax.broadcasted_iota(jnp.int32, (1, n_cache), 1)
    valid_c = (q_seq == (c_col >> _log2(WINDOW))) & ((c_col & (WINDOW - 1)) > q_i)
    n_col = lax.broadcasted_iota(jnp.int32, (1, rows), 1)
    valid_n = (q_seq == (n_col >> _log2(dec_seq))) & ((n_col & (dec_seq - 1)) <= q_i)

    heads = []
    for kvh in range(N_KV_HEADS):
        dims = slice(kvh * HEAD_DIM, (kvh + 1) * HEAD_DIM)
        qg = _stack_heads(q_scr, slice(None), kvh)
        s_c = jnp.where(valid_c, _dot(qg, all_windows(kc_ref, kvh)), NEG)
        s_n = jnp.where(valid_n, _dot(qg, kn_ref[dims, :].astype(BF16)), NEG)
        (p_c, p_n), inv = _softmax_parts([s_c, s_n], _sink_column(sink_ref, kvh, rows))
        o = (_dot_nt(p_c.astype(BF16), all_windows(vc_ref, kvh))
             + _dot_nt(p_n.astype(BF16), vn_ref[dims, :].astype(BF16))) * inv
        heads += [o[g * rows:(g + 1) * rows, :] for g in range(GROUP)]
    attn = jnp.concatenate(heads, axis=1).astype(BF16)
    o_ref[...] = x_ref[...] + _mod_rows(gt_ref, dec_seq) * _dot(attn, wo_ref[...])

    keep = lax.broadcasted_iota(jnp.int32, (1, WINDOW), 1) < WINDOW - dec_seq
    for src_ref, new_ref, dst_ref in ((kc_ref, kn_ref, kout_ref), (vc_ref, vn_ref, vout_ref)):
        new = new_ref[...]
        for s in range(n_seq):
            blk = slice(s * KV_DIM, (s + 1) * KV_DIM)
            old = pltpu.roll(src_ref[blk, :], WINDOW - dec_seq, axis=1)
            app = pltpu.roll(new, (WINDOW - dec_seq - s * dec_seq) % rows, axis=1)
            dst_ref[blk, :] = jnp.where(keep, old, app)


def _attn_sample(x, n_seq_total, dec_seq, mods, norm_g, w_q, w_o, k_cache, v_cache,
                 k_new, v_new, tables, sinks):
    n_seq = SAMPLE_ATTN_SEQS
    rows = n_seq * dec_seq
    assert rows == WINDOW == LANES
    grid = (n_seq_total // n_seq,)
    row_spec = lambda width: pl.BlockSpec((rows, width), lambda t: (t, 0))
    mod_spec = lambda col: pl.BlockSpec((None, n_seq, D_MODEL), lambda t: (1, t, col))
    cache_spec = pl.BlockSpec((n_seq * KV_DIM, WINDOW), lambda t: (t, 0))
    new_spec = pl.BlockSpec((KV_DIM, rows), lambda t: (0, t))
    table_spec = pl.BlockSpec((rows, LANES), lambda t: (0, 0))
    in_specs = [
        row_spec(D_MODEL), mod_spec(3), mod_spec(4), mod_spec(5),
        _resident((None, 1, D_MODEL), lambda t: (4, 0, 0)),
        _resident((D_MODEL, D_MODEL), lambda t: (0, 0)),
        _resident((D_MODEL, D_MODEL), lambda t: (0, 0)),
        cache_spec, cache_spec, new_spec, new_spec,
        table_spec, table_spec, table_spec,
        pl.BlockSpec(memory_space=pltpu.SMEM),
    ]
    cache_shape = jax.ShapeDtypeStruct((n_seq_total * KV_DIM, WINDOW), F32)
    return pl.pallas_call(
        functools.partial(_attn_sample_kernel, dec_seq=dec_seq),
        grid=grid,
        in_specs=in_specs,
        out_specs=[row_spec(D_MODEL), cache_spec, cache_spec],
        out_shape=[jax.ShapeDtypeStruct((n_seq_total * dec_seq, D_MODEL), F32),
                   cache_shape, cache_shape],
        scratch_shapes=[pltpu.VMEM((rows, D_MODEL), BF16)],
        compiler_params=pltpu.CompilerParams(
            dimension_semantics=("arbitrary",),
            vmem_limit_bytes=VMEM_LIMIT_BYTES),
        name="attn_sample",
    )(x, mods, mods, mods, norm_g, w_q, w_o, k_cache, v_cache, k_new, v_new, *tables, sinks)


def _rope_angles(pos):
    inv_freq = ROPE_THETA ** (-jnp.arange(0, ROT_DIM, 2, dtype=F32) / ROT_DIM)
    ang = pos.astype(F32)[:, None] * inv_freq[None, :]
    return jnp.cos(ang), jnp.sin(ang)


def _rope_tables(pos):
    half = ROT_DIM // 2
    cos, sin = _rope_angles(pos)
    n = pos.shape[0]
    rest = HEAD_DIM - ROT_DIM
    zeros = jnp.zeros((n, half), F32)
    cos_t = jnp.concatenate([cos, cos, jnp.ones((n, rest), F32)], axis=1)
    sin_up = jnp.concatenate([-sin, zeros, jnp.zeros((n, rest), F32)], axis=1)
    sin_dn = jnp.concatenate([zeros, sin, jnp.zeros((n, rest), F32)], axis=1)
    reps = LANES // HEAD_DIM
    return tuple(jnp.tile(t, (1, reps)) for t in (cos_t, sin_up, sin_dn))


def kernel(x_prompt, x_sample, state_conv, cache_k_win, cache_v_win, c_prompt, c_sample, norm_g, w_ada, b_ada, w_ffn_gate, w_ffn_up, w_ffn_down, conv_w_in, conv_w, conv_w_out, kv_norm_g, w_ada_kv, b_ada_kv, w_k, w_v, attn_w_q, attn_sinks, attn_w_o, final_norm_g):
    batch, seq, d = x_prompt.shape
    dec_batch, dec_seq, _ = x_sample.shape
    w_buf = cache_k_win.shape[1]
    past_len = 16384
    assert d == D_MODEL and w_buf == WINDOW and dec_seq == SUBLANES
    assert w_ada.shape[0] == 2 and conv_w_in.shape[0] == 1 and attn_w_q.shape[0] == 1

    pad = (-(dec_batch + batch)) % SUBLANES
    c_all = jnp.concatenate([c_sample, c_prompt, jnp.zeros((pad, d), F32)], axis=0)
    mods = _ada(c_all, w_ada, b_ada)
    mods_kv = _ada(c_all, w_ada_kv[None], b_ada_kv[None])
    prompt_mod_block = dec_batch // SUBLANES

    make_groups = lambda tile: {
        "prompt": _Group(batch, seq, None, prompt_mod_block, tile),
        "sample": _Group(dec_batch, dec_seq, dec_seq, prompt_mod_block, tile),
    }
    ffn_groups, mixer_groups = make_groups(FFN_TILE), make_groups(MIXER_TILE)
    xp, xs = x_prompt.reshape(batch * seq, d), x_sample.reshape(dec_batch * dec_seq, d)

    norm_g3 = norm_g.reshape(-1, 1, d)
    w_in, w_out = conv_w_in[0].astype(BF16), conv_w_out[0].astype(BF16)
    w_q, w_o = attn_w_q[0].astype(BF16), attn_w_o[0].astype(BF16)
    sinks = attn_sinks[0]

    pos_dec = past_len + jnp.arange(dec_seq, dtype=jnp.int32)
    pos_prompt = jnp.arange(seq, dtype=jnp.int32)
    tab_prompt = _rope_tables(pos_prompt)
    tab_dec = tuple(jnp.tile(t, (SAMPLE_ATTN_SEQS, 1)) for t in _rope_tables(pos_dec))
    tab_prompt_t = tuple(t.T for t in _rope_angles(pos_prompt))
    tab_dec_t = tuple(jnp.tile(t.T, (1, FFN_TILE // dec_seq)) for t in _rope_angles(pos_dec))
    kv_common = dict(norm_g=kv_norm_g.reshape(1, d), mods=mods_kv, w_v=w_v.T.astype(BF16))
    kv_prompt = dict(kv_common, transposed=False, tables=tab_prompt, w_k=w_k.astype(BF16))
    kv_decode = dict(kv_common, transposed=True, tables=tab_dec_t, w_k=w_k.T.astype(BF16))
    to_rows = lambda c: jnp.transpose(c, (0, 2, 3, 1)).reshape(dec_batch * KV_DIM, w_buf)
    from_rows = lambda c: jnp.transpose(c.reshape(dec_batch, N_KV_HEADS, HEAD_DIM, w_buf), (0, 3, 1, 2))

    ffn = functools.partial(_ffn, mods=mods, norm_g=norm_g3, wg=w_ffn_gate, wu=w_ffn_up, wd=w_ffn_down)
    gp, gs = ffn_groups["prompt"], ffn_groups["sample"]
    (xp,), (xs,) = ffn(xp, gp, layer=0, sub=0, slot=0), ffn(xs, gs, layer=0, sub=0, slot=0)
    xp, u_p = _conv_mixer(xp, mixer_groups["prompt"], mods, norm_g3, w_in, conv_w[0], w_out)
    xs, u_s = _conv_mixer(xs, mixer_groups["sample"], mods, norm_g3, w_in, conv_w[0], w_out,
                          (state_conv[0, :, 0, :], state_conv[0, :, 1, :]))
    (xp,), (xs,) = ffn(xp, gp, layer=0, sub=2, slot=1), ffn(xs, gs, layer=0, sub=2, slot=1)
    xp, k_p, vt_p = ffn(xp, gp, layer=1, sub=0, slot=0, kv=kv_prompt)
    xs, kt_s, vt_s = ffn(xs, gs, layer=1, sub=0, slot=0, kv=kv_decode)
    xp = _attn_prompt(xp, mixer_groups["prompt"], mods, norm_g3, w_q, w_o, k_p, vt_p, tab_prompt_t, sinks)
    xs, k_win, v_win = _attn_sample(xs, dec_batch, dec_seq, mods, norm_g3, w_q, w_o,
                                    to_rows(cache_k_win), to_rows(cache_v_win), kt_s, vt_s, tab_dec, sinks)
    final_g = final_norm_g.reshape(1, d)
    (y_p,) = ffn(xp, gp, layer=1, sub=2, slot=1, final_g=final_g)
    (y_s,) = ffn(xs, gs, layer=1, sub=2, slot=1, final_g=final_g)

    k_state_p = k_p.reshape(batch, seq, KV_DIM)[:, seq - WINDOW:].reshape(
        batch, WINDOW, N_KV_HEADS, HEAD_DIM)
    v_state_p = jnp.stack([vt_p[:, (b + 1) * seq - WINDOW:(b + 1) * seq] for b in range(batch)])
    v_state_p = jnp.transpose(v_state_p.reshape(batch, N_KV_HEADS, HEAD_DIM, WINDOW), (0, 3, 1, 2))
    k_state_s, v_state_s = from_rows(k_win), from_rows(v_win)
    tail = CONV_WIDTH - 1
    conv_p = u_p.reshape(batch, SUBLANES, d)[:, SUBLANES - tail:][None]
    conv_s = u_s.reshape(dec_batch, dec_seq, d)[:, dec_seq - tail:][None]
    return (y_p.reshape(batch, seq, d), y_s.reshape(dec_batch, dec_seq, d), conv_p, conv_s,
            k_state_p, v_state_p, k_state_s, v_state_s)
```

```python
import functools

import jax
import jax.numpy as jnp
from jax import lax
from jax.experimental import pallas as pl
from jax.experimental.pallas import tpu as pltpu

F32 = jnp.float32
BF16 = jnp.bfloat16

D_MODEL = 1024
D_FF = 2816
HEAD_DIM = 64
N_HEADS = 16
N_KV_HEADS = 4
GROUP = N_HEADS // N_KV_HEADS
KV_DIM = N_KV_HEADS * HEAD_DIM
WINDOW = 128
ROT_DIM = 16
ROPE_THETA = 500000.0
CONV_WIDTH = 3
N_MOD = 9
EPS = 1e-6
NEG = -1e30
FMAX = float(jnp.finfo(jnp.float32).max)

SUBLANES = 8
LANES = 128
VMEM_LIMIT_BYTES = 60 * 1024 * 1024

FFN_TILE = 512
FF_STAGE = 256
MIXER_TILE = 512
ATTN_CHUNK = 256
LOG2E = 1.4426950408889634
SAMPLE_ATTN_SEQS = 16
FF_CHUNKS = ((0, 1024), (1024, 2048), (2048, D_FF))


def _dot(a, b):
    return jnp.dot(a, b, preferred_element_type=F32)


def _dot_nt(a, b):
    return lax.dot_general(a, b, (((1,), (1,)), ((), ())), preferred_element_type=F32)


def _log2(n):
    assert n & (n - 1) == 0, n
    return n.bit_length() - 1


def _silu(x):
    return x * jax.nn.sigmoid(x)


def _unit_rms(x):
    return x * lax.rsqrt(jnp.mean(x * x, axis=-1, keepdims=True) + EPS)


def _repeat_rows(ref, reps):
    n = ref.shape[1]
    return jnp.concatenate(
        [jnp.broadcast_to(ref[i:i + 1, :], (reps, n)) for i in range(ref.shape[0])], axis=0)


def _mod_rows(ref, per_seq_rows):
    if per_seq_rows is None:
        return ref[pl.ds(pl.program_id(0), 1), :]
    return _repeat_rows(ref, per_seq_rows)


def _rope(x, cos_t, sin_up_t, sin_dn_t):
    n = x.shape[1]
    reps = n // LANES
    half = ROT_DIM // 2
    cos = jnp.concatenate([cos_t] * reps, axis=1)
    sin_up = jnp.concatenate([sin_up_t] * reps, axis=1)
    sin_dn = jnp.concatenate([sin_dn_t] * reps, axis=1)
    x_up = pltpu.roll(x, n - half, axis=1)
    x_dn = pltpu.roll(x, half, axis=1)
    return x * cos + x_up * sin_up + x_dn * sin_dn


def _ada_kernel(c_ref, w_ref, b_ref, o_ref):
    a = _silu(c_ref[...]).astype(BF16)
    o_ref[...] = _dot(a, w_ref[...].astype(BF16)) + b_ref[...]


def _ada(c_all, w, b):
    n_layers, _, n = w.shape
    m = c_all.shape[0]
    tn = D_MODEL
    return pl.pallas_call(
        _ada_kernel,
        grid=(n_layers, n // tn),
        in_specs=[
            pl.BlockSpec((m, D_MODEL), lambda l, j: (0, 0)),
            pl.BlockSpec((None, D_MODEL, tn), lambda l, j: (l, 0, j)),
            pl.BlockSpec((None, 1, tn), lambda l, j: (l, 0, j)),
        ],
        out_specs=pl.BlockSpec((None, m, tn), lambda l, j: (l, 0, j)),
        out_shape=jax.ShapeDtypeStruct((n_layers, m, n), F32),
        compiler_params=pltpu.CompilerParams(
            dimension_semantics=("arbitrary", "arbitrary"),
            vmem_limit_bytes=VMEM_LIMIT_BYTES),
        name="adaln_tables",
    )(c_all, w, b.reshape(n_layers, 1, n))


def _rope_t(xt, cos_t, sin_t):
    half = ROT_DIM // 2
    out = []
    for h in range(xt.shape[0] // HEAD_DIM):
        base = h * HEAD_DIM
        x1 = xt[base:base + half, :]
        x2 = xt[base + half:base + ROT_DIM, :]
        out += [x1 * cos_t - x2 * sin_t, x2 * cos_t + x1 * sin_t,
                xt[base + ROT_DIM:base + HEAD_DIM, :]]
    return jnp.concatenate(out, axis=0)


def _ffn_kernel(*refs, n_prompt_tiles, tiles_per_seq, dec_seq, layer, slot, with_kv, final):
    it = iter(refs)
    take = lambda n: [next(it) for _ in range(n)]
    xp_ref, xs_ref = take(2)
    p_mods, s_mods = take(3), take(3)
    ng_ref, wg_hbm, wu_hbm, wd_hbm = take(4)
    if with_kv:
        (kvg_ref,), p_kvmods, s_kvmods = take(1), take(2), take(2)
        wk_ref, wkt_ref, wvt_ref = take(3)
        p_tabs, s_tabs = take(3), take(2)
    else:
        p_kvmods = s_kvmods = None
    if final:
        (fg_ref,) = take(1)
    op_ref, os_ref = take(2)
    if with_kv:
        kp_ref, vtp_ref, kts_ref, vts_ref = take(4)
    wg_s, wu_s, wd_s, stage_in, stage_out, sem = take(6)

    i = pl.program_id(0)
    prompt_rows = lambda ref: ref[pl.ds(i // tiles_per_seq, 1), :]
    decode_rows = lambda ref: _repeat_rows(ref, dec_seq)

    def weight_copies(c):
        cols = pl.ds(c * FF_STAGE, FF_STAGE)
        buf = c % 2
        return (
            pltpu.make_async_copy(wg_hbm.at[layer, slot, :, cols], stage_in.at[buf, 0], sem.at[buf, 0]),
            pltpu.make_async_copy(wu_hbm.at[layer, slot, :, cols], stage_in.at[buf, 1], sem.at[buf, 1]),
            pltpu.make_async_copy(wd_hbm.at[layer, slot, cols, :], stage_out.at[buf], sem.at[buf, 2]),
        )

    def body(x_ref, o_ref, rows_of, mods, kvmods, stage_weights):
        x = x_ref[...]
        xn = _unit_rms(x)
        hb = ((xn * ng_ref[...]) * (1.0 + rows_of(mods[1])) + rows_of(mods[0])).astype(BF16)

        def partial_ffn(lo, hi):
            g = _dot(hb, wg_s[:, lo:hi])
            u = _dot(hb, wu_s[:, lo:hi])
            return _dot((_silu(g) * u).astype(BF16), wd_s[lo:hi, :])

        y = None
        if stage_weights:
            n_slices = D_FF // FF_STAGE
            for c in range(min(2, n_slices)):
                for cp in weight_copies(c):
                    cp.start()
            for c in range(n_slices):
                lo, hi = c * FF_STAGE, (c + 1) * FF_STAGE
                for cp in weight_copies(c):
                    cp.wait()
                wg_s[:, lo:hi] = stage_in[c % 2, 0].astype(BF16)
                wu_s[:, lo:hi] = stage_in[c % 2, 1].astype(BF16)
                wd_s[lo:hi, :] = stage_out[c % 2].astype(BF16)
                if c + 2 < n_slices:
                    for cp in weight_copies(c + 2):
                        cp.start()
                part = partial_ffn(lo, hi)
                y = part if y is None else y + part
        else:
            for lo, hi in FF_CHUNKS:
                part = partial_ffn(lo, hi)
                y = part if y is None else y + part
        x_new = x + (0.5 * rows_of(mods[2])) * y
        o_ref[...] = _unit_rms(x_new) * fg_ref[...] if final else x_new
        if with_kv:
            return ((xn * kvg_ref[...]) * (1.0 + rows_of(kvmods[1])) + rows_of(kvmods[0])).astype(BF16)

    def prompt_step(stage_weights):
        hk = body(xp_ref, op_ref, prompt_rows, p_mods, p_kvmods, stage_weights)
        if with_kv:
            kp_ref[...] = _rope(_dot(hk, wk_ref[...]), *(t[...] for t in p_tabs))
            vtp_ref[...] = _dot_nt(wvt_ref[...], hk)

    @pl.when(i == 0)
    def _():
        prompt_step(True)

    @pl.when((i > 0) & (i < n_prompt_tiles))
    def _():
        prompt_step(False)

    @pl.when(i >= n_prompt_tiles)
    def _():
        hk = body(xs_ref, os_ref, decode_rows, s_mods, s_kvmods, False)
        if with_kv:
            kts_ref[...] = _rope_t(_dot_nt(wkt_ref[...], hk), *(t[...] for t in s_tabs))
            vts_ref[...] = _dot_nt(wvt_ref[...], hk)


def _resident(shape, index_map):
    return pl.BlockSpec(shape, index_map, pipeline_mode=pl.Buffered(1))


def _mod_spec(layer, col, group):
    if group.per_seq_rows is None:
        return pl.BlockSpec((None, SUBLANES, D_MODEL),
                            lambda b, t: (layer, group.prompt_mod_block, col))
    return pl.BlockSpec((None, group.tile // group.per_seq_rows, D_MODEL),
                        lambda b, t: (layer, t, col))


class _Group:
    def __init__(self, n_seq, seq_len, per_seq_rows, prompt_mod_block, tile):
        self.tile = tile
        self.rows = n_seq * seq_len
        self.per_seq_rows = per_seq_rows
        self.prompt_mod_block = prompt_mod_block
        if per_seq_rows is None:
            self.grid = (n_seq, seq_len // self.tile)
        else:
            self.grid = (1, self.rows // self.tile)
        self.tiles_per_seq = self.grid[1]

    def row_spec(self, width):
        nt = self.grid[1]
        return pl.BlockSpec((self.tile, width), lambda b, t: (b * nt + t, 0))

    def col_spec(self, height):
        nt = self.grid[1]
        return pl.BlockSpec((height, self.tile), lambda b, t: (0, b * nt + t))

    def table_t_spec(self):
        if self.per_seq_rows is None:
            return pl.BlockSpec((ROT_DIM // 2, self.tile), lambda b, t: (0, t))
        return pl.BlockSpec((ROT_DIM // 2, self.tile), lambda b, t: (0, 0))

    def table_spec(self):
        if self.per_seq_rows is None:
            return pl.BlockSpec((self.tile, LANES), lambda b, t: (t, 0))
        return pl.BlockSpec((self.tile, LANES), lambda b, t: (0, 0))


def _ffn(xp, xs, prompt, decode, mods, layer, sub, norm_g, wg, wu, wd, slot, kv=None, final_g=None):
    with_kv = kv is not None
    final = final_g is not None
    tile = prompt.tile
    assert decode.tile == tile
    n_p, n_s = prompt.rows // tile, decode.rows // tile
    tiles_per_seq = prompt.tiles_per_seq
    p_tile = lambda i: jnp.minimum(i, n_p - 1)
    s_tile = lambda i: jnp.maximum(i - n_p, 0)
    n_dec_seq = tile // decode.per_seq_rows

    def mod_specs(table_layer, cols):
        p = [pl.BlockSpec((None, SUBLANES, D_MODEL),
                          functools.partial(lambda i, c: (table_layer, prompt.prompt_mod_block, c), c=c))
             for c in cols]
        s = [pl.BlockSpec((None, n_dec_seq, D_MODEL),
                          functools.partial(lambda i, c: (table_layer, s_tile(i), c), c=c),
                          pipeline_mode=pl.Buffered(1))
             for c in cols]
        return p + s

    p_rows = lambda width: pl.BlockSpec((tile, width), lambda i: (p_tile(i), 0))
    s_rows = lambda width, **kw: pl.BlockSpec((tile, width), lambda i: (s_tile(i), 0), **kw)
    p_cols = lambda height: pl.BlockSpec((height, tile), lambda i: (0, p_tile(i)))
    s_cols = lambda height: pl.BlockSpec((height, tile), lambda i: (0, s_tile(i)))
    const = lambda shape: _resident(shape, lambda i: (0,) * len(shape))
    hbm = pl.BlockSpec(memory_space=pl.ANY)

    in_specs = [p_rows(D_MODEL), s_rows(D_MODEL, pipeline_mode=pl.Buffered(1))]
    in_specs += mod_specs(layer, [3 * sub + 0, 3 * sub + 1, 3 * sub + 2])
    in_specs += [_resident((None, 1, D_MODEL), lambda i: (layer * 3 + sub, 0, 0)), hbm, hbm, hbm]
    args = [xp, xs] + [mods] * 6 + [norm_g, wg, wu, wd]
    out_specs = [p_rows(D_MODEL), s_rows(D_MODEL)]
    out_shape = [jax.ShapeDtypeStruct((prompt.rows, D_MODEL), F32),
                 jax.ShapeDtypeStruct((decode.rows, D_MODEL), F32)]
    if with_kv:
        in_specs += [const((1, D_MODEL))] + mod_specs(0, [0, 1])
        in_specs += [const((D_MODEL, KV_DIM)), const((KV_DIM, D_MODEL)), const((KV_DIM, D_MODEL))]
        in_specs += [pl.BlockSpec((tile, LANES), lambda i: (p_tile(i) % tiles_per_seq, 0))] * 3
        in_specs += [const((ROT_DIM // 2, tile))] * 2
        args += [kv["norm_g"]] + [kv["mods"]] * 4 + [kv["w_k"], kv["w_k_t"], kv["w_v_t"]]
        args += [*kv["prompt_tables"], *kv["decode_tables_t"]]
        out_specs += [p_rows(KV_DIM), p_cols(KV_DIM), s_cols(KV_DIM), s_cols(KV_DIM)]
        out_shape += [jax.ShapeDtypeStruct((prompt.rows, KV_DIM), F32),
                      jax.ShapeDtypeStruct((KV_DIM, prompt.rows), F32),
                      jax.ShapeDtypeStruct((KV_DIM, decode.rows), F32),
                      jax.ShapeDtypeStruct((KV_DIM, decode.rows), F32)]
    if final:
        in_specs.append(const((1, D_MODEL)))
        args.append(final_g)
    scratch = [
        pltpu.VMEM((D_MODEL, D_FF), BF16), pltpu.VMEM((D_MODEL, D_FF), BF16),
        pltpu.VMEM((D_FF, D_MODEL), BF16),
        pltpu.VMEM((2, 2, D_MODEL, FF_STAGE), F32),
        pltpu.VMEM((2, FF_STAGE, D_MODEL), F32),
        pltpu.SemaphoreType.DMA((2, 3)),
    ]
    return pl.pallas_call(
        functools.partial(_ffn_kernel, n_prompt_tiles=n_p, tiles_per_seq=tiles_per_seq,
                          dec_seq=decode.per_seq_rows, layer=layer, slot=slot,
                          with_kv=with_kv, final=final),
        grid=(n_p + n_s,),
        in_specs=in_specs,
        out_specs=out_specs,
        out_shape=out_shape,
        scratch_shapes=scratch,
        compiler_params=pltpu.CompilerParams(
            dimension_semantics=("arbitrary",),
            vmem_limit_bytes=VMEM_LIMIT_BYTES),
        name=f"ffn_l{layer}_s{sub}",
    )(*args)


def _conv_kernel(*refs, per_seq_rows):
    it = iter(refs)
    x_ref, sh_ref, sc_ref, gt_ref, ng_ref, win_ref, cw_ref, wout_ref = (next(it) for _ in range(8))
    if per_seq_rows is not None:
        st0_ref, st1_ref = next(it), next(it)
    o_ref, u_ref = next(it), next(it)
    if per_seq_rows is None:
        carry_ref = next(it)

    tile = x_ref.shape[0]
    x = x_ref[...]
    h = ((_unit_rms(x) * ng_ref[...]) * (1.0 + _mod_rows(sc_ref, per_seq_rows))
         + _mod_rows(sh_ref, per_seq_rows)).astype(BF16)
    bcv = _dot(h, win_ref[...])
    b_g = bcv[:, :D_MODEL]
    u = bcv[:, D_MODEL:2 * D_MODEL] * bcv[:, 2 * D_MODEL:]

    row = lax.broadcasted_iota(jnp.int32, (tile, 1), 0)
    if per_seq_rows is None:
        @pl.when(pl.program_id(1) == 0)
        def _():
            carry_ref[...] = jnp.zeros_like(carry_ref)
        prev2 = carry_ref[SUBLANES - 2:SUBLANES - 1, :]
        prev1 = carry_ref[SUBLANES - 1:SUBLANES, :]
        t = row
    else:
        prev2 = _repeat_rows(st0_ref, per_seq_rows)
        prev1 = _repeat_rows(st1_ref, per_seq_rows)
        t = row & (per_seq_rows - 1)
    u1 = jnp.where(t == 0, prev1, pltpu.roll(u, 1, axis=0))
    u2 = jnp.where(t == 0, prev2, jnp.where(t == 1, prev1, pltpu.roll(u, 2, axis=0)))
    conv = u2 * cw_ref[0:1, :] + u1 * cw_ref[1:2, :] + u * cw_ref[2:3, :]
    y = _dot((b_g * conv).astype(BF16), wout_ref[...])
    o_ref[...] = x + _mod_rows(gt_ref, per_seq_rows) * y

    if per_seq_rows is None:
        tail = u[tile - SUBLANES:, :]
        carry_ref[...] = tail
        u_ref[...] = tail
    else:
        u_ref[...] = u


def _conv_mixer(x, group, mods, norm_g, w_in, conv_w, w_out, state=None):
    prompt = group.per_seq_rows is None
    in_specs = [
        group.row_spec(D_MODEL),
        _mod_spec(0, 3, group), _mod_spec(0, 4, group), _mod_spec(0, 5, group),
        _resident((None, 1, D_MODEL), lambda b, t: (1, 0, 0)),
        _resident((D_MODEL, 3 * D_MODEL), lambda b, t: (0, 0)),
        _resident((CONV_WIDTH, D_MODEL), lambda b, t: (0, 0)),
        _resident((D_MODEL, D_MODEL), lambda b, t: (0, 0)),
    ]
    args = [x, mods, mods, mods, norm_g, w_in, conv_w, w_out]
    scratch = []
    if prompt:
        u_spec = pl.BlockSpec((SUBLANES, D_MODEL), lambda b, t: (b, 0))
        u_shape = jax.ShapeDtypeStruct((group.grid[0] * SUBLANES, D_MODEL), F32)
        scratch.append(pltpu.VMEM((SUBLANES, D_MODEL), F32))
    else:
        n_seq = group.tile // group.per_seq_rows
        st_spec = pl.BlockSpec((n_seq, D_MODEL), lambda b, t: (t, 0))
        in_specs += [st_spec, st_spec]
        args += [state[0], state[1]]
        u_spec = group.row_spec(D_MODEL)
        u_shape = jax.ShapeDtypeStruct((group.rows, D_MODEL), F32)
    return pl.pallas_call(
        functools.partial(_conv_kernel, per_seq_rows=group.per_seq_rows),
        grid=group.grid,
        in_specs=in_specs,
        out_specs=[group.row_spec(D_MODEL), u_spec],
        out_shape=[jax.ShapeDtypeStruct((group.rows, D_MODEL), F32), u_shape],
        scratch_shapes=scratch,
        compiler_params=pltpu.CompilerParams(
            dimension_semantics=("arbitrary", "arbitrary"),
            vmem_limit_bytes=VMEM_LIMIT_BYTES),
        name="conv_mixer",
    )(*args)


def _sink_column(sink_ref, kv_head, q_rows):
    grp = lax.broadcasted_iota(jnp.int32, (GROUP * q_rows, 1), 0) >> _log2(q_rows)
    col = jnp.full((GROUP * q_rows, 1), sink_ref[kv_head * GROUP + GROUP - 1], F32)
    for g in range(GROUP - 2, -1, -1):
        col = jnp.where(grp == g, sink_ref[kv_head * GROUP + g], col)
    return col


def _stack_heads(q_ref, rows, kv_head):
    return jnp.concatenate(
        [q_ref[rows, (kv_head * GROUP + g) * HEAD_DIM:(kv_head * GROUP + g + 1) * HEAD_DIM]
         for g in range(GROUP)], axis=0)


def _softmax_parts(scores, sink):
    m = sink
    for s in scores:
        m = jnp.maximum(m, jnp.max(s, axis=-1, keepdims=True))
    ps = [jnp.exp(s - m) for s in scores]
    denom = jnp.exp(sink - m)
    for p in ps:
        denom = denom + jnp.sum(p, axis=-1, keepdims=True)
    return ps, 1.0 / denom


def _project_q(x_ref, sh_ref, sc_ref, ng_ref, wq_ref, tables, per_seq_rows):
    h = ((_unit_rms(x_ref[...]) * ng_ref[...]) * (1.0 + _mod_rows(sc_ref, per_seq_rows))
         + _mod_rows(sh_ref, per_seq_rows)).astype(BF16)
    q = _rope(_dot(h, wq_ref[...]), *tables)
    return (q * (HEAD_DIM ** -0.5)).astype(BF16)


def _band_caps():
    shape = (2 * WINDOW, GROUP * WINDOW)
    kj = lax.broadcasted_iota(jnp.int32, shape, 0)
    qi = lax.broadcasted_iota(jnp.int32, shape, 1) & (WINDOW - 1)
    in_band = (kj > qi) & (kj <= qi + WINDOW)
    return (jnp.where(in_band, FMAX, NEG), jnp.where(in_band & (kj >= WINDOW), FMAX, NEG))


def _attn_prompt_kernel(x_ref, sh_ref, sc_ref, gt_ref, ng_ref, wq_ref, wo_ref,
                        kp_ref, kc_ref, vp_ref, vc_ref, cos_ref, sin_ref, sink_ref,
                        o_ref, q_scr, k_scr, v_scr, a_scr, cap_scr, h_scr, at_scr):
    tile = x_ref.shape[0]

    @pl.when((pl.program_id(0) == 0) & (pl.program_id(1) == 0))
    def _():
        general, first = _band_caps()
        cap_scr[0] = general
        cap_scr[1] = first

    k_scr[:WINDOW, :] = kp_ref[...].astype(BF16)
    k_scr[WINDOW:, :] = kc_ref[...].astype(BF16)
    v_scr[:, :WINDOW] = vp_ref[...].astype(BF16)
    v_scr[:, WINDOW:] = vc_ref[...].astype(BF16)
    first_tile = pl.program_id(1) == 0
    norm_gain = ng_ref[...]
    scale1 = 1.0 + _mod_rows(sc_ref, None)
    shift = _mod_rows(sh_ref, None)
    gate = _mod_rows(gt_ref, None)

    group_dims = GROUP * HEAD_DIM

    def chunk_rows(c):
        return slice(c * ATTN_CHUNK, (c + 1) * ATTN_CHUNK)

    def normalize(c):
        rows = chunk_rows(c)
        h_scr[rows, :] = ((_unit_rms(x_ref[rows, :]) * norm_gain) * scale1 + shift).astype(BF16)

    def project_piece(c, kvh):
        rows = chunk_rows(c)
        dims = slice(kvh * group_dims, (kvh + 1) * group_dims)
        qt = _rope_t(_dot(h_scr[rows, :], wq_ref[:, dims]).T, cos_ref[:, rows], sin_ref[:, rows])
        q_scr[dims, rows] = (qt * (HEAD_DIM ** -0.5 * LOG2E)).astype(BF16)

    def attend(blk, kvh):
        cols = slice(blk * WINDOW, (blk + 1) * WINDOW)
        keys = slice(blk * WINDOW, (blk + 2) * WINDOW)
        cap = cap_scr[jnp.where(first_tile, 1, 0)] if blk == 0 else cap_scr[0]
        dims = slice(kvh * HEAD_DIM, (kvh + 1) * HEAD_DIM)
        heads = [slice((kvh * GROUP + g) * HEAD_DIM, (kvh * GROUP + g + 1) * HEAD_DIM)
                 for g in range(GROUP)]
        qg = jnp.concatenate([q_scr[hd, cols] for hd in heads], axis=1)
        s = jnp.minimum(_dot(k_scr[keys, dims], qg), cap)
        sink = LOG2E * jnp.concatenate(
            [jnp.full((1, WINDOW), sink_ref[kvh * GROUP + g], F32) for g in range(GROUP)], axis=1)
        m = jnp.maximum(jnp.max(s, axis=0, keepdims=True), sink)
        p = jnp.exp2(s - m)
        denom = jnp.sum(p, axis=0, keepdims=True) + jnp.exp2(sink - m)
        o = _dot(v_scr[dims, keys], p.astype(BF16)) * (1.0 / denom)
        for g, hd in enumerate(heads):
            a_scr[hd, cols] = o[:, g * WINDOW:(g + 1) * WINDOW]

    def transpose_piece(c, kvh):
        rows = chunk_rows(c)
        dims = slice(kvh * group_dims, (kvh + 1) * group_dims)
        at_scr[rows, dims] = a_scr[dims, rows].T.astype(BF16)

    def emit_piece(c, n):
        rows = chunk_rows(c)
        cols = slice(n * group_dims, (n + 1) * group_dims)
        y = _dot(at_scr[rows, :], wo_ref[:, cols])
        o_ref[rows, cols] = x_ref[rows, cols] + gate[:, cols] * y

    n_chunks = tile // ATTN_CHUNK
    blocks_per_chunk = ATTN_CHUNK // WINDOW
    n_pieces = D_MODEL // group_dims
    normalize(0)
    for kvh in range(N_KV_HEADS):
        project_piece(0, kvh)
    for c in range(n_chunks):
        if c + 1 < n_chunks:
            normalize(c + 1)
        step = 0
        for b in range(blocks_per_chunk):
            for kvh in range(N_KV_HEADS):
                attend(c * blocks_per_chunk + b, kvh)
                if b == blocks_per_chunk - 1:
                    transpose_piece(c, kvh)
                if step % 2 == 0 and c + 1 < n_chunks and step // 2 < N_KV_HEADS:
                    project_piece(c + 1, step // 2)
                if step % 2 == 1 and c > 0 and step // 2 < n_pieces:
                    emit_piece(c - 1, step // 2)
                step += 1
    for n in range(n_pieces):
        emit_piece(n_chunks - 1, n)


def _attn_prompt(x, group, mods, norm_g, w_q, w_o, k, v_t, tables_t, sinks):
    tile = group.tile
    nt = group.grid[1]
    blocks_per_tile = tile // WINDOW

    def prev_block(b, t):
        return jnp.maximum((b * nt + t) * blocks_per_tile - 1, 0)

    in_specs = [
        group.row_spec(D_MODEL),
        _mod_spec(1, 3, group), _mod_spec(1, 4, group), _mod_spec(1, 5, group),
        _resident((None, 1, D_MODEL), lambda b, t: (4, 0, 0)),
        _resident((D_MODEL, D_MODEL), lambda b, t: (0, 0)),
        _resident((D_MODEL, D_MODEL), lambda b, t: (0, 0)),
        pl.BlockSpec((WINDOW, KV_DIM), lambda b, t: (prev_block(b, t), 0)), group.row_spec(KV_DIM),
        pl.BlockSpec((KV_DIM, WINDOW), lambda b, t: (0, prev_block(b, t))), group.col_spec(KV_DIM),
        group.table_t_spec(), group.table_t_spec(),
        pl.BlockSpec(memory_space=pltpu.SMEM),
    ]
    return pl.pallas_call(
        _attn_prompt_kernel,
        grid=group.grid,
        in_specs=in_specs,
        out_specs=group.row_spec(D_MODEL),
        out_shape=jax.ShapeDtypeStruct((group.rows, D_MODEL), F32),
        scratch_shapes=[
            pltpu.VMEM((D_MODEL, tile), BF16),
            pltpu.VMEM((tile + WINDOW, KV_DIM), BF16),
            pltpu.VMEM((KV_DIM, tile + WINDOW), BF16),
            pltpu.VMEM((D_MODEL, tile), F32),
            pltpu.VMEM((2, 2 * WINDOW, GROUP * WINDOW), F32),
            pltpu.VMEM((tile, D_MODEL), BF16),
            pltpu.VMEM((tile, D_MODEL), BF16),
        ],
        compiler_params=pltpu.CompilerParams(
            dimension_semantics=("arbitrary", "arbitrary"),
            vmem_limit_bytes=VMEM_LIMIT_BYTES),
        name="attn_prompt",
    )(x, mods, mods, mods, norm_g, w_q, w_o, k, k, v_t, v_t, *tables_t, sinks)


def _attn_sample_kernel(x_ref, sh_ref, sc_ref, gt_ref, ng_ref, wq_ref, wo_ref,
                        kc_ref, vc_ref, kn_ref, vn_ref, cos_ref, sup_ref, sdn_ref, sink_ref,
                        o_ref, kout_ref, vout_ref, q_scr, *, dec_seq):
    rows = x_ref.shape[0]
    n_seq = rows // dec_seq
    n_cache = n_seq * WINDOW
    q_scr[...] = _project_q(x_ref, sh_ref, sc_ref, ng_ref, wq_ref,
                            (cos_ref[...], sup_ref[...], sdn_ref[...]), dec_seq)

    def all_windows(ref, kvh):
        return jnp.concatenate(
            [ref[(s * N_KV_HEADS + kvh) * HEAD_DIM:(s * N_KV_HEADS + kvh + 1) * HEAD_DIM, :].astype(BF16)
             for s in range(n_seq)], axis=1)

    q_row = lax.broadcasted_iota(jnp.int32, (GROUP * rows, 1), 0) & (rows - 1)
    q_seq, q_i = q_row >> _log2(dec_seq), q_row & (dec_seq - 1)
    c_col = lax.broadcasted_iota(jnp.int32, (1, n_cache), 1)
    valid_c = (q_seq == (c_col >> _log2(WINDOW))) & ((c_col & (WINDOW - 1)) > q_i)
    n_col = lax.broadcasted_iota(jnp.int32, (1, rows), 1)
    valid_n = (q_seq == (n_col >> _log2(dec_seq))) & ((n_col & (dec_seq - 1)) <= q_i)

    heads = []
    for kvh in range(N_KV_HEADS):
        dims = slice(kvh * HEAD_DIM, (kvh + 1) * HEAD_DIM)
        qg = _stack_heads(q_scr, slice(None), kvh)
        s_c = jnp.where(valid_c, _dot(qg, all_windows(kc_ref, kvh)), NEG)
        s_n = jnp.where(valid_n, _dot(qg, kn_ref[dims, :].astype(BF16)), NEG)
        (p_c, p_n), inv = _softmax_parts([s_c, s_n], _sink_column(sink_ref, kvh, rows))
        o = (_dot_nt(p_c.astype(BF16), all_windows(vc_ref, kvh))
             + _dot_nt(p_n.astype(BF16), vn_ref[dims, :].astype(BF16))) * inv
        heads += [o[g * rows:(g + 1) * rows, :] for g in range(GROUP)]
    attn = jnp.concatenate(heads, axis=1).astype(BF16)
    o_ref[...] = x_ref[...] + _mod_rows(gt_ref, dec_seq) * _dot(attn, wo_ref[...])

    keep = lax.broadcasted_iota(jnp.int32, (1, WINDOW), 1) < WINDOW - dec_seq
    for src_ref, new_ref, dst_ref in ((kc_ref, kn_ref, kout_ref), (vc_ref, vn_ref, vout_ref)):
        new = new_ref[...]
        for s in range(n_seq):
            blk = slice(s * KV_DIM, (s + 1) * KV_DIM)
            old = pltpu.roll(src_ref[blk, :], WINDOW - dec_seq, axis=1)
            app = pltpu.roll(new, (WINDOW - dec_seq - s * dec_seq) % rows, axis=1)
            dst_ref[blk, :] = jnp.where(keep, old, app)


def _attn_sample(x, n_seq_total, dec_seq, mods, norm_g, w_q, w_o, k_cache, v_cache,
                 k_new, v_new, tables, sinks):
    n_seq = SAMPLE_ATTN_SEQS
    rows = n_seq * dec_seq
    assert rows == WINDOW == LANES
    grid = (n_seq_total // n_seq,)
    row_spec = lambda width: pl.BlockSpec((rows, width), lambda t: (t, 0))
    mod_spec = lambda col: pl.BlockSpec((None, n_seq, D_MODEL), lambda t: (1, t, col))
    cache_spec = pl.BlockSpec((n_seq * KV_DIM, WINDOW), lambda t: (t, 0))
    new_spec = pl.BlockSpec((KV_DIM, rows), lambda t: (0, t))
    table_spec = pl.BlockSpec((rows, LANES), lambda t: (0, 0))
    in_specs = [
        row_spec(D_MODEL), mod_spec(3), mod_spec(4), mod_spec(5),
        _resident((None, 1, D_MODEL), lambda t: (4, 0, 0)),
        _resident((D_MODEL, D_MODEL), lambda t: (0, 0)),
        _resident((D_MODEL, D_MODEL), lambda t: (0, 0)),
        cache_spec, cache_spec, new_spec, new_spec,
        table_spec, table_spec, table_spec,
        pl.BlockSpec(memory_space=pltpu.SMEM),
    ]
    cache_shape = jax.ShapeDtypeStruct((n_seq_total * KV_DIM, WINDOW), F32)
    return pl.pallas_call(
        functools.partial(_attn_sample_kernel, dec_seq=dec_seq),
        grid=grid,
        in_specs=in_specs,
        out_specs=[row_spec(D_MODEL), cache_spec, cache_spec],
        out_shape=[jax.ShapeDtypeStruct((n_seq_total * dec_seq, D_MODEL), F32),
                   cache_shape, cache_shape],
        scratch_shapes=[pltpu.VMEM((rows, D_MODEL), BF16)],
        compiler_params=pltpu.CompilerParams(
            dimension_semantics=("arbitrary",),
            vmem_limit_bytes=VMEM_LIMIT_BYTES),
        name="attn_sample",
    )(x, mods, mods, mods, norm_g, w_q, w_o, k_cache, v_cache, k_new, v_new, *tables, sinks)


def _rope_angles(pos):
    inv_freq = ROPE_THETA ** (-jnp.arange(0, ROT_DIM, 2, dtype=F32) / ROT_DIM)
    ang = pos.astype(F32)[:, None] * inv_freq[None, :]
    return jnp.cos(ang), jnp.sin(ang)


def _rope_tables(pos):
    half = ROT_DIM // 2
    cos, sin = _rope_angles(pos)
    n = pos.shape[0]
    rest = HEAD_DIM - ROT_DIM
    zeros = jnp.zeros((n, half), F32)
    cos_t = jnp.concatenate([cos, cos, jnp.ones((n, rest), F32)], axis=1)
    sin_up = jnp.concatenate([-sin, zeros, jnp.zeros((n, rest), F32)], axis=1)
    sin_dn = jnp.concatenate([zeros, sin, jnp.zeros((n, rest), F32)], axis=1)
    reps = LANES // HEAD_DIM
    return tuple(jnp.tile(t, (1, reps)) for t in (cos_t, sin_up, sin_dn))


def kernel(x_prompt, x_sample, state_conv, cache_k_win, cache_v_win, c_prompt, c_sample, norm_g, w_ada, b_ada, w_ffn_gate, w_ffn_up, w_ffn_down, conv_w_in, conv_w, conv_w_out, kv_norm_g, w_ada_kv, b_ada_kv, w_k, w_v, attn_w_q, attn_sinks, attn_w_o, final_norm_g):
    batch, seq, d = x_prompt.shape
    dec_batch, dec_seq, _ = x_sample.shape
    w_buf = cache_k_win.shape[1]
    past_len = 16384
    assert d == D_MODEL and w_buf == WINDOW and dec_seq == SUBLANES
    assert w_ada.shape[0] == 2 and conv_w_in.shape[0] == 1 and attn_w_q.shape[0] == 1

    pad = (-(dec_batch + batch)) % SUBLANES
    c_all = jnp.concatenate([c_sample, c_prompt, jnp.zeros((pad, d), F32)], axis=0)
    mods = _ada(c_all, w_ada, b_ada)
    mods_kv = _ada(c_all, w_ada_kv[None], b_ada_kv[None])
    prompt_mod_block = dec_batch // SUBLANES

    make_groups = lambda tile: {
        "prompt": _Group(batch, seq, None, prompt_mod_block, tile),
        "sample": _Group(dec_batch, dec_seq, dec_seq, prompt_mod_block, tile),
    }
    ffn_groups, mixer_groups = make_groups(FFN_TILE), make_groups(MIXER_TILE)
    xp, xs = x_prompt.reshape(batch * seq, d), x_sample.reshape(dec_batch * dec_seq, d)

    norm_g3 = norm_g.reshape(-1, 1, d)
    w_in, w_out = conv_w_in[0].astype(BF16), conv_w_out[0].astype(BF16)
    w_q, w_o = attn_w_q[0].astype(BF16), attn_w_o[0].astype(BF16)
    sinks = attn_sinks[0]

    pos_dec = past_len + jnp.arange(dec_seq, dtype=jnp.int32)
    pos_prompt = jnp.arange(seq, dtype=jnp.int32)
    tab_prompt = _rope_tables(pos_prompt)
    tab_dec = tuple(jnp.tile(t, (SAMPLE_ATTN_SEQS, 1)) for t in _rope_tables(pos_dec))
    tab_prompt_t = tuple(t.T for t in _rope_angles(pos_prompt))
    tab_dec_t = tuple(jnp.tile(t.T, (1, FFN_TILE // dec_seq)) for t in _rope_angles(pos_dec))
    kv_args = dict(norm_g=kv_norm_g.reshape(1, d), mods=mods_kv, w_k=w_k.astype(BF16),
                   w_k_t=w_k.T.astype(BF16), w_v_t=w_v.T.astype(BF16),
                   prompt_tables=tab_prompt, decode_tables_t=tab_dec_t)
    to_rows = lambda c: jnp.transpose(c, (0, 2, 3, 1)).reshape(dec_batch * KV_DIM, w_buf)
    from_rows = lambda c: jnp.transpose(c.reshape(dec_batch, N_KV_HEADS, HEAD_DIM, w_buf), (0, 3, 1, 2))

    ffn = functools.partial(_ffn, prompt=ffn_groups["prompt"], decode=ffn_groups["sample"], mods=mods,
                            norm_g=norm_g3, wg=w_ffn_gate, wu=w_ffn_up, wd=w_ffn_down)
    xp, xs = ffn(xp, xs, layer=0, sub=0, slot=0)
    xp, u_p = _conv_mixer(xp, mixer_groups["prompt"], mods, norm_g3, w_in, conv_w[0], w_out)
    xs, u_s = _conv_mixer(xs, mixer_groups["sample"], mods, norm_g3, w_in, conv_w[0], w_out,
                          (state_conv[0, :, 0, :], state_conv[0, :, 1, :]))
    xp, xs = ffn(xp, xs, layer=0, sub=2, slot=1)
    xp, xs, k_p, vt_p, kt_s, vt_s = ffn(xp, xs, layer=1, sub=0, slot=0, kv=kv_args)
    xp = _attn_prompt(xp, mixer_groups["prompt"], mods, norm_g3, w_q, w_o, k_p, vt_p, tab_prompt_t, sinks)
    xs, k_win, v_win = _attn_sample(xs, dec_batch, dec_seq, mods, norm_g3, w_q, w_o,
                                    to_rows(cache_k_win), to_rows(cache_v_win), kt_s, vt_s, tab_dec, sinks)
    y_p, y_s = ffn(xp, xs, layer=1, sub=2, slot=1, final_g=final_norm_g.reshape(1, d))

    k_state_p = k_p.reshape(batch, seq, KV_DIM)[:, seq - WINDOW:].reshape(
        batch, WINDOW, N_KV_HEADS, HEAD_DIM)
    v_state_p = jnp.stack([vt_p[:, (b + 1) * seq - WINDOW:(b + 1) * seq] for b in range(batch)])
    v_state_p = jnp.transpose(v_state_p.reshape(batch, N_KV_HEADS, HEAD_DIM, WINDOW), (0, 3, 1, 2))
    k_state_s, v_state_s = from_rows(k_win), from_rows(v_win)
    tail = CONV_WIDTH - 1
    conv_p = u_p.reshape(batch, SUBLANES, d)[:, SUBLANES - tail:][None]
    conv_s = u_s.reshape(dec_batch, dec_seq, d)[:, dec_seq - tail:][None]
    return (y_p.reshape(batch, seq, d), y_s.reshape(dec_batch, dec_seq, d), conv_p, conv_s,
            k_state_p, v_state_p, k_state_s, v_state_s)
```

```python
import functools

import jax
import jax.numpy as jnp
from jax import lax
from jax.experimental import pallas as pl
from jax.experimental.pallas import tpu as pltpu

F32 = jnp.float32
BF16 = jnp.bfloat16

D_MODEL = 1024
D_FF = 2816
HEAD_DIM = 64
N_HEADS = 16
N_KV_HEADS = 4
GROUP = N_HEADS // N_KV_HEADS
KV_DIM = N_KV_HEADS * HEAD_DIM
WINDOW = 128
ROT_DIM = 16
ROPE_THETA = 500000.0
CONV_WIDTH = 3
N_MOD = 9
EPS = 1e-6
NEG = -1e30
FMAX = float(jnp.finfo(jnp.float32).max)

SUBLANES = 8
LANES = 128
VMEM_LIMIT_BYTES = 60 * 1024 * 1024

FFN_TILE = 512
FF_STAGE = 256
MIXER_TILE = 512
ATTN_CHUNK = 256
LOG2E = 1.4426950408889634
SAMPLE_ATTN_SEQS = 16
DEC_ATTN_PACK = 4
FF_CHUNKS = ((0, 1024), (1024, 2048), (2048, D_FF))


def _dot(a, b):
    return jnp.dot(a, b, preferred_element_type=F32)


def _dot_nt(a, b):
    return lax.dot_general(a, b, (((1,), (1,)), ((), ())), preferred_element_type=F32)


def _log2(n):
    assert n & (n - 1) == 0, n
    return n.bit_length() - 1


def _silu(x):
    return x * jax.nn.sigmoid(x)


def _unit_rms(x):
    return x * lax.rsqrt(jnp.mean(x * x, axis=-1, keepdims=True) + EPS)


def _repeat_rows(ref, reps):
    n = ref.shape[1]
    return jnp.concatenate(
        [jnp.broadcast_to(ref[i:i + 1, :], (reps, n)) for i in range(ref.shape[0])], axis=0)


def _mod_rows(ref, per_seq_rows):
    if per_seq_rows is None:
        return ref[pl.ds(pl.program_id(0), 1), :]
    return _repeat_rows(ref, per_seq_rows)


def _rope(x, cos_t, sin_up_t, sin_dn_t):
    n = x.shape[1]
    reps = n // LANES
    half = ROT_DIM // 2
    cos = jnp.concatenate([cos_t] * reps, axis=1)
    sin_up = jnp.concatenate([sin_up_t] * reps, axis=1)
    sin_dn = jnp.concatenate([sin_dn_t] * reps, axis=1)
    x_up = pltpu.roll(x, n - half, axis=1)
    x_dn = pltpu.roll(x, half, axis=1)
    return x * cos + x_up * sin_up + x_dn * sin_dn


def _ada_kernel(c_ref, w_ref, b_ref, o_ref):
    a = _silu(c_ref[...]).astype(BF16)
    o_ref[...] = _dot(a, w_ref[...].astype(BF16)) + b_ref[...]


def _ada(c_all, w, b):
    n_layers, _, n = w.shape
    m = c_all.shape[0]
    tn = D_MODEL
    return pl.pallas_call(
        _ada_kernel,
        grid=(n_layers, n // tn),
        in_specs=[
            pl.BlockSpec((m, D_MODEL), lambda l, j: (0, 0)),
            pl.BlockSpec((None, D_MODEL, tn), lambda l, j: (l, 0, j)),
            pl.BlockSpec((None, 1, tn), lambda l, j: (l, 0, j)),
        ],
        out_specs=pl.BlockSpec((None, m, tn), lambda l, j: (l, 0, j)),
        out_shape=jax.ShapeDtypeStruct((n_layers, m, n), F32),
        compiler_params=pltpu.CompilerParams(
            dimension_semantics=("arbitrary", "arbitrary"),
            vmem_limit_bytes=VMEM_LIMIT_BYTES),
        name="adaln_tables",
    )(c_all, w, b.reshape(n_layers, 1, n))


def _rope_t(xt, cos_t, sin_t):
    half = ROT_DIM // 2
    out = []
    for h in range(xt.shape[0] // HEAD_DIM):
        base = h * HEAD_DIM
        x1 = xt[base:base + half, :]
        x2 = xt[base + half:base + ROT_DIM, :]
        out += [x1 * cos_t - x2 * sin_t, x2 * cos_t + x1 * sin_t,
                xt[base + ROT_DIM:base + HEAD_DIM, :]]
    return jnp.concatenate(out, axis=0)


def _ffn_kernel(*refs, n_prompt_tiles, tiles_per_seq, dec_seq, layer, slot, with_kv, final):
    it = iter(refs)
    take = lambda n: [next(it) for _ in range(n)]
    xp_ref, xs_ref = take(2)
    p_mods, s_mods = take(3), take(3)
    ng_ref, wg_hbm, wu_hbm, wd_hbm = take(4)
    if with_kv:
        (kvg_ref,), p_kvmods, s_kvmods = take(1), take(2), take(2)
        wk_ref, wkt_ref, wvt_ref = take(3)
        p_tabs, s_tabs = take(3), take(2)
    else:
        p_kvmods = s_kvmods = None
    if final:
        (fg_ref,) = take(1)
    op_ref, os_ref = take(2)
    if with_kv:
        kp_ref, vtp_ref, kts_ref, vts_ref = take(4)
    wg_s, wu_s, wd_s, stage_in, stage_out, sem = take(6)

    i = pl.program_id(0)
    prompt_rows = lambda ref: ref[pl.ds(i // tiles_per_seq, 1), :]
    decode_rows = lambda ref: _repeat_rows(ref, dec_seq)

    def weight_copies(c):
        cols = pl.ds(c * FF_STAGE, FF_STAGE)
        buf = c % 2
        return (
            pltpu.make_async_copy(wg_hbm.at[layer, slot, :, cols], stage_in.at[buf, 0], sem.at[buf, 0]),
            pltpu.make_async_copy(wu_hbm.at[layer, slot, :, cols], stage_in.at[buf, 1], sem.at[buf, 1]),
            pltpu.make_async_copy(wd_hbm.at[layer, slot, cols, :], stage_out.at[buf], sem.at[buf, 2]),
        )

    def body(x_ref, o_ref, rows_of, mods, kvmods, stage_weights):
        x = x_ref[...]
        xn = _unit_rms(x)
        hb = ((xn * ng_ref[...]) * (1.0 + rows_of(mods[1])) + rows_of(mods[0])).astype(BF16)

        def partial_ffn(lo, hi):
            g = _dot(hb, wg_s[:, lo:hi])
            u = _dot(hb, wu_s[:, lo:hi])
            return _dot((_silu(g) * u).astype(BF16), wd_s[lo:hi, :])

        y = None
        if stage_weights:
            n_slices = D_FF // FF_STAGE
            for c in range(min(2, n_slices)):
                for cp in weight_copies(c):
                    cp.start()
            for c in range(n_slices):
                lo, hi = c * FF_STAGE, (c + 1) * FF_STAGE
                for cp in weight_copies(c):
                    cp.wait()
                wg_s[:, lo:hi] = stage_in[c % 2, 0].astype(BF16)
                wu_s[:, lo:hi] = stage_in[c % 2, 1].astype(BF16)
                wd_s[lo:hi, :] = stage_out[c % 2].astype(BF16)
                if c + 2 < n_slices:
                    for cp in weight_copies(c + 2):
                        cp.start()
                part = partial_ffn(lo, hi)
                y = part if y is None else y + part
        else:
            for lo, hi in FF_CHUNKS:
                part = partial_ffn(lo, hi)
                y = part if y is None else y + part
        x_new = x + (0.5 * rows_of(mods[2])) * y
        o_ref[...] = _unit_rms(x_new) * fg_ref[...] if final else x_new
        if with_kv:
            return ((xn * kvg_ref[...]) * (1.0 + rows_of(kvmods[1])) + rows_of(kvmods[0])).astype(BF16)

    def prompt_step(stage_weights):
        hk = body(xp_ref, op_ref, prompt_rows, p_mods, p_kvmods, stage_weights)
        if with_kv:
            kp_ref[...] = _rope(_dot(hk, wk_ref[...]), *(t[...] for t in p_tabs))
            vtp_ref[...] = _dot_nt(wvt_ref[...], hk)

    @pl.when(i == 0)
    def _():
        prompt_step(True)

    @pl.when((i > 0) & (i < n_prompt_tiles))
    def _():
        prompt_step(False)

    @pl.when(i >= n_prompt_tiles)
    def _():
        hk = body(xs_ref, os_ref, decode_rows, s_mods, s_kvmods, False)
        if with_kv:
            kts_ref[...] = _rope_t(_dot_nt(wkt_ref[...], hk), *(t[...] for t in s_tabs))
            vts_ref[...] = _dot_nt(wvt_ref[...], hk)


def _resident(shape, index_map):
    return pl.BlockSpec(shape, index_map, pipeline_mode=pl.Buffered(1))


def _mod_spec(layer, col, group):
    if group.per_seq_rows is None:
        return pl.BlockSpec((None, SUBLANES, D_MODEL),
                            lambda b, t: (layer, group.prompt_mod_block, col))
    return pl.BlockSpec((None, group.tile // group.per_seq_rows, D_MODEL),
                        lambda b, t: (layer, t, col))


class _Group:
    def __init__(self, n_seq, seq_len, per_seq_rows, prompt_mod_block, tile):
        self.tile = tile
        self.rows = n_seq * seq_len
        self.per_seq_rows = per_seq_rows
        self.prompt_mod_block = prompt_mod_block
        if per_seq_rows is None:
            self.grid = (n_seq, seq_len // self.tile)
        else:
            self.grid = (1, self.rows // self.tile)
        self.tiles_per_seq = self.grid[1]

    def row_spec(self, width):
        nt = self.grid[1]
        return pl.BlockSpec((self.tile, width), lambda b, t: (b * nt + t, 0))

    def col_spec(self, height):
        nt = self.grid[1]
        return pl.BlockSpec((height, self.tile), lambda b, t: (0, b * nt + t))

    def table_t_spec(self):
        if self.per_seq_rows is None:
            return pl.BlockSpec((ROT_DIM // 2, self.tile), lambda b, t: (0, t))
        return pl.BlockSpec((ROT_DIM // 2, self.tile), lambda b, t: (0, 0))

    def table_spec(self):
        if self.per_seq_rows is None:
            return pl.BlockSpec((self.tile, LANES), lambda b, t: (t, 0))
        return pl.BlockSpec((self.tile, LANES), lambda b, t: (0, 0))


def _ffn(xp, xs, prompt, decode, mods, layer, sub, norm_g, wg, wu, wd, slot, kv=None, final_g=None):
    with_kv = kv is not None
    final = final_g is not None
    tile = prompt.tile
    assert decode.tile == tile
    n_p, n_s = prompt.rows // tile, decode.rows // tile
    tiles_per_seq = prompt.tiles_per_seq
    p_tile = lambda i: jnp.minimum(i, n_p - 1)
    s_tile = lambda i: jnp.maximum(i - n_p, 0)
    n_dec_seq = tile // decode.per_seq_rows

    def mod_specs(table_layer, cols):
        p = [pl.BlockSpec((None, SUBLANES, D_MODEL),
                          functools.partial(lambda i, c: (table_layer, prompt.prompt_mod_block, c), c=c))
             for c in cols]
        s = [pl.BlockSpec((None, n_dec_seq, D_MODEL),
                          functools.partial(lambda i, c: (table_layer, s_tile(i), c), c=c),
                          pipeline_mode=pl.Buffered(1))
             for c in cols]
        return p + s

    p_rows = lambda width: pl.BlockSpec((tile, width), lambda i: (p_tile(i), 0))
    s_rows = lambda width, **kw: pl.BlockSpec((tile, width), lambda i: (s_tile(i), 0), **kw)
    p_cols = lambda height: pl.BlockSpec((height, tile), lambda i: (0, p_tile(i)))
    s_cols = lambda height: pl.BlockSpec((height, tile), lambda i: (0, s_tile(i)))
    const = lambda shape: _resident(shape, lambda i: (0,) * len(shape))
    hbm = pl.BlockSpec(memory_space=pl.ANY)

    in_specs = [p_rows(D_MODEL), s_rows(D_MODEL, pipeline_mode=pl.Buffered(1))]
    in_specs += mod_specs(layer, [3 * sub + 0, 3 * sub + 1, 3 * sub + 2])
    in_specs += [_resident((None, 1, D_MODEL), lambda i: (layer * 3 + sub, 0, 0)), hbm, hbm, hbm]
    args = [xp, xs] + [mods] * 6 + [norm_g, wg, wu, wd]
    out_specs = [p_rows(D_MODEL), s_rows(D_MODEL)]
    out_shape = [jax.ShapeDtypeStruct((prompt.rows, D_MODEL), F32),
                 jax.ShapeDtypeStruct((decode.rows, D_MODEL), F32)]
    if with_kv:
        in_specs += [const((1, D_MODEL))] + mod_specs(0, [0, 1])
        in_specs += [const((D_MODEL, KV_DIM)), const((KV_DIM, D_MODEL)), const((KV_DIM, D_MODEL))]
        in_specs += [pl.BlockSpec((tile, LANES), lambda i: (p_tile(i) % tiles_per_seq, 0))] * 3
        in_specs += [const((ROT_DIM // 2, tile))] * 2
        args += [kv["norm_g"]] + [kv["mods"]] * 4 + [kv["w_k"], kv["w_k_t"], kv["w_v_t"]]
        args += [*kv["prompt_tables"], *kv["decode_tables_t"]]
        out_specs += [p_rows(KV_DIM), p_cols(KV_DIM), s_cols(KV_DIM), s_cols(KV_DIM)]
        out_shape += [jax.ShapeDtypeStruct((prompt.rows, KV_DIM), F32),
                      jax.ShapeDtypeStruct((KV_DIM, prompt.rows), F32),
                      jax.ShapeDtypeStruct((KV_DIM, decode.rows), F32),
                      jax.ShapeDtypeStruct((KV_DIM, decode.rows), F32)]
    if final:
        in_specs.append(const((1, D_MODEL)))
        args.append(final_g)
    scratch = [
        pltpu.VMEM((D_MODEL, D_FF), BF16), pltpu.VMEM((D_MODEL, D_FF), BF16),
        pltpu.VMEM((D_FF, D_MODEL), BF16),
        pltpu.VMEM((2, 2, D_MODEL, FF_STAGE), F32),
        pltpu.VMEM((2, FF_STAGE, D_MODEL), F32),
        pltpu.SemaphoreType.DMA((2, 3)),
    ]
    return pl.pallas_call(
        functools.partial(_ffn_kernel, n_prompt_tiles=n_p, tiles_per_seq=tiles_per_seq,
                          dec_seq=decode.per_seq_rows, layer=layer, slot=slot,
                          with_kv=with_kv, final=final),
        grid=(n_p + n_s,),
        in_specs=in_specs,
        out_specs=out_specs,
        out_shape=out_shape,
        scratch_shapes=scratch,
        compiler_params=pltpu.CompilerParams(
            dimension_semantics=("arbitrary",),
            vmem_limit_bytes=VMEM_LIMIT_BYTES),
        name=f"ffn_l{layer}_s{sub}",
    )(*args)


def _conv_kernel(*refs, per_seq_rows):
    it = iter(refs)
    x_ref, sh_ref, sc_ref, gt_ref, ng_ref, win_ref, cw_ref, wout_ref = (next(it) for _ in range(8))
    if per_seq_rows is not None:
        st0_ref, st1_ref = next(it), next(it)
    o_ref, u_ref = next(it), next(it)
    if per_seq_rows is None:
        carry_ref = next(it)

    tile = x_ref.shape[0]
    x = x_ref[...]
    h = ((_unit_rms(x) * ng_ref[...]) * (1.0 + _mod_rows(sc_ref, per_seq_rows))
         + _mod_rows(sh_ref, per_seq_rows)).astype(BF16)
    bcv = _dot(h, win_ref[...])
    b_g = bcv[:, :D_MODEL]
    u = bcv[:, D_MODEL:2 * D_MODEL] * bcv[:, 2 * D_MODEL:]

    row = lax.broadcasted_iota(jnp.int32, (tile, 1), 0)
    if per_seq_rows is None:
        @pl.when(pl.program_id(1) == 0)
        def _():
            carry_ref[...] = jnp.zeros_like(carry_ref)
        prev2 = carry_ref[SUBLANES - 2:SUBLANES - 1, :]
        prev1 = carry_ref[SUBLANES - 1:SUBLANES, :]
        t = row
    else:
        prev2 = _repeat_rows(st0_ref, per_seq_rows)
        prev1 = _repeat_rows(st1_ref, per_seq_rows)
        t = row & (per_seq_rows - 1)
    u1 = jnp.where(t == 0, prev1, pltpu.roll(u, 1, axis=0))
    u2 = jnp.where(t == 0, prev2, jnp.where(t == 1, prev1, pltpu.roll(u, 2, axis=0)))
    conv = u2 * cw_ref[0:1, :] + u1 * cw_ref[1:2, :] + u * cw_ref[2:3, :]
    y = _dot((b_g * conv).astype(BF16), wout_ref[...])
    o_ref[...] = x + _mod_rows(gt_ref, per_seq_rows) * y

    if per_seq_rows is None:
        tail = u[tile - SUBLANES:, :]
        carry_ref[...] = tail
        u_ref[...] = tail
    else:
        u_ref[...] = u


def _conv_mixer(x, group, mods, norm_g, w_in, conv_w, w_out, state=None):
    prompt = group.per_seq_rows is None
    in_specs = [
        group.row_spec(D_MODEL),
        _mod_spec(0, 3, group), _mod_spec(0, 4, group), _mod_spec(0, 5, group),
        _resident((None, 1, D_MODEL), lambda b, t: (1, 0, 0)),
        _resident((D_MODEL, 3 * D_MODEL), lambda b, t: (0, 0)),
        _resident((CONV_WIDTH, D_MODEL), lambda b, t: (0, 0)),
        _resident((D_MODEL, D_MODEL), lambda b, t: (0, 0)),
    ]
    args = [x, mods, mods, mods, norm_g, w_in, conv_w, w_out]
    scratch = []
    if prompt:
        u_spec = pl.BlockSpec((SUBLANES, D_MODEL), lambda b, t: (b, 0))
        u_shape = jax.ShapeDtypeStruct((group.grid[0] * SUBLANES, D_MODEL), F32)
        scratch.append(pltpu.VMEM((SUBLANES, D_MODEL), F32))
    else:
        n_seq = group.tile // group.per_seq_rows
        st_spec = pl.BlockSpec((n_seq, D_MODEL), lambda b, t: (t, 0))
        in_specs += [st_spec, st_spec]
        args += [state[0], state[1]]
        u_spec = group.row_spec(D_MODEL)
        u_shape = jax.ShapeDtypeStruct((group.rows, D_MODEL), F32)
    return pl.pallas_call(
        functools.partial(_conv_kernel, per_seq_rows=group.per_seq_rows),
        grid=group.grid,
        in_specs=in_specs,
        out_specs=[group.row_spec(D_MODEL), u_spec],
        out_shape=[jax.ShapeDtypeStruct((group.rows, D_MODEL), F32), u_shape],
        scratch_shapes=scratch,
        compiler_params=pltpu.CompilerParams(
            dimension_semantics=("arbitrary", "arbitrary"),
            vmem_limit_bytes=VMEM_LIMIT_BYTES),
        name="conv_mixer",
    )(*args)


def _sink_column(sink_ref, kv_head, q_rows):
    grp = lax.broadcasted_iota(jnp.int32, (GROUP * q_rows, 1), 0) >> _log2(q_rows)
    col = jnp.full((GROUP * q_rows, 1), sink_ref[kv_head * GROUP + GROUP - 1], F32)
    for g in range(GROUP - 2, -1, -1):
        col = jnp.where(grp == g, sink_ref[kv_head * GROUP + g], col)
    return col


def _stack_heads(q_ref, rows, kv_head):
    return jnp.concatenate(
        [q_ref[rows, (kv_head * GROUP + g) * HEAD_DIM:(kv_head * GROUP + g + 1) * HEAD_DIM]
         for g in range(GROUP)], axis=0)


def _softmax_parts(scores, sink):
    m = sink
    for s in scores:
        m = jnp.maximum(m, jnp.max(s, axis=-1, keepdims=True))
    ps = [jnp.exp(s - m) for s in scores]
    denom = jnp.exp(sink - m)
    for p in ps:
        denom = denom + jnp.sum(p, axis=-1, keepdims=True)
    return ps, 1.0 / denom


def _project_q(x_ref, sh_ref, sc_ref, ng_ref, wq_ref, tables, per_seq_rows):
    h = ((_unit_rms(x_ref[...]) * ng_ref[...]) * (1.0 + _mod_rows(sc_ref, per_seq_rows))
         + _mod_rows(sh_ref, per_seq_rows)).astype(BF16)
    q = _rope(_dot(h, wq_ref[...]), *tables)
    return (q * (HEAD_DIM ** -0.5)).astype(BF16)


def _band_caps():
    shape = (2 * WINDOW, GROUP * WINDOW)
    kj = lax.broadcasted_iota(jnp.int32, shape, 0)
    qi = lax.broadcasted_iota(jnp.int32, shape, 1) & (WINDOW - 1)
    in_band = (kj > qi) & (kj <= qi + WINDOW)
    return (jnp.where(in_band, FMAX, NEG), jnp.where(in_band & (kj >= WINDOW), FMAX, NEG))


def _attn_prompt_kernel(x_ref, sh_ref, sc_ref, gt_ref, ng_ref, wq_ref, wo_ref,
                        kp_ref, kc_ref, vp_ref, vc_ref, cos_ref, sin_ref, sink_ref,
                        o_ref, q_scr, k_scr, v_scr, a_scr, cap_scr, h_scr, at_scr):
    tile = x_ref.shape[0]

    @pl.when((pl.program_id(0) == 0) & (pl.program_id(1) == 0))
    def _():
        general, first = _band_caps()
        cap_scr[0] = general
        cap_scr[1] = first

    k_scr[:WINDOW, :] = kp_ref[...].astype(BF16)
    k_scr[WINDOW:, :] = kc_ref[...].astype(BF16)
    v_scr[:, :WINDOW] = vp_ref[...].astype(BF16)
    v_scr[:, WINDOW:] = vc_ref[...].astype(BF16)
    first_tile = pl.program_id(1) == 0
    norm_gain = ng_ref[...]
    scale1 = 1.0 + _mod_rows(sc_ref, None)
    shift = _mod_rows(sh_ref, None)
    gate = _mod_rows(gt_ref, None)

    group_dims = GROUP * HEAD_DIM

    def chunk_rows(c):
        return slice(c * ATTN_CHUNK, (c + 1) * ATTN_CHUNK)

    def normalize(c):
        rows = chunk_rows(c)
        h_scr[rows, :] = ((_unit_rms(x_ref[rows, :]) * norm_gain) * scale1 + shift).astype(BF16)

    def project_piece(c, kvh):
        rows = chunk_rows(c)
        dims = slice(kvh * group_dims, (kvh + 1) * group_dims)
        qt = _rope_t(_dot(h_scr[rows, :], wq_ref[:, dims]).T, cos_ref[:, rows], sin_ref[:, rows])
        q_scr[dims, rows] = (qt * (HEAD_DIM ** -0.5 * LOG2E)).astype(BF16)

    def attend(blk, kvh):
        cols = slice(blk * WINDOW, (blk + 1) * WINDOW)
        keys = slice(blk * WINDOW, (blk + 2) * WINDOW)
        cap = cap_scr[jnp.where(first_tile, 1, 0)] if blk == 0 else cap_scr[0]
        dims = slice(kvh * HEAD_DIM, (kvh + 1) * HEAD_DIM)
        heads = [slice((kvh * GROUP + g) * HEAD_DIM, (kvh * GROUP + g + 1) * HEAD_DIM)
                 for g in range(GROUP)]
        qg = jnp.concatenate([q_scr[hd, cols] for hd in heads], axis=1)
        s = jnp.minimum(_dot(k_scr[keys, dims], qg), cap)
        sink = LOG2E * jnp.concatenate(
            [jnp.full((1, WINDOW), sink_ref[kvh * GROUP + g], F32) for g in range(GROUP)], axis=1)
        m = jnp.maximum(jnp.max(s, axis=0, keepdims=True), sink)
        p = jnp.exp2(s - m)
        denom = jnp.sum(p, axis=0, keepdims=True) + jnp.exp2(sink - m)
        o = _dot(v_scr[dims, keys], p.astype(BF16)) * (1.0 / denom)
        for g, hd in enumerate(heads):
            a_scr[hd, cols] = o[:, g * WINDOW:(g + 1) * WINDOW]

    def transpose_piece(c, kvh):
        rows = chunk_rows(c)
        dims = slice(kvh * group_dims, (kvh + 1) * group_dims)
        at_scr[rows, dims] = a_scr[dims, rows].T.astype(BF16)

    def emit_piece(c, n):
        rows = chunk_rows(c)
        cols = slice(n * group_dims, (n + 1) * group_dims)
        y = _dot(at_scr[rows, :], wo_ref[:, cols])
        o_ref[rows, cols] = x_ref[rows, cols] + gate[:, cols] * y

    n_chunks = tile // ATTN_CHUNK
    blocks_per_chunk = ATTN_CHUNK // WINDOW
    n_pieces = D_MODEL // group_dims
    normalize(0)
    for kvh in range(N_KV_HEADS):
        project_piece(0, kvh)
    for c in range(n_chunks):
        if c + 1 < n_chunks:
            normalize(c + 1)
        step = 0
        for b in range(blocks_per_chunk):
            for kvh in range(N_KV_HEADS):
                attend(c * blocks_per_chunk + b, kvh)
                if b == blocks_per_chunk - 1:
                    transpose_piece(c, kvh)
                if step % 2 == 0 and c + 1 < n_chunks and step // 2 < N_KV_HEADS:
                    project_piece(c + 1, step // 2)
                if step % 2 == 1 and c > 0 and step // 2 < n_pieces:
                    emit_piece(c - 1, step // 2)
                step += 1
    for n in range(n_pieces):
        emit_piece(n_chunks - 1, n)


def _attn_prompt(x, group, mods, norm_g, w_q, w_o, k, v_t, tables_t, sinks):
    tile = group.tile
    nt = group.grid[1]
    blocks_per_tile = tile // WINDOW

    def prev_block(b, t):
        return jnp.maximum((b * nt + t) * blocks_per_tile - 1, 0)

    in_specs = [
        group.row_spec(D_MODEL),
        _mod_spec(1, 3, group), _mod_spec(1, 4, group), _mod_spec(1, 5, group),
        _resident((None, 1, D_MODEL), lambda b, t: (4, 0, 0)),
        _resident((D_MODEL, D_MODEL), lambda b, t: (0, 0)),
        _resident((D_MODEL, D_MODEL), lambda b, t: (0, 0)),
        pl.BlockSpec((WINDOW, KV_DIM), lambda b, t: (prev_block(b, t), 0)), group.row_spec(KV_DIM),
        pl.BlockSpec((KV_DIM, WINDOW), lambda b, t: (0, prev_block(b, t))), group.col_spec(KV_DIM),
        group.table_t_spec(), group.table_t_spec(),
        pl.BlockSpec(memory_space=pltpu.SMEM),
    ]
    return pl.pallas_call(
        _attn_prompt_kernel,
        grid=group.grid,
        in_specs=in_specs,
        out_specs=group.row_spec(D_MODEL),
        out_shape=jax.ShapeDtypeStruct((group.rows, D_MODEL), F32),
        scratch_shapes=[
            pltpu.VMEM((D_MODEL, tile), BF16),
            pltpu.VMEM((tile + WINDOW, KV_DIM), BF16),
            pltpu.VMEM((KV_DIM, tile + WINDOW), BF16),
            pltpu.VMEM((D_MODEL, tile), F32),
            pltpu.VMEM((2, 2 * WINDOW, GROUP * WINDOW), F32),
            pltpu.VMEM((tile, D_MODEL), BF16),
            pltpu.VMEM((tile, D_MODEL), BF16),
        ],
        compiler_params=pltpu.CompilerParams(
            dimension_semantics=("arbitrary", "arbitrary"),
            vmem_limit_bytes=VMEM_LIMIT_BYTES),
        name="attn_prompt",
    )(x, mods, mods, mods, norm_g, w_q, w_o, k, k, v_t, v_t, *tables_t, sinks)


def _attn_sample_kernel(x_ref, sh_ref, sc_ref, gt_ref, ng_ref, wq_ref, wo_ref,
                        kc_ref, vc_ref, kn_ref, vn_ref, cos_ref, sup_ref, sdn_ref, sink_ref,
                        o_ref, kout_ref, vout_ref, q_scr, *, dec_seq):
    rows = x_ref.shape[0]
    n_seq = rows // dec_seq
    n_cache = n_seq * WINDOW
    q_scr[...] = _project_q(x_ref, sh_ref, sc_ref, ng_ref, wq_ref,
                            (cos_ref[...], sup_ref[...], sdn_ref[...]), dec_seq)

    pack = DEC_ATTN_PACK
    pack_rows = pack * dec_seq

    def windows(ref, first_seq, kvh):
        return jnp.concatenate(
            [ref[(s * N_KV_HEADS + kvh) * HEAD_DIM:(s * N_KV_HEADS + kvh + 1) * HEAD_DIM, :].astype(BF16)
             for s in range(first_seq, first_seq + pack)], axis=1)

    q_row = lax.broadcasted_iota(jnp.int32, (GROUP * pack_rows, 1), 0) & (pack_rows - 1)
    q_seq, q_i = q_row >> _log2(dec_seq), q_row & (dec_seq - 1)
    c_col = lax.broadcasted_iota(jnp.int32, (1, pack * WINDOW), 1)
    valid_c = (q_seq == (c_col >> _log2(WINDOW))) & ((c_col & (WINDOW - 1)) > q_i)
    n_col = lax.broadcasted_iota(jnp.int32, (1, rows), 1)
    n_seq_col, n_visible = n_col >> _log2(dec_seq), (n_col & (dec_seq - 1)) <= q_i
    kn = [kn_ref[kvh * HEAD_DIM:(kvh + 1) * HEAD_DIM, :].astype(BF16) for kvh in range(N_KV_HEADS)]
    vn = [vn_ref[kvh * HEAD_DIM:(kvh + 1) * HEAD_DIM, :].astype(BF16) for kvh in range(N_KV_HEADS)]

    keep = lax.broadcasted_iota(jnp.int32, (1, WINDOW), 1) < WINDOW - dec_seq

    def update_windows(first_seq):
        for src_ref, new_ref, dst_ref in ((kc_ref, kn_ref, kout_ref), (vc_ref, vn_ref, vout_ref)):
            new = new_ref[...]
            for s in range(first_seq, first_seq + pack):
                blk = slice(s * KV_DIM, (s + 1) * KV_DIM)
                old = pltpu.roll(src_ref[blk, :], WINDOW - dec_seq, axis=1)
                app = pltpu.roll(new, (WINDOW - dec_seq - s * dec_seq) % rows, axis=1)
                dst_ref[blk, :] = jnp.where(keep, old, app)

    packs = []
    for first_seq in range(0, n_seq, pack):
        update_windows(first_seq)
        q_rows = slice(first_seq * dec_seq, (first_seq + pack) * dec_seq)
        valid_n = ((q_seq + first_seq) == n_seq_col) & n_visible
        heads = []
        for kvh in range(N_KV_HEADS):
            qg = _stack_heads(q_scr, q_rows, kvh)
            s_c = jnp.where(valid_c, _dot(qg, windows(kc_ref, first_seq, kvh)), NEG)
            s_n = jnp.where(valid_n, _dot(qg, kn[kvh]), NEG)
            (p_c, p_n), inv = _softmax_parts([s_c, s_n], _sink_column(sink_ref, kvh, pack_rows))
            o = (_dot_nt(p_c.astype(BF16), windows(vc_ref, first_seq, kvh))
                 + _dot_nt(p_n.astype(BF16), vn[kvh])) * inv
            heads += [o[g * pack_rows:(g + 1) * pack_rows, :] for g in range(GROUP)]
        packs.append(jnp.concatenate(heads, axis=1))
    attn = jnp.concatenate(packs, axis=0).astype(BF16)
    o_ref[...] = x_ref[...] + _mod_rows(gt_ref, dec_seq) * _dot(attn, wo_ref[...])


def _attn_sample(x, n_seq_total, dec_seq, mods, norm_g, w_q, w_o, k_cache, v_cache,
                 k_new, v_new, tables, sinks):
    n_seq = SAMPLE_ATTN_SEQS
    rows = n_seq * dec_seq
    assert rows == WINDOW == LANES
    grid = (n_seq_total // n_seq,)
    row_spec = lambda width: pl.BlockSpec((rows, width), lambda t: (t, 0))
    mod_spec = lambda col: pl.BlockSpec((None, n_seq, D_MODEL), lambda t: (1, t, col))
    cache_spec = pl.BlockSpec((n_seq * KV_DIM, WINDOW), lambda t: (t, 0))
    new_spec = pl.BlockSpec((KV_DIM, rows), lambda t: (0, t))
    table_spec = pl.BlockSpec((rows, LANES), lambda t: (0, 0))
    in_specs = [
        row_spec(D_MODEL), mod_spec(3), mod_spec(4), mod_spec(5),
        _resident((None, 1, D_MODEL), lambda t: (4, 0, 0)),
        _resident((D_MODEL, D_MODEL), lambda t: (0, 0)),
        _resident((D_MODEL, D_MODEL), lambda t: (0, 0)),
        cache_spec, cache_spec, new_spec, new_spec,
        table_spec, table_spec, table_spec,
        pl.BlockSpec(memory_space=pltpu.SMEM),
    ]
    cache_shape = jax.ShapeDtypeStruct((n_seq_total * KV_DIM, WINDOW), F32)
    return pl.pallas_call(
        functools.partial(_attn_sample_kernel, dec_seq=dec_seq),
        grid=grid,
        in_specs=in_specs,
        out_specs=[row_spec(D_MODEL), cache_spec, cache_spec],
        out_shape=[jax.ShapeDtypeStruct((n_seq_total * dec_seq, D_MODEL), F32),
                   cache_shape, cache_shape],
        scratch_shapes=[pltpu.VMEM((rows, D_MODEL), BF16)],
        compiler_params=pltpu.CompilerParams(
            dimension_semantics=("arbitrary",),
            vmem_limit_bytes=VMEM_LIMIT_BYTES),
        name="attn_sample",
    )(x, mods, mods, mods, norm_g, w_q, w_o, k_cache, v_cache, k_new, v_new, *tables, sinks)


def _rope_angles(pos):
    inv_freq = ROPE_THETA ** (-jnp.arange(0, ROT_DIM, 2, dtype=F32) / ROT_DIM)
    ang = pos.astype(F32)[:, None] * inv_freq[None, :]
    return jnp.cos(ang), jnp.sin(ang)


def _rope_tables(pos):
    half = ROT_DIM // 2
    cos, sin = _rope_angles(pos)
    n = pos.shape[0]
    rest = HEAD_DIM - ROT_DIM
    zeros = jnp.zeros((n, half), F32)
    cos_t = jnp.concatenate([cos, cos, jnp.ones((n, rest), F32)], axis=1)
    sin_up = jnp.concatenate([-sin, zeros, jnp.zeros((n, rest), F32)], axis=1)
    sin_dn = jnp.concatenate([zeros, sin, jnp.zeros((n, rest), F32)], axis=1)
    reps = LANES // HEAD_DIM
    return tuple(jnp.tile(t, (1, reps)) for t in (cos_t, sin_up, sin_dn))


def kernel(x_prompt, x_sample, state_conv, cache_k_win, cache_v_win, c_prompt, c_sample, norm_g, w_ada, b_ada, w_ffn_gate, w_ffn_up, w_ffn_down, conv_w_in, conv_w, conv_w_out, kv_norm_g, w_ada_kv, b_ada_kv, w_k, w_v, attn_w_q, attn_sinks, attn_w_o, final_norm_g):
    batch, seq, d = x_prompt.shape
    dec_batch, dec_seq, _ = x_sample.shape
    w_buf = cache_k_win.shape[1]
    past_len = 16384
    assert d == D_MODEL and w_buf == WINDOW and dec_seq == SUBLANES
    assert w_ada.shape[0] == 2 and conv_w_in.shape[0] == 1 and attn_w_q.shape[0] == 1

    pad = (-(dec_batch + batch)) % SUBLANES
    c_all = jnp.concatenate([c_sample, c_prompt, jnp.zeros((pad, d), F32)], axis=0)
    mods = _ada(c_all, w_ada, b_ada)
    mods_kv = _ada(c_all, w_ada_kv[None], b_ada_kv[None])
    prompt_mod_block = dec_batch // SUBLANES

    make_groups = lambda tile: {
        "prompt": _Group(batch, seq, None, prompt_mod_block, tile),
        "sample": _Group(dec_batch, dec_seq, dec_seq, prompt_mod_block, tile),
    }
    ffn_groups, mixer_groups = make_groups(FFN_TILE), make_groups(MIXER_TILE)
    xp, xs = x_prompt.reshape(batch * seq, d), x_sample.reshape(dec_batch * dec_seq, d)

    norm_g3 = norm_g.reshape(-1, 1, d)
    w_in, w_out = conv_w_in[0].astype(BF16), conv_w_out[0].astype(BF16)
    w_q, w_o = attn_w_q[0].astype(BF16), attn_w_o[0].astype(BF16)
    sinks = attn_sinks[0]

    pos_dec = past_len + jnp.arange(dec_seq, dtype=jnp.int32)
    pos_prompt = jnp.arange(seq, dtype=jnp.int32)
    tab_prompt = _rope_tables(pos_prompt)
    tab_dec = tuple(jnp.tile(t, (SAMPLE_ATTN_SEQS, 1)) for t in _rope_tables(pos_dec))
    tab_prompt_t = tuple(t.T for t in _rope_angles(pos_prompt))
    tab_dec_t = tuple(jnp.tile(t.T, (1, FFN_TILE // dec_seq)) for t in _rope_angles(pos_dec))
    kv_args = dict(norm_g=kv_norm_g.reshape(1, d), mods=mods_kv, w_k=w_k.astype(BF16),
                   w_k_t=w_k.T.astype(BF16), w_v_t=w_v.T.astype(BF16),
                   prompt_tables=tab_prompt, decode_tables_t=tab_dec_t)
    to_rows = lambda c: jnp.transpose(c, (0, 2, 3, 1)).reshape(dec_batch * KV_DIM, w_buf)
    from_rows = lambda c: jnp.transpose(c.reshape(dec_batch, N_KV_HEADS, HEAD_DIM, w_buf), (0, 3, 1, 2))

    ffn = functools.partial(_ffn, prompt=ffn_groups["prompt"], decode=ffn_groups["sample"], mods=mods,
                            norm_g=norm_g3, wg=w_ffn_gate, wu=w_ffn_up, wd=w_ffn_down)
    xp, xs = ffn(xp, xs, layer=0, sub=0, slot=0)
    xp, u_p = _conv_mixer(xp, mixer_groups["prompt"], mods, norm_g3, w_in, conv_w[0], w_out)
    xs, u_s = _conv_mixer(xs, mixer_groups["sample"], mods, norm_g3, w_in, conv_w[0], w_out,
                          (state_conv[0, :, 0, :], state_conv[0, :, 1, :]))
    xp, xs = ffn(xp, xs, layer=0, sub=2, slot=1)
    xp, xs, k_p, vt_p, kt_s, vt_s = ffn(xp, xs, layer=1, sub=0, slot=0, kv=kv_args)
    xp = _attn_prompt(xp, mixer_groups["prompt"], mods, norm_g3, w_q, w_o, k_p, vt_p, tab_prompt_t, sinks)
    xs, k_win, v_win = _attn_sample(xs, dec_batch, dec_seq, mods, norm_g3, w_q, w_o,
                                    to_rows(cache_k_win), to_rows(cache_v_win), kt_s, vt_s, tab_dec, sinks)
    y_p, y_s = ffn(xp, xs, layer=1, sub=2, slot=1, final_g=final_norm_g.reshape(1, d))

    k_state_p = k_p.reshape(batch, seq, KV_DIM)[:, seq - WINDOW:].reshape(
        batch, WINDOW, N_KV_HEADS, HEAD_DIM)
    v_state_p = jnp.stack([vt_p[:, (b + 1) * seq - WINDOW:(b + 1) * seq] for b in range(batch)])
    v_state_p = jnp.transpose(v_state_p.reshape(batch, N_KV_HEADS, HEAD_DIM, WINDOW), (0, 3, 1, 2))
    k_state_s, v_state_s = from_rows(k_win), from_rows(v_win)
    tail = CONV_WIDTH - 1
    conv_p = u_p.reshape(batch, SUBLANES, d)[:, SUBLANES - tail:][None]
    conv_s = u_s.reshape(dec_batch, dec_seq, d)[:, dec_seq - tail:][None]
    return (y_p.reshape(batch, seq, d), y_s.reshape(dec_batch, dec_seq, d), conv_p, conv_s,
            k_state_p, v_state_p, k_state_s, v_state_s)
```

```python
import functools

import jax
import jax.numpy as jnp
from jax import lax
from jax.experimental import pallas as pl
from jax.experimental.pallas import tpu as pltpu

F32 = jnp.float32
BF16 = jnp.bfloat16

D_MODEL = 1024
D_FF = 2816
HEAD_DIM = 64
N_HEADS = 16
N_KV_HEADS = 4
GROUP = N_HEADS // N_KV_HEADS
KV_DIM = N_KV_HEADS * HEAD_DIM
WINDOW = 128
ROT_DIM = 16
ROPE_THETA = 500000.0
CONV_WIDTH = 3
N_MOD = 9
EPS = 1e-6
NEG = -1e30
FMAX = float(jnp.finfo(jnp.float32).max)

SUBLANES = 8
LANES = 128
VMEM_LIMIT_BYTES = 60 * 1024 * 1024

FFN_TILE = 512
FF_STAGE = 256
MIXER_TILE = 512
ATTN_TILE = 1024
ATTN_CHUNK = 256
LOG2E = 1.4426950408889634
SAMPLE_ATTN_SEQS = 16
DEC_ATTN_PACK = 4
FF_CHUNKS = ((0, 1024), (1024, 2048), (2048, D_FF))


def _dot(a, b):
    return jnp.dot(a, b, preferred_element_type=F32)


def _dot_nt(a, b):
    return lax.dot_general(a, b, (((1,), (1,)), ((), ())), preferred_element_type=F32)


def _log2(n):
    assert n & (n - 1) == 0, n
    return n.bit_length() - 1


def _silu(x):
    return x * jax.nn.sigmoid(x)


def _unit_rms(x):
    return x * lax.rsqrt(jnp.mean(x * x, axis=-1, keepdims=True) + EPS)


def _repeat_rows(ref, reps):
    n = ref.shape[1]
    return jnp.concatenate(
        [jnp.broadcast_to(ref[i:i + 1, :], (reps, n)) for i in range(ref.shape[0])], axis=0)


def _mod_rows(ref, per_seq_rows):
    if per_seq_rows is None:
        return ref[pl.ds(pl.program_id(0), 1), :]
    return _repeat_rows(ref, per_seq_rows)


def _rope(x, cos_t, sin_up_t, sin_dn_t):
    n = x.shape[1]
    reps = n // LANES
    half = ROT_DIM // 2
    cos = jnp.concatenate([cos_t] * reps, axis=1)
    sin_up = jnp.concatenate([sin_up_t] * reps, axis=1)
    sin_dn = jnp.concatenate([sin_dn_t] * reps, axis=1)
    x_up = pltpu.roll(x, n - half, axis=1)
    x_dn = pltpu.roll(x, half, axis=1)
    return x * cos + x_up * sin_up + x_dn * sin_dn


def _ada_kernel(c_ref, w_ref, b_ref, o_ref):
    a = _silu(c_ref[...]).astype(BF16)
    o_ref[...] = _dot(a, w_ref[...].astype(BF16)) + b_ref[...]


def _ada(c_all, w, b):
    n_layers, _, n = w.shape
    m = c_all.shape[0]
    tn = D_MODEL
    return pl.pallas_call(
        _ada_kernel,
        grid=(n_layers, n // tn),
        in_specs=[
            pl.BlockSpec((m, D_MODEL), lambda l, j: (0, 0)),
            pl.BlockSpec((None, D_MODEL, tn), lambda l, j: (l, 0, j)),
            pl.BlockSpec((None, 1, tn), lambda l, j: (l, 0, j)),
        ],
        out_specs=pl.BlockSpec((None, m, tn), lambda l, j: (l, 0, j)),
        out_shape=jax.ShapeDtypeStruct((n_layers, m, n), F32),
        compiler_params=pltpu.CompilerParams(
            dimension_semantics=("arbitrary", "arbitrary"),
            vmem_limit_bytes=VMEM_LIMIT_BYTES),
        name="adaln_tables",
    )(c_all, w, b.reshape(n_layers, 1, n))


def _rope_t(xt, cos_t, sin_t):
    half = ROT_DIM // 2
    out = []
    for h in range(xt.shape[0] // HEAD_DIM):
        base = h * HEAD_DIM
        x1 = xt[base:base + half, :]
        x2 = xt[base + half:base + ROT_DIM, :]
        out += [x1 * cos_t - x2 * sin_t, x2 * cos_t + x1 * sin_t,
                xt[base + ROT_DIM:base + HEAD_DIM, :]]
    return jnp.concatenate(out, axis=0)


def _ffn_kernel(*refs, n_prompt_tiles, tiles_per_seq, dec_seq, layer, slot, with_kv, final):
    it = iter(refs)
    take = lambda n: [next(it) for _ in range(n)]
    xp_ref, xs_ref = take(2)
    p_mods, s_mods = take(3), take(3)
    ng_ref, wg_hbm, wu_hbm, wd_hbm = take(4)
    if with_kv:
        (kvg_ref,), p_kvmods, s_kvmods = take(1), take(2), take(2)
        wk_ref, wkt_ref, wvt_ref = take(3)
        p_tabs, s_tabs = take(3), take(2)
    else:
        p_kvmods = s_kvmods = None
    if final:
        (fg_ref,) = take(1)
    op_ref, os_ref = take(2)
    if with_kv:
        kp_ref, vtp_ref, kts_ref, vts_ref = take(4)
    wg_s, wu_s, wd_s, stage_in, stage_out, sem = take(6)

    i = pl.program_id(0)
    prompt_rows = lambda ref: ref[pl.ds(i // tiles_per_seq, 1), :]
    decode_rows = lambda ref: _repeat_rows(ref, dec_seq)

    def weight_copies(c):
        cols = pl.ds(c * FF_STAGE, FF_STAGE)
        buf = c % 2
        return (
            pltpu.make_async_copy(wg_hbm.at[layer, slot, :, cols], stage_in.at[buf, 0], sem.at[buf, 0]),
            pltpu.make_async_copy(wu_hbm.at[layer, slot, :, cols], stage_in.at[buf, 1], sem.at[buf, 1]),
            pltpu.make_async_copy(wd_hbm.at[layer, slot, cols, :], stage_out.at[buf], sem.at[buf, 2]),
        )

    def body(x_ref, o_ref, rows_of, mods, kvmods, stage_weights):
        x = x_ref[...]
        xn = _unit_rms(x)
        hb = ((xn * ng_ref[...]) * (1.0 + rows_of(mods[1])) + rows_of(mods[0])).astype(BF16)

        def partial_ffn(lo, hi):
            g = _dot(hb, wg_s[:, lo:hi])
            u = _dot(hb, wu_s[:, lo:hi])
            return _dot((_silu(g) * u).astype(BF16), wd_s[lo:hi, :])

        y = None
        if stage_weights:
            n_slices = D_FF // FF_STAGE
            for c in range(min(2, n_slices)):
                for cp in weight_copies(c):
                    cp.start()
            for c in range(n_slices):
                lo, hi = c * FF_STAGE, (c + 1) * FF_STAGE
                for cp in weight_copies(c):
                    cp.wait()
                wg_s[:, lo:hi] = stage_in[c % 2, 0].astype(BF16)
                wu_s[:, lo:hi] = stage_in[c % 2, 1].astype(BF16)
                wd_s[lo:hi, :] = stage_out[c % 2].astype(BF16)
                if c + 2 < n_slices:
                    for cp in weight_copies(c + 2):
                        cp.start()
                part = partial_ffn(lo, hi)
                y = part if y is None else y + part
        else:
            for lo, hi in FF_CHUNKS:
                part = partial_ffn(lo, hi)
                y = part if y is None else y + part
        x_new = x + (0.5 * rows_of(mods[2])) * y
        o_ref[...] = _unit_rms(x_new) * fg_ref[...] if final else x_new
        if with_kv:
            return ((xn * kvg_ref[...]) * (1.0 + rows_of(kvmods[1])) + rows_of(kvmods[0])).astype(BF16)

    def prompt_step(stage_weights):
        hk = body(xp_ref, op_ref, prompt_rows, p_mods, p_kvmods, stage_weights)
        if with_kv:
            kp_ref[...] = _rope(_dot(hk, wk_ref[...]), *(t[...] for t in p_tabs))
            vtp_ref[...] = _dot_nt(wvt_ref[...], hk)

    @pl.when(i == 0)
    def _():
        prompt_step(True)

    @pl.when((i > 0) & (i < n_prompt_tiles))
    def _():
        prompt_step(False)

    @pl.when(i >= n_prompt_tiles)
    def _():
        hk = body(xs_ref, os_ref, decode_rows, s_mods, s_kvmods, False)
        if with_kv:
            kts_ref[...] = _rope_t(_dot_nt(wkt_ref[...], hk), *(t[...] for t in s_tabs))
            vts_ref[...] = _dot_nt(wvt_ref[...], hk)


def _resident(shape, index_map):
    return pl.BlockSpec(shape, index_map, pipeline_mode=pl.Buffered(1))


def _mod_spec(layer, col, group):
    if group.per_seq_rows is None:
        return pl.BlockSpec((None, SUBLANES, D_MODEL),
                            lambda b, t: (layer, group.prompt_mod_block, col))
    return pl.BlockSpec((None, group.tile // group.per_seq_rows, D_MODEL),
                        lambda b, t: (layer, t, col))


class _Group:
    def __init__(self, n_seq, seq_len, per_seq_rows, prompt_mod_block, tile):
        self.tile = tile
        self.rows = n_seq * seq_len
        self.per_seq_rows = per_seq_rows
        self.prompt_mod_block = prompt_mod_block
        if per_seq_rows is None:
            self.grid = (n_seq, seq_len // self.tile)
        else:
            self.grid = (1, self.rows // self.tile)
        self.tiles_per_seq = self.grid[1]

    def row_spec(self, width):
        nt = self.grid[1]
        return pl.BlockSpec((self.tile, width), lambda b, t: (b * nt + t, 0))

    def col_spec(self, height):
        nt = self.grid[1]
        return pl.BlockSpec((height, self.tile), lambda b, t: (0, b * nt + t))

    def table_t_spec(self):
        if self.per_seq_rows is None:
            return pl.BlockSpec((ROT_DIM // 2, self.tile), lambda b, t: (0, t))
        return pl.BlockSpec((ROT_DIM // 2, self.tile), lambda b, t: (0, 0))

    def table_spec(self):
        if self.per_seq_rows is None:
            return pl.BlockSpec((self.tile, LANES), lambda b, t: (t, 0))
        return pl.BlockSpec((self.tile, LANES), lambda b, t: (0, 0))


def _ffn(xp, xs, prompt, decode, mods, layer, sub, norm_g, wg, wu, wd, slot, kv=None, final_g=None):
    with_kv = kv is not None
    final = final_g is not None
    tile = prompt.tile
    assert decode.tile == tile
    n_p, n_s = prompt.rows // tile, decode.rows // tile
    tiles_per_seq = prompt.tiles_per_seq
    p_tile = lambda i: jnp.minimum(i, n_p - 1)
    s_tile = lambda i: jnp.maximum(i - n_p, 0)
    n_dec_seq = tile // decode.per_seq_rows

    def mod_specs(table_layer, cols):
        p = [pl.BlockSpec((None, SUBLANES, D_MODEL),
                          functools.partial(lambda i, c: (table_layer, prompt.prompt_mod_block, c), c=c))
             for c in cols]
        s = [pl.BlockSpec((None, n_dec_seq, D_MODEL),
                          functools.partial(lambda i, c: (table_layer, s_tile(i), c), c=c),
                          pipeline_mode=pl.Buffered(1))
             for c in cols]
        return p + s

    p_rows = lambda width: pl.BlockSpec((tile, width), lambda i: (p_tile(i), 0))
    s_rows = lambda width, **kw: pl.BlockSpec((tile, width), lambda i: (s_tile(i), 0), **kw)
    p_cols = lambda height: pl.BlockSpec((height, tile), lambda i: (0, p_tile(i)))
    s_cols = lambda height: pl.BlockSpec((height, tile), lambda i: (0, s_tile(i)))
    const = lambda shape: _resident(shape, lambda i: (0,) * len(shape))
    hbm = pl.BlockSpec(memory_space=pl.ANY)

    in_specs = [p_rows(D_MODEL), s_rows(D_MODEL, pipeline_mode=pl.Buffered(1))]
    in_specs += mod_specs(layer, [3 * sub + 0, 3 * sub + 1, 3 * sub + 2])
    in_specs += [_resident((None, 1, D_MODEL), lambda i: (layer * 3 + sub, 0, 0)), hbm, hbm, hbm]
    args = [xp, xs] + [mods] * 6 + [norm_g, wg, wu, wd]
    out_specs = [p_rows(D_MODEL), s_rows(D_MODEL)]
    out_shape = [jax.ShapeDtypeStruct((prompt.rows, D_MODEL), F32),
                 jax.ShapeDtypeStruct((decode.rows, D_MODEL), F32)]
    if with_kv:
        in_specs += [const((1, D_MODEL))] + mod_specs(0, [0, 1])
        in_specs += [const((D_MODEL, KV_DIM)), const((KV_DIM, D_MODEL)), const((KV_DIM, D_MODEL))]
        in_specs += [pl.BlockSpec((tile, LANES), lambda i: (p_tile(i) % tiles_per_seq, 0))] * 3
        in_specs += [const((ROT_DIM // 2, tile))] * 2
        args += [kv["norm_g"]] + [kv["mods"]] * 4 + [kv["w_k"], kv["w_k_t"], kv["w_v_t"]]
        args += [*kv["prompt_tables"], *kv["decode_tables_t"]]
        out_specs += [p_rows(KV_DIM), p_cols(KV_DIM), s_cols(KV_DIM), s_cols(KV_DIM)]
        out_shape += [jax.ShapeDtypeStruct((prompt.rows, KV_DIM), F32),
                      jax.ShapeDtypeStruct((KV_DIM, prompt.rows), F32),
                      jax.ShapeDtypeStruct((KV_DIM, decode.rows), F32),
                      jax.ShapeDtypeStruct((KV_DIM, decode.rows), F32)]
    if final:
        in_specs.append(const((1, D_MODEL)))
        args.append(final_g)
    scratch = [
        pltpu.VMEM((D_MODEL, D_FF), BF16), pltpu.VMEM((D_MODEL, D_FF), BF16),
        pltpu.VMEM((D_FF, D_MODEL), BF16),
        pltpu.VMEM((2, 2, D_MODEL, FF_STAGE), F32),
        pltpu.VMEM((2, FF_STAGE, D_MODEL), F32),
        pltpu.SemaphoreType.DMA((2, 3)),
    ]
    return pl.pallas_call(
        functools.partial(_ffn_kernel, n_prompt_tiles=n_p, tiles_per_seq=tiles_per_seq,
                          dec_seq=decode.per_seq_rows, layer=layer, slot=slot,
                          with_kv=with_kv, final=final),
        grid=(n_p + n_s,),
        in_specs=in_specs,
        out_specs=out_specs,
        out_shape=out_shape,
        scratch_shapes=scratch,
        compiler_params=pltpu.CompilerParams(
            dimension_semantics=("arbitrary",),
            vmem_limit_bytes=VMEM_LIMIT_BYTES),
        name=f"ffn_l{layer}_s{sub}",
    )(*args)


def _conv_kernel(*refs, per_seq_rows):
    it = iter(refs)
    x_ref, sh_ref, sc_ref, gt_ref, ng_ref, win_ref, cw_ref, wout_ref = (next(it) for _ in range(8))
    if per_seq_rows is not None:
        st0_ref, st1_ref = next(it), next(it)
    o_ref, u_ref = next(it), next(it)
    if per_seq_rows is None:
        carry_ref = next(it)

    tile = x_ref.shape[0]
    x = x_ref[...]
    h = ((_unit_rms(x) * ng_ref[...]) * (1.0 + _mod_rows(sc_ref, per_seq_rows))
         + _mod_rows(sh_ref, per_seq_rows)).astype(BF16)
    bcv = _dot(h, win_ref[...])
    b_g = bcv[:, :D_MODEL]
    u = bcv[:, D_MODEL:2 * D_MODEL] * bcv[:, 2 * D_MODEL:]

    row = lax.broadcasted_iota(jnp.int32, (tile, 1), 0)
    if per_seq_rows is None:
        @pl.when(pl.program_id(1) == 0)
        def _():
            carry_ref[...] = jnp.zeros_like(carry_ref)
        prev2 = carry_ref[SUBLANES - 2:SUBLANES - 1, :]
        prev1 = carry_ref[SUBLANES - 1:SUBLANES, :]
        t = row
    else:
        prev2 = _repeat_rows(st0_ref, per_seq_rows)
        prev1 = _repeat_rows(st1_ref, per_seq_rows)
        t = row & (per_seq_rows - 1)
    u1 = jnp.where(t == 0, prev1, pltpu.roll(u, 1, axis=0))
    u2 = jnp.where(t == 0, prev2, jnp.where(t == 1, prev1, pltpu.roll(u, 2, axis=0)))
    conv = u2 * cw_ref[0:1, :] + u1 * cw_ref[1:2, :] + u * cw_ref[2:3, :]
    y = _dot((b_g * conv).astype(BF16), wout_ref[...])
    o_ref[...] = x + _mod_rows(gt_ref, per_seq_rows) * y

    if per_seq_rows is None:
        tail = u[tile - SUBLANES:, :]
        carry_ref[...] = tail
        u_ref[...] = tail
    else:
        u_ref[...] = u


def _conv_mixer(x, group, mods, norm_g, w_in, conv_w, w_out, state=None):
    prompt = group.per_seq_rows is None
    in_specs = [
        group.row_spec(D_MODEL),
        _mod_spec(0, 3, group), _mod_spec(0, 4, group), _mod_spec(0, 5, group),
        _resident((None, 1, D_MODEL), lambda b, t: (1, 0, 0)),
        _resident((D_MODEL, 3 * D_MODEL), lambda b, t: (0, 0)),
        _resident((CONV_WIDTH, D_MODEL), lambda b, t: (0, 0)),
        _resident((D_MODEL, D_MODEL), lambda b, t: (0, 0)),
    ]
    args = [x, mods, mods, mods, norm_g, w_in, conv_w, w_out]
    scratch = []
    if prompt:
        u_spec = pl.BlockSpec((SUBLANES, D_MODEL), lambda b, t: (b, 0))
        u_shape = jax.ShapeDtypeStruct((group.grid[0] * SUBLANES, D_MODEL), F32)
        scratch.append(pltpu.VMEM((SUBLANES, D_MODEL), F32))
    else:
        n_seq = group.tile // group.per_seq_rows
        st_spec = pl.BlockSpec((n_seq, D_MODEL), lambda b, t: (t, 0))
        in_specs += [st_spec, st_spec]
        args += [state[0], state[1]]
        u_spec = group.row_spec(D_MODEL)
        u_shape = jax.ShapeDtypeStruct((group.rows, D_MODEL), F32)
    return pl.pallas_call(
        functools.partial(_conv_kernel, per_seq_rows=group.per_seq_rows),
        grid=group.grid,
        in_specs=in_specs,
        out_specs=[group.row_spec(D_MODEL), u_spec],
        out_shape=[jax.ShapeDtypeStruct((group.rows, D_MODEL), F32), u_shape],
        scratch_shapes=scratch,
        compiler_params=pltpu.CompilerParams(
            dimension_semantics=("arbitrary", "arbitrary"),
            vmem_limit_bytes=VMEM_LIMIT_BYTES),
        name="conv_mixer",
    )(*args)


def _sink_column(sink_ref, kv_head, q_rows):
    grp = lax.broadcasted_iota(jnp.int32, (GROUP * q_rows, 1), 0) >> _log2(q_rows)
    col = jnp.full((GROUP * q_rows, 1), sink_ref[kv_head * GROUP + GROUP - 1], F32)
    for g in range(GROUP - 2, -1, -1):
        col = jnp.where(grp == g, sink_ref[kv_head * GROUP + g], col)
    return col


def _stack_heads(q_ref, rows, kv_head):
    return jnp.concatenate(
        [q_ref[rows, (kv_head * GROUP + g) * HEAD_DIM:(kv_head * GROUP + g + 1) * HEAD_DIM]
         for g in range(GROUP)], axis=0)


def _softmax_parts(scores, sink):
    m = sink
    for s in scores:
        m = jnp.maximum(m, jnp.max(s, axis=-1, keepdims=True))
    ps = [jnp.exp(s - m) for s in scores]
    denom = jnp.exp(sink - m)
    for p in ps:
        denom = denom + jnp.sum(p, axis=-1, keepdims=True)
    return ps, 1.0 / denom


def _project_q(x_ref, sh_ref, sc_ref, ng_ref, wq_ref, tables, per_seq_rows):
    h = ((_unit_rms(x_ref[...]) * ng_ref[...]) * (1.0 + _mod_rows(sc_ref, per_seq_rows))
         + _mod_rows(sh_ref, per_seq_rows)).astype(BF16)
    q = _rope(_dot(h, wq_ref[...]), *tables)
    return (q * (HEAD_DIM ** -0.5)).astype(BF16)


def _band_caps():
    shape = (2 * WINDOW, GROUP * WINDOW)
    kj = lax.broadcasted_iota(jnp.int32, shape, 0)
    qi = lax.broadcasted_iota(jnp.int32, shape, 1) & (WINDOW - 1)
    in_band = (kj > qi) & (kj <= qi + WINDOW)
    return (jnp.where(in_band, FMAX, NEG), jnp.where(in_band & (kj >= WINDOW), FMAX, NEG))


def _attn_prompt_kernel(x_ref, sh_ref, sc_ref, gt_ref, ng_ref, wq_ref, wo_ref,
                        kp_ref, kc_ref, vp_ref, vc_ref, cos_ref, sin_ref, sink_ref,
                        o_ref, q_scr, k_scr, v_scr, a_scr, cap_scr, h_scr, at_scr):
    tile = x_ref.shape[0]

    @pl.when((pl.program_id(0) == 0) & (pl.program_id(1) == 0))
    def _():
        general, first = _band_caps()
        cap_scr[0] = general
        cap_scr[1] = first

    k_scr[:WINDOW, :] = kp_ref[...].astype(BF16)
    k_scr[WINDOW:, :] = kc_ref[...].astype(BF16)
    v_scr[:, :WINDOW] = vp_ref[...].astype(BF16)
    v_scr[:, WINDOW:] = vc_ref[...].astype(BF16)
    first_tile = pl.program_id(1) == 0
    norm_gain = ng_ref[...]
    scale1 = 1.0 + _mod_rows(sc_ref, None)
    shift = _mod_rows(sh_ref, None)
    gate = _mod_rows(gt_ref, None)

    group_dims = GROUP * HEAD_DIM

    def chunk_rows(c):
        return slice(c * ATTN_CHUNK, (c + 1) * ATTN_CHUNK)

    def normalize(c):
        rows = chunk_rows(c)
        h_scr[rows, :] = ((_unit_rms(x_ref[rows, :]) * norm_gain) * scale1 + shift).astype(BF16)

    def project_piece(c, kvh):
        rows = chunk_rows(c)
        dims = slice(kvh * group_dims, (kvh + 1) * group_dims)
        qt = _rope_t(_dot(h_scr[rows, :], wq_ref[:, dims]).T, cos_ref[:, rows], sin_ref[:, rows])
        q_scr[dims, rows] = (qt * (HEAD_DIM ** -0.5 * LOG2E)).astype(BF16)

    def attend(blk, kvh):
        cols = slice(blk * WINDOW, (blk + 1) * WINDOW)
        keys = slice(blk * WINDOW, (blk + 2) * WINDOW)
        cap = cap_scr[jnp.where(first_tile, 1, 0)] if blk == 0 else cap_scr[0]
        dims = slice(kvh * HEAD_DIM, (kvh + 1) * HEAD_DIM)
        heads = [slice((kvh * GROUP + g) * HEAD_DIM, (kvh * GROUP + g + 1) * HEAD_DIM)
                 for g in range(GROUP)]
        qg = jnp.concatenate([q_scr[hd, cols] for hd in heads], axis=1)
        s = jnp.minimum(_dot(k_scr[keys, dims], qg), cap)
        sink = LOG2E * jnp.concatenate(
            [jnp.full((1, WINDOW), sink_ref[kvh * GROUP + g], F32) for g in range(GROUP)], axis=1)
        m = jnp.maximum(jnp.max(s, axis=0, keepdims=True), sink)
        p = jnp.exp2(s - m)
        denom = jnp.sum(p, axis=0, keepdims=True) + jnp.exp2(sink - m)
        o = _dot(v_scr[dims, keys], p.astype(BF16)) * (1.0 / denom)
        for g, hd in enumerate(heads):
            a_scr[hd, cols] = o[:, g * WINDOW:(g + 1) * WINDOW]

    def transpose_piece(c, kvh):
        rows = chunk_rows(c)
        dims = slice(kvh * group_dims, (kvh + 1) * group_dims)
        at_scr[rows, dims] = a_scr[dims, rows].T.astype(BF16)

    def emit_piece(c, n):
        rows = chunk_rows(c)
        cols = slice(n * group_dims, (n + 1) * group_dims)
        y = _dot(at_scr[rows, :], wo_ref[:, cols])
        o_ref[rows, cols] = x_ref[rows, cols] + gate[:, cols] * y

    n_chunks = tile // ATTN_CHUNK
    blocks_per_chunk = ATTN_CHUNK // WINDOW
    n_pieces = D_MODEL // group_dims
    normalize(0)
    for kvh in range(N_KV_HEADS):
        project_piece(0, kvh)
    for c in range(n_chunks):
        if c + 1 < n_chunks:
            normalize(c + 1)
        step = 0
        for b in range(blocks_per_chunk):
            for kvh in range(N_KV_HEADS):
                attend(c * blocks_per_chunk + b, kvh)
                if b == blocks_per_chunk - 1:
                    transpose_piece(c, kvh)
                if step % 2 == 0 and c + 1 < n_chunks and step // 2 < N_KV_HEADS:
                    project_piece(c + 1, step // 2)
                if step % 2 == 1 and c > 0 and step // 2 < n_pieces:
                    emit_piece(c - 1, step // 2)
                step += 1
    for n in range(n_pieces):
        emit_piece(n_chunks - 1, n)


def _attn_prompt(x, group, mods, norm_g, w_q, w_o, k, v_t, tables_t, sinks):
    tile = group.tile
    nt = group.grid[1]
    blocks_per_tile = tile // WINDOW

    def prev_block(b, t):
        return jnp.maximum((b * nt + t) * blocks_per_tile - 1, 0)

    in_specs = [
        group.row_spec(D_MODEL),
        _mod_spec(1, 3, group), _mod_spec(1, 4, group), _mod_spec(1, 5, group),
        _resident((None, 1, D_MODEL), lambda b, t: (4, 0, 0)),
        _resident((D_MODEL, D_MODEL), lambda b, t: (0, 0)),
        _resident((D_MODEL, D_MODEL), lambda b, t: (0, 0)),
        pl.BlockSpec((WINDOW, KV_DIM), lambda b, t: (prev_block(b, t), 0)), group.row_spec(KV_DIM),
        pl.BlockSpec((KV_DIM, WINDOW), lambda b, t: (0, prev_block(b, t))), group.col_spec(KV_DIM),
        group.table_t_spec(), group.table_t_spec(),
        pl.BlockSpec(memory_space=pltpu.SMEM),
    ]
    return pl.pallas_call(
        _attn_prompt_kernel,
        grid=group.grid,
        in_specs=in_specs,
        out_specs=group.row_spec(D_MODEL),
        out_shape=jax.ShapeDtypeStruct((group.rows, D_MODEL), F32),
        scratch_shapes=[
            pltpu.VMEM((D_MODEL, tile), BF16),
            pltpu.VMEM((tile + WINDOW, KV_DIM), BF16),
            pltpu.VMEM((KV_DIM, tile + WINDOW), BF16),
            pltpu.VMEM((D_MODEL, tile), F32),
            pltpu.VMEM((2, 2 * WINDOW, GROUP * WINDOW), F32),
            pltpu.VMEM((tile, D_MODEL), BF16),
            pltpu.VMEM((tile, D_MODEL), BF16),
        ],
        compiler_params=pltpu.CompilerParams(
            dimension_semantics=("arbitrary", "arbitrary"),
            vmem_limit_bytes=VMEM_LIMIT_BYTES),
        name="attn_prompt",
    )(x, mods, mods, mods, norm_g, w_q, w_o, k, k, v_t, v_t, *tables_t, sinks)


def _attn_sample_kernel(x_ref, sh_ref, sc_ref, gt_ref, ng_ref, wq_ref, wo_ref,
                        kc_ref, vc_ref, kn_ref, vn_ref, cos_ref, sup_ref, sdn_ref, sink_ref,
                        o_ref, kout_ref, vout_ref, q_scr, *, dec_seq):
    rows = x_ref.shape[0]
    n_seq = rows // dec_seq
    n_cache = n_seq * WINDOW
    q_scr[...] = _project_q(x_ref, sh_ref, sc_ref, ng_ref, wq_ref,
                            (cos_ref[...], sup_ref[...], sdn_ref[...]), dec_seq)

    pack = DEC_ATTN_PACK
    pack_rows = pack * dec_seq

    def windows(ref, first_seq, kvh):
        return jnp.concatenate(
            [ref[(s * N_KV_HEADS + kvh) * HEAD_DIM:(s * N_KV_HEADS + kvh + 1) * HEAD_DIM, :].astype(BF16)
             for s in range(first_seq, first_seq + pack)], axis=1)

    q_row = lax.broadcasted_iota(jnp.int32, (GROUP * pack_rows, 1), 0) & (pack_rows - 1)
    q_seq, q_i = q_row >> _log2(dec_seq), q_row & (dec_seq - 1)
    c_col = lax.broadcasted_iota(jnp.int32, (1, pack * WINDOW), 1)
    valid_c = (q_seq == (c_col >> _log2(WINDOW))) & ((c_col & (WINDOW - 1)) > q_i)
    n_col = lax.broadcasted_iota(jnp.int32, (1, rows), 1)
    n_seq_col, n_visible = n_col >> _log2(dec_seq), (n_col & (dec_seq - 1)) <= q_i
    kn = [kn_ref[kvh * HEAD_DIM:(kvh + 1) * HEAD_DIM, :].astype(BF16) for kvh in range(N_KV_HEADS)]
    vn = [vn_ref[kvh * HEAD_DIM:(kvh + 1) * HEAD_DIM, :].astype(BF16) for kvh in range(N_KV_HEADS)]

    keep = lax.broadcasted_iota(jnp.int32, (1, WINDOW), 1) < WINDOW - dec_seq

    def update_windows(first_seq):
        for src_ref, new_ref, dst_ref in ((kc_ref, kn_ref, kout_ref), (vc_ref, vn_ref, vout_ref)):
            new = new_ref[...]
            for s in range(first_seq, first_seq + pack):
                blk = slice(s * KV_DIM, (s + 1) * KV_DIM)
                old = pltpu.roll(src_ref[blk, :], WINDOW - dec_seq, axis=1)
                app = pltpu.roll(new, (WINDOW - dec_seq - s * dec_seq) % rows, axis=1)
                dst_ref[blk, :] = jnp.where(keep, old, app)

    packs = []
    for first_seq in range(0, n_seq, pack):
        update_windows(first_seq)
        q_rows = slice(first_seq * dec_seq, (first_seq + pack) * dec_seq)
        valid_n = ((q_seq + first_seq) == n_seq_col) & n_visible
        heads = []
        for kvh in range(N_KV_HEADS):
            qg = _stack_heads(q_scr, q_rows, kvh)
            s_c = jnp.where(valid_c, _dot(qg, windows(kc_ref, first_seq, kvh)), NEG)
            s_n = jnp.where(valid_n, _dot(qg, kn[kvh]), NEG)
            (p_c, p_n), inv = _softmax_parts([s_c, s_n], _sink_column(sink_ref, kvh, pack_rows))
            o = (_dot_nt(p_c.astype(BF16), windows(vc_ref, first_seq, kvh))
                 + _dot_nt(p_n.astype(BF16), vn[kvh])) * inv
            heads += [o[g * pack_rows:(g + 1) * pack_rows, :] for g in range(GROUP)]
        packs.append(jnp.concatenate(heads, axis=1))
    attn = jnp.concatenate(packs, axis=0).astype(BF16)
    o_ref[...] = x_ref[...] + _mod_rows(gt_ref, dec_seq) * _dot(attn, wo_ref[...])


def _attn_sample(x, n_seq_total, dec_seq, mods, norm_g, w_q, w_o, k_cache, v_cache,
                 k_new, v_new, tables, sinks):
    n_seq = SAMPLE_ATTN_SEQS
    rows = n_seq * dec_seq
    assert rows == WINDOW == LANES
    grid = (n_seq_total // n_seq,)
    row_spec = lambda width: pl.BlockSpec((rows, width), lambda t: (t, 0))
    mod_spec = lambda col: pl.BlockSpec((None, n_seq, D_MODEL), lambda t: (1, t, col))
    cache_spec = pl.BlockSpec((n_seq * KV_DIM, WINDOW), lambda t: (t, 0))
    new_spec = pl.BlockSpec((KV_DIM, rows), lambda t: (0, t))
    table_spec = pl.BlockSpec((rows, LANES), lambda t: (0, 0))
    in_specs = [
        row_spec(D_MODEL), mod_spec(3), mod_spec(4), mod_spec(5),
        _resident((None, 1, D_MODEL), lambda t: (4, 0, 0)),
        _resident((D_MODEL, D_MODEL), lambda t: (0, 0)),
        _resident((D_MODEL, D_MODEL), lambda t: (0, 0)),
        cache_spec, cache_spec, new_spec, new_spec,
        table_spec, table_spec, table_spec,
        pl.BlockSpec(memory_space=pltpu.SMEM),
    ]
    cache_shape = jax.ShapeDtypeStruct((n_seq_total * KV_DIM, WINDOW), F32)
    return pl.pallas_call(
        functools.partial(_attn_sample_kernel, dec_seq=dec_seq),
        grid=grid,
        in_specs=in_specs,
        out_specs=[row_spec(D_MODEL), cache_spec, cache_spec],
        out_shape=[jax.ShapeDtypeStruct((n_seq_total * dec_seq, D_MODEL), F32),
                   cache_shape, cache_shape],
        scratch_shapes=[pltpu.VMEM((rows, D_MODEL), BF16)],
        compiler_params=pltpu.CompilerParams(
            dimension_semantics=("arbitrary",),
            vmem_limit_bytes=VMEM_LIMIT_BYTES),
        name="attn_sample",
    )(x, mods, mods, mods, norm_g, w_q, w_o, k_cache, v_cache, k_new, v_new, *tables, sinks)


def _rope_angles(pos):
    inv_freq = ROPE_THETA ** (-jnp.arange(0, ROT_DIM, 2, dtype=F32) / ROT_DIM)
    ang = pos.astype(F32)[:, None] * inv_freq[None, :]
    return jnp.cos(ang), jnp.sin(ang)


def _rope_tables(pos):
    half = ROT_DIM // 2
    cos, sin = _rope_angles(pos)
    n = pos.shape[0]
    rest = HEAD_DIM - ROT_DIM
    zeros = jnp.zeros((n, half), F32)
    cos_t = jnp.concatenate([cos, cos, jnp.ones((n, rest), F32)], axis=1)
    sin_up = jnp.concatenate([-sin, zeros, jnp.zeros((n, rest), F32)], axis=1)
    sin_dn = jnp.concatenate([zeros, sin, jnp.zeros((n, rest), F32)], axis=1)
    reps = LANES // HEAD_DIM
    return tuple(jnp.tile(t, (1, reps)) for t in (cos_t, sin_up, sin_dn))


def kernel(x_prompt, x_sample, state_conv, cache_k_win, cache_v_win, c_prompt, c_sample, norm_g, w_ada, b_ada, w_ffn_gate, w_ffn_up, w_ffn_down, conv_w_in, conv_w, conv_w_out, kv_norm_g, w_ada_kv, b_ada_kv, w_k, w_v, attn_w_q, attn_sinks, attn_w_o, final_norm_g):
    batch, seq, d = x_prompt.shape
    dec_batch, dec_seq, _ = x_sample.shape
    w_buf = cache_k_win.shape[1]
    past_len = 16384
    assert d == D_MODEL and w_buf == WINDOW and dec_seq == SUBLANES
    assert w_ada.shape[0] == 2 and conv_w_in.shape[0] == 1 and attn_w_q.shape[0] == 1

    pad = (-(dec_batch + batch)) % SUBLANES
    c_all = jnp.concatenate([c_sample, c_prompt, jnp.zeros((pad, d), F32)], axis=0)
    mods = _ada(c_all, w_ada, b_ada)
    mods_kv = _ada(c_all, w_ada_kv[None], b_ada_kv[None])
    prompt_mod_block = dec_batch // SUBLANES

    make_groups = lambda tile: {
        "prompt": _Group(batch, seq, None, prompt_mod_block, tile),
        "sample": _Group(dec_batch, dec_seq, dec_seq, prompt_mod_block, tile),
    }
    ffn_groups, mixer_groups = make_groups(FFN_TILE), make_groups(MIXER_TILE)
    xp, xs = x_prompt.reshape(batch * seq, d), x_sample.reshape(dec_batch * dec_seq, d)

    norm_g3 = norm_g.reshape(-1, 1, d)
    w_in, w_out = conv_w_in[0].astype(BF16), conv_w_out[0].astype(BF16)
    w_q, w_o = attn_w_q[0].astype(BF16), attn_w_o[0].astype(BF16)
    sinks = attn_sinks[0]

    pos_dec = past_len + jnp.arange(dec_seq, dtype=jnp.int32)
    pos_prompt = jnp.arange(seq, dtype=jnp.int32)
    tab_prompt = _rope_tables(pos_prompt)
    tab_dec = tuple(jnp.tile(t, (SAMPLE_ATTN_SEQS, 1)) for t in _rope_tables(pos_dec))
    tab_prompt_t = tuple(t.T for t in _rope_angles(pos_prompt))
    tab_dec_t = tuple(jnp.tile(t.T, (1, FFN_TILE // dec_seq)) for t in _rope_angles(pos_dec))
    kv_args = dict(norm_g=kv_norm_g.reshape(1, d), mods=mods_kv, w_k=w_k.astype(BF16),
                   w_k_t=w_k.T.astype(BF16), w_v_t=w_v.T.astype(BF16),
                   prompt_tables=tab_prompt, decode_tables_t=tab_dec_t)
    to_rows = lambda c: jnp.transpose(c, (0, 2, 3, 1)).reshape(dec_batch * KV_DIM, w_buf)
    from_rows = lambda c: jnp.transpose(c.reshape(dec_batch, N_KV_HEADS, HEAD_DIM, w_buf), (0, 3, 1, 2))

    ffn = functools.partial(_ffn, prompt=ffn_groups["prompt"], decode=ffn_groups["sample"], mods=mods,
                            norm_g=norm_g3, wg=w_ffn_gate, wu=w_ffn_up, wd=w_ffn_down)
    xp, xs = ffn(xp, xs, layer=0, sub=0, slot=0)
    xp, u_p = _conv_mixer(xp, mixer_groups["prompt"], mods, norm_g3, w_in, conv_w[0], w_out)
    xs, u_s = _conv_mixer(xs, mixer_groups["sample"], mods, norm_g3, w_in, conv_w[0], w_out,
                          (state_conv[0, :, 0, :], state_conv[0, :, 1, :]))
    xp, xs = ffn(xp, xs, layer=0, sub=2, slot=1)
    xp, xs, k_p, vt_p, kt_s, vt_s = ffn(xp, xs, layer=1, sub=0, slot=0, kv=kv_args)
    xp = _attn_prompt(xp, make_groups(ATTN_TILE)["prompt"], mods, norm_g3, w_q, w_o, k_p, vt_p, tab_prompt_t, sinks)
    xs, k_win, v_win = _attn_sample(xs, dec_batch, dec_seq, mods, norm_g3, w_q, w_o,
                                    to_rows(cache_k_win), to_rows(cache_v_win), kt_s, vt_s, tab_dec, sinks)
    y_p, y_s = ffn(xp, xs, layer=1, sub=2, slot=1, final_g=final_norm_g.reshape(1, d))

    k_state_p = k_p.reshape(batch, seq, KV_DIM)[:, seq - WINDOW:].reshape(
        batch, WINDOW, N_KV_HEADS, HEAD_DIM)
    v_state_p = jnp.stack([vt_p[:, (b + 1) * seq - WINDOW:(b + 1) * seq] for b in range(batch)])
    v_state_p = jnp.transpose(v_state_p.reshape(batch, N_KV_HEADS, HEAD_DIM, WINDOW), (0, 3, 1, 2))
    k_state_s, v_state_s = from_rows(k_win), from_rows(v_win)
    tail = CONV_WIDTH - 1
    conv_p = u_p.reshape(batch, SUBLANES, d)[:, SUBLANES - tail:][None]
    conv_s = u_s.reshape(dec_batch, dec_seq, d)[:, dec_seq - tail:][None]
    return (y_p.reshape(batch, seq, d), y_s.reshape(dec_batch, dec_seq, d), conv_p, conv_s,
            k_state_p, v_state_p, k_state_s, v_state_s)
```

```python
import functools

import jax
import jax.numpy as jnp
from jax import lax
from jax.experimental import pallas as pl
from jax.experimental.pallas import tpu as pltpu

F32 = jnp.float32
BF16 = jnp.bfloat16

D_MODEL = 1024
D_FF = 2816
HEAD_DIM = 64
N_HEADS = 16
N_KV_HEADS = 4
GROUP = N_HEADS // N_KV_HEADS
KV_DIM = N_KV_HEADS * HEAD_DIM
WINDOW = 128
ROT_DIM = 16
ROPE_THETA = 500000.0
CONV_WIDTH = 3
N_MOD = 9
EPS = 1e-6
NEG = -1e30
FMAX = float(jnp.finfo(jnp.float32).max)

SUBLANES = 8
LANES = 128
VMEM_LIMIT_BYTES = 60 * 1024 * 1024

ADA_COLS = 2304
FFN_TILE = 512
FFN_SUBTILES = 2
FF_STAGE = 256
MIXER_TILE = 512
ATTN_TILE = 1024
ATTN_CHUNK = 256
LOG2E = 1.4426950408889634
SAMPLE_ATTN_SEQS = 16
DEC_ATTN_PACK = 4
FF_CHUNKS = ((0, 1024), (1024, 2048), (2048, D_FF))


def _dot(a, b):
    return jnp.dot(a, b, preferred_element_type=F32)


def _dot_nt(a, b):
    return lax.dot_general(a, b, (((1,), (1,)), ((), ())), preferred_element_type=F32)


def _log2(n):
    assert n & (n - 1) == 0, n
    return n.bit_length() - 1


def _silu(x):
    return x * jax.nn.sigmoid(x)


def _unit_rms(x):
    return x * lax.rsqrt(jnp.mean(x * x, axis=-1, keepdims=True) + EPS)


def _repeat_rows(ref, reps):
    n = ref.shape[1]
    return jnp.concatenate(
        [jnp.broadcast_to(ref[i:i + 1, :], (reps, n)) for i in range(ref.shape[0])], axis=0)


def _mod_rows(ref, per_seq_rows):
    if per_seq_rows is None:
        return ref[pl.ds(pl.program_id(0), 1), :]
    return _repeat_rows(ref, per_seq_rows)


def _rope(x, cos_t, sin_up_t, sin_dn_t):
    n = x.shape[1]
    reps = n // LANES
    half = ROT_DIM // 2
    cos = jnp.concatenate([cos_t] * reps, axis=1)
    sin_up = jnp.concatenate([sin_up_t] * reps, axis=1)
    sin_dn = jnp.concatenate([sin_dn_t] * reps, axis=1)
    x_up = pltpu.roll(x, n - half, axis=1)
    x_dn = pltpu.roll(x, half, axis=1)
    return x * cos + x_up * sin_up + x_dn * sin_dn


def _ada_kernel(c_ref, w_ref, b_ref, o_ref):
    a = _silu(c_ref[...]).astype(BF16)
    o_ref[...] = _dot(a, w_ref[...].astype(BF16)) + b_ref[...]


def _ada(c_all, w, b):
    n_layers, _, n = w.shape
    m = c_all.shape[0]
    tn = ADA_COLS if n % ADA_COLS == 0 else D_MODEL
    return pl.pallas_call(
        _ada_kernel,
        grid=(n_layers, n // tn),
        in_specs=[
            pl.BlockSpec((m, D_MODEL), lambda l, j: (0, 0)),
            pl.BlockSpec((None, D_MODEL, tn), lambda l, j: (l, 0, j)),
            pl.BlockSpec((None, 1, tn), lambda l, j: (l, 0, j)),
        ],
        out_specs=pl.BlockSpec((None, m, tn), lambda l, j: (l, 0, j)),
        out_shape=jax.ShapeDtypeStruct((n_layers, m, n), F32),
        compiler_params=pltpu.CompilerParams(
            dimension_semantics=("arbitrary", "arbitrary"),
            vmem_limit_bytes=VMEM_LIMIT_BYTES),
        name="adaln_tables",
    )(c_all, w, b.reshape(n_layers, 1, n))


def _rope_t(xt, cos_t, sin_t):
    half = ROT_DIM // 2
    out = []
    for h in range(xt.shape[0] // HEAD_DIM):
        base = h * HEAD_DIM
        x1 = xt[base:base + half, :]
        x2 = xt[base + half:base + ROT_DIM, :]
        out += [x1 * cos_t - x2 * sin_t, x2 * cos_t + x1 * sin_t,
                xt[base + ROT_DIM:base + HEAD_DIM, :]]
    return jnp.concatenate(out, axis=0)


def _ffn_kernel(*refs, n_prompt_tiles, tiles_per_seq, dec_seq, layer, slot, with_kv, final):
    it = iter(refs)
    take = lambda n: [next(it) for _ in range(n)]
    xp_ref, xs_ref = take(2)
    p_mods, s_mods = take(3), take(3)
    ng_ref, wg_hbm, wu_hbm, wd_hbm = take(4)
    if with_kv:
        (kvg_ref,), p_kvmods, s_kvmods = take(1), take(2), take(2)
        wk_ref, wkt_ref, wvt_ref = take(3)
        p_tabs, s_tabs = take(3), take(2)
    else:
        p_kvmods = s_kvmods = None
    if final:
        (fg_ref,) = take(1)
    op_ref, os_ref = take(2)
    if with_kv:
        kp_ref, vtp_ref, kts_ref, vts_ref = take(4)
    wg_s, wu_s, wd_s, stage_in, stage_out, sem = take(6)

    i = pl.program_id(0)
    prompt_rows = lambda ref: ref[pl.ds(i // tiles_per_seq, 1), :]
    decode_rows = lambda ref: _repeat_rows(ref, dec_seq)

    def weight_copies(c):
        cols = pl.ds(c * FF_STAGE, FF_STAGE)
        buf = c % 2
        return (
            pltpu.make_async_copy(wg_hbm.at[layer, slot, :, cols], stage_in.at[buf, 0], sem.at[buf, 0]),
            pltpu.make_async_copy(wu_hbm.at[layer, slot, :, cols], stage_in.at[buf, 1], sem.at[buf, 1]),
            pltpu.make_async_copy(wd_hbm.at[layer, slot, cols, :], stage_out.at[buf], sem.at[buf, 2]),
        )

    def partial_ffn(hb, lo, hi):
        g = _dot(hb, wg_s[:, lo:hi])
        u = _dot(hb, wu_s[:, lo:hi])
        return _dot((_silu(g) * u).astype(BF16), wd_s[lo:hi, :])

    def split_body(x_ref, o_ref, rows_of, mods, kvmods):
        sub = x_ref.shape[0] // FFN_SUBTILES
        groups = [slice(k * sub, (k + 1) * sub) for k in range(FFN_SUBTILES)]
        rows_in = lambda v, r: v if v.shape[0] == 1 else v[r]
        shift, scale, gate = (rows_of(m) for m in mods)
        xs_, xns, hbs = [], [], []
        ys = [None] * FFN_SUBTILES
        for c, (lo, hi) in enumerate(FF_CHUNKS):
            for k, r in enumerate(groups):
                if c == 0:
                    x = x_ref[r, :]
                    xn = _unit_rms(x)
                    xs_.append(x)
                    xns.append(xn)
                    hbs.append(((xn * ng_ref[...]) * (1.0 + rows_in(scale, r))
                                + rows_in(shift, r)).astype(BF16))
                part = partial_ffn(hbs[k], lo, hi)
                ys[k] = part if ys[k] is None else ys[k] + part
                if c == len(FF_CHUNKS) - 1:
                    x_new = xs_[k] + (0.5 * rows_in(gate, r)) * ys[k]
                    o_ref[r, :] = _unit_rms(x_new) * fg_ref[...] if final else x_new
        if with_kv:
            kv_shift, kv_scale = rows_of(kvmods[0]), rows_of(kvmods[1])
            return jnp.concatenate(
                [((xns[k] * kvg_ref[...]) * (1.0 + rows_in(kv_scale, r)) + rows_in(kv_shift, r)).astype(BF16)
                 for k, r in enumerate(groups)], axis=0)

    def body(x_ref, o_ref, rows_of, mods, kvmods, stage_weights):
        if not stage_weights:
            return split_body(x_ref, o_ref, rows_of, mods, kvmods)
        x = x_ref[...]
        xn = _unit_rms(x)
        hb = ((xn * ng_ref[...]) * (1.0 + rows_of(mods[1])) + rows_of(mods[0])).astype(BF16)
        y = None
        n_slices = D_FF // FF_STAGE
        for c in range(min(2, n_slices)):
            for cp in weight_copies(c):
                cp.start()
        for c in range(n_slices):
            lo, hi = c * FF_STAGE, (c + 1) * FF_STAGE
            for cp in weight_copies(c):
                cp.wait()
            wg_s[:, lo:hi] = stage_in[c % 2, 0].astype(BF16)
            wu_s[:, lo:hi] = stage_in[c % 2, 1].astype(BF16)
            wd_s[lo:hi, :] = stage_out[c % 2].astype(BF16)
            if c + 2 < n_slices:
                for cp in weight_copies(c + 2):
                    cp.start()
            part = partial_ffn(hb, lo, hi)
            y = part if y is None else y + part
        x_new = x + (0.5 * rows_of(mods[2])) * y
        o_ref[...] = _unit_rms(x_new) * fg_ref[...] if final else x_new
        if with_kv:
            return ((xn * kvg_ref[...]) * (1.0 + rows_of(kvmods[1])) + rows_of(kvmods[0])).astype(BF16)

    def prompt_step(stage_weights):
        hk = body(xp_ref, op_ref, prompt_rows, p_mods, p_kvmods, stage_weights)
        if with_kv:
            kp_ref[...] = _rope(_dot(hk, wk_ref[...]), *(t[...] for t in p_tabs))
            vtp_ref[...] = _dot_nt(wvt_ref[...], hk)

    @pl.when(i == 0)
    def _():
        prompt_step(True)

    @pl.when((i > 0) & (i < n_prompt_tiles))
    def _():
        prompt_step(False)

    @pl.when(i >= n_prompt_tiles)
    def _():
        hk = body(xs_ref, os_ref, decode_rows, s_mods, s_kvmods, False)
        if with_kv:
            kts_ref[...] = _rope_t(_dot_nt(wkt_ref[...], hk), *(t[...] for t in s_tabs))
            vts_ref[...] = _dot_nt(wvt_ref[...], hk)


def _resident(shape, index_map):
    return pl.BlockSpec(shape, index_map, pipeline_mode=pl.Buffered(1))


def _mod_spec(layer, col, group):
    if group.per_seq_rows is None:
        return pl.BlockSpec((None, SUBLANES, D_MODEL),
                            lambda b, t: (layer, group.prompt_mod_block, col))
    return pl.BlockSpec((None, group.tile // group.per_seq_rows, D_MODEL),
                        lambda b, t: (layer, t, col))


class _Group:
    def __init__(self, n_seq, seq_len, per_seq_rows, prompt_mod_block, tile):
        self.tile = tile
        self.rows = n_seq * seq_len
        self.per_seq_rows = per_seq_rows
        self.prompt_mod_block = prompt_mod_block
        if per_seq_rows is None:
            self.grid = (n_seq, seq_len // self.tile)
        else:
            self.grid = (1, self.rows // self.tile)
        self.tiles_per_seq = self.grid[1]

    def row_spec(self, width):
        nt = self.grid[1]
        return pl.BlockSpec((self.tile, width), lambda b, t: (b * nt + t, 0))

    def col_spec(self, height):
        nt = self.grid[1]
        return pl.BlockSpec((height, self.tile), lambda b, t: (0, b * nt + t))

    def table_t_spec(self):
        if self.per_seq_rows is None:
            return pl.BlockSpec((ROT_DIM // 2, self.tile), lambda b, t: (0, t))
        return pl.BlockSpec((ROT_DIM // 2, self.tile), lambda b, t: (0, 0))

    def table_spec(self):
        if self.per_seq_rows is None:
            return pl.BlockSpec((self.tile, LANES), lambda b, t: (t, 0))
        return pl.BlockSpec((self.tile, LANES), lambda b, t: (0, 0))


def _ffn(xp, xs, prompt, decode, mods, layer, sub, norm_g, wg, wu, wd, slot, kv=None, final_g=None):
    with_kv = kv is not None
    final = final_g is not None
    tile = prompt.tile
    assert decode.tile == tile
    n_p, n_s = prompt.rows // tile, decode.rows // tile
    tiles_per_seq = prompt.tiles_per_seq
    p_tile = lambda i: jnp.minimum(i, n_p - 1)
    s_tile = lambda i: jnp.maximum(i - n_p, 0)
    n_dec_seq = tile // decode.per_seq_rows

    def mod_specs(table_layer, cols):
        p = [pl.BlockSpec((None, SUBLANES, D_MODEL),
                          functools.partial(lambda i, c: (table_layer, prompt.prompt_mod_block, c), c=c))
             for c in cols]
        s = [pl.BlockSpec((None, n_dec_seq, D_MODEL),
                          functools.partial(lambda i, c: (table_layer, s_tile(i), c), c=c),
                          pipeline_mode=pl.Buffered(1))
             for c in cols]
        return p + s

    p_rows = lambda width: pl.BlockSpec((tile, width), lambda i: (p_tile(i), 0))
    s_rows = lambda width, **kw: pl.BlockSpec((tile, width), lambda i: (s_tile(i), 0), **kw)
    p_cols = lambda height: pl.BlockSpec((height, tile), lambda i: (0, p_tile(i)))
    s_cols = lambda height: pl.BlockSpec((height, tile), lambda i: (0, s_tile(i)))
    const = lambda shape: _resident(shape, lambda i: (0,) * len(shape))
    hbm = pl.BlockSpec(memory_space=pl.ANY)

    in_specs = [p_rows(D_MODEL), s_rows(D_MODEL, pipeline_mode=pl.Buffered(1))]
    in_specs += mod_specs(layer, [3 * sub + 0, 3 * sub + 1, 3 * sub + 2])
    in_specs += [_resident((None, 1, D_MODEL), lambda i: (layer * 3 + sub, 0, 0)), hbm, hbm, hbm]
    args = [xp, xs] + [mods] * 6 + [norm_g, wg, wu, wd]
    out_specs = [p_rows(D_MODEL), s_rows(D_MODEL)]
    out_shape = [jax.ShapeDtypeStruct((prompt.rows, D_MODEL), F32),
                 jax.ShapeDtypeStruct((decode.rows, D_MODEL), F32)]
    if with_kv:
        in_specs += [const((1, D_MODEL))] + mod_specs(0, [0, 1])
        in_specs += [const((D_MODEL, KV_DIM)), const((KV_DIM, D_MODEL)), const((KV_DIM, D_MODEL))]
        in_specs += [pl.BlockSpec((tile, LANES), lambda i: (p_tile(i) % tiles_per_seq, 0))] * 3
        in_specs += [const((ROT_DIM // 2, tile))] * 2
        args += [kv["norm_g"]] + [kv["mods"]] * 4 + [kv["w_k"], kv["w_k_t"], kv["w_v_t"]]
        args += [*kv["prompt_tables"], *kv["decode_tables_t"]]
        out_specs += [p_rows(KV_DIM), p_cols(KV_DIM), s_cols(KV_DIM), s_cols(KV_DIM)]
        out_shape += [jax.ShapeDtypeStruct((prompt.rows, KV_DIM), F32),
                      jax.ShapeDtypeStruct((KV_DIM, prompt.rows), F32),
                      jax.ShapeDtypeStruct((KV_DIM, decode.rows), F32),
                      jax.ShapeDtypeStruct((KV_DIM, decode.rows), F32)]
    if final:
        in_specs.append(const((1, D_MODEL)))
        args.append(final_g)
    scratch = [
        pltpu.VMEM((D_MODEL, D_FF), BF16), pltpu.VMEM((D_MODEL, D_FF), BF16),
        pltpu.VMEM((D_FF, D_MODEL), BF16),
        pltpu.VMEM((2, 2, D_MODEL, FF_STAGE), F32),
        pltpu.VMEM((2, FF_STAGE, D_MODEL), F32),
        pltpu.SemaphoreType.DMA((2, 3)),
    ]
    return pl.pallas_call(
        functools.partial(_ffn_kernel, n_prompt_tiles=n_p, tiles_per_seq=tiles_per_seq,
                          dec_seq=decode.per_seq_rows, layer=layer, slot=slot,
                          with_kv=with_kv, final=final),
        grid=(n_p + n_s,),
        in_specs=in_specs,
        out_specs=out_specs,
        out_shape=out_shape,
        scratch_shapes=scratch,
        compiler_params=pltpu.CompilerParams(
            dimension_semantics=("arbitrary",),
            vmem_limit_bytes=VMEM_LIMIT_BYTES),
        name=f"ffn_l{layer}_s{sub}",
    )(*args)


def _conv_kernel(*refs, per_seq_rows):
    it = iter(refs)
    x_ref, sh_ref, sc_ref, gt_ref, ng_ref, win_ref, cw_ref, wout_ref = (next(it) for _ in range(8))
    if per_seq_rows is not None:
        st0_ref, st1_ref = next(it), next(it)
    o_ref, u_ref = next(it), next(it)
    if per_seq_rows is None:
        carry_ref = next(it)

    tile = x_ref.shape[0]
    x = x_ref[...]
    h = ((_unit_rms(x) * ng_ref[...]) * (1.0 + _mod_rows(sc_ref, per_seq_rows))
         + _mod_rows(sh_ref, per_seq_rows)).astype(BF16)
    bcv = _dot(h, win_ref[...])
    b_g = bcv[:, :D_MODEL]
    u = bcv[:, D_MODEL:2 * D_MODEL] * bcv[:, 2 * D_MODEL:]

    row = lax.broadcasted_iota(jnp.int32, (tile, 1), 0)
    if per_seq_rows is None:
        @pl.when(pl.program_id(1) == 0)
        def _():
            carry_ref[...] = jnp.zeros_like(carry_ref)
        prev2 = carry_ref[SUBLANES - 2:SUBLANES - 1, :]
        prev1 = carry_ref[SUBLANES - 1:SUBLANES, :]
        t = row
    else:
        prev2 = _repeat_rows(st0_ref, per_seq_rows)
        prev1 = _repeat_rows(st1_ref, per_seq_rows)
        t = row & (per_seq_rows - 1)
    u1 = jnp.where(t == 0, prev1, pltpu.roll(u, 1, axis=0))
    u2 = jnp.where(t == 0, prev2, jnp.where(t == 1, prev1, pltpu.roll(u, 2, axis=0)))
    conv = u2 * cw_ref[0:1, :] + u1 * cw_ref[1:2, :] + u * cw_ref[2:3, :]
    y = _dot((b_g * conv).astype(BF16), wout_ref[...])
    o_ref[...] = x + _mod_rows(gt_ref, per_seq_rows) * y

    if per_seq_rows is None:
        tail = u[tile - SUBLANES:, :]
        carry_ref[...] = tail
        u_ref[...] = tail
    else:
        u_ref[...] = u


def _conv_mixer(x, group, mods, norm_g, w_in, conv_w, w_out, state=None):
    prompt = group.per_seq_rows is None
    in_specs = [
        group.row_spec(D_MODEL),
        _mod_spec(0, 3, group), _mod_spec(0, 4, group), _mod_spec(0, 5, group),
        _resident((None, 1, D_MODEL), lambda b, t: (1, 0, 0)),
        _resident((D_MODEL, 3 * D_MODEL), lambda b, t: (0, 0)),
        _resident((CONV_WIDTH, D_MODEL), lambda b, t: (0, 0)),
        _resident((D_MODEL, D_MODEL), lambda b, t: (0, 0)),
    ]
    args = [x, mods, mods, mods, norm_g, w_in, conv_w, w_out]
    scratch = []
    if prompt:
        u_spec = pl.BlockSpec((SUBLANES, D_MODEL), lambda b, t: (b, 0))
        u_shape = jax.ShapeDtypeStruct((group.grid[0] * SUBLANES, D_MODEL), F32)
        scratch.append(pltpu.VMEM((SUBLANES, D_MODEL), F32))
    else:
        n_seq = group.tile // group.per_seq_rows
        st_spec = pl.BlockSpec((n_seq, D_MODEL), lambda b, t: (t, 0))
        in_specs += [st_spec, st_spec]
        args += [state[0], state[1]]
        u_spec = group.row_spec(D_MODEL)
        u_shape = jax.ShapeDtypeStruct((group.rows, D_MODEL), F32)
    return pl.pallas_call(
        functools.partial(_conv_kernel, per_seq_rows=group.per_seq_rows),
        grid=group.grid,
        in_specs=in_specs,
        out_specs=[group.row_spec(D_MODEL), u_spec],
        out_shape=[jax.ShapeDtypeStruct((group.rows, D_MODEL), F32), u_shape],
        scratch_shapes=scratch,
        compiler_params=pltpu.CompilerParams(
            dimension_semantics=("arbitrary", "arbitrary"),
            vmem_limit_bytes=VMEM_LIMIT_BYTES),
        name="conv_mixer",
    )(*args)


def _sink_column(sink_ref, kv_head, q_rows):
    grp = lax.broadcasted_iota(jnp.int32, (GROUP * q_rows, 1), 0) >> _log2(q_rows)
    col = jnp.full((GROUP * q_rows, 1), sink_ref[kv_head * GROUP + GROUP - 1], F32)
    for g in range(GROUP - 2, -1, -1):
        col = jnp.where(grp == g, sink_ref[kv_head * GROUP + g], col)
    return col


def _stack_heads(q_ref, rows, kv_head):
    return jnp.concatenate(
        [q_ref[rows, (kv_head * GROUP + g) * HEAD_DIM:(kv_head * GROUP + g + 1) * HEAD_DIM]
         for g in range(GROUP)], axis=0)


def _softmax_parts(scores, sink):
    m = sink
    for s in scores:
        m = jnp.maximum(m, jnp.max(s, axis=-1, keepdims=True))
    ps = [jnp.exp(s - m) for s in scores]
    denom = jnp.exp(sink - m)
    for p in ps:
        denom = denom + jnp.sum(p, axis=-1, keepdims=True)
    return ps, 1.0 / denom


def _project_q(x_ref, sh_ref, sc_ref, ng_ref, wq_ref, tables, per_seq_rows):
    h = ((_unit_rms(x_ref[...]) * ng_ref[...]) * (1.0 + _mod_rows(sc_ref, per_seq_rows))
         + _mod_rows(sh_ref, per_seq_rows)).astype(BF16)
    q = _rope(_dot(h, wq_ref[...]), *tables)
    return (q * (HEAD_DIM ** -0.5)).astype(BF16)


def _band_caps():
    shape = (2 * WINDOW, GROUP * WINDOW)
    kj = lax.broadcasted_iota(jnp.int32, shape, 0)
    qi = lax.broadcasted_iota(jnp.int32, shape, 1) & (WINDOW - 1)
    in_band = (kj > qi) & (kj <= qi + WINDOW)
    return (jnp.where(in_band, FMAX, NEG), jnp.where(in_band & (kj >= WINDOW), FMAX, NEG))


def _attn_prompt_kernel(x_ref, sh_ref, sc_ref, gt_ref, ng_ref, wq_ref, wo_ref,
                        kp_ref, kc_ref, vp_ref, vc_ref, cos_ref, sin_ref, sink_ref,
                        o_ref, q_scr, k_scr, v_scr, a_scr, cap_scr, h_scr, at_scr):
    tile = x_ref.shape[0]

    @pl.when((pl.program_id(0) == 0) & (pl.program_id(1) == 0))
    def _():
        general, first = _band_caps()
        cap_scr[0] = general
        cap_scr[1] = first

    k_scr[:WINDOW, :] = kp_ref[...].astype(BF16)
    k_scr[WINDOW:, :] = kc_ref[...].astype(BF16)
    v_scr[:, :WINDOW] = vp_ref[...].astype(BF16)
    v_scr[:, WINDOW:] = vc_ref[...].astype(BF16)
    first_tile = pl.program_id(1) == 0
    norm_gain = ng_ref[...]
    scale1 = 1.0 + _mod_rows(sc_ref, None)
    shift = _mod_rows(sh_ref, None)
    gate = _mod_rows(gt_ref, None)

    group_dims = GROUP * HEAD_DIM

    def chunk_rows(c):
        return slice(c * ATTN_CHUNK, (c + 1) * ATTN_CHUNK)

    def normalize(c):
        rows = chunk_rows(c)
        h_scr[rows, :] = ((_unit_rms(x_ref[rows, :]) * norm_gain) * scale1 + shift).astype(BF16)

    def project_piece(c, kvh):
        rows = chunk_rows(c)
        dims = slice(kvh * group_dims, (kvh + 1) * group_dims)
        qt = _rope_t(_dot(h_scr[rows, :], wq_ref[:, dims]).T, cos_ref[:, rows], sin_ref[:, rows])
        q_scr[dims, rows] = (qt * (HEAD_DIM ** -0.5 * LOG2E)).astype(BF16)

    def attend(blk, kvh):
        cols = slice(blk * WINDOW, (blk + 1) * WINDOW)
        keys = slice(blk * WINDOW, (blk + 2) * WINDOW)
        cap = cap_scr[jnp.where(first_tile, 1, 0)] if blk == 0 else cap_scr[0]
        dims = slice(kvh * HEAD_DIM, (kvh + 1) * HEAD_DIM)
        heads = [slice((kvh * GROUP + g) * HEAD_DIM, (kvh * GROUP + g + 1) * HEAD_DIM)
                 for g in range(GROUP)]
        qg = jnp.concatenate([q_scr[hd, cols] for hd in heads], axis=1)
        s = jnp.minimum(_dot(k_scr[keys, dims], qg), cap)
        sink = LOG2E * jnp.concatenate(
            [jnp.full((1, WINDOW), sink_ref[kvh * GROUP + g], F32) for g in range(GROUP)], axis=1)
        m = jnp.maximum(jnp.max(s, axis=0, keepdims=True), sink)
        p = jnp.exp2(s - m)
        denom = jnp.sum(p, axis=0, keepdims=True) + jnp.exp2(sink - m)
        o = _dot(v_scr[dims, keys], p.astype(BF16)) * (1.0 / denom)
        for g, hd in enumerate(heads):
            a_scr[hd, cols] = o[:, g * WINDOW:(g + 1) * WINDOW]

    def transpose_piece(c, kvh):
        rows = chunk_rows(c)
        dims = slice(kvh * group_dims, (kvh + 1) * group_dims)
        at_scr[rows, dims] = a_scr[dims, rows].T.astype(BF16)

    def emit_piece(c, n):
        rows = chunk_rows(c)
        cols = slice(n * group_dims, (n + 1) * group_dims)
        y = _dot(at_scr[rows, :], wo_ref[:, cols])
        o_ref[rows, cols] = x_ref[rows, cols] + gate[:, cols] * y

    n_chunks = tile // ATTN_CHUNK
    blocks_per_chunk = ATTN_CHUNK // WINDOW
    n_pieces = D_MODEL // group_dims
    normalize(0)
    for kvh in range(N_KV_HEADS):
        project_piece(0, kvh)
    for c in range(n_chunks):
        if c + 1 < n_chunks:
            normalize(c + 1)
        step = 0
        for b in range(blocks_per_chunk):
            for kvh in range(N_KV_HEADS):
                attend(c * blocks_per_chunk + b, kvh)
                if b == blocks_per_chunk - 1:
                    transpose_piece(c, kvh)
                if step % 2 == 0 and c + 1 < n_chunks and step // 2 < N_KV_HEADS:
                    project_piece(c + 1, step // 2)
                if step % 2 == 1 and c > 0 and step // 2 < n_pieces:
                    emit_piece(c - 1, step // 2)
                step += 1
    for n in range(n_pieces):
        emit_piece(n_chunks - 1, n)


def _attn_prompt(x, group, mods, norm_g, w_q, w_o, k, v_t, tables_t, sinks):
    tile = group.tile
    nt = group.grid[1]
    blocks_per_tile = tile // WINDOW

    def prev_block(b, t):
        return jnp.maximum((b * nt + t) * blocks_per_tile - 1, 0)

    in_specs = [
        group.row_spec(D_MODEL),
        _mod_spec(1, 3, group), _mod_spec(1, 4, group), _mod_spec(1, 5, group),
        _resident((None, 1, D_MODEL), lambda b, t: (4, 0, 0)),
        _resident((D_MODEL, D_MODEL), lambda b, t: (0, 0)),
        _resident((D_MODEL, D_MODEL), lambda b, t: (0, 0)),
        pl.BlockSpec((WINDOW, KV_DIM), lambda b, t: (prev_block(b, t), 0)), group.row_spec(KV_DIM),
        pl.BlockSpec((KV_DIM, WINDOW), lambda b, t: (0, prev_block(b, t))), group.col_spec(KV_DIM),
        group.table_t_spec(), group.table_t_spec(),
        pl.BlockSpec(memory_space=pltpu.SMEM),
    ]
    return pl.pallas_call(
        _attn_prompt_kernel,
        grid=group.grid,
        in_specs=in_specs,
        out_specs=group.row_spec(D_MODEL),
        out_shape=jax.ShapeDtypeStruct((group.rows, D_MODEL), F32),
        scratch_shapes=[
            pltpu.VMEM((D_MODEL, tile), BF16),
            pltpu.VMEM((tile + WINDOW, KV_DIM), BF16),
            pltpu.VMEM((KV_DIM, tile + WINDOW), BF16),
            pltpu.VMEM((D_MODEL, tile), F32),
            pltpu.VMEM((2, 2 * WINDOW, GROUP * WINDOW), F32),
            pltpu.VMEM((tile, D_MODEL), BF16),
            pltpu.VMEM((tile, D_MODEL), BF16),
        ],
        compiler_params=pltpu.CompilerParams(
            dimension_semantics=("arbitrary", "arbitrary"),
            vmem_limit_bytes=VMEM_LIMIT_BYTES),
        name="attn_prompt",
    )(x, mods, mods, mods, norm_g, w_q, w_o, k, k, v_t, v_t, *tables_t, sinks)


def _attn_sample_kernel(x_ref, sh_ref, sc_ref, gt_ref, ng_ref, wq_ref, wo_ref,
                        kc_ref, vc_ref, kn_ref, vn_ref, cos_ref, sup_ref, sdn_ref, sink_ref,
                        o_ref, kout_ref, vout_ref, q_scr, *, dec_seq):
    rows = x_ref.shape[0]
    n_seq = rows // dec_seq
    n_cache = n_seq * WINDOW
    q_scr[...] = _project_q(x_ref, sh_ref, sc_ref, ng_ref, wq_ref,
                            (cos_ref[...], sup_ref[...], sdn_ref[...]), dec_seq)

    pack = DEC_ATTN_PACK
    pack_rows = pack * dec_seq

    def windows(ref, first_seq, kvh):
        return jnp.concatenate(
            [ref[(s * N_KV_HEADS + kvh) * HEAD_DIM:(s * N_KV_HEADS + kvh + 1) * HEAD_DIM, :].astype(BF16)
             for s in range(first_seq, first_seq + pack)], axis=1)

    q_row = lax.broadcasted_iota(jnp.int32, (GROUP * pack_rows, 1), 0) & (pack_rows - 1)
    q_seq, q_i = q_row >> _log2(dec_seq), q_row & (dec_seq - 1)
    c_col = lax.broadcasted_iota(jnp.int32, (1, pack * WINDOW), 1)
    valid_c = (q_seq == (c_col >> _log2(WINDOW))) & ((c_col & (WINDOW - 1)) > q_i)
    n_col = lax.broadcasted_iota(jnp.int32, (1, rows), 1)
    n_seq_col, n_visible = n_col >> _log2(dec_seq), (n_col & (dec_seq - 1)) <= q_i
    kn = [kn_ref[kvh * HEAD_DIM:(kvh + 1) * HEAD_DIM, :].astype(BF16) for kvh in range(N_KV_HEADS)]
    vn = [vn_ref[kvh * HEAD_DIM:(kvh + 1) * HEAD_DIM, :].astype(BF16) for kvh in range(N_KV_HEADS)]

    keep = lax.broadcasted_iota(jnp.int32, (1, WINDOW), 1) < WINDOW - dec_seq

    def update_windows(first_seq):
        for src_ref, new_ref, dst_ref in ((kc_ref, kn_ref, kout_ref), (vc_ref, vn_ref, vout_ref)):
            new = new_ref[...]
            for s in range(first_seq, first_seq + pack):
                blk = slice(s * KV_DIM, (s + 1) * KV_DIM)
                old = pltpu.roll(src_ref[blk, :], WINDOW - dec_seq, axis=1)
                app = pltpu.roll(new, (WINDOW - dec_seq - s * dec_seq) % rows, axis=1)
                dst_ref[blk, :] = jnp.where(keep, old, app)

    packs = []
    for first_seq in range(0, n_seq, pack):
        update_windows(first_seq)
        q_rows = slice(first_seq * dec_seq, (first_seq + pack) * dec_seq)
        valid_n = ((q_seq + first_seq) == n_seq_col) & n_visible
        heads = []
        for kvh in range(N_KV_HEADS):
            qg = _stack_heads(q_scr, q_rows, kvh)
            s_c = jnp.where(valid_c, _dot(qg, windows(kc_ref, first_seq, kvh)), NEG)
            s_n = jnp.where(valid_n, _dot(qg, kn[kvh]), NEG)
            (p_c, p_n), inv = _softmax_parts([s_c, s_n], _sink_column(sink_ref, kvh, pack_rows))
            o = (_dot_nt(p_c.astype(BF16), windows(vc_ref, first_seq, kvh))
                 + _dot_nt(p_n.astype(BF16), vn[kvh])) * inv
            heads += [o[g * pack_rows:(g + 1) * pack_rows, :] for g in range(GROUP)]
        packs.append(jnp.concatenate(heads, axis=1))
    attn = jnp.concatenate(packs, axis=0).astype(BF16)
    o_ref[...] = x_ref[...] + _mod_rows(gt_ref, dec_seq) * _dot(attn, wo_ref[...])


def _attn_sample(x, n_seq_total, dec_seq, mods, norm_g, w_q, w_o, k_cache, v_cache,
                 k_new, v_new, tables, sinks):
    n_seq = SAMPLE_ATTN_SEQS
    rows = n_seq * dec_seq
    assert rows == WINDOW == LANES
    grid = (n_seq_total // n_seq,)
    row_spec = lambda width: pl.BlockSpec((rows, width), lambda t: (t, 0))
    mod_spec = lambda col: pl.BlockSpec((None, n_seq, D_MODEL), lambda t: (1, t, col))
    cache_spec = pl.BlockSpec((n_seq * KV_DIM, WINDOW), lambda t: (t, 0))
    new_spec = pl.BlockSpec((KV_DIM, rows), lambda t: (0, t))
    table_spec = pl.BlockSpec((rows, LANES), lambda t: (0, 0))
    in_specs = [
        row_spec(D_MODEL), mod_spec(3), mod_spec(4), mod_spec(5),
        _resident((None, 1, D_MODEL), lambda t: (4, 0, 0)),
        _resident((D_MODEL, D_MODEL), lambda t: (0, 0)),
        _resident((D_MODEL, D_MODEL), lambda t: (0, 0)),
        cache_spec, cache_spec, new_spec, new_spec,
        table_spec, table_spec, table_spec,
        pl.BlockSpec(memory_space=pltpu.SMEM),
    ]
    cache_shape = jax.ShapeDtypeStruct((n_seq_total * KV_DIM, WINDOW), F32)
    return pl.pallas_call(
        functools.partial(_attn_sample_kernel, dec_seq=dec_seq),
        grid=grid,
        in_specs=in_specs,
        out_specs=[row_spec(D_MODEL), cache_spec, cache_spec],
        out_shape=[jax.ShapeDtypeStruct((n_seq_total * dec_seq, D_MODEL), F32),
                   cache_shape, cache_shape],
        scratch_shapes=[pltpu.VMEM((rows, D_MODEL), BF16)],
        compiler_params=pltpu.CompilerParams(
            dimension_semantics=("arbitrary",),
            vmem_limit_bytes=VMEM_LIMIT_BYTES),
        name="attn_sample",
    )(x, mods, mods, mods, norm_g, w_q, w_o, k_cache, v_cache, k_new, v_new, *tables, sinks)


def _rope_angles(pos):
    inv_freq = ROPE_THETA ** (-jnp.arange(0, ROT_DIM, 2, dtype=F32) / ROT_DIM)
    ang = pos.astype(F32)[:, None] * inv_freq[None, :]
    return jnp.cos(ang), jnp.sin(ang)


def _rope_tables(pos):
    half = ROT_DIM // 2
    cos, sin = _rope_angles(pos)
    n = pos.shape[0]
    rest = HEAD_DIM - ROT_DIM
    zeros = jnp.zeros((n, half), F32)
    cos_t = jnp.concatenate([cos, cos, jnp.ones((n, rest), F32)], axis=1)
    sin_up = jnp.concatenate([-sin, zeros, jnp.zeros((n, rest), F32)], axis=1)
    sin_dn = jnp.concatenate([zeros, sin, jnp.zeros((n, rest), F32)], axis=1)
    reps = LANES // HEAD_DIM
    return tuple(jnp.tile(t, (1, reps)) for t in (cos_t, sin_up, sin_dn))


def kernel(x_prompt, x_sample, state_conv, cache_k_win, cache_v_win, c_prompt, c_sample, norm_g, w_ada, b_ada, w_ffn_gate, w_ffn_up, w_ffn_down, conv_w_in, conv_w, conv_w_out, kv_norm_g, w_ada_kv, b_ada_kv, w_k, w_v, attn_w_q, attn_sinks, attn_w_o, final_norm_g):
    batch, seq, d = x_prompt.shape
    dec_batch, dec_seq, _ = x_sample.shape
    w_buf = cache_k_win.shape[1]
    past_len = 16384
    assert d == D_MODEL and w_buf == WINDOW and dec_seq == SUBLANES
    assert w_ada.shape[0] == 2 and conv_w_in.shape[0] == 1 and attn_w_q.shape[0] == 1

    pad = (-(dec_batch + batch)) % SUBLANES
    c_all = jnp.concatenate([c_sample, c_prompt, jnp.zeros((pad, d), F32)], axis=0)
    mods = _ada(c_all, w_ada, b_ada)
    mods_kv = _ada(c_all, w_ada_kv[None], b_ada_kv[None])
    prompt_mod_block = dec_batch // SUBLANES

    make_groups = lambda tile: {
        "prompt": _Group(batch, seq, None, prompt_mod_block, tile),
        "sample": _Group(dec_batch, dec_seq, dec_seq, prompt_mod_block, tile),
    }
    ffn_groups, mixer_groups = make_groups(FFN_TILE), make_groups(MIXER_TILE)
    xp, xs = x_prompt.reshape(batch * seq, d), x_sample.reshape(dec_batch * dec_seq, d)

    norm_g3 = norm_g.reshape(-1, 1, d)
    w_in, w_out = conv_w_in[0].astype(BF16), conv_w_out[0].astype(BF16)
    w_q, w_o = attn_w_q[0].astype(BF16), attn_w_o[0].astype(BF16)
    sinks = attn_sinks[0]

    pos_dec = past_len + jnp.arange(dec_seq, dtype=jnp.int32)
    pos_prompt = jnp.arange(seq, dtype=jnp.int32)
    tab_prompt = _rope_tables(pos_prompt)
    tab_dec = tuple(jnp.tile(t, (SAMPLE_ATTN_SEQS, 1)) for t in _rope_tables(pos_dec))
    tab_prompt_t = tuple(t.T for t in _rope_angles(pos_prompt))
    tab_dec_t = tuple(jnp.tile(t.T, (1, FFN_TILE // dec_seq)) for t in _rope_angles(pos_dec))
    kv_args = dict(norm_g=kv_norm_g.reshape(1, d), mods=mods_kv, w_k=w_k.astype(BF16),
                   w_k_t=w_k.T.astype(BF16), w_v_t=w_v.T.astype(BF16),
                   prompt_tables=tab_prompt, decode_tables_t=tab_dec_t)
    to_rows = lambda c: jnp.transpose(c, (0, 2, 3, 1)).reshape(dec_batch * KV_DIM, w_buf)
    from_rows = lambda c: jnp.transpose(c.reshape(dec_batch, N_KV_HEADS, HEAD_DIM, w_buf), (0, 3, 1, 2))

    ffn = functools.partial(_ffn, prompt=ffn_groups["prompt"], decode=ffn_groups["sample"], mods=mods,
                            norm_g=norm_g3, wg=w_ffn_gate, wu=w_ffn_up, wd=w_ffn_down)
    xp, xs = ffn(xp, xs, layer=0, sub=0, slot=0)
    xp, u_p = _conv_mixer(xp, mixer_groups["prompt"], mods, norm_g3, w_in, conv_w[0], w_out)
    xs, u_s = _conv_mixer(xs, mixer_groups["sample"], mods, norm_g3, w_in, conv_w[0], w_out,
                          (state_conv[0, :, 0, :], state_conv[0, :, 1, :]))
    xp, xs = ffn(xp, xs, layer=0, sub=2, slot=1)
    xp, xs, k_p, vt_p, kt_s, vt_s = ffn(xp, xs, layer=1, sub=0, slot=0, kv=kv_args)
    xp = _attn_prompt(xp, make_groups(ATTN_TILE)["prompt"], mods, norm_g3, w_q, w_o, k_p, vt_p, tab_prompt_t, sinks)
    xs, k_win, v_win = _attn_sample(xs, dec_batch, dec_seq, mods, norm_g3, w_q, w_o,
                                    to_rows(cache_k_win), to_rows(cache_v_win), kt_s, vt_s, tab_dec, sinks)
    y_p, y_s = ffn(xp, xs, layer=1, sub=2, slot=1, final_g=final_norm_g.reshape(1, d))

    k_state_p = k_p.reshape(batch, seq, KV_DIM)[:, seq - WINDOW:].reshape(
        batch, WINDOW, N_KV_HEADS, HEAD_DIM)
    v_state_p = jnp.stack([vt_p[:, (b + 1) * seq - WINDOW:(b + 1) * seq] for b in range(batch)])
    v_state_p = jnp.transpose(v_state_p.reshape(batch, N_KV_HEADS, HEAD_DIM, WINDOW), (0, 3, 1, 2))
    k_state_s, v_state_s = from_rows(k_win), from_rows(v_win)
    tail = CONV_WIDTH - 1
    conv_p = u_p.reshape(batch, SUBLANES, d)[:, SUBLANES - tail:][None]
    conv_s = u_s.reshape(dec_batch, dec_seq, d)[:, dec_seq - tail:][None]
    return (y_p.reshape(batch, seq, d), y_s.reshape(dec_batch, dec_seq, d), conv_p, conv_s,
            k_state_p, v_state_p, k_state_s, v_state_s)
```

```python
import functools

import jax
import jax.numpy as jnp
from jax import lax
from jax.experimental import pallas as pl
from jax.experimental.pallas import tpu as pltpu

F32 = jnp.float32
BF16 = jnp.bfloat16

D_MODEL = 1024
D_FF = 2816
HEAD_DIM = 64
N_HEADS = 16
N_KV_HEADS = 4
GROUP = N_HEADS // N_KV_HEADS
KV_DIM = N_KV_HEADS * HEAD_DIM
WINDOW = 128
ROT_DIM = 16
ROPE_THETA = 500000.0
CONV_WIDTH = 3
N_MOD = 9
EPS = 1e-6
NEG = -1e30
FMAX = float(jnp.finfo(jnp.float32).max)

SUBLANES = 8
LANES = 128
VMEM_LIMIT_BYTES = 60 * 1024 * 1024

ADA_COLS = 2304
FFN_TILE = 512
FFN_SUBTILES = 2
FF_STAGE = 256
MIXER_TILE = 512
ATTN_TILE = 1024
ATTN_CHUNK = 256
LOG2E = 1.4426950408889634
SAMPLE_ATTN_SEQS = 16
DEC_ATTN_PACK = 4
FF_CHUNKS = ((0, 1024), (1024, 2048), (2048, D_FF))


def _dot(a, b):
    return jnp.dot(a, b, preferred_element_type=F32)


def _dot_nt(a, b):
    return lax.dot_general(a, b, (((1,), (1,)), ((), ())), preferred_element_type=F32)


def _log2(n):
    assert n & (n - 1) == 0, n
    return n.bit_length() - 1


def _silu(x):
    return x * jax.nn.sigmoid(x)


def _unit_rms(x):
    return x * lax.rsqrt(jnp.mean(x * x, axis=-1, keepdims=True) + EPS)


def _repeat_rows(ref, reps):
    n = ref.shape[1]
    return jnp.concatenate(
        [jnp.broadcast_to(ref[i:i + 1, :], (reps, n)) for i in range(ref.shape[0])], axis=0)


def _mod_rows(ref, per_seq_rows):
    if per_seq_rows is None:
        return ref[pl.ds(pl.program_id(0), 1), :]
    return _repeat_rows(ref, per_seq_rows)


def _rope(x, cos_t, sin_up_t, sin_dn_t):
    n = x.shape[1]
    reps = n // LANES
    half = ROT_DIM // 2
    cos = jnp.concatenate([cos_t] * reps, axis=1)
    sin_up = jnp.concatenate([sin_up_t] * reps, axis=1)
    sin_dn = jnp.concatenate([sin_dn_t] * reps, axis=1)
    x_up = pltpu.roll(x, n - half, axis=1)
    x_dn = pltpu.roll(x, half, axis=1)
    return x * cos + x_up * sin_up + x_dn * sin_dn


def _ada_kernel(c_ref, w_ref, b_ref, o_ref):
    a = _silu(c_ref[...]).astype(BF16)
    o_ref[...] = _dot(a, w_ref[...].astype(BF16)) + b_ref[...]


def _ada(c_all, w, b):
    n_layers, _, n = w.shape
    m = c_all.shape[0]
    tn = ADA_COLS if n % ADA_COLS == 0 else D_MODEL
    return pl.pallas_call(
        _ada_kernel,
        grid=(n_layers, n // tn),
        in_specs=[
            pl.BlockSpec((m, D_MODEL), lambda l, j: (0, 0)),
            pl.BlockSpec((None, D_MODEL, tn), lambda l, j: (l, 0, j)),
            pl.BlockSpec((None, 1, tn), lambda l, j: (l, 0, j)),
        ],
        out_specs=pl.BlockSpec((None, m, tn), lambda l, j: (l, 0, j)),
        out_shape=jax.ShapeDtypeStruct((n_layers, m, n), F32),
        compiler_params=pltpu.CompilerParams(
            dimension_semantics=("arbitrary", "arbitrary"),
            vmem_limit_bytes=VMEM_LIMIT_BYTES),
        name="adaln_tables",
    )(c_all, w, b.reshape(n_layers, 1, n))


def _rope_t(xt, cos_t, sin_t):
    half = ROT_DIM // 2
    out = []
    for h in range(xt.shape[0] // HEAD_DIM):
        base = h * HEAD_DIM
        x1 = xt[base:base + half, :]
        x2 = xt[base + half:base + ROT_DIM, :]
        out += [x1 * cos_t - x2 * sin_t, x2 * cos_t + x1 * sin_t,
                xt[base + ROT_DIM:base + HEAD_DIM, :]]
    return jnp.concatenate(out, axis=0)


def _ffn_kernel(*refs, n_prompt_tiles, tiles_per_seq, dec_seq, layer, slot, with_kv, final):
    it = iter(refs)
    take = lambda n: [next(it) for _ in range(n)]
    xp_ref, xs_ref = take(2)
    p_mods, s_mods = take(3), take(3)
    ng_ref, wg_hbm, wu_hbm, wd_hbm = take(4)
    if with_kv:
        (kvg_ref,), p_kvmods, s_kvmods = take(1), take(2), take(2)
        (wkvt_ref,) = take(1)
        p_tabs, s_tabs = take(2), take(2)
    else:
        p_kvmods = s_kvmods = None
    if final:
        (fg_ref,) = take(1)
    op_ref, os_ref = take(2)
    if with_kv:
        kvp_ref, kvs_ref = take(2)
    wg_s, wu_s, wd_s, stage_in, stage_out, sem = take(6)

    i = pl.program_id(0)
    prompt_rows = lambda ref: ref[pl.ds(i // tiles_per_seq, 1), :]
    decode_rows = lambda ref: _repeat_rows(ref, dec_seq)

    def weight_copies(c):
        cols = pl.ds(c * FF_STAGE, FF_STAGE)
        buf = c % 2
        return (
            pltpu.make_async_copy(wg_hbm.at[layer, slot, :, cols], stage_in.at[buf, 0], sem.at[buf, 0]),
            pltpu.make_async_copy(wu_hbm.at[layer, slot, :, cols], stage_in.at[buf, 1], sem.at[buf, 1]),
            pltpu.make_async_copy(wd_hbm.at[layer, slot, cols, :], stage_out.at[buf], sem.at[buf, 2]),
        )

    def partial_ffn(hb, lo, hi):
        g = _dot(hb, wg_s[:, lo:hi])
        u = _dot(hb, wu_s[:, lo:hi])
        return _dot((_silu(g) * u).astype(BF16), wd_s[lo:hi, :])

    def split_body(x_ref, o_ref, rows_of, mods, kvmods):
        sub = x_ref.shape[0] // FFN_SUBTILES
        groups = [slice(k * sub, (k + 1) * sub) for k in range(FFN_SUBTILES)]
        rows_in = lambda v, r: v if v.shape[0] == 1 else v[r]
        shift, scale, gate = (rows_of(m) for m in mods)
        xs_, xns, hbs = [], [], []
        ys = [None] * FFN_SUBTILES
        for c, (lo, hi) in enumerate(FF_CHUNKS):
            for k, r in enumerate(groups):
                if c == 0:
                    x = x_ref[r, :]
                    xn = _unit_rms(x)
                    xs_.append(x)
                    xns.append(xn)
                    hbs.append(((xn * ng_ref[...]) * (1.0 + rows_in(scale, r))
                                + rows_in(shift, r)).astype(BF16))
                part = partial_ffn(hbs[k], lo, hi)
                ys[k] = part if ys[k] is None else ys[k] + part
                if c == len(FF_CHUNKS) - 1:
                    x_new = xs_[k] + (0.5 * rows_in(gate, r)) * ys[k]
                    o_ref[r, :] = _unit_rms(x_new) * fg_ref[...] if final else x_new
        if with_kv:
            kv_shift, kv_scale = rows_of(kvmods[0]), rows_of(kvmods[1])
            return jnp.concatenate(
                [((xns[k] * kvg_ref[...]) * (1.0 + rows_in(kv_scale, r)) + rows_in(kv_shift, r)).astype(BF16)
                 for k, r in enumerate(groups)], axis=0)

    def body(x_ref, o_ref, rows_of, mods, kvmods, stage_weights):
        if not stage_weights:
            return split_body(x_ref, o_ref, rows_of, mods, kvmods)
        x = x_ref[...]
        xn = _unit_rms(x)
        hb = ((xn * ng_ref[...]) * (1.0 + rows_of(mods[1])) + rows_of(mods[0])).astype(BF16)
        y = None
        n_slices = D_FF // FF_STAGE
        for c in range(min(2, n_slices)):
            for cp in weight_copies(c):
                cp.start()
        for c in range(n_slices):
            lo, hi = c * FF_STAGE, (c + 1) * FF_STAGE
            for cp in weight_copies(c):
                cp.wait()
            wg_s[:, lo:hi] = stage_in[c % 2, 0].astype(BF16)
            wu_s[:, lo:hi] = stage_in[c % 2, 1].astype(BF16)
            wd_s[lo:hi, :] = stage_out[c % 2].astype(BF16)
            if c + 2 < n_slices:
                for cp in weight_copies(c + 2):
                    cp.start()
            part = partial_ffn(hb, lo, hi)
            y = part if y is None else y + part
        x_new = x + (0.5 * rows_of(mods[2])) * y
        o_ref[...] = _unit_rms(x_new) * fg_ref[...] if final else x_new
        if with_kv:
            return ((xn * kvg_ref[...]) * (1.0 + rows_of(kvmods[1])) + rows_of(kvmods[0])).astype(BF16)

    def emit_kv(hk, tabs, kv_ref):
        kv_t = _dot_nt(wkvt_ref[...], hk)
        kv_ref[:KV_DIM, :] = _rope_t(kv_t[:KV_DIM, :], tabs[0][...], tabs[1][...])
        kv_ref[KV_DIM:, :] = kv_t[KV_DIM:, :]

    def prompt_step(stage_weights):
        hk = body(xp_ref, op_ref, prompt_rows, p_mods, p_kvmods, stage_weights)
        if with_kv:
            emit_kv(hk, p_tabs, kvp_ref)

    @pl.when(i == 0)
    def _():
        prompt_step(True)

    @pl.when((i > 0) & (i < n_prompt_tiles))
    def _():
        prompt_step(False)

    @pl.when(i >= n_prompt_tiles)
    def _():
        hk = body(xs_ref, os_ref, decode_rows, s_mods, s_kvmods, False)
        if with_kv:
            emit_kv(hk, s_tabs, kvs_ref)


def _resident(shape, index_map):
    return pl.BlockSpec(shape, index_map, pipeline_mode=pl.Buffered(1))


def _mod_spec(layer, col, group):
    if group.per_seq_rows is None:
        return pl.BlockSpec((None, SUBLANES, D_MODEL),
                            lambda b, t: (layer, group.prompt_mod_block, col))
    return pl.BlockSpec((None, group.tile // group.per_seq_rows, D_MODEL),
                        lambda b, t: (layer, t, col))


class _Group:
    def __init__(self, n_seq, seq_len, per_seq_rows, prompt_mod_block, tile):
        self.tile = tile
        self.rows = n_seq * seq_len
        self.per_seq_rows = per_seq_rows
        self.prompt_mod_block = prompt_mod_block
        if per_seq_rows is None:
            self.grid = (n_seq, seq_len // self.tile)
        else:
            self.grid = (1, self.rows // self.tile)
        self.tiles_per_seq = self.grid[1]

    def row_spec(self, width):
        nt = self.grid[1]
        return pl.BlockSpec((self.tile, width), lambda b, t: (b * nt + t, 0))

    def col_spec(self, height):
        nt = self.grid[1]
        return pl.BlockSpec((height, self.tile), lambda b, t: (0, b * nt + t))

    def table_t_spec(self):
        if self.per_seq_rows is None:
            return pl.BlockSpec((ROT_DIM // 2, self.tile), lambda b, t: (0, t))
        return pl.BlockSpec((ROT_DIM // 2, self.tile), lambda b, t: (0, 0))

    def table_spec(self):
        if self.per_seq_rows is None:
            return pl.BlockSpec((self.tile, LANES), lambda b, t: (t, 0))
        return pl.BlockSpec((self.tile, LANES), lambda b, t: (0, 0))


def _ffn(xp, xs, prompt, decode, mods, layer, sub, norm_g, wg, wu, wd, slot, kv=None, final_g=None):
    with_kv = kv is not None
    final = final_g is not None
    tile = prompt.tile
    assert decode.tile == tile
    n_p, n_s = prompt.rows // tile, decode.rows // tile
    tiles_per_seq = prompt.tiles_per_seq
    p_tile = lambda i: jnp.minimum(i, n_p - 1)
    s_tile = lambda i: jnp.maximum(i - n_p, 0)
    n_dec_seq = tile // decode.per_seq_rows

    def mod_specs(table_layer, cols):
        p = [pl.BlockSpec((None, SUBLANES, D_MODEL),
                          functools.partial(lambda i, c: (table_layer, prompt.prompt_mod_block, c), c=c))
             for c in cols]
        s = [pl.BlockSpec((None, n_dec_seq, D_MODEL),
                          functools.partial(lambda i, c: (table_layer, s_tile(i), c), c=c),
                          pipeline_mode=pl.Buffered(1))
             for c in cols]
        return p + s

    p_rows = lambda width: pl.BlockSpec((tile, width), lambda i: (p_tile(i), 0))
    s_rows = lambda width, **kw: pl.BlockSpec((tile, width), lambda i: (s_tile(i), 0), **kw)
    p_cols = lambda height: pl.BlockSpec((height, tile), lambda i: (0, p_tile(i)))
    s_cols = lambda height: pl.BlockSpec((height, tile), lambda i: (0, s_tile(i)))
    const = lambda shape: _resident(shape, lambda i: (0,) * len(shape))
    hbm = pl.BlockSpec(memory_space=pl.ANY)

    in_specs = [p_rows(D_MODEL), s_rows(D_MODEL, pipeline_mode=pl.Buffered(1))]
    in_specs += mod_specs(layer, [3 * sub + 0, 3 * sub + 1, 3 * sub + 2])
    in_specs += [_resident((None, 1, D_MODEL), lambda i: (layer * 3 + sub, 0, 0)), hbm, hbm, hbm]
    args = [xp, xs] + [mods] * 6 + [norm_g, wg, wu, wd]
    out_specs = [p_rows(D_MODEL), s_rows(D_MODEL)]
    out_shape = [jax.ShapeDtypeStruct((prompt.rows, D_MODEL), F32),
                 jax.ShapeDtypeStruct((decode.rows, D_MODEL), F32)]
    if with_kv:
        in_specs += [const((1, D_MODEL))] + mod_specs(0, [0, 1])
        in_specs += [const((2 * KV_DIM, D_MODEL))]
        in_specs += [pl.BlockSpec((ROT_DIM // 2, tile), lambda i: (0, p_tile(i) % tiles_per_seq))] * 2
        in_specs += [const((ROT_DIM // 2, tile))] * 2
        args += [kv["norm_g"]] + [kv["mods"]] * 4 + [kv["w_kv_t"]]
        args += [*kv["prompt_tables_t"], *kv["decode_tables_t"]]
        out_specs += [p_cols(2 * KV_DIM), s_cols(2 * KV_DIM)]
        out_shape += [jax.ShapeDtypeStruct((2 * KV_DIM, prompt.rows), F32),
                      jax.ShapeDtypeStruct((2 * KV_DIM, decode.rows), F32)]
    if final:
        in_specs.append(const((1, D_MODEL)))
        args.append(final_g)
    scratch = [
        pltpu.VMEM((D_MODEL, D_FF), BF16), pltpu.VMEM((D_MODEL, D_FF), BF16),
        pltpu.VMEM((D_FF, D_MODEL), BF16),
        pltpu.VMEM((2, 2, D_MODEL, FF_STAGE), F32),
        pltpu.VMEM((2, FF_STAGE, D_MODEL), F32),
        pltpu.SemaphoreType.DMA((2, 3)),
    ]
    return pl.pallas_call(
        functools.partial(_ffn_kernel, n_prompt_tiles=n_p, tiles_per_seq=tiles_per_seq,
                          dec_seq=decode.per_seq_rows, layer=layer, slot=slot,
                          with_kv=with_kv, final=final),
        grid=(n_p + n_s,),
        in_specs=in_specs,
        out_specs=out_specs,
        out_shape=out_shape,
        scratch_shapes=scratch,
        compiler_params=pltpu.CompilerParams(
            dimension_semantics=("arbitrary",),
            vmem_limit_bytes=VMEM_LIMIT_BYTES),
        name=f"ffn_l{layer}_s{sub}",
    )(*args)


def _conv_kernel(*refs, per_seq_rows):
    it = iter(refs)
    x_ref, sh_ref, sc_ref, gt_ref, ng_ref, win_ref, cw_ref, wout_ref = (next(it) for _ in range(8))
    if per_seq_rows is not None:
        st0_ref, st1_ref = next(it), next(it)
    o_ref, u_ref = next(it), next(it)
    if per_seq_rows is None:
        carry_ref = next(it)

    tile = x_ref.shape[0]
    x = x_ref[...]
    h = ((_unit_rms(x) * ng_ref[...]) * (1.0 + _mod_rows(sc_ref, per_seq_rows))
         + _mod_rows(sh_ref, per_seq_rows)).astype(BF16)
    bcv = _dot(h, win_ref[...])
    b_g = bcv[:, :D_MODEL]
    u = bcv[:, D_MODEL:2 * D_MODEL] * bcv[:, 2 * D_MODEL:]

    row = lax.broadcasted_iota(jnp.int32, (tile, 1), 0)
    if per_seq_rows is None:
        @pl.when(pl.program_id(1) == 0)
        def _():
            carry_ref[...] = jnp.zeros_like(carry_ref)
        prev2 = carry_ref[SUBLANES - 2:SUBLANES - 1, :]
        prev1 = carry_ref[SUBLANES - 1:SUBLANES, :]
        t = row
    else:
        prev2 = _repeat_rows(st0_ref, per_seq_rows)
        prev1 = _repeat_rows(st1_ref, per_seq_rows)
        t = row & (per_seq_rows - 1)
    u1 = jnp.where(t == 0, prev1, pltpu.roll(u, 1, axis=0))
    u2 = jnp.where(t == 0, prev2, jnp.where(t == 1, prev1, pltpu.roll(u, 2, axis=0)))
    conv = u2 * cw_ref[0:1, :] + u1 * cw_ref[1:2, :] + u * cw_ref[2:3, :]
    y = _dot((b_g * conv).astype(BF16), wout_ref[...])
    o_ref[...] = x + _mod_rows(gt_ref, per_seq_rows) * y

    if per_seq_rows is None:
        tail = u[tile - SUBLANES:, :]
        carry_ref[...] = tail
        u_ref[...] = tail
    else:
        u_ref[...] = u


def _conv_mixer(x, group, mods, norm_g, w_in, conv_w, w_out, state=None):
    prompt = group.per_seq_rows is None
    in_specs = [
        group.row_spec(D_MODEL),
        _mod_spec(0, 3, group), _mod_spec(0, 4, group), _mod_spec(0, 5, group),
        _resident((None, 1, D_MODEL), lambda b, t: (1, 0, 0)),
        _resident((D_MODEL, 3 * D_MODEL), lambda b, t: (0, 0)),
        _resident((CONV_WIDTH, D_MODEL), lambda b, t: (0, 0)),
        _resident((D_MODEL, D_MODEL), lambda b, t: (0, 0)),
    ]
    args = [x, mods, mods, mods, norm_g, w_in, conv_w, w_out]
    scratch = []
    if prompt:
        u_spec = pl.BlockSpec((SUBLANES, D_MODEL), lambda b, t: (b, 0))
        u_shape = jax.ShapeDtypeStruct((group.grid[0] * SUBLANES, D_MODEL), F32)
        scratch.append(pltpu.VMEM((SUBLANES, D_MODEL), F32))
    else:
        n_seq = group.tile // group.per_seq_rows
        st_spec = pl.BlockSpec((n_seq, D_MODEL), lambda b, t: (t, 0))
        in_specs += [st_spec, st_spec]
        args += [state[0], state[1]]
        u_spec = group.row_spec(D_MODEL)
        u_shape = jax.ShapeDtypeStruct((group.rows, D_MODEL), F32)
    return pl.pallas_call(
        functools.partial(_conv_kernel, per_seq_rows=group.per_seq_rows),
        grid=group.grid,
        in_specs=in_specs,
        out_specs=[group.row_spec(D_MODEL), u_spec],
        out_shape=[jax.ShapeDtypeStruct((group.rows, D_MODEL), F32), u_shape],
        scratch_shapes=scratch,
        compiler_params=pltpu.CompilerParams(
            dimension_semantics=("arbitrary", "arbitrary"),
            vmem_limit_bytes=VMEM_LIMIT_BYTES),
        name="conv_mixer",
    )(*args)


def _sink_column(sink_ref, kv_head, q_rows):
    grp = lax.broadcasted_iota(jnp.int32, (GROUP * q_rows, 1), 0) >> _log2(q_rows)
    col = jnp.full((GROUP * q_rows, 1), sink_ref[kv_head * GROUP + GROUP - 1], F32)
    for g in range(GROUP - 2, -1, -1):
        col = jnp.where(grp == g, sink_ref[kv_head * GROUP + g], col)
    return col


def _stack_heads(q_ref, rows, kv_head):
    return jnp.concatenate(
        [q_ref[rows, (kv_head * GROUP + g) * HEAD_DIM:(kv_head * GROUP + g + 1) * HEAD_DIM]
         for g in range(GROUP)], axis=0)


def _softmax_parts(scores, sink):
    m = sink
    for s in scores:
        m = jnp.maximum(m, jnp.max(s, axis=-1, keepdims=True))
    ps = [jnp.exp(s - m) for s in scores]
    denom = jnp.exp(sink - m)
    for p in ps:
        denom = denom + jnp.sum(p, axis=-1, keepdims=True)
    return ps, 1.0 / denom


def _project_q(x_ref, sh_ref, sc_ref, ng_ref, wq_ref, tables, per_seq_rows):
    h = ((_unit_rms(x_ref[...]) * ng_ref[...]) * (1.0 + _mod_rows(sc_ref, per_seq_rows))
         + _mod_rows(sh_ref, per_seq_rows)).astype(BF16)
    q = _rope(_dot(h, wq_ref[...]), *tables)
    return (q * (HEAD_DIM ** -0.5)).astype(BF16)


def _band_caps():
    shape = (2 * WINDOW, GROUP * WINDOW)
    kj = lax.broadcasted_iota(jnp.int32, shape, 0)
    qi = lax.broadcasted_iota(jnp.int32, shape, 1) & (WINDOW - 1)
    in_band = (kj > qi) & (kj <= qi + WINDOW)
    return (jnp.where(in_band, FMAX, NEG), jnp.where(in_band & (kj >= WINDOW), FMAX, NEG))


def _attn_prompt_kernel(x_ref, sh_ref, sc_ref, gt_ref, ng_ref, wq_ref, wo_ref,
                        kp_ref, kc_ref, vp_ref, vc_ref, cos_ref, sin_ref, sink_ref,
                        o_ref, q_scr, k_scr, v_scr, a_scr, cap_scr, h_scr, at_scr):
    tile = x_ref.shape[0]

    @pl.when((pl.program_id(0) == 0) & (pl.program_id(1) == 0))
    def _():
        general, first = _band_caps()
        cap_scr[0] = general
        cap_scr[1] = first

    k_scr[:WINDOW, :] = kp_ref[...].T.astype(BF16)
    k_scr[WINDOW:, :] = kc_ref[...].T.astype(BF16)
    v_scr[:, :WINDOW] = vp_ref[...].astype(BF16)
    v_scr[:, WINDOW:] = vc_ref[...].astype(BF16)
    first_tile = pl.program_id(1) == 0
    norm_gain = ng_ref[...]
    scale1 = 1.0 + _mod_rows(sc_ref, None)
    shift = _mod_rows(sh_ref, None)
    gate = _mod_rows(gt_ref, None)

    group_dims = GROUP * HEAD_DIM

    def chunk_rows(c):
        return slice(c * ATTN_CHUNK, (c + 1) * ATTN_CHUNK)

    def normalize(c):
        rows = chunk_rows(c)
        h_scr[rows, :] = ((_unit_rms(x_ref[rows, :]) * norm_gain) * scale1 + shift).astype(BF16)

    def project_piece(c, kvh):
        rows = chunk_rows(c)
        dims = slice(kvh * group_dims, (kvh + 1) * group_dims)
        qt = _rope_t(_dot(h_scr[rows, :], wq_ref[:, dims]).T, cos_ref[:, rows], sin_ref[:, rows])
        q_scr[dims, rows] = (qt * (HEAD_DIM ** -0.5 * LOG2E)).astype(BF16)

    def attend(blk, kvh):
        cols = slice(blk * WINDOW, (blk + 1) * WINDOW)
        keys = slice(blk * WINDOW, (blk + 2) * WINDOW)
        cap = cap_scr[jnp.where(first_tile, 1, 0)] if blk == 0 else cap_scr[0]
        dims = slice(kvh * HEAD_DIM, (kvh + 1) * HEAD_DIM)
        heads = [slice((kvh * GROUP + g) * HEAD_DIM, (kvh * GROUP + g + 1) * HEAD_DIM)
                 for g in range(GROUP)]
        qg = jnp.concatenate([q_scr[hd, cols] for hd in heads], axis=1)
        s = jnp.minimum(_dot(k_scr[keys, dims], qg), cap)
        sink = LOG2E * jnp.concatenate(
            [jnp.full((1, WINDOW), sink_ref[kvh * GROUP + g], F32) for g in range(GROUP)], axis=1)
        m = jnp.maximum(jnp.max(s, axis=0, keepdims=True), sink)
        p = jnp.exp2(s - m)
        denom = jnp.sum(p, axis=0, keepdims=True) + jnp.exp2(sink - m)
        o = _dot(v_scr[dims, keys], p.astype(BF16)) * (1.0 / denom)
        for g, hd in enumerate(heads):
            a_scr[hd, cols] = o[:, g * WINDOW:(g + 1) * WINDOW]

    def transpose_piece(c, kvh):
        rows = chunk_rows(c)
        dims = slice(kvh * group_dims, (kvh + 1) * group_dims)
        at_scr[rows, dims] = a_scr[dims, rows].T.astype(BF16)

    def emit_piece(c, n):
        rows = chunk_rows(c)
        cols = slice(n * group_dims, (n + 1) * group_dims)
        y = _dot(at_scr[rows, :], wo_ref[:, cols])
        o_ref[rows, cols] = x_ref[rows, cols] + gate[:, cols] * y

    n_chunks = tile // ATTN_CHUNK
    blocks_per_chunk = ATTN_CHUNK // WINDOW
    n_pieces = D_MODEL // group_dims
    normalize(0)
    for kvh in range(N_KV_HEADS):
        project_piece(0, kvh)
    for c in range(n_chunks):
        if c + 1 < n_chunks:
            normalize(c + 1)
        step = 0
        for b in range(blocks_per_chunk):
            for kvh in range(N_KV_HEADS):
                attend(c * blocks_per_chunk + b, kvh)
                if b == blocks_per_chunk - 1:
                    transpose_piece(c, kvh)
                if step % 2 == 0 and c + 1 < n_chunks and step // 2 < N_KV_HEADS:
                    project_piece(c + 1, step // 2)
                if step % 2 == 1 and c > 0 and step // 2 < n_pieces:
                    emit_piece(c - 1, step // 2)
                step += 1
    for n in range(n_pieces):
        emit_piece(n_chunks - 1, n)


def _attn_prompt(x, group, mods, norm_g, w_q, w_o, kv_t, tables_t, sinks):
    tile = group.tile
    nt = group.grid[1]
    blocks_per_tile = tile // WINDOW

    def prev_block(b, t):
        return jnp.maximum((b * nt + t) * blocks_per_tile - 1, 0)

    kv_cur = lambda part: pl.BlockSpec((KV_DIM, tile), lambda b, t: (part, b * nt + t))

    in_specs = [
        group.row_spec(D_MODEL),
        _mod_spec(1, 3, group), _mod_spec(1, 4, group), _mod_spec(1, 5, group),
        _resident((None, 1, D_MODEL), lambda b, t: (4, 0, 0)),
        _resident((D_MODEL, D_MODEL), lambda b, t: (0, 0)),
        _resident((D_MODEL, D_MODEL), lambda b, t: (0, 0)),
        pl.BlockSpec((KV_DIM, WINDOW), lambda b, t: (0, prev_block(b, t))), kv_cur(0),
        pl.BlockSpec((KV_DIM, WINDOW), lambda b, t: (1, prev_block(b, t))), kv_cur(1),
        group.table_t_spec(), group.table_t_spec(),
        pl.BlockSpec(memory_space=pltpu.SMEM),
    ]
    return pl.pallas_call(
        _attn_prompt_kernel,
        grid=group.grid,
        in_specs=in_specs,
        out_specs=group.row_spec(D_MODEL),
        out_shape=jax.ShapeDtypeStruct((group.rows, D_MODEL), F32),
        scratch_shapes=[
            pltpu.VMEM((D_MODEL, tile), BF16),
            pltpu.VMEM((tile + WINDOW, KV_DIM), BF16),
            pltpu.VMEM((KV_DIM, tile + WINDOW), BF16),
            pltpu.VMEM((D_MODEL, tile), F32),
            pltpu.VMEM((2, 2 * WINDOW, GROUP * WINDOW), F32),
            pltpu.VMEM((tile, D_MODEL), BF16),
            pltpu.VMEM((tile, D_MODEL), BF16),
        ],
        compiler_params=pltpu.CompilerParams(
            dimension_semantics=("arbitrary", "arbitrary"),
            vmem_limit_bytes=VMEM_LIMIT_BYTES),
        name="attn_prompt",
    )(x, mods, mods, mods, norm_g, w_q, w_o, kv_t, kv_t, kv_t, kv_t, *tables_t, sinks)


def _attn_sample_kernel(x_ref, sh_ref, sc_ref, gt_ref, ng_ref, wq_ref, wo_ref,
                        kc_ref, vc_ref, kn_ref, vn_ref, cos_ref, sup_ref, sdn_ref, sink_ref,
                        o_ref, kout_ref, vout_ref, q_scr, *, dec_seq):
    rows = x_ref.shape[0]
    n_seq = rows // dec_seq
    n_cache = n_seq * WINDOW
    q_scr[...] = _project_q(x_ref, sh_ref, sc_ref, ng_ref, wq_ref,
                            (cos_ref[...], sup_ref[...], sdn_ref[...]), dec_seq)

    pack = DEC_ATTN_PACK
    pack_rows = pack * dec_seq

    def windows(ref, first_seq, kvh):
        return jnp.concatenate(
            [ref[(s * N_KV_HEADS + kvh) * HEAD_DIM:(s * N_KV_HEADS + kvh + 1) * HEAD_DIM, :].astype(BF16)
             for s in range(first_seq, first_seq + pack)], axis=1)

    q_row = lax.broadcasted_iota(jnp.int32, (GROUP * pack_rows, 1), 0) & (pack_rows - 1)
    q_seq, q_i = q_row >> _log2(dec_seq), q_row & (dec_seq - 1)
    c_col = lax.broadcasted_iota(jnp.int32, (1, pack * WINDOW), 1)
    valid_c = (q_seq == (c_col >> _log2(WINDOW))) & ((c_col & (WINDOW - 1)) > q_i)
    n_col = lax.broadcasted_iota(jnp.int32, (1, rows), 1)
    n_seq_col, n_visible = n_col >> _log2(dec_seq), (n_col & (dec_seq - 1)) <= q_i
    kn = [kn_ref[kvh * HEAD_DIM:(kvh + 1) * HEAD_DIM, :].astype(BF16) for kvh in range(N_KV_HEADS)]
    vn = [vn_ref[kvh * HEAD_DIM:(kvh + 1) * HEAD_DIM, :].astype(BF16) for kvh in range(N_KV_HEADS)]

    keep = lax.broadcasted_iota(jnp.int32, (1, WINDOW), 1) < WINDOW - dec_seq

    def update_windows(first_seq):
        for src_ref, new_ref, dst_ref in ((kc_ref, kn_ref, kout_ref), (vc_ref, vn_ref, vout_ref)):
            new = new_ref[...]
            for s in range(first_seq, first_seq + pack):
                blk = slice(s * KV_DIM, (s + 1) * KV_DIM)
                old = pltpu.roll(src_ref[blk, :], WINDOW - dec_seq, axis=1)
                app = pltpu.roll(new, (WINDOW - dec_seq - s * dec_seq) % rows, axis=1)
                dst_ref[blk, :] = jnp.where(keep, old, app)

    packs = []
    for first_seq in range(0, n_seq, pack):
        update_windows(first_seq)
        q_rows = slice(first_seq * dec_seq, (first_seq + pack) * dec_seq)
        valid_n = ((q_seq + first_seq) == n_seq_col) & n_visible
        heads = []
        for kvh in range(N_KV_HEADS):
            qg = _stack_heads(q_scr, q_rows, kvh)
            s_c = jnp.where(valid_c, _dot(qg, windows(kc_ref, first_seq, kvh)), NEG)
            s_n = jnp.where(valid_n, _dot(qg, kn[kvh]), NEG)
            (p_c, p_n), inv = _softmax_parts([s_c, s_n], _sink_column(sink_ref, kvh, pack_rows))
            o = (_dot_nt(p_c.astype(BF16), windows(vc_ref, first_seq, kvh))
                 + _dot_nt(p_n.astype(BF16), vn[kvh])) * inv
            heads += [o[g * pack_rows:(g + 1) * pack_rows, :] for g in range(GROUP)]
        packs.append(jnp.concatenate(heads, axis=1))
    attn = jnp.concatenate(packs, axis=0).astype(BF16)
    o_ref[...] = x_ref[...] + _mod_rows(gt_ref, dec_seq) * _dot(attn, wo_ref[...])


def _attn_sample(x, n_seq_total, dec_seq, mods, norm_g, w_q, w_o, k_cache, v_cache,
                 kv_new, tables, sinks):
    n_seq = SAMPLE_ATTN_SEQS
    rows = n_seq * dec_seq
    assert rows == WINDOW == LANES
    grid = (n_seq_total // n_seq,)
    row_spec = lambda width: pl.BlockSpec((rows, width), lambda t: (t, 0))
    mod_spec = lambda col: pl.BlockSpec((None, n_seq, D_MODEL), lambda t: (1, t, col))
    cache_spec = pl.BlockSpec((n_seq * KV_DIM, WINDOW), lambda t: (t, 0))
    new_spec = lambda part: pl.BlockSpec((KV_DIM, rows), lambda t: (part, t))
    table_spec = pl.BlockSpec((rows, LANES), lambda t: (0, 0))
    in_specs = [
        row_spec(D_MODEL), mod_spec(3), mod_spec(4), mod_spec(5),
        _resident((None, 1, D_MODEL), lambda t: (4, 0, 0)),
        _resident((D_MODEL, D_MODEL), lambda t: (0, 0)),
        _resident((D_MODEL, D_MODEL), lambda t: (0, 0)),
        cache_spec, cache_spec, new_spec(0), new_spec(1),
        table_spec, table_spec, table_spec,
        pl.BlockSpec(memory_space=pltpu.SMEM),
    ]
    cache_shape = jax.ShapeDtypeStruct((n_seq_total * KV_DIM, WINDOW), F32)
    return pl.pallas_call(
        functools.partial(_attn_sample_kernel, dec_seq=dec_seq),
        grid=grid,
        in_specs=in_specs,
        out_specs=[row_spec(D_MODEL), cache_spec, cache_spec],
        out_shape=[jax.ShapeDtypeStruct((n_seq_total * dec_seq, D_MODEL), F32),
                   cache_shape, cache_shape],
        scratch_shapes=[pltpu.VMEM((rows, D_MODEL), BF16)],
        compiler_params=pltpu.CompilerParams(
            dimension_semantics=("arbitrary",),
            vmem_limit_bytes=VMEM_LIMIT_BYTES),
        name="attn_sample",
    )(x, mods, mods, mods, norm_g, w_q, w_o, k_cache, v_cache, kv_new, kv_new, *tables, sinks)


def _rope_angles(pos):
    inv_freq = ROPE_THETA ** (-jnp.arange(0, ROT_DIM, 2, dtype=F32) / ROT_DIM)
    ang = pos.astype(F32)[:, None] * inv_freq[None, :]
    return jnp.cos(ang), jnp.sin(ang)


def _rope_tables(pos):
    half = ROT_DIM // 2
    cos, sin = _rope_angles(pos)
    n = pos.shape[0]
    rest = HEAD_DIM - ROT_DIM
    zeros = jnp.zeros((n, half), F32)
    cos_t = jnp.concatenate([cos, cos, jnp.ones((n, rest), F32)], axis=1)
    sin_up = jnp.concatenate([-sin, zeros, jnp.zeros((n, rest), F32)], axis=1)
    sin_dn = jnp.concatenate([zeros, sin, jnp.zeros((n, rest), F32)], axis=1)
    reps = LANES // HEAD_DIM
    return tuple(jnp.tile(t, (1, reps)) for t in (cos_t, sin_up, sin_dn))


def kernel(x_prompt, x_sample, state_conv, cache_k_win, cache_v_win, c_prompt, c_sample, norm_g, w_ada, b_ada, w_ffn_gate, w_ffn_up, w_ffn_down, conv_w_in, conv_w, conv_w_out, kv_norm_g, w_ada_kv, b_ada_kv, w_k, w_v, attn_w_q, attn_sinks, attn_w_o, final_norm_g):
    batch, seq, d = x_prompt.shape
    dec_batch, dec_seq, _ = x_sample.shape
    w_buf = cache_k_win.shape[1]
    past_len = 16384
    assert d == D_MODEL and w_buf == WINDOW and dec_seq == SUBLANES
    assert w_ada.shape[0] == 2 and conv_w_in.shape[0] == 1 and attn_w_q.shape[0] == 1

    pad = (-(dec_batch + batch)) % SUBLANES
    c_all = jnp.concatenate([c_sample, c_prompt, jnp.zeros((pad, d), F32)], axis=0)
    mods = _ada(c_all, w_ada, b_ada)
    mods_kv = _ada(c_all, w_ada_kv[None], b_ada_kv[None])
    prompt_mod_block = dec_batch // SUBLANES

    make_groups = lambda tile: {
        "prompt": _Group(batch, seq, None, prompt_mod_block, tile),
        "sample": _Group(dec_batch, dec_seq, dec_seq, prompt_mod_block, tile),
    }
    ffn_groups, mixer_groups = make_groups(FFN_TILE), make_groups(MIXER_TILE)
    xp, xs = x_prompt.reshape(batch * seq, d), x_sample.reshape(dec_batch * dec_seq, d)

    norm_g3 = norm_g.reshape(-1, 1, d)
    w_in, w_out = conv_w_in[0].astype(BF16), conv_w_out[0].astype(BF16)
    w_q, w_o = attn_w_q[0].astype(BF16), attn_w_o[0].astype(BF16)
    sinks = attn_sinks[0]

    pos_dec = past_len + jnp.arange(dec_seq, dtype=jnp.int32)
    pos_prompt = jnp.arange(seq, dtype=jnp.int32)
    tab_dec = tuple(jnp.tile(t, (SAMPLE_ATTN_SEQS, 1)) for t in _rope_tables(pos_dec))
    tab_prompt_t = tuple(t.T for t in _rope_angles(pos_prompt))
    tab_dec_t = tuple(jnp.tile(t.T, (1, FFN_TILE // dec_seq)) for t in _rope_angles(pos_dec))
    kv_args = dict(norm_g=kv_norm_g.reshape(1, d), mods=mods_kv,
                   w_kv_t=jnp.concatenate([w_k.T, w_v.T], axis=0).astype(BF16),
                   prompt_tables_t=tab_prompt_t, decode_tables_t=tab_dec_t)
    to_rows = lambda c: jnp.transpose(c, (0, 2, 3, 1)).reshape(dec_batch * KV_DIM, w_buf)
    from_rows = lambda c: jnp.transpose(c.reshape(dec_batch, N_KV_HEADS, HEAD_DIM, w_buf), (0, 3, 1, 2))

    ffn = functools.partial(_ffn, prompt=ffn_groups["prompt"], decode=ffn_groups["sample"], mods=mods,
                            norm_g=norm_g3, wg=w_ffn_gate, wu=w_ffn_up, wd=w_ffn_down)
    xp, xs = ffn(xp, xs, layer=0, sub=0, slot=0)
    xp, u_p = _conv_mixer(xp, mixer_groups["prompt"], mods, norm_g3, w_in, conv_w[0], w_out)
    xs, u_s = _conv_mixer(xs, mixer_groups["sample"], mods, norm_g3, w_in, conv_w[0], w_out,
                          (state_conv[0, :, 0, :], state_conv[0, :, 1, :]))
    xp, xs = ffn(xp, xs, layer=0, sub=2, slot=1)
    xp, xs, kvt_p, kvt_s = ffn(xp, xs, layer=1, sub=0, slot=0, kv=kv_args)
    xp = _attn_prompt(xp, make_groups(ATTN_TILE)["prompt"], mods, norm_g3, w_q, w_o, kvt_p, tab_prompt_t, sinks)
    xs, k_win, v_win = _attn_sample(xs, dec_batch, dec_seq, mods, norm_g3, w_q, w_o,
                                    to_rows(cache_k_win), to_rows(cache_v_win), kvt_s, tab_dec, sinks)
    y_p, y_s = ffn(xp, xs, layer=1, sub=2, slot=1, final_g=final_norm_g.reshape(1, d))

    kv_state_p = jnp.stack([kvt_p[:, (b + 1) * seq - WINDOW:(b + 1) * seq] for b in range(batch)])
    kv_state_p = jnp.transpose(kv_state_p.reshape(batch, 2, N_KV_HEADS, HEAD_DIM, WINDOW), (1, 0, 4, 2, 3))
    k_state_p, v_state_p = kv_state_p[0], kv_state_p[1]
    k_state_s, v_state_s = from_rows(k_win), from_rows(v_win)
    tail = CONV_WIDTH - 1
    conv_p = u_p.reshape(batch, SUBLANES, d)[:, SUBLANES - tail:][None]
    conv_s = u_s.reshape(dec_batch, dec_seq, d)[:, dec_seq - tail:][None]
    return (y_p.reshape(batch, seq, d), y_s.reshape(dec_batch, dec_seq, d), conv_p, conv_s,
            k_state_p, v_state_p, k_state_s, v_state_s)
```

```python
import functools

import jax
import jax.numpy as jnp
from jax import lax
from jax.experimental import pallas as pl
from jax.experimental.pallas import tpu as pltpu

F32 = jnp.float32
BF16 = jnp.bfloat16

D_MODEL = 1024
D_FF = 2816
HEAD_DIM = 64
N_HEADS = 16
N_KV_HEADS = 4
GROUP = N_HEADS // N_KV_HEADS
KV_DIM = N_KV_HEADS * HEAD_DIM
WINDOW = 128
ROT_DIM = 16
ROPE_THETA = 500000.0
CONV_WIDTH = 3
N_MOD = 9
EPS = 1e-6
NEG = -1e30
FMAX = float(jnp.finfo(jnp.float32).max)

SUBLANES = 8
LANES = 128
VMEM_LIMIT_BYTES = 60 * 1024 * 1024

ADA_COLS = 2304
FFN_TILE = 512
FFN_SUBTILES = 2
FF_STAGE = 256
CONV_ROWS = 256
MIXER_TILE = 1024
ATTN_TILE = 1024
ATTN_CHUNK = 256
LOG2E = 1.4426950408889634
SAMPLE_ATTN_SEQS = 16
DEC_ATTN_PACK = 4
FF_CHUNKS = ((0, 1024), (1024, 2048), (2048, D_FF))


def _dot(a, b):
    return jnp.dot(a, b, preferred_element_type=F32)


def _dot_nt(a, b):
    return lax.dot_general(a, b, (((1,), (1,)), ((), ())), preferred_element_type=F32)


def _log2(n):
    assert n & (n - 1) == 0, n
    return n.bit_length() - 1


def _silu(x):
    return x * jax.nn.sigmoid(x)


def _unit_rms(x):
    return x * lax.rsqrt(jnp.mean(x * x, axis=-1, keepdims=True) + EPS)


def _repeat_rows(ref, reps):
    n = ref.shape[1]
    return jnp.concatenate(
        [jnp.broadcast_to(ref[i:i + 1, :], (reps, n)) for i in range(ref.shape[0])], axis=0)


def _mod_rows(ref, per_seq_rows):
    if per_seq_rows is None:
        return ref[pl.ds(pl.program_id(0), 1), :]
    return _repeat_rows(ref, per_seq_rows)


def _rope(x, cos_t, sin_up_t, sin_dn_t):
    n = x.shape[1]
    reps = n // LANES
    half = ROT_DIM // 2
    cos = jnp.concatenate([cos_t] * reps, axis=1)
    sin_up = jnp.concatenate([sin_up_t] * reps, axis=1)
    sin_dn = jnp.concatenate([sin_dn_t] * reps, axis=1)
    x_up = pltpu.roll(x, n - half, axis=1)
    x_dn = pltpu.roll(x, half, axis=1)
    return x * cos + x_up * sin_up + x_dn * sin_dn


def _ada_kernel(c_ref, w_ref, b_ref, o_ref):
    a = _silu(c_ref[...]).astype(BF16)
    o_ref[...] = _dot(a, w_ref[...].astype(BF16)) + b_ref[...]


def _ada(c_all, w, b):
    n_layers, _, n = w.shape
    m = c_all.shape[0]
    tn = ADA_COLS if n % ADA_COLS == 0 else D_MODEL
    return pl.pallas_call(
        _ada_kernel,
        grid=(n_layers, n // tn),
        in_specs=[
            pl.BlockSpec((m, D_MODEL), lambda l, j: (0, 0)),
            pl.BlockSpec((None, D_MODEL, tn), lambda l, j: (l, 0, j)),
            pl.BlockSpec((None, 1, tn), lambda l, j: (l, 0, j)),
        ],
        out_specs=pl.BlockSpec((None, m, tn), lambda l, j: (l, 0, j)),
        out_shape=jax.ShapeDtypeStruct((n_layers, m, n), F32),
        compiler_params=pltpu.CompilerParams(
            dimension_semantics=("arbitrary", "arbitrary"),
            vmem_limit_bytes=VMEM_LIMIT_BYTES),
        name="adaln_tables",
    )(c_all, w, b.reshape(n_layers, 1, n))


def _rope_t(xt, cos_t, sin_t):
    half = ROT_DIM // 2
    out = []
    for h in range(xt.shape[0] // HEAD_DIM):
        base = h * HEAD_DIM
        x1 = xt[base:base + half, :]
        x2 = xt[base + half:base + ROT_DIM, :]
        out += [x1 * cos_t - x2 * sin_t, x2 * cos_t + x1 * sin_t,
                xt[base + ROT_DIM:base + HEAD_DIM, :]]
    return jnp.concatenate(out, axis=0)


def _ffn_kernel(*refs, n_prompt_tiles, tiles_per_seq, dec_seq, layer, slot, with_kv, final):
    it = iter(refs)
    take = lambda n: [next(it) for _ in range(n)]
    xp_ref, xs_ref = take(2)
    p_mods, s_mods = take(3), take(3)
    ng_ref, wg_hbm, wu_hbm, wd_hbm = take(4)
    if with_kv:
        (kvg_ref,), p_kvmods, s_kvmods = take(1), take(2), take(2)
        (wkvt_ref,) = take(1)
        p_tabs, s_tabs = take(2), take(2)
    else:
        p_kvmods = s_kvmods = None
    if final:
        (fg_ref,) = take(1)
    op_ref, os_ref = take(2)
    if with_kv:
        kvp_ref, kvs_ref = take(2)
    wg_s, wu_s, wd_s, stage_in, stage_out, sem = take(6)

    i = pl.program_id(0)
    prompt_rows = lambda ref: ref[pl.ds(i // tiles_per_seq, 1), :]
    decode_rows = lambda ref: _repeat_rows(ref, dec_seq)

    def weight_copies(c):
        cols = pl.ds(c * FF_STAGE, FF_STAGE)
        buf = c % 2
        return (
            pltpu.make_async_copy(wg_hbm.at[layer, slot, :, cols], stage_in.at[buf, 0], sem.at[buf, 0]),
            pltpu.make_async_copy(wu_hbm.at[layer, slot, :, cols], stage_in.at[buf, 1], sem.at[buf, 1]),
            pltpu.make_async_copy(wd_hbm.at[layer, slot, cols, :], stage_out.at[buf], sem.at[buf, 2]),
        )

    def partial_ffn(hb, lo, hi):
        g = _dot(hb, wg_s[:, lo:hi])
        u = _dot(hb, wu_s[:, lo:hi])
        return _dot((_silu(g) * u).astype(BF16), wd_s[lo:hi, :])

    def split_body(x_ref, o_ref, rows_of, mods, kvmods):
        sub = x_ref.shape[0] // FFN_SUBTILES
        groups = [slice(k * sub, (k + 1) * sub) for k in range(FFN_SUBTILES)]
        rows_in = lambda v, r: v if v.shape[0] == 1 else v[r]
        shift, scale, gate = (rows_of(m) for m in mods)
        xs_, xns, hbs = [], [], []
        ys = [None] * FFN_SUBTILES
        for c, (lo, hi) in enumerate(FF_CHUNKS):
            for k, r in enumerate(groups):
                if c == 0:
                    x = x_ref[r, :]
                    xn = _unit_rms(x)
                    xs_.append(x)
                    xns.append(xn)
                    hbs.append(((xn * ng_ref[...]) * (1.0 + rows_in(scale, r))
                                + rows_in(shift, r)).astype(BF16))
                part = partial_ffn(hbs[k], lo, hi)
                ys[k] = part if ys[k] is None else ys[k] + part
                if c == len(FF_CHUNKS) - 1:
                    x_new = xs_[k] + (0.5 * rows_in(gate, r)) * ys[k]
                    o_ref[r, :] = _unit_rms(x_new) * fg_ref[...] if final else x_new
        if with_kv:
            kv_shift, kv_scale = rows_of(kvmods[0]), rows_of(kvmods[1])
            return jnp.concatenate(
                [((xns[k] * kvg_ref[...]) * (1.0 + rows_in(kv_scale, r)) + rows_in(kv_shift, r)).astype(BF16)
                 for k, r in enumerate(groups)], axis=0)

    def body(x_ref, o_ref, rows_of, mods, kvmods, stage_weights):
        if not stage_weights:
            return split_body(x_ref, o_ref, rows_of, mods, kvmods)
        x = x_ref[...]
        xn = _unit_rms(x)
        hb = ((xn * ng_ref[...]) * (1.0 + rows_of(mods[1])) + rows_of(mods[0])).astype(BF16)
        y = None
        n_slices = D_FF // FF_STAGE
        for c in range(min(2, n_slices)):
            for cp in weight_copies(c):
                cp.start()
        for c in range(n_slices):
            lo, hi = c * FF_STAGE, (c + 1) * FF_STAGE
            for cp in weight_copies(c):
                cp.wait()
            wg_s[:, lo:hi] = stage_in[c % 2, 0].astype(BF16)
            wu_s[:, lo:hi] = stage_in[c % 2, 1].astype(BF16)
            wd_s[lo:hi, :] = stage_out[c % 2].astype(BF16)
            if c + 2 < n_slices:
                for cp in weight_copies(c + 2):
                    cp.start()
            part = partial_ffn(hb, lo, hi)
            y = part if y is None else y + part
        x_new = x + (0.5 * rows_of(mods[2])) * y
        o_ref[...] = _unit_rms(x_new) * fg_ref[...] if final else x_new
        if with_kv:
            return ((xn * kvg_ref[...]) * (1.0 + rows_of(kvmods[1])) + rows_of(kvmods[0])).astype(BF16)

    def emit_kv(hk, tabs, kv_ref):
        kv_t = _dot_nt(wkvt_ref[...], hk)
        kv_ref[:KV_DIM, :] = _rope_t(kv_t[:KV_DIM, :], tabs[0][...], tabs[1][...])
        kv_ref[KV_DIM:, :] = kv_t[KV_DIM:, :]

    def prompt_step(stage_weights):
        hk = body(xp_ref, op_ref, prompt_rows, p_mods, p_kvmods, stage_weights)
        if with_kv:
            emit_kv(hk, p_tabs, kvp_ref)

    @pl.when(i == 0)
    def _():
        prompt_step(True)

    @pl.when((i > 0) & (i < n_prompt_tiles))
    def _():
        prompt_step(False)

    @pl.when(i >= n_prompt_tiles)
    def _():
        hk = body(xs_ref, os_ref, decode_rows, s_mods, s_kvmods, False)
        if with_kv:
            emit_kv(hk, s_tabs, kvs_ref)


def _resident(shape, index_map):
    return pl.BlockSpec(shape, index_map, pipeline_mode=pl.Buffered(1))


def _mod_spec(layer, col, group):
    if group.per_seq_rows is None:
        return pl.BlockSpec((None, SUBLANES, D_MODEL),
                            lambda b, t: (layer, group.prompt_mod_block, col))
    return pl.BlockSpec((None, group.tile // group.per_seq_rows, D_MODEL),
                        lambda b, t: (layer, t, col))


class _Group:
    def __init__(self, n_seq, seq_len, per_seq_rows, prompt_mod_block, tile):
        self.tile = tile
        self.rows = n_seq * seq_len
        self.per_seq_rows = per_seq_rows
        self.prompt_mod_block = prompt_mod_block
        if per_seq_rows is None:
            self.grid = (n_seq, seq_len // self.tile)
        else:
            self.grid = (1, self.rows // self.tile)
        self.tiles_per_seq = self.grid[1]

    def row_spec(self, width):
        nt = self.grid[1]
        return pl.BlockSpec((self.tile, width), lambda b, t: (b * nt + t, 0))

    def col_spec(self, height):
        nt = self.grid[1]
        return pl.BlockSpec((height, self.tile), lambda b, t: (0, b * nt + t))

    def table_t_spec(self):
        if self.per_seq_rows is None:
            return pl.BlockSpec((ROT_DIM // 2, self.tile), lambda b, t: (0, t))
        return pl.BlockSpec((ROT_DIM // 2, self.tile), lambda b, t: (0, 0))

    def table_spec(self):
        if self.per_seq_rows is None:
            return pl.BlockSpec((self.tile, LANES), lambda b, t: (t, 0))
        return pl.BlockSpec((self.tile, LANES), lambda b, t: (0, 0))


def _ffn(xp, xs, prompt, decode, mods, layer, sub, norm_g, wg, wu, wd, slot, kv=None, final_g=None):
    with_kv = kv is not None
    final = final_g is not None
    tile = prompt.tile
    assert decode.tile == tile
    n_p, n_s = prompt.rows // tile, decode.rows // tile
    tiles_per_seq = prompt.tiles_per_seq
    p_tile = lambda i: jnp.minimum(i, n_p - 1)
    s_tile = lambda i: jnp.maximum(i - n_p, 0)
    n_dec_seq = tile // decode.per_seq_rows

    def mod_specs(table_layer, cols):
        p = [pl.BlockSpec((None, SUBLANES, D_MODEL),
                          functools.partial(lambda i, c: (table_layer, prompt.prompt_mod_block, c), c=c))
             for c in cols]
        s = [pl.BlockSpec((None, n_dec_seq, D_MODEL),
                          functools.partial(lambda i, c: (table_layer, s_tile(i), c), c=c),
                          pipeline_mode=pl.Buffered(1))
             for c in cols]
        return p + s

    p_rows = lambda width: pl.BlockSpec((tile, width), lambda i: (p_tile(i), 0))
    s_rows = lambda width, **kw: pl.BlockSpec((tile, width), lambda i: (s_tile(i), 0), **kw)
    p_cols = lambda height: pl.BlockSpec((height, tile), lambda i: (0, p_tile(i)))
    s_cols = lambda height: pl.BlockSpec((height, tile), lambda i: (0, s_tile(i)))
    const = lambda shape: _resident(shape, lambda i: (0,) * len(shape))
    hbm = pl.BlockSpec(memory_space=pl.ANY)

    in_specs = [p_rows(D_MODEL), s_rows(D_MODEL, pipeline_mode=pl.Buffered(1))]
    in_specs += mod_specs(layer, [3 * sub + 0, 3 * sub + 1, 3 * sub + 2])
    in_specs += [_resident((None, 1, D_MODEL), lambda i: (layer * 3 + sub, 0, 0)), hbm, hbm, hbm]
    args = [xp, xs] + [mods] * 6 + [norm_g, wg, wu, wd]
    out_specs = [p_rows(D_MODEL), s_rows(D_MODEL)]
    out_shape = [jax.ShapeDtypeStruct((prompt.rows, D_MODEL), F32),
                 jax.ShapeDtypeStruct((decode.rows, D_MODEL), F32)]
    if with_kv:
        in_specs += [const((1, D_MODEL))] + mod_specs(0, [0, 1])
        in_specs += [const((2 * KV_DIM, D_MODEL))]
        in_specs += [pl.BlockSpec((ROT_DIM // 2, tile), lambda i: (0, p_tile(i) % tiles_per_seq))] * 2
        in_specs += [const((ROT_DIM // 2, tile))] * 2
        args += [kv["norm_g"]] + [kv["mods"]] * 4 + [kv["w_kv_t"]]
        args += [*kv["prompt_tables_t"], *kv["decode_tables_t"]]
        out_specs += [p_cols(2 * KV_DIM), s_cols(2 * KV_DIM)]
        out_shape += [jax.ShapeDtypeStruct((2 * KV_DIM, prompt.rows), F32),
                      jax.ShapeDtypeStruct((2 * KV_DIM, decode.rows), F32)]
    if final:
        in_specs.append(const((1, D_MODEL)))
        args.append(final_g)
    scratch = [
        pltpu.VMEM((D_MODEL, D_FF), BF16), pltpu.VMEM((D_MODEL, D_FF), BF16),
        pltpu.VMEM((D_FF, D_MODEL), BF16),
        pltpu.VMEM((2, 2, D_MODEL, FF_STAGE), F32),
        pltpu.VMEM((2, FF_STAGE, D_MODEL), F32),
        pltpu.SemaphoreType.DMA((2, 3)),
    ]
    return pl.pallas_call(
        functools.partial(_ffn_kernel, n_prompt_tiles=n_p, tiles_per_seq=tiles_per_seq,
                          dec_seq=decode.per_seq_rows, layer=layer, slot=slot,
                          with_kv=with_kv, final=final),
        grid=(n_p + n_s,),
        in_specs=in_specs,
        out_specs=out_specs,
        out_shape=out_shape,
        scratch_shapes=scratch,
        compiler_params=pltpu.CompilerParams(
            dimension_semantics=("arbitrary",),
            vmem_limit_bytes=VMEM_LIMIT_BYTES),
        name=f"ffn_l{layer}_s{sub}",
    )(*args)


def _conv_kernel(*refs, per_seq_rows):
    it = iter(refs)
    x_ref, sh_ref, sc_ref, gt_ref, ng_ref, win_ref, cw_ref, wout_ref = (next(it) for _ in range(8))
    if per_seq_rows is not None:
        st0_ref, st1_ref = next(it), next(it)
    o_ref, u_ref = next(it), next(it)
    if per_seq_rows is None:
        carry_ref = next(it)

    tile = x_ref.shape[0]
    prompt = per_seq_rows is None
    rows_in = lambda v, r: v if v.shape[0] == 1 else v[r]
    shift, scale, gate = (_mod_rows(ref, per_seq_rows) for ref in (sh_ref, sc_ref, gt_ref))
    row = lax.broadcasted_iota(jnp.int32, (CONV_ROWS, 1), 0)
    if prompt:
        @pl.when(pl.program_id(1) == 0)
        def _():
            carry_ref[...] = jnp.zeros_like(carry_ref)
        prev2 = carry_ref[SUBLANES - 2:SUBLANES - 1, :]
        prev1 = carry_ref[SUBLANES - 1:SUBLANES, :]
        t = row
    else:
        state2 = _repeat_rows(st0_ref, per_seq_rows)
        state1 = _repeat_rows(st1_ref, per_seq_rows)
        t = row & (per_seq_rows - 1)

    for k in range(tile // CONV_ROWS):
        r = slice(k * CONV_ROWS, (k + 1) * CONV_ROWS)
        x = x_ref[r, :]
        h = ((_unit_rms(x) * ng_ref[...]) * (1.0 + rows_in(scale, r)) + rows_in(shift, r)).astype(BF16)
        bcv = _dot(h, win_ref[...])
        b_g = bcv[:, :D_MODEL]
        u = bcv[:, D_MODEL:2 * D_MODEL] * bcv[:, 2 * D_MODEL:]
        if not prompt:
            prev2, prev1 = state2[r], state1[r]
        u1 = jnp.where(t == 0, prev1, pltpu.roll(u, 1, axis=0))
        u2 = jnp.where(t == 0, prev2, jnp.where(t == 1, prev1, pltpu.roll(u, 2, axis=0)))
        conv = u2 * cw_ref[0:1, :] + u1 * cw_ref[1:2, :] + u * cw_ref[2:3, :]
        y = _dot((b_g * conv).astype(BF16), wout_ref[...])
        o_ref[r, :] = x + rows_in(gate, r) * y
        if prompt:
            prev2, prev1 = u[CONV_ROWS - 2:CONV_ROWS - 1, :], u[CONV_ROWS - 1:, :]
        else:
            u_ref[r, :] = u

    if prompt:
        tail = u[CONV_ROWS - SUBLANES:, :]
        carry_ref[...] = tail
        u_ref[...] = tail


def _conv_mixer(x, group, mods, norm_g, w_in, conv_w, w_out, state=None):
    prompt = group.per_seq_rows is None
    in_specs = [
        group.row_spec(D_MODEL),
        _mod_spec(0, 3, group), _mod_spec(0, 4, group), _mod_spec(0, 5, group),
        _resident((None, 1, D_MODEL), lambda b, t: (1, 0, 0)),
        _resident((D_MODEL, 3 * D_MODEL), lambda b, t: (0, 0)),
        _resident((CONV_WIDTH, D_MODEL), lambda b, t: (0, 0)),
        _resident((D_MODEL, D_MODEL), lambda b, t: (0, 0)),
    ]
    args = [x, mods, mods, mods, norm_g, w_in, conv_w, w_out]
    scratch = []
    if prompt:
        u_spec = pl.BlockSpec((SUBLANES, D_MODEL), lambda b, t: (b, 0))
        u_shape = jax.ShapeDtypeStruct((group.grid[0] * SUBLANES, D_MODEL), F32)
        scratch.append(pltpu.VMEM((SUBLANES, D_MODEL), F32))
    else:
        n_seq = group.tile // group.per_seq_rows
        st_spec = pl.BlockSpec((n_seq, D_MODEL), lambda b, t: (t, 0))
        in_specs += [st_spec, st_spec]
        args += [state[0], state[1]]
        u_spec = group.row_spec(D_MODEL)
        u_shape = jax.ShapeDtypeStruct((group.rows, D_MODEL), F32)
    return pl.pallas_call(
        functools.partial(_conv_kernel, per_seq_rows=group.per_seq_rows),
        grid=group.grid,
        in_specs=in_specs,
        out_specs=[group.row_spec(D_MODEL), u_spec],
        out_shape=[jax.ShapeDtypeStruct((group.rows, D_MODEL), F32), u_shape],
        scratch_shapes=scratch,
        compiler_params=pltpu.CompilerParams(
            dimension_semantics=("arbitrary", "arbitrary"),
            vmem_limit_bytes=VMEM_LIMIT_BYTES),
        name="conv_mixer",
    )(*args)


def _sink_column(sink_ref, kv_head, q_rows):
    grp = lax.broadcasted_iota(jnp.int32, (GROUP * q_rows, 1), 0) >> _log2(q_rows)
    col = jnp.full((GROUP * q_rows, 1), sink_ref[kv_head * GROUP + GROUP - 1], F32)
    for g in range(GROUP - 2, -1, -1):
        col = jnp.where(grp == g, sink_ref[kv_head * GROUP + g], col)
    return col


def _stack_heads(q_ref, rows, kv_head):
    return jnp.concatenate(
        [q_ref[rows, (kv_head * GROUP + g) * HEAD_DIM:(kv_head * GROUP + g + 1) * HEAD_DIM]
         for g in range(GROUP)], axis=0)


def _softmax_parts(scores, sink):
    m = sink
    for s in scores:
        m = jnp.maximum(m, jnp.max(s, axis=-1, keepdims=True))
    ps = [jnp.exp(s - m) for s in scores]
    denom = jnp.exp(sink - m)
    for p in ps:
        denom = denom + jnp.sum(p, axis=-1, keepdims=True)
    return ps, 1.0 / denom


def _project_q(x_ref, sh_ref, sc_ref, ng_ref, wq_ref, tables, per_seq_rows):
    h = ((_unit_rms(x_ref[...]) * ng_ref[...]) * (1.0 + _mod_rows(sc_ref, per_seq_rows))
         + _mod_rows(sh_ref, per_seq_rows)).astype(BF16)
    q = _rope(_dot(h, wq_ref[...]), *tables)
    return (q * (HEAD_DIM ** -0.5)).astype(BF16)


def _band_caps():
    shape = (2 * WINDOW, GROUP * WINDOW)
    kj = lax.broadcasted_iota(jnp.int32, shape, 0)
    qi = lax.broadcasted_iota(jnp.int32, shape, 1) & (WINDOW - 1)
    in_band = (kj > qi) & (kj <= qi + WINDOW)
    return (jnp.where(in_band, FMAX, NEG), jnp.where(in_band & (kj >= WINDOW), FMAX, NEG))


def _attn_prompt_kernel(x_ref, sh_ref, sc_ref, gt_ref, ng_ref, wq_ref, wo_ref,
                        kp_ref, kc_ref, vp_ref, vc_ref, cos_ref, sin_ref, sink_ref,
                        o_ref, q_scr, k_scr, v_scr, a_scr, cap_scr, h_scr, at_scr):
    tile = x_ref.shape[0]

    @pl.when((pl.program_id(0) == 0) & (pl.program_id(1) == 0))
    def _():
        general, first = _band_caps()
        cap_scr[0] = general
        cap_scr[1] = first

    k_scr[:WINDOW, :] = kp_ref[...].T.astype(BF16)
    k_scr[WINDOW:, :] = kc_ref[...].T.astype(BF16)
    v_scr[:, :WINDOW] = vp_ref[...].astype(BF16)
    v_scr[:, WINDOW:] = vc_ref[...].astype(BF16)
    first_tile = pl.program_id(1) == 0
    norm_gain = ng_ref[...]
    scale1 = 1.0 + _mod_rows(sc_ref, None)
    shift = _mod_rows(sh_ref, None)
    gate = _mod_rows(gt_ref, None)

    group_dims = GROUP * HEAD_DIM

    def chunk_rows(c):
        return slice(c * ATTN_CHUNK, (c + 1) * ATTN_CHUNK)

    def normalize(c):
        rows = chunk_rows(c)
        h_scr[rows, :] = ((_unit_rms(x_ref[rows, :]) * norm_gain) * scale1 + shift).astype(BF16)

    def project_piece(c, kvh):
        rows = chunk_rows(c)
        dims = slice(kvh * group_dims, (kvh + 1) * group_dims)
        qt = _rope_t(_dot(h_scr[rows, :], wq_ref[:, dims]).T, cos_ref[:, rows], sin_ref[:, rows])
        q_scr[dims, rows] = (qt * (HEAD_DIM ** -0.5 * LOG2E)).astype(BF16)

    def attend(blk, kvh):
        cols = slice(blk * WINDOW, (blk + 1) * WINDOW)
        keys = slice(blk * WINDOW, (blk + 2) * WINDOW)
        cap = cap_scr[jnp.where(first_tile, 1, 0)] if blk == 0 else cap_scr[0]
        dims = slice(kvh * HEAD_DIM, (kvh + 1) * HEAD_DIM)
        heads = [slice((kvh * GROUP + g) * HEAD_DIM, (kvh * GROUP + g + 1) * HEAD_DIM)
                 for g in range(GROUP)]
        qg = jnp.concatenate([q_scr[hd, cols] for hd in heads], axis=1)
        s = jnp.minimum(_dot(k_scr[keys, dims], qg), cap)
        sink = LOG2E * jnp.concatenate(
            [jnp.full((1, WINDOW), sink_ref[kvh * GROUP + g], F32) for g in range(GROUP)], axis=1)
        m = jnp.maximum(jnp.max(s, axis=0, keepdims=True), sink)
        p = jnp.exp2(s - m)
        denom = jnp.sum(p, axis=0, keepdims=True) + jnp.exp2(sink - m)
        o = _dot(v_scr[dims, keys], p.astype(BF16)) * (1.0 / denom)
        for g, hd in enumerate(heads):
            a_scr[hd, cols] = o[:, g * WINDOW:(g + 1) * WINDOW]

    def transpose_piece(c, kvh):
        rows = chunk_rows(c)
        dims = slice(kvh * group_dims, (kvh + 1) * group_dims)
        at_scr[rows, dims] = a_scr[dims, rows].T.astype(BF16)

    def emit_piece(c, n):
        rows = chunk_rows(c)
        cols = slice(n * group_dims, (n + 1) * group_dims)
        y = _dot(at_scr[rows, :], wo_ref[:, cols])
        o_ref[rows, cols] = x_ref[rows, cols] + gate[:, cols] * y

    n_chunks = tile // ATTN_CHUNK
    blocks_per_chunk = ATTN_CHUNK // WINDOW
    n_pieces = D_MODEL // group_dims
    normalize(0)
    for kvh in range(N_KV_HEADS):
        project_piece(0, kvh)
    for c in range(n_chunks):
        if c + 1 < n_chunks:
            normalize(c + 1)
        step = 0
        for b in range(blocks_per_chunk):
            for kvh in range(N_KV_HEADS):
                attend(c * blocks_per_chunk + b, kvh)
                if b == blocks_per_chunk - 1:
                    transpose_piece(c, kvh)
                if step % 2 == 0 and c + 1 < n_chunks and step // 2 < N_KV_HEADS:
                    project_piece(c + 1, step // 2)
                if step % 2 == 1 and c > 0 and step // 2 < n_pieces:
                    emit_piece(c - 1, step // 2)
                step += 1
    for n in range(n_pieces):
        emit_piece(n_chunks - 1, n)


def _attn_prompt(x, group, mods, norm_g, w_q, w_o, kv_t, tables_t, sinks):
    tile = group.tile
    nt = group.grid[1]
    blocks_per_tile = tile // WINDOW

    def prev_block(b, t):
        return jnp.maximum((b * nt + t) * blocks_per_tile - 1, 0)

    kv_cur = lambda part: pl.BlockSpec((KV_DIM, tile), lambda b, t: (part, b * nt + t))

    in_specs = [
        group.row_spec(D_MODEL),
        _mod_spec(1, 3, group), _mod_spec(1, 4, group), _mod_spec(1, 5, group),
        _resident((None, 1, D_MODEL), lambda b, t: (4, 0, 0)),
        _resident((D_MODEL, D_MODEL), lambda b, t: (0, 0)),
        _resident((D_MODEL, D_MODEL), lambda b, t: (0, 0)),
        pl.BlockSpec((KV_DIM, WINDOW), lambda b, t: (0, prev_block(b, t))), kv_cur(0),
        pl.BlockSpec((KV_DIM, WINDOW), lambda b, t: (1, prev_block(b, t))), kv_cur(1),
        group.table_t_spec(), group.table_t_spec(),
        pl.BlockSpec(memory_space=pltpu.SMEM),
    ]
    return pl.pallas_call(
        _attn_prompt_kernel,
        grid=group.grid,
        in_specs=in_specs,
        out_specs=group.row_spec(D_MODEL),
        out_shape=jax.ShapeDtypeStruct((group.rows, D_MODEL), F32),
        scratch_shapes=[
            pltpu.VMEM((D_MODEL, tile), BF16),
            pltpu.VMEM((tile + WINDOW, KV_DIM), BF16),
            pltpu.VMEM((KV_DIM, tile + WINDOW), BF16),
            pltpu.VMEM((D_MODEL, tile), F32),
            pltpu.VMEM((2, 2 * WINDOW, GROUP * WINDOW), F32),
            pltpu.VMEM((tile, D_MODEL), BF16),
            pltpu.VMEM((tile, D_MODEL), BF16),
        ],
        compiler_params=pltpu.CompilerParams(
            dimension_semantics=("arbitrary", "arbitrary"),
            vmem_limit_bytes=VMEM_LIMIT_BYTES),
        name="attn_prompt",
    )(x, mods, mods, mods, norm_g, w_q, w_o, kv_t, kv_t, kv_t, kv_t, *tables_t, sinks)


def _attn_sample_kernel(x_ref, sh_ref, sc_ref, gt_ref, ng_ref, wq_ref, wo_ref,
                        kc_ref, vc_ref, kn_ref, vn_ref, cos_ref, sup_ref, sdn_ref, sink_ref,
                        o_ref, kout_ref, vout_ref, q_scr, *, dec_seq):
    rows = x_ref.shape[0]
    n_seq = rows // dec_seq
    n_cache = n_seq * WINDOW
    q_scr[...] = _project_q(x_ref, sh_ref, sc_ref, ng_ref, wq_ref,
                            (cos_ref[...], sup_ref[...], sdn_ref[...]), dec_seq)

    pack = DEC_ATTN_PACK
    pack_rows = pack * dec_seq

    def windows(ref, first_seq, kvh):
        return jnp.concatenate(
            [ref[(s * N_KV_HEADS + kvh) * HEAD_DIM:(s * N_KV_HEADS + kvh + 1) * HEAD_DIM, :].astype(BF16)
             for s in range(first_seq, first_seq + pack)], axis=1)

    q_row = lax.broadcasted_iota(jnp.int32, (GROUP * pack_rows, 1), 0) & (pack_rows - 1)
    q_seq, q_i = q_row >> _log2(dec_seq), q_row & (dec_seq - 1)
    c_col = lax.broadcasted_iota(jnp.int32, (1, pack * WINDOW), 1)
    valid_c = (q_seq == (c_col >> _log2(WINDOW))) & ((c_col & (WINDOW - 1)) > q_i)
    n_col = lax.broadcasted_iota(jnp.int32, (1, rows), 1)
    n_seq_col, n_visible = n_col >> _log2(dec_seq), (n_col & (dec_seq - 1)) <= q_i
    kn = [kn_ref[kvh * HEAD_DIM:(kvh + 1) * HEAD_DIM, :].astype(BF16) for kvh in range(N_KV_HEADS)]
    vn = [vn_ref[kvh * HEAD_DIM:(kvh + 1) * HEAD_DIM, :].astype(BF16) for kvh in range(N_KV_HEADS)]

    keep = lax.broadcasted_iota(jnp.int32, (1, WINDOW), 1) < WINDOW - dec_seq

    def update_windows(first_seq):
        for src_ref, new_ref, dst_ref in ((kc_ref, kn_ref, kout_ref), (vc_ref, vn_ref, vout_ref)):
            new = new_ref[...]
            for s in range(first_seq, first_seq + pack):
                blk = slice(s * KV_DIM, (s + 1) * KV_DIM)
                old = pltpu.roll(src_ref[blk, :], WINDOW - dec_seq, axis=1)
                app = pltpu.roll(new, (WINDOW - dec_seq - s * dec_seq) % rows, axis=1)
                dst_ref[blk, :] = jnp.where(keep, old, app)

    packs = []
    for first_seq in range(0, n_seq, pack):
        update_windows(first_seq)
        q_rows = slice(first_seq * dec_seq, (first_seq + pack) * dec_seq)
        valid_n = ((q_seq + first_seq) == n_seq_col) & n_visible
        heads = []
        for kvh in range(N_KV_HEADS):
            qg = _stack_heads(q_scr, q_rows, kvh)
            s_c = jnp.where(valid_c, _dot(qg, windows(kc_ref, first_seq, kvh)), NEG)
            s_n = jnp.where(valid_n, _dot(qg, kn[kvh]), NEG)
            (p_c, p_n), inv = _softmax_parts([s_c, s_n], _sink_column(sink_ref, kvh, pack_rows))
            o = (_dot_nt(p_c.astype(BF16), windows(vc_ref, first_seq, kvh))
                 + _dot_nt(p_n.astype(BF16), vn[kvh])) * inv
            heads += [o[g * pack_rows:(g + 1) * pack_rows, :] for g in range(GROUP)]
        packs.append(jnp.concatenate(heads, axis=1))
    attn = jnp.concatenate(packs, axis=0).astype(BF16)
    o_ref[...] = x_ref[...] + _mod_rows(gt_ref, dec_seq) * _dot(attn, wo_ref[...])


def _attn_sample(x, n_seq_total, dec_seq, mods, norm_g, w_q, w_o, k_cache, v_cache,
                 kv_new, tables, sinks):
    n_seq = SAMPLE_ATTN_SEQS
    rows = n_seq * dec_seq
    assert rows == WINDOW == LANES
    grid = (n_seq_total // n_seq,)
    row_spec = lambda width: pl.BlockSpec((rows, width), lambda t: (t, 0))
    mod_spec = lambda col: pl.BlockSpec((None, n_seq, D_MODEL), lambda t: (1, t, col))
    cache_spec = pl.BlockSpec((n_seq * KV_DIM, WINDOW), lambda t: (t, 0))
    new_spec = lambda part: pl.BlockSpec((KV_DIM, rows), lambda t: (part, t))
    table_spec = pl.BlockSpec((rows, LANES), lambda t: (0, 0))
    in_specs = [
        row_spec(D_MODEL), mod_spec(3), mod_spec(4), mod_spec(5),
        _resident((None, 1, D_MODEL), lambda t: (4, 0, 0)),
        _resident((D_MODEL, D_MODEL), lambda t: (0, 0)),
        _resident((D_MODEL, D_MODEL), lambda t: (0, 0)),
        cache_spec, cache_spec, new_spec(0), new_spec(1),
        table_spec, table_spec, table_spec,
        pl.BlockSpec(memory_space=pltpu.SMEM),
    ]
    cache_shape = jax.ShapeDtypeStruct((n_seq_total * KV_DIM, WINDOW), F32)
    return pl.pallas_call(
        functools.partial(_attn_sample_kernel, dec_seq=dec_seq),
        grid=grid,
        in_specs=in_specs,
        out_specs=[row_spec(D_MODEL), cache_spec, cache_spec],
        out_shape=[jax.ShapeDtypeStruct((n_seq_total * dec_seq, D_MODEL), F32),
                   cache_shape, cache_shape],
        scratch_shapes=[pltpu.VMEM((rows, D_MODEL), BF16)],
        compiler_params=pltpu.CompilerParams(
            dimension_semantics=("arbitrary",),
            vmem_limit_bytes=VMEM_LIMIT_BYTES),
        name="attn_sample",
    )(x, mods, mods, mods, norm_g, w_q, w_o, k_cache, v_cache, kv_new, kv_new, *tables, sinks)


def _rope_angles(pos):
    inv_freq = ROPE_THETA ** (-jnp.arange(0, ROT_DIM, 2, dtype=F32) / ROT_DIM)
    ang = pos.astype(F32)[:, None] * inv_freq[None, :]
    return jnp.cos(ang), jnp.sin(ang)


def _rope_tables(pos):
    half = ROT_DIM // 2
    cos, sin = _rope_angles(pos)
    n = pos.shape[0]
    rest = HEAD_DIM - ROT_DIM
    zeros = jnp.zeros((n, half), F32)
    cos_t = jnp.concatenate([cos, cos, jnp.ones((n, rest), F32)], axis=1)
    sin_up = jnp.concatenate([-sin, zeros, jnp.zeros((n, rest), F32)], axis=1)
    sin_dn = jnp.concatenate([zeros, sin, jnp.zeros((n, rest), F32)], axis=1)
    reps = LANES // HEAD_DIM
    return tuple(jnp.tile(t, (1, reps)) for t in (cos_t, sin_up, sin_dn))


def kernel(x_prompt, x_sample, state_conv, cache_k_win, cache_v_win, c_prompt, c_sample, norm_g, w_ada, b_ada, w_ffn_gate, w_ffn_up, w_ffn_down, conv_w_in, conv_w, conv_w_out, kv_norm_g, w_ada_kv, b_ada_kv, w_k, w_v, attn_w_q, attn_sinks, attn_w_o, final_norm_g):
    batch, seq, d = x_prompt.shape
    dec_batch, dec_seq, _ = x_sample.shape
    w_buf = cache_k_win.shape[1]
    past_len = 16384
    assert d == D_MODEL and w_buf == WINDOW and dec_seq == SUBLANES
    assert w_ada.shape[0] == 2 and conv_w_in.shape[0] == 1 and attn_w_q.shape[0] == 1

    pad = (-(dec_batch + batch)) % SUBLANES
    c_all = jnp.concatenate([c_sample, c_prompt, jnp.zeros((pad, d), F32)], axis=0)
    mods = _ada(c_all, w_ada, b_ada)
    mods_kv = _ada(c_all, w_ada_kv[None], b_ada_kv[None])
    prompt_mod_block = dec_batch // SUBLANES

    make_groups = lambda tile: {
        "prompt": _Group(batch, seq, None, prompt_mod_block, tile),
        "sample": _Group(dec_batch, dec_seq, dec_seq, prompt_mod_block, tile),
    }
    ffn_groups, mixer_groups = make_groups(FFN_TILE), make_groups(MIXER_TILE)
    xp, xs = x_prompt.reshape(batch * seq, d), x_sample.reshape(dec_batch * dec_seq, d)

    norm_g3 = norm_g.reshape(-1, 1, d)
    w_in, w_out = conv_w_in[0].astype(BF16), conv_w_out[0].astype(BF16)
    w_q, w_o = attn_w_q[0].astype(BF16), attn_w_o[0].astype(BF16)
    sinks = attn_sinks[0]

    pos_dec = past_len + jnp.arange(dec_seq, dtype=jnp.int32)
    pos_prompt = jnp.arange(seq, dtype=jnp.int32)
    tab_dec = tuple(jnp.tile(t, (SAMPLE_ATTN_SEQS, 1)) for t in _rope_tables(pos_dec))
    tab_prompt_t = tuple(t.T for t in _rope_angles(pos_prompt))
    tab_dec_t = tuple(jnp.tile(t.T, (1, FFN_TILE // dec_seq)) for t in _rope_angles(pos_dec))
    kv_args = dict(norm_g=kv_norm_g.reshape(1, d), mods=mods_kv,
                   w_kv_t=jnp.concatenate([w_k.T, w_v.T], axis=0).astype(BF16),
                   prompt_tables_t=tab_prompt_t, decode_tables_t=tab_dec_t)
    to_rows = lambda c: jnp.transpose(c, (0, 2, 3, 1)).reshape(dec_batch * KV_DIM, w_buf)
    from_rows = lambda c: jnp.transpose(c.reshape(dec_batch, N_KV_HEADS, HEAD_DIM, w_buf), (0, 3, 1, 2))

    ffn = functools.partial(_ffn, prompt=ffn_groups["prompt"], decode=ffn_groups["sample"], mods=mods,
                            norm_g=norm_g3, wg=w_ffn_gate, wu=w_ffn_up, wd=w_ffn_down)
    xp, xs = ffn(xp, xs, layer=0, sub=0, slot=0)
    xp, u_p = _conv_mixer(xp, mixer_groups["prompt"], mods, norm_g3, w_in, conv_w[0], w_out)
    xs, u_s = _conv_mixer(xs, mixer_groups["sample"], mods, norm_g3, w_in, conv_w[0], w_out,
                          (state_conv[0, :, 0, :], state_conv[0, :, 1, :]))
    xp, xs = ffn(xp, xs, layer=0, sub=2, slot=1)
    xp, xs, kvt_p, kvt_s = ffn(xp, xs, layer=1, sub=0, slot=0, kv=kv_args)
    xp = _attn_prompt(xp, make_groups(ATTN_TILE)["prompt"], mods, norm_g3, w_q, w_o, kvt_p, tab_prompt_t, sinks)
    xs, k_win, v_win = _attn_sample(xs, dec_batch, dec_seq, mods, norm_g3, w_q, w_o,
                                    to_rows(cache_k_win), to_rows(cache_v_win), kvt_s, tab_dec, sinks)
    y_p, y_s = ffn(xp, xs, layer=1, sub=2, slot=1, final_g=final_norm_g.reshape(1, d))

    kv_state_p = jnp.stack([kvt_p[:, (b + 1) * seq - WINDOW:(b + 1) * seq] for b in range(batch)])
    kv_state_p = jnp.transpose(kv_state_p.reshape(batch, 2, N_KV_HEADS, HEAD_DIM, WINDOW), (1, 0, 4, 2, 3))
    k_state_p, v_state_p = kv_state_p[0], kv_state_p[1]
    k_state_s, v_state_s = from_rows(k_win), from_rows(v_win)
    tail = CONV_WIDTH - 1
    conv_p = u_p.reshape(batch, SUBLANES, d)[:, SUBLANES - tail:][None]
    conv_s = u_s.reshape(dec_batch, dec_seq, d)[:, dec_seq - tail:][None]
    return (y_p.reshape(batch, seq, d), y_s.reshape(dec_batch, dec_seq, d), conv_p, conv_s,
            k_state_p, v_state_p, k_state_s, v_state_s)
```

```python
import functools

import jax
import jax.numpy as jnp
from jax import lax
from jax.experimental import pallas as pl
from jax.experimental.pallas import tpu as pltpu

F32 = jnp.float32
BF16 = jnp.bfloat16

D_MODEL = 1024
D_FF = 2816
HEAD_DIM = 64
N_HEADS = 16
N_KV_HEADS = 4
GROUP = N_HEADS // N_KV_HEADS
KV_DIM = N_KV_HEADS * HEAD_DIM
WINDOW = 128
ROT_DIM = 16
ROPE_THETA = 500000.0
CONV_WIDTH = 3
N_MOD = 9
EPS = 1e-6
NEG = -1e30
FMAX = float(jnp.finfo(jnp.float32).max)

SUBLANES = 8
LANES = 128
VMEM_LIMIT_BYTES = 60 * 1024 * 1024

ADA_COLS = 2304
FFN_TILE = 512
FFN_SUBTILES = 2
FF_STAGE = 256
MIXER_TILE = 512
ATTN_TILE = 1024
Q_ROWS = 128
ATTN_CHUNK = 256
LOG2E = 1.4426950408889634
SAMPLE_ATTN_SEQS = 16
DEC_ATTN_PACK = 4
FF_CHUNKS = ((0, 1024), (1024, 2048), (2048, D_FF))


def _dot(a, b):
    return jnp.dot(a, b, preferred_element_type=F32)


def _dot_nt(a, b):
    return lax.dot_general(a, b, (((1,), (1,)), ((), ())), preferred_element_type=F32)


def _log2(n):
    assert n & (n - 1) == 0, n
    return n.bit_length() - 1


def _silu(x):
    return x * jax.nn.sigmoid(x)


def _unit_rms(x):
    return x * lax.rsqrt(jnp.mean(x * x, axis=-1, keepdims=True) + EPS)


def _repeat_rows(ref, reps):
    n = ref.shape[1]
    return jnp.concatenate(
        [jnp.broadcast_to(ref[i:i + 1, :], (reps, n)) for i in range(ref.shape[0])], axis=0)


def _mod_rows(ref, per_seq_rows):
    if per_seq_rows is None:
        return ref[pl.ds(pl.program_id(0), 1), :]
    return _repeat_rows(ref, per_seq_rows)


def _rope(x, cos_t, sin_up_t, sin_dn_t):
    n = x.shape[1]
    reps = n // LANES
    half = ROT_DIM // 2
    cos = jnp.concatenate([cos_t] * reps, axis=1)
    sin_up = jnp.concatenate([sin_up_t] * reps, axis=1)
    sin_dn = jnp.concatenate([sin_dn_t] * reps, axis=1)
    x_up = pltpu.roll(x, n - half, axis=1)
    x_dn = pltpu.roll(x, half, axis=1)
    return x * cos + x_up * sin_up + x_dn * sin_dn


def _ada_kernel(c_ref, w_ref, b_ref, o_ref):
    a = _silu(c_ref[...]).astype(BF16)
    o_ref[...] = _dot(a, w_ref[...].astype(BF16)) + b_ref[...]


def _ada(c_all, w, b):
    n_layers, _, n = w.shape
    m = c_all.shape[0]
    tn = ADA_COLS if n % ADA_COLS == 0 else D_MODEL
    return pl.pallas_call(
        _ada_kernel,
        grid=(n_layers, n // tn),
        in_specs=[
            pl.BlockSpec((m, D_MODEL), lambda l, j: (0, 0)),
            pl.BlockSpec((None, D_MODEL, tn), lambda l, j: (l, 0, j)),
            pl.BlockSpec((None, 1, tn), lambda l, j: (l, 0, j)),
        ],
        out_specs=pl.BlockSpec((None, m, tn), lambda l, j: (l, 0, j)),
        out_shape=jax.ShapeDtypeStruct((n_layers, m, n), F32),
        compiler_params=pltpu.CompilerParams(
            dimension_semantics=("arbitrary", "arbitrary"),
            vmem_limit_bytes=VMEM_LIMIT_BYTES),
        name="adaln_tables",
    )(c_all, w, b.reshape(n_layers, 1, n))


def _rope_t(xt, cos_t, sin_t):
    half = ROT_DIM // 2
    out = []
    for h in range(xt.shape[0] // HEAD_DIM):
        base = h * HEAD_DIM
        x1 = xt[base:base + half, :]
        x2 = xt[base + half:base + ROT_DIM, :]
        out += [x1 * cos_t - x2 * sin_t, x2 * cos_t + x1 * sin_t,
                xt[base + ROT_DIM:base + HEAD_DIM, :]]
    return jnp.concatenate(out, axis=0)


def _ffn_kernel(*refs, n_prompt_tiles, tiles_per_seq, dec_seq, layer, slot, with_kv, final):
    it = iter(refs)
    take = lambda n: [next(it) for _ in range(n)]
    xp_ref, xs_ref = take(2)
    p_mods, s_mods = take(3), take(3)
    ng_ref, wg_hbm, wu_hbm, wd_hbm = take(4)
    if with_kv:
        (kvg_ref,), p_kvmods, s_kvmods = take(1), take(2), take(2)
        (wkvt_ref,) = take(1)
        p_tabs, s_tabs = take(2), take(2)
    else:
        p_kvmods = s_kvmods = None
    if final:
        (fg_ref,) = take(1)
    op_ref, os_ref = take(2)
    if with_kv:
        kvp_ref, kvs_ref = take(2)
    wg_s, wu_s, wd_s, stage_in, stage_out, sem = take(6)

    i = pl.program_id(0)
    prompt_rows = lambda ref: ref[pl.ds(i // tiles_per_seq, 1), :]
    decode_rows = lambda ref: _repeat_rows(ref, dec_seq)

    def weight_copies(c):
        cols = pl.ds(c * FF_STAGE, FF_STAGE)
        buf = c % 2
        return (
            pltpu.make_async_copy(wg_hbm.at[layer, slot, :, cols], stage_in.at[buf, 0], sem.at[buf, 0]),
            pltpu.make_async_copy(wu_hbm.at[layer, slot, :, cols], stage_in.at[buf, 1], sem.at[buf, 1]),
            pltpu.make_async_copy(wd_hbm.at[layer, slot, cols, :], stage_out.at[buf], sem.at[buf, 2]),
        )

    def partial_ffn(hb, lo, hi):
        g = _dot(hb, wg_s[:, lo:hi])
        u = _dot(hb, wu_s[:, lo:hi])
        return _dot((_silu(g) * u).astype(BF16), wd_s[lo:hi, :])

    def split_body(x_ref, o_ref, rows_of, mods, kvmods):
        sub = x_ref.shape[0] // FFN_SUBTILES
        groups = [slice(k * sub, (k + 1) * sub) for k in range(FFN_SUBTILES)]
        rows_in = lambda v, r: v if v.shape[0] == 1 else v[r]
        shift, scale, gate = (rows_of(m) for m in mods)
        xs_, xns, hbs = [], [], []
        ys = [None] * FFN_SUBTILES
        for c, (lo, hi) in enumerate(FF_CHUNKS):
            for k, r in enumerate(groups):
                if c == 0:
                    x = x_ref[r, :]
                    xn = _unit_rms(x)
                    xs_.append(x)
                    xns.append(xn)
                    hbs.append(((xn * ng_ref[...]) * (1.0 + rows_in(scale, r))
                                + rows_in(shift, r)).astype(BF16))
                part = partial_ffn(hbs[k], lo, hi)
                ys[k] = part if ys[k] is None else ys[k] + part
                if c == len(FF_CHUNKS) - 1:
                    x_new = xs_[k] + (0.5 * rows_in(gate, r)) * ys[k]
                    o_ref[r, :] = _unit_rms(x_new) * fg_ref[...] if final else x_new
        if with_kv:
            kv_shift, kv_scale = rows_of(kvmods[0]), rows_of(kvmods[1])
            return jnp.concatenate(
                [((xns[k] * kvg_ref[...]) * (1.0 + rows_in(kv_scale, r)) + rows_in(kv_shift, r)).astype(BF16)
                 for k, r in enumerate(groups)], axis=0)

    def body(x_ref, o_ref, rows_of, mods, kvmods, stage_weights):
        if not stage_weights:
            return split_body(x_ref, o_ref, rows_of, mods, kvmods)
        x = x_ref[...]
        xn = _unit_rms(x)
        hb = ((xn * ng_ref[...]) * (1.0 + rows_of(mods[1])) + rows_of(mods[0])).astype(BF16)
        y = None
        n_slices = D_FF // FF_STAGE
        for c in range(min(2, n_slices)):
            for cp in weight_copies(c):
                cp.start()
        for c in range(n_slices):
            lo, hi = c * FF_STAGE, (c + 1) * FF_STAGE
            for cp in weight_copies(c):
                cp.wait()
            wg_s[:, lo:hi] = stage_in[c % 2, 0].astype(BF16)
            wu_s[:, lo:hi] = stage_in[c % 2, 1].astype(BF16)
            wd_s[lo:hi, :] = stage_out[c % 2].astype(BF16)
            if c + 2 < n_slices:
                for cp in weight_copies(c + 2):
                    cp.start()
            part = partial_ffn(hb, lo, hi)
            y = part if y is None else y + part
        x_new = x + (0.5 * rows_of(mods[2])) * y
        o_ref[...] = _unit_rms(x_new) * fg_ref[...] if final else x_new
        if with_kv:
            return ((xn * kvg_ref[...]) * (1.0 + rows_of(kvmods[1])) + rows_of(kvmods[0])).astype(BF16)

    def emit_kv(hk, tabs, kv_ref):
        kv_t = _dot_nt(wkvt_ref[...], hk)
        kv_ref[:KV_DIM, :] = _rope_t(kv_t[:KV_DIM, :], tabs[0][...], tabs[1][...])
        kv_ref[KV_DIM:, :] = kv_t[KV_DIM:, :]

    def prompt_step(stage_weights):
        hk = body(xp_ref, op_ref, prompt_rows, p_mods, p_kvmods, stage_weights)
        if with_kv:
            emit_kv(hk, p_tabs, kvp_ref)

    @pl.when(i == 0)
    def _():
        prompt_step(True)

    @pl.when((i > 0) & (i < n_prompt_tiles))
    def _():
        prompt_step(False)

    @pl.when(i >= n_prompt_tiles)
    def _():
        hk = body(xs_ref, os_ref, decode_rows, s_mods, s_kvmods, False)
        if with_kv:
            emit_kv(hk, s_tabs, kvs_ref)


def _resident(shape, index_map):
    return pl.BlockSpec(shape, index_map, pipeline_mode=pl.Buffered(1))


def _mod_spec(layer, col, group):
    if group.per_seq_rows is None:
        return pl.BlockSpec((None, SUBLANES, D_MODEL),
                            lambda b, t: (layer, group.prompt_mod_block, col))
    return pl.BlockSpec((None, group.tile // group.per_seq_rows, D_MODEL),
                        lambda b, t: (layer, t, col))


class _Group:
    def __init__(self, n_seq, seq_len, per_seq_rows, prompt_mod_block, tile):
        self.tile = tile
        self.rows = n_seq * seq_len
        self.per_seq_rows = per_seq_rows
        self.prompt_mod_block = prompt_mod_block
        if per_seq_rows is None:
            self.grid = (n_seq, seq_len // self.tile)
        else:
            self.grid = (1, self.rows // self.tile)
        self.tiles_per_seq = self.grid[1]

    def row_spec(self, width):
        nt = self.grid[1]
        return pl.BlockSpec((self.tile, width), lambda b, t: (b * nt + t, 0))

    def col_spec(self, height):
        nt = self.grid[1]
        return pl.BlockSpec((height, self.tile), lambda b, t: (0, b * nt + t))

    def table_t_spec(self):
        if self.per_seq_rows is None:
            return pl.BlockSpec((ROT_DIM // 2, self.tile), lambda b, t: (0, t))
        return pl.BlockSpec((ROT_DIM // 2, self.tile), lambda b, t: (0, 0))

    def table_spec(self):
        if self.per_seq_rows is None:
            return pl.BlockSpec((self.tile, LANES), lambda b, t: (t, 0))
        return pl.BlockSpec((self.tile, LANES), lambda b, t: (0, 0))


def _ffn(xp, xs, prompt, decode, mods, layer, sub, norm_g, wg, wu, wd, slot, kv=None, final_g=None):
    with_kv = kv is not None
    final = final_g is not None
    tile = prompt.tile
    assert decode.tile == tile
    n_p, n_s = prompt.rows // tile, decode.rows // tile
    tiles_per_seq = prompt.tiles_per_seq
    p_tile = lambda i: jnp.minimum(i, n_p - 1)
    s_tile = lambda i: jnp.maximum(i - n_p, 0)
    n_dec_seq = tile // decode.per_seq_rows

    def mod_specs(table_layer, cols):
        p = [pl.BlockSpec((None, SUBLANES, D_MODEL),
                          functools.partial(lambda i, c: (table_layer, prompt.prompt_mod_block, c), c=c))
             for c in cols]
        s = [pl.BlockSpec((None, n_dec_seq, D_MODEL),
                          functools.partial(lambda i, c: (table_layer, s_tile(i), c), c=c),
                          pipeline_mode=pl.Buffered(1))
             for c in cols]
        return p + s

    p_rows = lambda width: pl.BlockSpec((tile, width), lambda i: (p_tile(i), 0))
    s_rows = lambda width, **kw: pl.BlockSpec((tile, width), lambda i: (s_tile(i), 0), **kw)
    p_cols = lambda height: pl.BlockSpec((height, tile), lambda i: (0, p_tile(i)))
    s_cols = lambda height: pl.BlockSpec((height, tile), lambda i: (0, s_tile(i)))
    const = lambda shape: _resident(shape, lambda i: (0,) * len(shape))
    hbm = pl.BlockSpec(memory_space=pl.ANY)

    in_specs = [p_rows(D_MODEL), s_rows(D_MODEL, pipeline_mode=pl.Buffered(1))]
    in_specs += mod_specs(layer, [3 * sub + 0, 3 * sub + 1, 3 * sub + 2])
    in_specs += [_resident((None, 1, D_MODEL), lambda i: (layer * 3 + sub, 0, 0)), hbm, hbm, hbm]
    args = [xp, xs] + [mods] * 6 + [norm_g, wg, wu, wd]
    out_specs = [p_rows(D_MODEL), s_rows(D_MODEL)]
    out_shape = [jax.ShapeDtypeStruct((prompt.rows, D_MODEL), F32),
                 jax.ShapeDtypeStruct((decode.rows, D_MODEL), F32)]
    if with_kv:
        in_specs += [const((1, D_MODEL))] + mod_specs(0, [0, 1])
        in_specs += [const((2 * KV_DIM, D_MODEL))]
        in_specs += [pl.BlockSpec((ROT_DIM // 2, tile), lambda i: (0, p_tile(i) % tiles_per_seq))] * 2
        in_specs += [const((ROT_DIM // 2, tile))] * 2
        args += [kv["norm_g"]] + [kv["mods"]] * 4 + [kv["w_kv_t"]]
        args += [*kv["prompt_tables_t"], *kv["decode_tables_t"]]
        out_specs += [p_cols(2 * KV_DIM), s_cols(2 * KV_DIM)]
        out_shape += [jax.ShapeDtypeStruct((2 * KV_DIM, prompt.rows), F32),
                      jax.ShapeDtypeStruct((2 * KV_DIM, decode.rows), F32)]
    if final:
        in_specs.append(const((1, D_MODEL)))
        args.append(final_g)
    scratch = [
        pltpu.VMEM((D_MODEL, D_FF), BF16), pltpu.VMEM((D_MODEL, D_FF), BF16),
        pltpu.VMEM((D_FF, D_MODEL), BF16),
        pltpu.VMEM((2, 2, D_MODEL, FF_STAGE), F32),
        pltpu.VMEM((2, FF_STAGE, D_MODEL), F32),
        pltpu.SemaphoreType.DMA((2, 3)),
    ]
    return pl.pallas_call(
        functools.partial(_ffn_kernel, n_prompt_tiles=n_p, tiles_per_seq=tiles_per_seq,
                          dec_seq=decode.per_seq_rows, layer=layer, slot=slot,
                          with_kv=with_kv, final=final),
        grid=(n_p + n_s,),
        in_specs=in_specs,
        out_specs=out_specs,
        out_shape=out_shape,
        scratch_shapes=scratch,
        compiler_params=pltpu.CompilerParams(
            dimension_semantics=("arbitrary",),
            vmem_limit_bytes=VMEM_LIMIT_BYTES),
        name=f"ffn_l{layer}_s{sub}",
    )(*args)


def _conv_kernel(*refs, per_seq_rows):
    it = iter(refs)
    x_ref, sh_ref, sc_ref, gt_ref, ng_ref, win_ref, cw_ref, wout_ref = (next(it) for _ in range(8))
    if per_seq_rows is not None:
        st0_ref, st1_ref = next(it), next(it)
    o_ref, u_ref = next(it), next(it)
    if per_seq_rows is None:
        carry_ref = next(it)

    tile = x_ref.shape[0]
    x = x_ref[...]
    h = ((_unit_rms(x) * ng_ref[...]) * (1.0 + _mod_rows(sc_ref, per_seq_rows))
         + _mod_rows(sh_ref, per_seq_rows)).astype(BF16)
    bcv = _dot(h, win_ref[...])
    b_g = bcv[:, :D_MODEL]
    u = bcv[:, D_MODEL:2 * D_MODEL] * bcv[:, 2 * D_MODEL:]

    row = lax.broadcasted_iota(jnp.int32, (tile, 1), 0)
    if per_seq_rows is None:
        @pl.when(pl.program_id(1) == 0)
        def _():
            carry_ref[...] = jnp.zeros_like(carry_ref)
        prev2 = carry_ref[SUBLANES - 2:SUBLANES - 1, :]
        prev1 = carry_ref[SUBLANES - 1:SUBLANES, :]
        t = row
    else:
        prev2 = _repeat_rows(st0_ref, per_seq_rows)
        prev1 = _repeat_rows(st1_ref, per_seq_rows)
        t = row & (per_seq_rows - 1)
    u1 = jnp.where(t == 0, prev1, pltpu.roll(u, 1, axis=0))
    u2 = jnp.where(t == 0, prev2, jnp.where(t == 1, prev1, pltpu.roll(u, 2, axis=0)))
    conv = u2 * cw_ref[0:1, :] + u1 * cw_ref[1:2, :] + u * cw_ref[2:3, :]
    y = _dot((b_g * conv).astype(BF16), wout_ref[...])
    o_ref[...] = x + _mod_rows(gt_ref, per_seq_rows) * y

    if per_seq_rows is None:
        tail = u[tile - SUBLANES:, :]
        carry_ref[...] = tail
        u_ref[...] = tail
    else:
        u_ref[...] = u


def _conv_mixer(x, group, mods, norm_g, w_in, conv_w, w_out, state=None):
    prompt = group.per_seq_rows is None
    in_specs = [
        group.row_spec(D_MODEL),
        _mod_spec(0, 3, group), _mod_spec(0, 4, group), _mod_spec(0, 5, group),
        _resident((None, 1, D_MODEL), lambda b, t: (1, 0, 0)),
        _resident((D_MODEL, 3 * D_MODEL), lambda b, t: (0, 0)),
        _resident((CONV_WIDTH, D_MODEL), lambda b, t: (0, 0)),
        _resident((D_MODEL, D_MODEL), lambda b, t: (0, 0)),
    ]
    args = [x, mods, mods, mods, norm_g, w_in, conv_w, w_out]
    scratch = []
    if prompt:
        u_spec = pl.BlockSpec((SUBLANES, D_MODEL), lambda b, t: (b, 0))
        u_shape = jax.ShapeDtypeStruct((group.grid[0] * SUBLANES, D_MODEL), F32)
        scratch.append(pltpu.VMEM((SUBLANES, D_MODEL), F32))
    else:
        n_seq = group.tile // group.per_seq_rows
        st_spec = pl.BlockSpec((n_seq, D_MODEL), lambda b, t: (t, 0))
        in_specs += [st_spec, st_spec]
        args += [state[0], state[1]]
        u_spec = group.row_spec(D_MODEL)
        u_shape = jax.ShapeDtypeStruct((group.rows, D_MODEL), F32)
    return pl.pallas_call(
        functools.partial(_conv_kernel, per_seq_rows=group.per_seq_rows),
        grid=group.grid,
        in_specs=in_specs,
        out_specs=[group.row_spec(D_MODEL), u_spec],
        out_shape=[jax.ShapeDtypeStruct((group.rows, D_MODEL), F32), u_shape],
        scratch_shapes=scratch,
        compiler_params=pltpu.CompilerParams(
            dimension_semantics=("arbitrary", "arbitrary"),
            vmem_limit_bytes=VMEM_LIMIT_BYTES),
        name="conv_mixer",
    )(*args)


def _sink_column(sink_ref, kv_head, q_rows):
    grp = lax.broadcasted_iota(jnp.int32, (GROUP * q_rows, 1), 0) >> _log2(q_rows)
    col = jnp.full((GROUP * q_rows, 1), sink_ref[kv_head * GROUP + GROUP - 1], F32)
    for g in range(GROUP - 2, -1, -1):
        col = jnp.where(grp == g, sink_ref[kv_head * GROUP + g], col)
    return col


def _stack_heads(q_ref, rows, kv_head):
    return jnp.concatenate(
        [q_ref[rows, (kv_head * GROUP + g) * HEAD_DIM:(kv_head * GROUP + g + 1) * HEAD_DIM]
         for g in range(GROUP)], axis=0)


def _softmax_parts(scores, sink):
    m = sink
    for s in scores:
        m = jnp.maximum(m, jnp.max(s, axis=-1, keepdims=True))
    ps = [jnp.exp(s - m) for s in scores]
    denom = jnp.exp(sink - m)
    for p in ps:
        denom = denom + jnp.sum(p, axis=-1, keepdims=True)
    return ps, 1.0 / denom


def _project_q(x_ref, sh_ref, sc_ref, ng_ref, wq_ref, tables, per_seq_rows):
    h = ((_unit_rms(x_ref[...]) * ng_ref[...]) * (1.0 + _mod_rows(sc_ref, per_seq_rows))
         + _mod_rows(sh_ref, per_seq_rows)).astype(BF16)
    q = _rope(_dot(h, wq_ref[...]), *tables)
    return (q * (HEAD_DIM ** -0.5)).astype(BF16)


def _band_caps():
    shape = (WINDOW + Q_ROWS, GROUP * Q_ROWS)
    kj = lax.broadcasted_iota(jnp.int32, shape, 0)
    qi = lax.broadcasted_iota(jnp.int32, shape, 1) & (Q_ROWS - 1)
    in_band = (kj > qi) & (kj <= qi + WINDOW)
    return tuple(jnp.where(in_band & (kj >= n * Q_ROWS), FMAX, NEG) for n in range(WINDOW // Q_ROWS + 1))


def _attn_prompt_kernel(x_ref, sh_ref, sc_ref, gt_ref, ng_ref, wq_ref, wo_ref,
                        kp_ref, kc_ref, vp_ref, vc_ref, cos_ref, sin_ref, sink_ref,
                        o_ref, q_scr, k_scr, v_scr, v_odd_scr, a_scr, cap_scr, h_scr, at_scr):
    tile = x_ref.shape[0]

    @pl.when((pl.program_id(0) == 0) & (pl.program_id(1) == 0))
    def _():
        for n, cap in enumerate(_band_caps()):
            cap_scr[n] = cap

    k_scr[:WINDOW, :] = kp_ref[...].T.astype(BF16)
    k_scr[WINDOW:, :] = kc_ref[...].T.astype(BF16)
    v_scr[:, :WINDOW] = vp_ref[...].astype(BF16)
    v_scr[:, WINDOW:] = vc_ref[...].astype(BF16)
    v_odd_scr[:, :tile + WINDOW - Q_ROWS] = v_scr[:, Q_ROWS:]
    first_tile = pl.program_id(1) == 0
    norm_gain = ng_ref[...]
    scale1 = 1.0 + _mod_rows(sc_ref, None)
    shift = _mod_rows(sh_ref, None)
    gate = _mod_rows(gt_ref, None)

    group_dims = GROUP * HEAD_DIM
    blocks_per_chunk = ATTN_CHUNK // Q_ROWS
    kv_rows = lambda kvh: slice(kvh * HEAD_DIM, (kvh + 1) * HEAD_DIM)
    block_lanes = lambda blk: slice(blk * GROUP * Q_ROWS, (blk + 1) * GROUP * Q_ROWS)

    def chunk_rows(c):
        return slice(c * ATTN_CHUNK, (c + 1) * ATTN_CHUNK)

    def normalize(c):
        rows = chunk_rows(c)
        h_scr[rows, :] = ((_unit_rms(x_ref[rows, :]) * norm_gain) * scale1 + shift).astype(BF16)

    def project_piece(c, kvh):
        rows = chunk_rows(c)
        dims = slice(kvh * group_dims, (kvh + 1) * group_dims)
        qt = _rope_t(_dot(h_scr[rows, :], wq_ref[:, dims]).T, cos_ref[:, rows], sin_ref[:, rows])
        qt = (qt * (HEAD_DIM ** -0.5 * LOG2E)).astype(BF16)
        for j in range(blocks_per_chunk):
            blk = c * blocks_per_chunk + j
            q_scr[kv_rows(kvh), block_lanes(blk)] = jnp.concatenate(
                [qt[g * HEAD_DIM:(g + 1) * HEAD_DIM, j * Q_ROWS:(j + 1) * Q_ROWS] for g in range(GROUP)],
                axis=1)

    def key_window(blk):
        return slice(blk * Q_ROWS, blk * Q_ROWS + WINDOW + Q_ROWS)

    def score(blk, kvh):
        hidden = WINDOW // Q_ROWS - blk
        cap = cap_scr[jnp.where(first_tile, hidden, 0)] if hidden > 0 else cap_scr[0]
        dims = kv_rows(kvh)
        s = jnp.minimum(_dot(k_scr[key_window(blk), dims], q_scr[dims, block_lanes(blk)]), cap)
        sink_lane = lax.broadcasted_iota(jnp.int32, (1, GROUP * Q_ROWS), 1) >> _log2(Q_ROWS)
        sink = jnp.full((1, GROUP * Q_ROWS), sink_ref[kvh * GROUP + GROUP - 1], F32)
        for g in range(GROUP - 1):
            sink = jnp.where(sink_lane == g, sink_ref[kvh * GROUP + g], sink)
        sink = LOG2E * sink
        return s, jnp.maximum(jnp.max(s, axis=0, keepdims=True), sink), sink

    def exponentiate(s, m, sink):
        p = jnp.exp2(s - m)
        return p.astype(BF16), jnp.sum(p, axis=0, keepdims=True) + jnp.exp2(sink - m)

    def weigh(blk, kvh, p, denom):
        dims = kv_rows(kvh)
        if blk % 2 == 0:
            v_win = v_scr[dims, key_window(blk)]
        else:
            v_win = v_odd_scr[dims, key_window(blk - 1)]
        a_scr[dims, block_lanes(blk)] = _dot(v_win, p) * (1.0 / denom)

    def transpose_piece(c, kvh):
        slab = a_scr[kv_rows(kvh), c * GROUP * ATTN_CHUNK:(c + 1) * GROUP * ATTN_CHUNK]
        per_head = [
            jnp.concatenate([slab[:, block_lanes(j)][:, g * Q_ROWS:(g + 1) * Q_ROWS]
                             for j in range(blocks_per_chunk)], axis=1)
            for g in range(GROUP)]
        dims = slice(kvh * group_dims, (kvh + 1) * group_dims)
        at_scr[chunk_rows(c), dims] = jnp.concatenate(per_head, axis=0).T.astype(BF16)

    def emit_piece(c, n):
        rows = chunk_rows(c)
        cols = slice(n * group_dims, (n + 1) * group_dims)
        y = _dot(at_scr[rows, :], wo_ref[:, cols])
        o_ref[rows, cols] = x_ref[rows, cols] + gate[:, cols] * y

    n_chunks = tile // ATTN_CHUNK
    n_pieces = D_MODEL // group_dims
    steps_per_chunk = blocks_per_chunk * N_KV_HEADS
    stride = steps_per_chunk // N_KV_HEADS
    steps = [(blk, kvh) for blk in range(n_chunks * blocks_per_chunk) for kvh in range(N_KV_HEADS)]
    scored, weights = {}, {}
    normalize(0)
    for kvh in range(N_KV_HEADS):
        project_piece(0, kvh)
    for i in range(len(steps) + 2):
        c, local = divmod(i, steps_per_chunk)
        if i < len(steps):
            if local == 0 and c + 1 < n_chunks:
                normalize(c + 1)
            scored[i] = score(*steps[i])
        if 0 <= i - 1 < len(steps):
            weights[i - 1] = exponentiate(*scored.pop(i - 1))
        if 0 <= i - 2 < len(steps):
            blk, kvh = steps[i - 2]
            weigh(blk, kvh, *weights.pop(i - 2))
            if blk % blocks_per_chunk == blocks_per_chunk - 1:
                transpose_piece(blk // blocks_per_chunk, kvh)
        if i < len(steps):
            if local % stride == 0 and c + 1 < n_chunks:
                project_piece(c + 1, local // stride)
            if local % stride == stride // 2 and c > 0:
                emit_piece(c - 1, local // stride)
    for n in range(n_pieces):
        emit_piece(n_chunks - 1, n)


def _attn_prompt(x, group, mods, norm_g, w_q, w_o, kv_t, tables_t, sinks):
    tile = group.tile
    nt = group.grid[1]
    blocks_per_tile = tile // WINDOW

    def prev_block(b, t):
        return jnp.maximum((b * nt + t) * blocks_per_tile - 1, 0)

    kv_cur = lambda part: pl.BlockSpec((KV_DIM, tile), lambda b, t: (part, b * nt + t))

    in_specs = [
        group.row_spec(D_MODEL),
        _mod_spec(1, 3, group), _mod_spec(1, 4, group), _mod_spec(1, 5, group),
        _resident((None, 1, D_MODEL), lambda b, t: (4, 0, 0)),
        _resident((D_MODEL, D_MODEL), lambda b, t: (0, 0)),
        _resident((D_MODEL, D_MODEL), lambda b, t: (0, 0)),
        pl.BlockSpec((KV_DIM, WINDOW), lambda b, t: (0, prev_block(b, t))), kv_cur(0),
        pl.BlockSpec((KV_DIM, WINDOW), lambda b, t: (1, prev_block(b, t))), kv_cur(1),
        group.table_t_spec(), group.table_t_spec(),
        pl.BlockSpec(memory_space=pltpu.SMEM),
    ]
    return pl.pallas_call(
        _attn_prompt_kernel,
        grid=group.grid,
        in_specs=in_specs,
        out_specs=group.row_spec(D_MODEL),
        out_shape=jax.ShapeDtypeStruct((group.rows, D_MODEL), F32),
        scratch_shapes=[
            pltpu.VMEM((KV_DIM, GROUP * tile), BF16),
            pltpu.VMEM((tile + WINDOW, KV_DIM), BF16),
            pltpu.VMEM((KV_DIM, tile + WINDOW), BF16),
            pltpu.VMEM((KV_DIM, tile + WINDOW), BF16),
            pltpu.VMEM((KV_DIM, GROUP * tile), F32),
            pltpu.VMEM((WINDOW // Q_ROWS + 1, WINDOW + Q_ROWS, GROUP * Q_ROWS), F32),
            pltpu.VMEM((tile, D_MODEL), BF16),
            pltpu.VMEM((tile, D_MODEL), BF16),
        ],
        compiler_params=pltpu.CompilerParams(
            dimension_semantics=("arbitrary", "arbitrary"),
            vmem_limit_bytes=VMEM_LIMIT_BYTES),
        name="attn_prompt",
    )(x, mods, mods, mods, norm_g, w_q, w_o, kv_t, kv_t, kv_t, kv_t, *tables_t, sinks)


def _attn_sample_kernel(x_ref, sh_ref, sc_ref, gt_ref, ng_ref, wq_ref, wo_ref,
                        kc_ref, vc_ref, kn_ref, vn_ref, cos_ref, sup_ref, sdn_ref, sink_ref,
                        o_ref, kout_ref, vout_ref, q_scr, *, dec_seq):
    rows = x_ref.shape[0]
    n_seq = rows // dec_seq
    n_cache = n_seq * WINDOW
    q_scr[...] = _project_q(x_ref, sh_ref, sc_ref, ng_ref, wq_ref,
                            (cos_ref[...], sup_ref[...], sdn_ref[...]), dec_seq)

    pack = DEC_ATTN_PACK
    pack_rows = pack * dec_seq

    def windows(ref, first_seq, kvh):
        return jnp.concatenate(
            [ref[(s * N_KV_HEADS + kvh) * HEAD_DIM:(s * N_KV_HEADS + kvh + 1) * HEAD_DIM, :].astype(BF16)
             for s in range(first_seq, first_seq + pack)], axis=1)

    q_row = lax.broadcasted_iota(jnp.int32, (GROUP * pack_rows, 1), 0) & (pack_rows - 1)
    q_seq, q_i = q_row >> _log2(dec_seq), q_row & (dec_seq - 1)
    c_col = lax.broadcasted_iota(jnp.int32, (1, pack * WINDOW), 1)
    valid_c = (q_seq == (c_col >> _log2(WINDOW))) & ((c_col & (WINDOW - 1)) > q_i)
    n_col = lax.broadcasted_iota(jnp.int32, (1, rows), 1)
    n_seq_col, n_visible = n_col >> _log2(dec_seq), (n_col & (dec_seq - 1)) <= q_i
    kn = [kn_ref[kvh * HEAD_DIM:(kvh + 1) * HEAD_DIM, :].astype(BF16) for kvh in range(N_KV_HEADS)]
    vn = [vn_ref[kvh * HEAD_DIM:(kvh + 1) * HEAD_DIM, :].astype(BF16) for kvh in range(N_KV_HEADS)]

    keep = lax.broadcasted_iota(jnp.int32, (1, WINDOW), 1) < WINDOW - dec_seq

    def update_windows(first_seq):
        for src_ref, new_ref, dst_ref in ((kc_ref, kn_ref, kout_ref), (vc_ref, vn_ref, vout_ref)):
            new = new_ref[...]
            for s in range(first_seq, first_seq + pack):
                blk = slice(s * KV_DIM, (s + 1) * KV_DIM)
                old = pltpu.roll(src_ref[blk, :], WINDOW - dec_seq, axis=1)
                app = pltpu.roll(new, (WINDOW - dec_seq - s * dec_seq) % rows, axis=1)
                dst_ref[blk, :] = jnp.where(keep, old, app)

    packs = []
    for first_seq in range(0, n_seq, pack):
        update_windows(first_seq)
        q_rows = slice(first_seq * dec_seq, (first_seq + pack) * dec_seq)
        valid_n = ((q_seq + first_seq) == n_seq_col) & n_visible
        heads = []
        for kvh in range(N_KV_HEADS):
            qg = _stack_heads(q_scr, q_rows, kvh)
            s_c = jnp.where(valid_c, _dot(qg, windows(kc_ref, first_seq, kvh)), NEG)
            s_n = jnp.where(valid_n, _dot(qg, kn[kvh]), NEG)
            (p_c, p_n), inv = _softmax_parts([s_c, s_n], _sink_column(sink_ref, kvh, pack_rows))
            o = (_dot_nt(p_c.astype(BF16), windows(vc_ref, first_seq, kvh))
                 + _dot_nt(p_n.astype(BF16), vn[kvh])) * inv
            heads += [o[g * pack_rows:(g + 1) * pack_rows, :] for g in range(GROUP)]
        packs.append(jnp.concatenate(heads, axis=1))
    attn = jnp.concatenate(packs, axis=0).astype(BF16)
    o_ref[...] = x_ref[...] + _mod_rows(gt_ref, dec_seq) * _dot(attn, wo_ref[...])


def _attn_sample(x, n_seq_total, dec_seq, mods, norm_g, w_q, w_o, k_cache, v_cache,
                 kv_new, tables, sinks):
    n_seq = SAMPLE_ATTN_SEQS
    rows = n_seq * dec_seq
    assert rows == WINDOW == LANES
    grid = (n_seq_total // n_seq,)
    row_spec = lambda width: pl.BlockSpec((rows, width), lambda t: (t, 0))
    mod_spec = lambda col: pl.BlockSpec((None, n_seq, D_MODEL), lambda t: (1, t, col))
    cache_spec = pl.BlockSpec((n_seq * KV_DIM, WINDOW), lambda t: (t, 0))
    new_spec = lambda part: pl.BlockSpec((KV_DIM, rows), lambda t: (part, t))
    table_spec = pl.BlockSpec((rows, LANES), lambda t: (0, 0))
    in_specs = [
        row_spec(D_MODEL), mod_spec(3), mod_spec(4), mod_spec(5),
        _resident((None, 1, D_MODEL), lambda t: (4, 0, 0)),
        _resident((D_MODEL, D_MODEL), lambda t: (0, 0)),
        _resident((D_MODEL, D_MODEL), lambda t: (0, 0)),
        cache_spec, cache_spec, new_spec(0), new_spec(1),
        table_spec, table_spec, table_spec,
        pl.BlockSpec(memory_space=pltpu.SMEM),
    ]
    cache_shape = jax.ShapeDtypeStruct((n_seq_total * KV_DIM, WINDOW), F32)
    return pl.pallas_call(
        functools.partial(_attn_sample_kernel, dec_seq=dec_seq),
        grid=grid,
        in_specs=in_specs,
        out_specs=[row_spec(D_MODEL), cache_spec, cache_spec],
        out_shape=[jax.ShapeDtypeStruct((n_seq_total * dec_seq, D_MODEL), F32),
                   cache_shape, cache_shape],
        scratch_shapes=[pltpu.VMEM((rows, D_MODEL), BF16)],
        compiler_params=pltpu.CompilerParams(
            dimension_semantics=("arbitrary",),
            vmem_limit_bytes=VMEM_LIMIT_BYTES),
        name="attn_sample",
    )(x, mods, mods, mods, norm_g, w_q, w_o, k_cache, v_cache, kv_new, kv_new, *tables, sinks)


def _rope_angles(pos):
    inv_freq = ROPE_THETA ** (-jnp.arange(0, ROT_DIM, 2, dtype=F32) / ROT_DIM)
    ang = pos.astype(F32)[:, None] * inv_freq[None, :]
    return jnp.cos(ang), jnp.sin(ang)


def _rope_tables(pos):
    half = ROT_DIM // 2
    cos, sin = _rope_angles(pos)
    n = pos.shape[0]
    rest = HEAD_DIM - ROT_DIM
    zeros = jnp.zeros((n, half), F32)
    cos_t = jnp.concatenate([cos, cos, jnp.ones((n, rest), F32)], axis=1)
    sin_up = jnp.concatenate([-sin, zeros, jnp.zeros((n, rest), F32)], axis=1)
    sin_dn = jnp.concatenate([zeros, sin, jnp.zeros((n, rest), F32)], axis=1)
    reps = LANES // HEAD_DIM
    return tuple(jnp.tile(t, (1, reps)) for t in (cos_t, sin_up, sin_dn))


def kernel(x_prompt, x_sample, state_conv, cache_k_win, cache_v_win, c_prompt, c_sample, norm_g, w_ada, b_ada, w_ffn_gate, w_ffn_up, w_ffn_down, conv_w_in, conv_w, conv_w_out, kv_norm_g, w_ada_kv, b_ada_kv, w_k, w_v, attn_w_q, attn_sinks, attn_w_o, final_norm_g):
    batch, seq, d = x_prompt.shape
    dec_batch, dec_seq, _ = x_sample.shape
    w_buf = cache_k_win.shape[1]
    past_len = 16384
    assert d == D_MODEL and w_buf == WINDOW and dec_seq == SUBLANES
    assert w_ada.shape[0] == 2 and conv_w_in.shape[0] == 1 and attn_w_q.shape[0] == 1

    pad = (-(dec_batch + batch)) % SUBLANES
    c_all = jnp.concatenate([c_sample, c_prompt, jnp.zeros((pad, d), F32)], axis=0)
    mods = _ada(c_all, w_ada, b_ada)
    mods_kv = _ada(c_all, w_ada_kv[None], b_ada_kv[None])
    prompt_mod_block = dec_batch // SUBLANES

    make_groups = lambda tile: {
        "prompt": _Group(batch, seq, None, prompt_mod_block, tile),
        "sample": _Group(dec_batch, dec_seq, dec_seq, prompt_mod_block, tile),
    }
    ffn_groups, mixer_groups = make_groups(FFN_TILE), make_groups(MIXER_TILE)
    xp, xs = x_prompt.reshape(batch * seq, d), x_sample.reshape(dec_batch * dec_seq, d)

    norm_g3 = norm_g.reshape(-1, 1, d)
    w_in, w_out = conv_w_in[0].astype(BF16), conv_w_out[0].astype(BF16)
    w_q, w_o = attn_w_q[0].astype(BF16), attn_w_o[0].astype(BF16)
    sinks = attn_sinks[0]

    pos_dec = past_len + jnp.arange(dec_seq, dtype=jnp.int32)
    pos_prompt = jnp.arange(seq, dtype=jnp.int32)
    tab_dec = tuple(jnp.tile(t, (SAMPLE_ATTN_SEQS, 1)) for t in _rope_tables(pos_dec))
    tab_prompt_t = tuple(t.T for t in _rope_angles(pos_prompt))
    tab_dec_t = tuple(jnp.tile(t.T, (1, FFN_TILE // dec_seq)) for t in _rope_angles(pos_dec))
    kv_args = dict(norm_g=kv_norm_g.reshape(1, d), mods=mods_kv,
                   w_kv_t=jnp.concatenate([w_k.T, w_v.T], axis=0).astype(BF16),
                   prompt_tables_t=tab_prompt_t, decode_tables_t=tab_dec_t)
    to_rows = lambda c: jnp.transpose(c, (0, 2, 3, 1)).reshape(dec_batch * KV_DIM, w_buf)
    from_rows = lambda c: jnp.transpose(c.reshape(dec_batch, N_KV_HEADS, HEAD_DIM, w_buf), (0, 3, 1, 2))

    ffn = functools.partial(_ffn, prompt=ffn_groups["prompt"], decode=ffn_groups["sample"], mods=mods,
                            norm_g=norm_g3, wg=w_ffn_gate, wu=w_ffn_up, wd=w_ffn_down)
    xp, xs = ffn(xp, xs, layer=0, sub=0, slot=0)
    xp, u_p = _conv_mixer(xp, mixer_groups["prompt"], mods, norm_g3, w_in, conv_w[0], w_out)
    xs, u_s = _conv_mixer(xs, mixer_groups["sample"], mods, norm_g3, w_in, conv_w[0], w_out,
                          (state_conv[0, :, 0, :], state_conv[0, :, 1, :]))
    xp, xs = ffn(xp, xs, layer=0, sub=2, slot=1)
    xp, xs, kvt_p, kvt_s = ffn(xp, xs, layer=1, sub=0, slot=0, kv=kv_args)
    xp = _attn_prompt(xp, make_groups(ATTN_TILE)["prompt"], mods, norm_g3, w_q, w_o, kvt_p, tab_prompt_t, sinks)
    xs, k_win, v_win = _attn_sample(xs, dec_batch, dec_seq, mods, norm_g3, w_q, w_o,
                                    to_rows(cache_k_win), to_rows(cache_v_win), kvt_s, tab_dec, sinks)
    y_p, y_s = ffn(xp, xs, layer=1, sub=2, slot=1, final_g=final_norm_g.reshape(1, d))

    kv_state_p = jnp.stack([kvt_p[:, (b + 1) * seq - WINDOW:(b + 1) * seq] for b in range(batch)])
    kv_state_p = jnp.transpose(kv_state_p.reshape(batch, 2, N_KV_HEADS, HEAD_DIM, WINDOW), (1, 0, 4, 2, 3))
    k_state_p, v_state_p = kv_state_p[0], kv_state_p[1]
    k_state_s, v_state_s = from_rows(k_win), from_rows(v_win)
    tail = CONV_WIDTH - 1
    conv_p = u_p.reshape(batch, SUBLANES, d)[:, SUBLANES - tail:][None]
    conv_s = u_s.reshape(dec_batch, dec_seq, d)[:, dec_seq - tail:][None]
    return (y_p.reshape(batch, seq, d), y_s.reshape(dec_batch, dec_seq, d), conv_p, conv_s,
            k_state_p, v_state_p, k_state_s, v_state_s)
```

```python
import functools

import jax
import jax.numpy as jnp
from jax import lax
from jax.experimental import pallas as pl
from jax.experimental.pallas import tpu as pltpu

F32 = jnp.float32
BF16 = jnp.bfloat16

D_MODEL = 1024
D_FF = 2816
HEAD_DIM = 64
N_HEADS = 16
N_KV_HEADS = 4
GROUP = N_HEADS // N_KV_HEADS
KV_DIM = N_KV_HEADS * HEAD_DIM
WINDOW = 128
ROT_DIM = 16
ROPE_THETA = 500000.0
CONV_WIDTH = 3
N_MOD = 9
EPS = 1e-6
NEG = -1e30
FMAX = float(jnp.finfo(jnp.float32).max)

SUBLANES = 8
LANES = 128
VMEM_LIMIT_BYTES = 60 * 1024 * 1024

ADA_COLS = 2304
FFN_TILE = 512
FFN_SUBTILES = 2
FF_STAGE = 256
MIXER_TILE = 512
ATTN_TILE = 1024
Q_ROWS = 64
ATTN_CHUNK = 256
LOG2E = 1.4426950408889634
SAMPLE_ATTN_SEQS = 16
DEC_ATTN_PACK = 4
FF_CHUNKS = ((0, 1024), (1024, 2048), (2048, D_FF))


def _dot(a, b):
    return jnp.dot(a, b, preferred_element_type=F32)


def _dot_nt(a, b):
    return lax.dot_general(a, b, (((1,), (1,)), ((), ())), preferred_element_type=F32)


def _log2(n):
    assert n & (n - 1) == 0, n
    return n.bit_length() - 1


def _silu(x):
    return x * jax.nn.sigmoid(x)


def _unit_rms(x):
    return x * lax.rsqrt(jnp.mean(x * x, axis=-1, keepdims=True) + EPS)


def _repeat_rows(ref, reps):
    n = ref.shape[1]
    return jnp.concatenate(
        [jnp.broadcast_to(ref[i:i + 1, :], (reps, n)) for i in range(ref.shape[0])], axis=0)


def _mod_rows(ref, per_seq_rows):
    if per_seq_rows is None:
        return ref[pl.ds(pl.program_id(0), 1), :]
    return _repeat_rows(ref, per_seq_rows)


def _rope(x, cos_t, sin_up_t, sin_dn_t):
    n = x.shape[1]
    reps = n // LANES
    half = ROT_DIM // 2
    cos = jnp.concatenate([cos_t] * reps, axis=1)
    sin_up = jnp.concatenate([sin_up_t] * reps, axis=1)
    sin_dn = jnp.concatenate([sin_dn_t] * reps, axis=1)
    x_up = pltpu.roll(x, n - half, axis=1)
    x_dn = pltpu.roll(x, half, axis=1)
    return x * cos + x_up * sin_up + x_dn * sin_dn


def _ada_kernel(c_ref, w_ref, b_ref, o_ref):
    a = _silu(c_ref[...]).astype(BF16)
    o_ref[...] = _dot(a, w_ref[...].astype(BF16)) + b_ref[...]


def _ada(c_all, w, b):
    n_layers, _, n = w.shape
    m = c_all.shape[0]
    tn = ADA_COLS if n % ADA_COLS == 0 else D_MODEL
    return pl.pallas_call(
        _ada_kernel,
        grid=(n_layers, n // tn),
        in_specs=[
            pl.BlockSpec((m, D_MODEL), lambda l, j: (0, 0)),
            pl.BlockSpec((None, D_MODEL, tn), lambda l, j: (l, 0, j)),
            pl.BlockSpec((None, 1, tn), lambda l, j: (l, 0, j)),
        ],
        out_specs=pl.BlockSpec((None, m, tn), lambda l, j: (l, 0, j)),
        out_shape=jax.ShapeDtypeStruct((n_layers, m, n), F32),
        compiler_params=pltpu.CompilerParams(
            dimension_semantics=("arbitrary", "arbitrary"),
            vmem_limit_bytes=VMEM_LIMIT_BYTES),
        name="adaln_tables",
    )(c_all, w, b.reshape(n_layers, 1, n))


def _rope_t(xt, cos_t, sin_t):
    half = ROT_DIM // 2
    out = []
    for h in range(xt.shape[0] // HEAD_DIM):
        base = h * HEAD_DIM
        x1 = xt[base:base + half, :]
        x2 = xt[base + half:base + ROT_DIM, :]
        out += [x1 * cos_t - x2 * sin_t, x2 * cos_t + x1 * sin_t,
                xt[base + ROT_DIM:base + HEAD_DIM, :]]
    return jnp.concatenate(out, axis=0)


def _ffn_kernel(*refs, n_prompt_tiles, tiles_per_seq, dec_seq, layer, slot, with_kv, final):
    it = iter(refs)
    take = lambda n: [next(it) for _ in range(n)]
    xp_ref, xs_ref = take(2)
    p_mods, s_mods = take(3), take(3)
    ng_ref, wg_hbm, wu_hbm, wd_hbm = take(4)
    if with_kv:
        (kvg_ref,), p_kvmods, s_kvmods = take(1), take(2), take(2)
        (wkvt_ref,) = take(1)
        p_tabs, s_tabs = take(2), take(2)
    else:
        p_kvmods = s_kvmods = None
    if final:
        (fg_ref,) = take(1)
    op_ref, os_ref = take(2)
    if with_kv:
        kvp_ref, kvs_ref = take(2)
    wg_s, wu_s, wd_s, stage_in, stage_out, sem = take(6)

    i = pl.program_id(0)
    prompt_rows = lambda ref: ref[pl.ds(i // tiles_per_seq, 1), :]
    decode_rows = lambda ref: _repeat_rows(ref, dec_seq)

    def weight_copies(c):
        cols = pl.ds(c * FF_STAGE, FF_STAGE)
        buf = c % 2
        return (
            pltpu.make_async_copy(wg_hbm.at[layer, slot, :, cols], stage_in.at[buf, 0], sem.at[buf, 0]),
            pltpu.make_async_copy(wu_hbm.at[layer, slot, :, cols], stage_in.at[buf, 1], sem.at[buf, 1]),
            pltpu.make_async_copy(wd_hbm.at[layer, slot, cols, :], stage_out.at[buf], sem.at[buf, 2]),
        )

    def partial_ffn(hb, lo, hi):
        g = _dot(hb, wg_s[:, lo:hi])
        u = _dot(hb, wu_s[:, lo:hi])
        return _dot((_silu(g) * u).astype(BF16), wd_s[lo:hi, :])

    def split_body(x_ref, o_ref, rows_of, mods, kvmods):
        sub = x_ref.shape[0] // FFN_SUBTILES
        groups = [slice(k * sub, (k + 1) * sub) for k in range(FFN_SUBTILES)]
        rows_in = lambda v, r: v if v.shape[0] == 1 else v[r]
        shift, scale, gate = (rows_of(m) for m in mods)
        xs_, xns, hbs = [], [], []
        ys = [None] * FFN_SUBTILES
        for c, (lo, hi) in enumerate(FF_CHUNKS):
            for k, r in enumerate(groups):
                if c == 0:
                    x = x_ref[r, :]
                    xn = _unit_rms(x)
                    xs_.append(x)
                    xns.append(xn)
                    hbs.append(((xn * ng_ref[...]) * (1.0 + rows_in(scale, r))
                                + rows_in(shift, r)).astype(BF16))
                part = partial_ffn(hbs[k], lo, hi)
                ys[k] = part if ys[k] is None else ys[k] + part
                if c == len(FF_CHUNKS) - 1:
                    x_new = xs_[k] + (0.5 * rows_in(gate, r)) * ys[k]
                    o_ref[r, :] = _unit_rms(x_new) * fg_ref[...] if final else x_new
        if with_kv:
            kv_shift, kv_scale = rows_of(kvmods[0]), rows_of(kvmods[1])
            return jnp.concatenate(
                [((xns[k] * kvg_ref[...]) * (1.0 + rows_in(kv_scale, r)) + rows_in(kv_shift, r)).astype(BF16)
                 for k, r in enumerate(groups)], axis=0)

    def body(x_ref, o_ref, rows_of, mods, kvmods, stage_weights):
        if not stage_weights:
            return split_body(x_ref, o_ref, rows_of, mods, kvmods)
        x = x_ref[...]
        xn = _unit_rms(x)
        hb = ((xn * ng_ref[...]) * (1.0 + rows_of(mods[1])) + rows_of(mods[0])).astype(BF16)
        y = None
        n_slices = D_FF // FF_STAGE
        for c in range(min(2, n_slices)):
            for cp in weight_copies(c):
                cp.start()
        for c in range(n_slices):
            lo, hi = c * FF_STAGE, (c + 1) * FF_STAGE
            for cp in weight_copies(c):
                cp.wait()
            wg_s[:, lo:hi] = stage_in[c % 2, 0].astype(BF16)
            wu_s[:, lo:hi] = stage_in[c % 2, 1].astype(BF16)
            wd_s[lo:hi, :] = stage_out[c % 2].astype(BF16)
            if c + 2 < n_slices:
                for cp in weight_copies(c + 2):
                    cp.start()
            part = partial_ffn(hb, lo, hi)
            y = part if y is None else y + part
        x_new = x + (0.5 * rows_of(mods[2])) * y
        o_ref[...] = _unit_rms(x_new) * fg_ref[...] if final else x_new
        if with_kv:
            return ((xn * kvg_ref[...]) * (1.0 + rows_of(kvmods[1])) + rows_of(kvmods[0])).astype(BF16)

    def emit_kv(hk, tabs, kv_ref):
        kv_t = _dot_nt(wkvt_ref[...], hk)
        kv_ref[:KV_DIM, :] = _rope_t(kv_t[:KV_DIM, :], tabs[0][...], tabs[1][...])
        kv_ref[KV_DIM:, :] = kv_t[KV_DIM:, :]

    def prompt_step(stage_weights):
        hk = body(xp_ref, op_ref, prompt_rows, p_mods, p_kvmods, stage_weights)
        if with_kv:
            emit_kv(hk, p_tabs, kvp_ref)

    @pl.when(i == 0)
    def _():
        prompt_step(True)

    @pl.when((i > 0) & (i < n_prompt_tiles))
    def _():
        prompt_step(False)

    @pl.when(i >= n_prompt_tiles)
    def _():
        hk = body(xs_ref, os_ref, decode_rows, s_mods, s_kvmods, False)
        if with_kv:
            emit_kv(hk, s_tabs, kvs_ref)


def _resident(shape, index_map):
    return pl.BlockSpec(shape, index_map, pipeline_mode=pl.Buffered(1))


def _mod_spec(layer, col, group):
    if group.per_seq_rows is None:
        return pl.BlockSpec((None, SUBLANES, D_MODEL),
                            lambda b, t: (layer, group.prompt_mod_block, col))
    return pl.BlockSpec((None, group.tile // group.per_seq_rows, D_MODEL),
                        lambda b, t: (layer, t, col))


class _Group:
    def __init__(self, n_seq, seq_len, per_seq_rows, prompt_mod_block, tile):
        self.tile = tile
        self.rows = n_seq * seq_len
        self.per_seq_rows = per_seq_rows
        self.prompt_mod_block = prompt_mod_block
        if per_seq_rows is None:
            self.grid = (n_seq, seq_len // self.tile)
        else:
            self.grid = (1, self.rows // self.tile)
        self.tiles_per_seq = self.grid[1]

    def row_spec(self, width):
        nt = self.grid[1]
        return pl.BlockSpec((self.tile, width), lambda b, t: (b * nt + t, 0))

    def col_spec(self, height):
        nt = self.grid[1]
        return pl.BlockSpec((height, self.tile), lambda b, t: (0, b * nt + t))

    def table_t_spec(self):
        if self.per_seq_rows is None:
            return pl.BlockSpec((ROT_DIM // 2, self.tile), lambda b, t: (0, t))
        return pl.BlockSpec((ROT_DIM // 2, self.tile), lambda b, t: (0, 0))

    def table_spec(self):
        if self.per_seq_rows is None:
            return pl.BlockSpec((self.tile, LANES), lambda b, t: (t, 0))
        return pl.BlockSpec((self.tile, LANES), lambda b, t: (0, 0))


def _ffn(xp, xs, prompt, decode, mods, layer, sub, norm_g, wg, wu, wd, slot, kv=None, final_g=None):
    with_kv = kv is not None
    final = final_g is not None
    tile = prompt.tile
    assert decode.tile == tile
    n_p, n_s = prompt.rows // tile, decode.rows // tile
    tiles_per_seq = prompt.tiles_per_seq
    p_tile = lambda i: jnp.minimum(i, n_p - 1)
    s_tile = lambda i: jnp.maximum(i - n_p, 0)
    n_dec_seq = tile // decode.per_seq_rows

    def mod_specs(table_layer, cols):
        p = [pl.BlockSpec((None, SUBLANES, D_MODEL),
                          functools.partial(lambda i, c: (table_layer, prompt.prompt_mod_block, c), c=c))
             for c in cols]
        s = [pl.BlockSpec((None, n_dec_seq, D_MODEL),
                          functools.partial(lambda i, c: (table_layer, s_tile(i), c), c=c),
                          pipeline_mode=pl.Buffered(1))
             for c in cols]
        return p + s

    p_rows = lambda width: pl.BlockSpec((tile, width), lambda i: (p_tile(i), 0))
    s_rows = lambda width, **kw: pl.BlockSpec((tile, width), lambda i: (s_tile(i), 0), **kw)
    p_cols = lambda height: pl.BlockSpec((height, tile), lambda i: (0, p_tile(i)))
    s_cols = lambda height: pl.BlockSpec((height, tile), lambda i: (0, s_tile(i)))
    const = lambda shape: _resident(shape, lambda i: (0,) * len(shape))
    hbm = pl.BlockSpec(memory_space=pl.ANY)

    in_specs = [p_rows(D_MODEL), s_rows(D_MODEL, pipeline_mode=pl.Buffered(1))]
    in_specs += mod_specs(layer, [3 * sub + 0, 3 * sub + 1, 3 * sub + 2])
    in_specs += [_resident((None, 1, D_MODEL), lambda i: (layer * 3 + sub, 0, 0)), hbm, hbm, hbm]
    args = [xp, xs] + [mods] * 6 + [norm_g, wg, wu, wd]
    out_specs = [p_rows(D_MODEL), s_rows(D_MODEL)]
    out_shape = [jax.ShapeDtypeStruct((prompt.rows, D_MODEL), F32),
                 jax.ShapeDtypeStruct((decode.rows, D_MODEL), F32)]
    if with_kv:
        in_specs += [const((1, D_MODEL))] + mod_specs(0, [0, 1])
        in_specs += [const((2 * KV_DIM, D_MODEL))]
        in_specs += [pl.BlockSpec((ROT_DIM // 2, tile), lambda i: (0, p_tile(i) % tiles_per_seq))] * 2
        in_specs += [const((ROT_DIM // 2, tile))] * 2
        args += [kv["norm_g"]] + [kv["mods"]] * 4 + [kv["w_kv_t"]]
        args += [*kv["prompt_tables_t"], *kv["decode_tables_t"]]
        out_specs += [p_cols(2 * KV_DIM), s_cols(2 * KV_DIM)]
        out_shape += [jax.ShapeDtypeStruct((2 * KV_DIM, prompt.rows), F32),
                      jax.ShapeDtypeStruct((2 * KV_DIM, decode.rows), F32)]
    if final:
        in_specs.append(const((1, D_MODEL)))
        args.append(final_g)
    scratch = [
        pltpu.VMEM((D_MODEL, D_FF), BF16), pltpu.VMEM((D_MODEL, D_FF), BF16),
        pltpu.VMEM((D_FF, D_MODEL), BF16),
        pltpu.VMEM((2, 2, D_MODEL, FF_STAGE), F32),
        pltpu.VMEM((2, FF_STAGE, D_MODEL), F32),
        pltpu.SemaphoreType.DMA((2, 3)),
    ]
    return pl.pallas_call(
        functools.partial(_ffn_kernel, n_prompt_tiles=n_p, tiles_per_seq=tiles_per_seq,
                          dec_seq=decode.per_seq_rows, layer=layer, slot=slot,
                          with_kv=with_kv, final=final),
        grid=(n_p + n_s,),
        in_specs=in_specs,
        out_specs=out_specs,
        out_shape=out_shape,
        scratch_shapes=scratch,
        compiler_params=pltpu.CompilerParams(
            dimension_semantics=("arbitrary",),
            vmem_limit_bytes=VMEM_LIMIT_BYTES),
        name=f"ffn_l{layer}_s{sub}",
    )(*args)


def _conv_kernel(*refs, per_seq_rows):
    it = iter(refs)
    x_ref, sh_ref, sc_ref, gt_ref, ng_ref, win_ref, cw_ref, wout_ref = (next(it) for _ in range(8))
    if per_seq_rows is not None:
        st0_ref, st1_ref = next(it), next(it)
    o_ref, u_ref = next(it), next(it)
    if per_seq_rows is None:
        carry_ref = next(it)

    tile = x_ref.shape[0]
    x = x_ref[...]
    h = ((_unit_rms(x) * ng_ref[...]) * (1.0 + _mod_rows(sc_ref, per_seq_rows))
         + _mod_rows(sh_ref, per_seq_rows)).astype(BF16)
    bcv = _dot(h, win_ref[...])
    b_g = bcv[:, :D_MODEL]
    u = bcv[:, D_MODEL:2 * D_MODEL] * bcv[:, 2 * D_MODEL:]

    row = lax.broadcasted_iota(jnp.int32, (tile, 1), 0)
    if per_seq_rows is None:
        @pl.when(pl.program_id(1) == 0)
        def _():
            carry_ref[...] = jnp.zeros_like(carry_ref)
        prev2 = carry_ref[SUBLANES - 2:SUBLANES - 1, :]
        prev1 = carry_ref[SUBLANES - 1:SUBLANES, :]
        t = row
    else:
        prev2 = _repeat_rows(st0_ref, per_seq_rows)
        prev1 = _repeat_rows(st1_ref, per_seq_rows)
        t = row & (per_seq_rows - 1)
    u1 = jnp.where(t == 0, prev1, pltpu.roll(u, 1, axis=0))
    u2 = jnp.where(t == 0, prev2, jnp.where(t == 1, prev1, pltpu.roll(u, 2, axis=0)))
    conv = u2 * cw_ref[0:1, :] + u1 * cw_ref[1:2, :] + u * cw_ref[2:3, :]
    y = _dot((b_g * conv).astype(BF16), wout_ref[...])
    o_ref[...] = x + _mod_rows(gt_ref, per_seq_rows) * y

    if per_seq_rows is None:
        tail = u[tile - SUBLANES:, :]
        carry_ref[...] = tail
        u_ref[...] = tail
    else:
        u_ref[...] = u


def _conv_mixer(x, group, mods, norm_g, w_in, conv_w, w_out, state=None):
    prompt = group.per_seq_rows is None
    in_specs = [
        group.row_spec(D_MODEL),
        _mod_spec(0, 3, group), _mod_spec(0, 4, group), _mod_spec(0, 5, group),
        _resident((None, 1, D_MODEL), lambda b, t: (1, 0, 0)),
        _resident((D_MODEL, 3 * D_MODEL), lambda b, t: (0, 0)),
        _resident((CONV_WIDTH, D_MODEL), lambda b, t: (0, 0)),
        _resident((D_MODEL, D_MODEL), lambda b, t: (0, 0)),
    ]
    args = [x, mods, mods, mods, norm_g, w_in, conv_w, w_out]
    scratch = []
    if prompt:
        u_spec = pl.BlockSpec((SUBLANES, D_MODEL), lambda b, t: (b, 0))
        u_shape = jax.ShapeDtypeStruct((group.grid[0] * SUBLANES, D_MODEL), F32)
        scratch.append(pltpu.VMEM((SUBLANES, D_MODEL), F32))
    else:
        n_seq = group.tile // group.per_seq_rows
        st_spec = pl.BlockSpec((n_seq, D_MODEL), lambda b, t: (t, 0))
        in_specs += [st_spec, st_spec]
        args += [state[0], state[1]]
        u_spec = group.row_spec(D_MODEL)
        u_shape = jax.ShapeDtypeStruct((group.rows, D_MODEL), F32)
    return pl.pallas_call(
        functools.partial(_conv_kernel, per_seq_rows=group.per_seq_rows),
        grid=group.grid,
        in_specs=in_specs,
        out_specs=[group.row_spec(D_MODEL), u_spec],
        out_shape=[jax.ShapeDtypeStruct((group.rows, D_MODEL), F32), u_shape],
        scratch_shapes=scratch,
        compiler_params=pltpu.CompilerParams(
            dimension_semantics=("arbitrary", "arbitrary"),
            vmem_limit_bytes=VMEM_LIMIT_BYTES),
        name="conv_mixer",
    )(*args)


def _sink_column(sink_ref, kv_head, q_rows):
    grp = lax.broadcasted_iota(jnp.int32, (GROUP * q_rows, 1), 0) >> _log2(q_rows)
    col = jnp.full((GROUP * q_rows, 1), sink_ref[kv_head * GROUP + GROUP - 1], F32)
    for g in range(GROUP - 2, -1, -1):
        col = jnp.where(grp == g, sink_ref[kv_head * GROUP + g], col)
    return col


def _stack_heads(q_ref, rows, kv_head):
    return jnp.concatenate(
        [q_ref[rows, (kv_head * GROUP + g) * HEAD_DIM:(kv_head * GROUP + g + 1) * HEAD_DIM]
         for g in range(GROUP)], axis=0)


def _softmax_parts(scores, sink):
    m = sink
    for s in scores:
        m = jnp.maximum(m, jnp.max(s, axis=-1, keepdims=True))
    ps = [jnp.exp(s - m) for s in scores]
    denom = jnp.exp(sink - m)
    for p in ps:
        denom = denom + jnp.sum(p, axis=-1, keepdims=True)
    return ps, 1.0 / denom


def _project_q(x_ref, sh_ref, sc_ref, ng_ref, wq_ref, tables, per_seq_rows):
    h = ((_unit_rms(x_ref[...]) * ng_ref[...]) * (1.0 + _mod_rows(sc_ref, per_seq_rows))
         + _mod_rows(sh_ref, per_seq_rows)).astype(BF16)
    q = _rope(_dot(h, wq_ref[...]), *tables)
    return (q * (HEAD_DIM ** -0.5)).astype(BF16)


def _band_caps():
    shape = (WINDOW + Q_ROWS, GROUP * Q_ROWS)
    kj = lax.broadcasted_iota(jnp.int32, shape, 0)
    qi = lax.broadcasted_iota(jnp.int32, shape, 1) & (Q_ROWS - 1)
    in_band = (kj > qi) & (kj <= qi + WINDOW)
    return tuple(jnp.where(in_band & (kj >= n * Q_ROWS), FMAX, NEG) for n in range(WINDOW // Q_ROWS + 1))


def _attn_prompt_kernel(x_ref, sh_ref, sc_ref, gt_ref, ng_ref, wq_ref, wo_ref,
                        kp_ref, kc_ref, vp_ref, vc_ref, cos_ref, sin_ref, sink_ref,
                        o_ref, q_scr, k_scr, v_scr, v_odd_scr, a_scr, cap_scr, h_scr, at_scr):
    tile = x_ref.shape[0]

    @pl.when((pl.program_id(0) == 0) & (pl.program_id(1) == 0))
    def _():
        for n, cap in enumerate(_band_caps()):
            cap_scr[n] = cap

    k_scr[:WINDOW, :] = kp_ref[...].T.astype(BF16)
    k_scr[WINDOW:, :] = kc_ref[...].T.astype(BF16)
    v_scr[:, :WINDOW] = vp_ref[...].astype(BF16)
    v_scr[:, WINDOW:] = vc_ref[...].astype(BF16)
    v_odd_scr[:, :tile + WINDOW - Q_ROWS] = v_scr[:, Q_ROWS:]
    first_tile = pl.program_id(1) == 0
    norm_gain = ng_ref[...]
    scale1 = 1.0 + _mod_rows(sc_ref, None)
    shift = _mod_rows(sh_ref, None)
    gate = _mod_rows(gt_ref, None)

    group_dims = GROUP * HEAD_DIM
    blocks_per_chunk = ATTN_CHUNK // Q_ROWS
    kv_rows = lambda kvh: slice(kvh * HEAD_DIM, (kvh + 1) * HEAD_DIM)
    block_lanes = lambda blk: slice(blk * GROUP * Q_ROWS, (blk + 1) * GROUP * Q_ROWS)

    def chunk_rows(c):
        return slice(c * ATTN_CHUNK, (c + 1) * ATTN_CHUNK)

    def normalize(c):
        rows = chunk_rows(c)
        h_scr[rows, :] = ((_unit_rms(x_ref[rows, :]) * norm_gain) * scale1 + shift).astype(BF16)

    def project_piece(c, kvh):
        rows = chunk_rows(c)
        dims = slice(kvh * group_dims, (kvh + 1) * group_dims)
        qt = _rope_t(_dot(h_scr[rows, :], wq_ref[:, dims]).T, cos_ref[:, rows], sin_ref[:, rows])
        qt = (qt * (HEAD_DIM ** -0.5 * LOG2E)).astype(BF16)
        for j in range(blocks_per_chunk):
            blk = c * blocks_per_chunk + j
            q_scr[kv_rows(kvh), block_lanes(blk)] = jnp.concatenate(
                [qt[g * HEAD_DIM:(g + 1) * HEAD_DIM, j * Q_ROWS:(j + 1) * Q_ROWS] for g in range(GROUP)],
                axis=1)

    def key_window(blk):
        return slice(blk * Q_ROWS, blk * Q_ROWS + WINDOW + Q_ROWS)

    def score(blk, kvh):
        hidden = WINDOW // Q_ROWS - blk
        cap = cap_scr[jnp.where(first_tile, hidden, 0)] if hidden > 0 else cap_scr[0]
        dims = kv_rows(kvh)
        s = jnp.minimum(_dot(k_scr[key_window(blk), dims], q_scr[dims, block_lanes(blk)]), cap)
        sink_lane = lax.broadcasted_iota(jnp.int32, (1, GROUP * Q_ROWS), 1) >> _log2(Q_ROWS)
        sink = jnp.full((1, GROUP * Q_ROWS), sink_ref[kvh * GROUP + GROUP - 1], F32)
        for g in range(GROUP - 1):
            sink = jnp.where(sink_lane == g, sink_ref[kvh * GROUP + g], sink)
        sink = LOG2E * sink
        return s, jnp.maximum(jnp.max(s, axis=0, keepdims=True), sink), sink

    def exponentiate(s, m, sink):
        p = jnp.exp2(s - m)
        return p.astype(BF16), jnp.sum(p, axis=0, keepdims=True) + jnp.exp2(sink - m)

    def weigh(blk, kvh, p, denom):
        dims = kv_rows(kvh)
        if blk % 2 == 0:
            v_win = v_scr[dims, key_window(blk)]
        else:
            v_win = v_odd_scr[dims, key_window(blk - 1)]
        a_scr[dims, block_lanes(blk)] = _dot(v_win, p) * (1.0 / denom)

    def transpose_piece(c, kvh):
        slab = a_scr[kv_rows(kvh), c * GROUP * ATTN_CHUNK:(c + 1) * GROUP * ATTN_CHUNK]
        per_head = [
            jnp.concatenate([slab[:, block_lanes(j)][:, g * Q_ROWS:(g + 1) * Q_ROWS]
                             for j in range(blocks_per_chunk)], axis=1)
            for g in range(GROUP)]
        dims = slice(kvh * group_dims, (kvh + 1) * group_dims)
        at_scr[chunk_rows(c), dims] = jnp.concatenate(per_head, axis=0).T.astype(BF16)

    def emit_piece(c, n):
        rows = chunk_rows(c)
        cols = slice(n * group_dims, (n + 1) * group_dims)
        y = _dot(at_scr[rows, :], wo_ref[:, cols])
        o_ref[rows, cols] = x_ref[rows, cols] + gate[:, cols] * y

    n_chunks = tile // ATTN_CHUNK
    n_pieces = D_MODEL // group_dims
    steps_per_chunk = blocks_per_chunk * N_KV_HEADS
    stride = steps_per_chunk // N_KV_HEADS
    steps = [(blk, kvh) for blk in range(n_chunks * blocks_per_chunk) for kvh in range(N_KV_HEADS)]
    scored, weights = {}, {}
    normalize(0)
    for kvh in range(N_KV_HEADS):
        project_piece(0, kvh)
    for i in range(len(steps) + 2):
        c, local = divmod(i, steps_per_chunk)
        if i < len(steps):
            if local == 0 and c + 1 < n_chunks:
                normalize(c + 1)
            scored[i] = score(*steps[i])
        if 0 <= i - 1 < len(steps):
            weights[i - 1] = exponentiate(*scored.pop(i - 1))
        if 0 <= i - 2 < len(steps):
            blk, kvh = steps[i - 2]
            weigh(blk, kvh, *weights.pop(i - 2))
            if blk % blocks_per_chunk == blocks_per_chunk - 1:
                transpose_piece(blk // blocks_per_chunk, kvh)
        if i < len(steps):
            if local % stride == 0 and c + 1 < n_chunks:
                project_piece(c + 1, local // stride)
            if local % stride == stride // 2 and c > 0:
                emit_piece(c - 1, local // stride)
    for n in range(n_pieces):
        emit_piece(n_chunks - 1, n)


def _attn_prompt(x, group, mods, norm_g, w_q, w_o, kv_t, tables_t, sinks):
    tile = group.tile
    nt = group.grid[1]
    blocks_per_tile = tile // WINDOW

    def prev_block(b, t):
        return jnp.maximum((b * nt + t) * blocks_per_tile - 1, 0)

    kv_cur = lambda part: pl.BlockSpec((KV_DIM, tile), lambda b, t: (part, b * nt + t))

    in_specs = [
        group.row_spec(D_MODEL),
        _mod_spec(1, 3, group), _mod_spec(1, 4, group), _mod_spec(1, 5, group),
        _resident((None, 1, D_MODEL), lambda b, t: (4, 0, 0)),
        _resident((D_MODEL, D_MODEL), lambda b, t: (0, 0)),
        _resident((D_MODEL, D_MODEL), lambda b, t: (0, 0)),
        pl.BlockSpec((KV_DIM, WINDOW), lambda b, t: (0, prev_block(b, t))), kv_cur(0),
        pl.BlockSpec((KV_DIM, WINDOW), lambda b, t: (1, prev_block(b, t))), kv_cur(1),
        group.table_t_spec(), group.table_t_spec(),
        pl.BlockSpec(memory_space=pltpu.SMEM),
    ]
    return pl.pallas_call(
        _attn_prompt_kernel,
        grid=group.grid,
        in_specs=in_specs,
        out_specs=group.row_spec(D_MODEL),
        out_shape=jax.ShapeDtypeStruct((group.rows, D_MODEL), F32),
        scratch_shapes=[
            pltpu.VMEM((KV_DIM, GROUP * tile), BF16),
            pltpu.VMEM((tile + WINDOW, KV_DIM), BF16),
            pltpu.VMEM((KV_DIM, tile + WINDOW), BF16),
            pltpu.VMEM((KV_DIM, tile + WINDOW), BF16),
            pltpu.VMEM((KV_DIM, GROUP * tile), F32),
            pltpu.VMEM((WINDOW // Q_ROWS + 1, WINDOW + Q_ROWS, GROUP * Q_ROWS), F32),
            pltpu.VMEM((tile, D_MODEL), BF16),
            pltpu.VMEM((tile, D_MODEL), BF16),
        ],
        compiler_params=pltpu.CompilerParams(
            dimension_semantics=("arbitrary", "arbitrary"),
            vmem_limit_bytes=VMEM_LIMIT_BYTES),
        name="attn_prompt",
    )(x, mods, mods, mods, norm_g, w_q, w_o, kv_t, kv_t, kv_t, kv_t, *tables_t, sinks)


def _attn_sample_kernel(x_ref, sh_ref, sc_ref, gt_ref, ng_ref, wq_ref, wo_ref,
                        kc_ref, vc_ref, kn_ref, vn_ref, cos_ref, sup_ref, sdn_ref, sink_ref,
                        o_ref, kout_ref, vout_ref, q_scr, *, dec_seq):
    rows = x_ref.shape[0]
    n_seq = rows // dec_seq
    n_cache = n_seq * WINDOW
    q_scr[...] = _project_q(x_ref, sh_ref, sc_ref, ng_ref, wq_ref,
                            (cos_ref[...], sup_ref[...], sdn_ref[...]), dec_seq)

    pack = DEC_ATTN_PACK
    pack_rows = pack * dec_seq

    def windows(ref, first_seq, kvh):
        return jnp.concatenate(
            [ref[(s * N_KV_HEADS + kvh) * HEAD_DIM:(s * N_KV_HEADS + kvh + 1) * HEAD_DIM, :].astype(BF16)
             for s in range(first_seq, first_seq + pack)], axis=1)

    q_row = lax.broadcasted_iota(jnp.int32, (GROUP * pack_rows, 1), 0) & (pack_rows - 1)
    q_seq, q_i = q_row >> _log2(dec_seq), q_row & (dec_seq - 1)
    c_col = lax.broadcasted_iota(jnp.int32, (1, pack * WINDOW), 1)
    valid_c = (q_seq == (c_col >> _log2(WINDOW))) & ((c_col & (WINDOW - 1)) > q_i)
    n_col = lax.broadcasted_iota(jnp.int32, (1, rows), 1)
    n_seq_col, n_visible = n_col >> _log2(dec_seq), (n_col & (dec_seq - 1)) <= q_i
    kn = [kn_ref[kvh * HEAD_DIM:(kvh + 1) * HEAD_DIM, :].astype(BF16) for kvh in range(N_KV_HEADS)]
    vn = [vn_ref[kvh * HEAD_DIM:(kvh + 1) * HEAD_DIM, :].astype(BF16) for kvh in range(N_KV_HEADS)]

    keep = lax.broadcasted_iota(jnp.int32, (1, WINDOW), 1) < WINDOW - dec_seq

    def update_windows(first_seq):
        for src_ref, new_ref, dst_ref in ((kc_ref, kn_ref, kout_ref), (vc_ref, vn_ref, vout_ref)):
            new = new_ref[...]
            for s in range(first_seq, first_seq + pack):
                blk = slice(s * KV_DIM, (s + 1) * KV_DIM)
                old = pltpu.roll(src_ref[blk, :], WINDOW - dec_seq, axis=1)
                app = pltpu.roll(new, (WINDOW - dec_seq - s * dec_seq) % rows, axis=1)
                dst_ref[blk, :] = jnp.where(keep, old, app)

    def score(first_seq, kvh):
        q_rows = slice(first_seq * dec_seq, (first_seq + pack) * dec_seq)
        valid_n = ((q_seq + first_seq) == n_seq_col) & n_visible
        qg = _stack_heads(q_scr, q_rows, kvh)
        return [jnp.where(valid_c, _dot(qg, windows(kc_ref, first_seq, kvh)), NEG),
                jnp.where(valid_n, _dot(qg, kn[kvh]), NEG)]

    def exponentiate(kvh, scores):
        (p_c, p_n), inv = _softmax_parts(scores, _sink_column(sink_ref, kvh, pack_rows))
        return p_c.astype(BF16), p_n.astype(BF16), inv

    def weigh(first_seq, kvh, p_c, p_n, inv):
        o = (_dot_nt(p_c, windows(vc_ref, first_seq, kvh)) + _dot_nt(p_n, vn[kvh])) * inv
        return [o[g * pack_rows:(g + 1) * pack_rows, :] for g in range(GROUP)]

    steps = [(first_seq, kvh) for first_seq in range(0, n_seq, pack) for kvh in range(N_KV_HEADS)]
    scored, weights, heads = {}, {}, []
    for i in range(len(steps) + 2):
        if i < len(steps):
            if steps[i][1] == 0:
                update_windows(steps[i][0])
            scored[i] = score(*steps[i])
        if 0 <= i - 1 < len(steps):
            weights[i - 1] = exponentiate(steps[i - 1][1], scored.pop(i - 1))
        if 0 <= i - 2 < len(steps):
            heads += weigh(*steps[i - 2], *weights.pop(i - 2))
    per_pack = GROUP * N_KV_HEADS
    attn = jnp.concatenate(
        [jnp.concatenate(heads[n:n + per_pack], axis=1) for n in range(0, len(heads), per_pack)],
        axis=0).astype(BF16)
    o_ref[...] = x_ref[...] + _mod_rows(gt_ref, dec_seq) * _dot(attn, wo_ref[...])


def _attn_sample(x, n_seq_total, dec_seq, mods, norm_g, w_q, w_o, k_cache, v_cache,
                 kv_new, tables, sinks):
    n_seq = SAMPLE_ATTN_SEQS
    rows = n_seq * dec_seq
    assert rows == WINDOW == LANES
    grid = (n_seq_total // n_seq,)
    row_spec = lambda width: pl.BlockSpec((rows, width), lambda t: (t, 0))
    mod_spec = lambda col: pl.BlockSpec((None, n_seq, D_MODEL), lambda t: (1, t, col))
    cache_spec = pl.BlockSpec((n_seq * KV_DIM, WINDOW), lambda t: (t, 0))
    new_spec = lambda part: pl.BlockSpec((KV_DIM, rows), lambda t: (part, t))
    table_spec = pl.BlockSpec((rows, LANES), lambda t: (0, 0))
    in_specs = [
        row_spec(D_MODEL), mod_spec(3), mod_spec(4), mod_spec(5),
        _resident((None, 1, D_MODEL), lambda t: (4, 0, 0)),
        _resident((D_MODEL, D_MODEL), lambda t: (0, 0)),
        _resident((D_MODEL, D_MODEL), lambda t: (0, 0)),
        cache_spec, cache_spec, new_spec(0), new_spec(1),
        table_spec, table_spec, table_spec,
        pl.BlockSpec(memory_space=pltpu.SMEM),
    ]
    cache_shape = jax.ShapeDtypeStruct((n_seq_total * KV_DIM, WINDOW), F32)
    return pl.pallas_call(
        functools.partial(_attn_sample_kernel, dec_seq=dec_seq),
        grid=grid,
        in_specs=in_specs,
        out_specs=[row_spec(D_MODEL), cache_spec, cache_spec],
        out_shape=[jax.ShapeDtypeStruct((n_seq_total * dec_seq, D_MODEL), F32),
                   cache_shape, cache_shape],
        scratch_shapes=[pltpu.VMEM((rows, D_MODEL), BF16)],
        compiler_params=pltpu.CompilerParams(
            dimension_semantics=("arbitrary",),
            vmem_limit_bytes=VMEM_LIMIT_BYTES),
        name="attn_sample",
    )(x, mods, mods, mods, norm_g, w_q, w_o, k_cache, v_cache, kv_new, kv_new, *tables, sinks)


def _rope_angles(pos):
    inv_freq = ROPE_THETA ** (-jnp.arange(0, ROT_DIM, 2, dtype=F32) / ROT_DIM)
    ang = pos.astype(F32)[:, None] * inv_freq[None, :]
    return jnp.cos(ang), jnp.sin(ang)


def _rope_tables(pos):
    half = ROT_DIM // 2
    cos, sin = _rope_angles(pos)
    n = pos.shape[0]
    rest = HEAD_DIM - ROT_DIM
    zeros = jnp.zeros((n, half), F32)
    cos_t = jnp.concatenate([cos, cos, jnp.ones((n, rest), F32)], axis=1)
    sin_up = jnp.concatenate([-sin, zeros, jnp.zeros((n, rest), F32)], axis=1)
    sin_dn = jnp.concatenate([zeros, sin, jnp.zeros((n, rest), F32)], axis=1)
    reps = LANES // HEAD_DIM
    return tuple(jnp.tile(t, (1, reps)) for t in (cos_t, sin_up, sin_dn))


def kernel(x_prompt, x_sample, state_conv, cache_k_win, cache_v_win, c_prompt, c_sample, norm_g, w_ada, b_ada, w_ffn_gate, w_ffn_up, w_ffn_down, conv_w_in, conv_w, conv_w_out, kv_norm_g, w_ada_kv, b_ada_kv, w_k, w_v, attn_w_q, attn_sinks, attn_w_o, final_norm_g):
    batch, seq, d = x_prompt.shape
    dec_batch, dec_seq, _ = x_sample.shape
    w_buf = cache_k_win.shape[1]
    past_len = 16384
    assert d == D_MODEL and w_buf == WINDOW and dec_seq == SUBLANES
    assert w_ada.shape[0] == 2 and conv_w_in.shape[0] == 1 and attn_w_q.shape[0] == 1

    pad = (-(dec_batch + batch)) % SUBLANES
    c_all = jnp.concatenate([c_sample, c_prompt, jnp.zeros((pad, d), F32)], axis=0)
    mods = _ada(c_all, w_ada, b_ada)
    mods_kv = _ada(c_all, w_ada_kv[None], b_ada_kv[None])
    prompt_mod_block = dec_batch // SUBLANES

    make_groups = lambda tile: {
        "prompt": _Group(batch, seq, None, prompt_mod_block, tile),
        "sample": _Group(dec_batch, dec_seq, dec_seq, prompt_mod_block, tile),
    }
    ffn_groups, mixer_groups = make_groups(FFN_TILE), make_groups(MIXER_TILE)
    xp, xs = x_prompt.reshape(batch * seq, d), x_sample.reshape(dec_batch * dec_seq, d)

    norm_g3 = norm_g.reshape(-1, 1, d)
    w_in, w_out = conv_w_in[0].astype(BF16), conv_w_out[0].astype(BF16)
    w_q, w_o = attn_w_q[0].astype(BF16), attn_w_o[0].astype(BF16)
    sinks = attn_sinks[0]

    pos_dec = past_len + jnp.arange(dec_seq, dtype=jnp.int32)
    pos_prompt = jnp.arange(seq, dtype=jnp.int32)
    tab_dec = tuple(jnp.tile(t, (SAMPLE_ATTN_SEQS, 1)) for t in _rope_tables(pos_dec))
    tab_prompt_t = tuple(t.T for t in _rope_angles(pos_prompt))
    tab_dec_t = tuple(jnp.tile(t.T, (1, FFN_TILE // dec_seq)) for t in _rope_angles(pos_dec))
    kv_args = dict(norm_g=kv_norm_g.reshape(1, d), mods=mods_kv,
                   w_kv_t=jnp.concatenate([w_k.T, w_v.T], axis=0).astype(BF16),
                   prompt_tables_t=tab_prompt_t, decode_tables_t=tab_dec_t)
    to_rows = lambda c: jnp.transpose(c, (0, 2, 3, 1)).reshape(dec_batch * KV_DIM, w_buf)
    from_rows = lambda c: jnp.transpose(c.reshape(dec_batch, N_KV_HEADS, HEAD_DIM, w_buf), (0, 3, 1, 2))

    ffn = functools.partial(_ffn, prompt=ffn_groups["prompt"], decode=ffn_groups["sample"], mods=mods,
                            norm_g=norm_g3, wg=w_ffn_gate, wu=w_ffn_up, wd=w_ffn_down)
    xp, xs = ffn(xp, xs, layer=0, sub=0, slot=0)
    xp, u_p = _conv_mixer(xp, mixer_groups["prompt"], mods, norm_g3, w_in, conv_w[0], w_out)
    xs, u_s = _conv_mixer(xs, mixer_groups["sample"], mods, norm_g3, w_in, conv_w[0], w_out,
                          (state_conv[0, :, 0, :], state_conv[0, :, 1, :]))
    xp, xs = ffn(xp, xs, layer=0, sub=2, slot=1)
    xp, xs, kvt_p, kvt_s = ffn(xp, xs, layer=1, sub=0, slot=0, kv=kv_args)
    xp = _attn_prompt(xp, make_groups(ATTN_TILE)["prompt"], mods, norm_g3, w_q, w_o, kvt_p, tab_prompt_t, sinks)
    xs, k_win, v_win = _attn_sample(xs, dec_batch, dec_seq, mods, norm_g3, w_q, w_o,
                                    to_rows(cache_k_win), to_rows(cache_v_win), kvt_s, tab_dec, sinks)
    y_p, y_s = ffn(xp, xs, layer=1, sub=2, slot=1, final_g=final_norm_g.reshape(1, d))

    kv_state_p = jnp.stack([kvt_p[:, (b + 1) * seq - WINDOW:(b + 1) * seq] for b in range(batch)])
    kv_state_p = jnp.transpose(kv_state_p.reshape(batch, 2, N_KV_HEADS, HEAD_DIM, WINDOW), (1, 0, 4, 2, 3))
    k_state_p, v_state_p = kv_state_p[0], kv_state_p[1]
    k_state_s, v_state_s = from_rows(k_win), from_rows(v_win)
    tail = CONV_WIDTH - 1
    conv_p = u_p.reshape(batch, SUBLANES, d)[:, SUBLANES - tail:][None]
    conv_s = u_s.reshape(dec_batch, dec_seq, d)[:, dec_seq - tail:][None]
    return (y_p.reshape(batch, seq, d), y_s.reshape(dec_batch, dec_seq, d), conv_p, conv_s,
            k_state_p, v_state_p, k_state_s, v_state_s)
```

```python
import functools

import jax
import jax.numpy as jnp
from jax import lax
from jax.experimental import pallas as pl
from jax.experimental.pallas import tpu as pltpu

F32 = jnp.float32
BF16 = jnp.bfloat16

D_MODEL = 1024
D_FF = 2816
HEAD_DIM = 64
N_HEADS = 16
N_KV_HEADS = 4
GROUP = N_HEADS // N_KV_HEADS
KV_DIM = N_KV_HEADS * HEAD_DIM
WINDOW = 128
ROT_DIM = 16
ROPE_THETA = 500000.0
CONV_WIDTH = 3
N_MOD = 9
EPS = 1e-6
NEG = -1e30
FMAX = float(jnp.finfo(jnp.float32).max)

SUBLANES = 8
LANES = 128
VMEM_LIMIT_BYTES = 60 * 1024 * 1024

ADA_COLS = 2304
FFN_TILE = 512
FFN_SUBTILES = 2
FF_STAGE = 256
CONV_ROWS = 256
MIXER_TILE = 1024
DEC_MIXER_TILE = 512
ATTN_TILE = 1024
Q_ROWS = 64
ATTN_CHUNK = 256
LOG2E = 1.4426950408889634
SAMPLE_ATTN_SEQS = 16
DEC_ATTN_PACK = 4
FF_CHUNKS = ((0, 1024), (1024, 2048), (2048, D_FF))


def _dot(a, b):
    return jnp.dot(a, b, preferred_element_type=F32)


def _dot_nt(a, b):
    return lax.dot_general(a, b, (((1,), (1,)), ((), ())), preferred_element_type=F32)


def _log2(n):
    assert n & (n - 1) == 0, n
    return n.bit_length() - 1


def _silu(x):
    return x * jax.nn.sigmoid(x)


def _unit_rms(x):
    return x * lax.rsqrt(jnp.mean(x * x, axis=-1, keepdims=True) + EPS)


def _repeat_rows(ref, reps):
    n = ref.shape[1]
    return jnp.concatenate(
        [jnp.broadcast_to(ref[i:i + 1, :], (reps, n)) for i in range(ref.shape[0])], axis=0)


def _mod_rows(ref, per_seq_rows):
    if per_seq_rows is None:
        return ref[pl.ds(pl.program_id(0), 1), :]
    return _repeat_rows(ref, per_seq_rows)


def _rope(x, cos_t, sin_up_t, sin_dn_t):
    n = x.shape[1]
    reps = n // LANES
    half = ROT_DIM // 2
    cos = jnp.concatenate([cos_t] * reps, axis=1)
    sin_up = jnp.concatenate([sin_up_t] * reps, axis=1)
    sin_dn = jnp.concatenate([sin_dn_t] * reps, axis=1)
    x_up = pltpu.roll(x, n - half, axis=1)
    x_dn = pltpu.roll(x, half, axis=1)
    return x * cos + x_up * sin_up + x_dn * sin_dn


def _ada_kernel(c_ref, w_ref, b_ref, o_ref):
    a = _silu(c_ref[...]).astype(BF16)
    o_ref[...] = _dot(a, w_ref[...].astype(BF16)) + b_ref[...]


def _ada(c_all, w, b):
    n_layers, _, n = w.shape
    m = c_all.shape[0]
    tn = ADA_COLS if n % ADA_COLS == 0 else D_MODEL
    return pl.pallas_call(
        _ada_kernel,
        grid=(n_layers, n // tn),
        in_specs=[
            pl.BlockSpec((m, D_MODEL), lambda l, j: (0, 0)),
            pl.BlockSpec((None, D_MODEL, tn), lambda l, j: (l, 0, j)),
            pl.BlockSpec((None, 1, tn), lambda l, j: (l, 0, j)),
        ],
        out_specs=pl.BlockSpec((None, m, tn), lambda l, j: (l, 0, j)),
        out_shape=jax.ShapeDtypeStruct((n_layers, m, n), F32),
        compiler_params=pltpu.CompilerParams(
            dimension_semantics=("arbitrary", "arbitrary"),
            vmem_limit_bytes=VMEM_LIMIT_BYTES),
        name="adaln_tables",
    )(c_all, w, b.reshape(n_layers, 1, n))


def _rope_t(xt, cos_t, sin_t):
    half = ROT_DIM // 2
    out = []
    for h in range(xt.shape[0] // HEAD_DIM):
        base = h * HEAD_DIM
        x1 = xt[base:base + half, :]
        x2 = xt[base + half:base + ROT_DIM, :]
        out += [x1 * cos_t - x2 * sin_t, x2 * cos_t + x1 * sin_t,
                xt[base + ROT_DIM:base + HEAD_DIM, :]]
    return jnp.concatenate(out, axis=0)


def _ffn_kernel(*refs, n_prompt_tiles, tiles_per_seq, dec_seq, layer, slot, with_kv, final):
    it = iter(refs)
    take = lambda n: [next(it) for _ in range(n)]
    xp_ref, xs_ref = take(2)
    p_mods, s_mods = take(3), take(3)
    ng_ref, wg_hbm, wu_hbm, wd_hbm = take(4)
    if with_kv:
        (kvg_ref,), p_kvmods, s_kvmods = take(1), take(2), take(2)
        (wkvt_ref,) = take(1)
        p_tabs, s_tabs = take(2), take(2)
    else:
        p_kvmods = s_kvmods = None
    if final:
        (fg_ref,) = take(1)
    op_ref, os_ref = take(2)
    if with_kv:
        kvp_ref, kvs_ref = take(2)
    wg_s, wu_s, wd_s, stage_in, stage_out, sem = take(6)

    i = pl.program_id(0)
    prompt_rows = lambda ref: ref[pl.ds(i // tiles_per_seq, 1), :]
    decode_rows = lambda ref: _repeat_rows(ref, dec_seq)

    def weight_copies(c):
        cols = pl.ds(c * FF_STAGE, FF_STAGE)
        buf = c % 2
        return (
            pltpu.make_async_copy(wg_hbm.at[layer, slot, :, cols], stage_in.at[buf, 0], sem.at[buf, 0]),
            pltpu.make_async_copy(wu_hbm.at[layer, slot, :, cols], stage_in.at[buf, 1], sem.at[buf, 1]),
            pltpu.make_async_copy(wd_hbm.at[layer, slot, cols, :], stage_out.at[buf], sem.at[buf, 2]),
        )

    def partial_ffn(hb, lo, hi):
        g = _dot(hb, wg_s[:, lo:hi])
        u = _dot(hb, wu_s[:, lo:hi])
        return _dot((_silu(g) * u).astype(BF16), wd_s[lo:hi, :])

    def split_body(x_ref, o_ref, rows_of, mods, kvmods):
        sub = x_ref.shape[0] // FFN_SUBTILES
        groups = [slice(k * sub, (k + 1) * sub) for k in range(FFN_SUBTILES)]
        rows_in = lambda v, r: v if v.shape[0] == 1 else v[r]
        shift, scale, gate = (rows_of(m) for m in mods)
        xs_, xns, hbs = [], [], []
        ys = [None] * FFN_SUBTILES
        for c, (lo, hi) in enumerate(FF_CHUNKS):
            for k, r in enumerate(groups):
                if c == 0:
                    x = x_ref[r, :]
                    xn = _unit_rms(x)
                    xs_.append(x)
                    xns.append(xn)
                    hbs.append(((xn * ng_ref[...]) * (1.0 + rows_in(scale, r))
                                + rows_in(shift, r)).astype(BF16))
                part = partial_ffn(hbs[k], lo, hi)
                ys[k] = part if ys[k] is None else ys[k] + part
                if c == len(FF_CHUNKS) - 1:
                    x_new = xs_[k] + (0.5 * rows_in(gate, r)) * ys[k]
                    o_ref[r, :] = _unit_rms(x_new) * fg_ref[...] if final else x_new
        if with_kv:
            kv_shift, kv_scale = rows_of(kvmods[0]), rows_of(kvmods[1])
            return jnp.concatenate(
                [((xns[k] * kvg_ref[...]) * (1.0 + rows_in(kv_scale, r)) + rows_in(kv_shift, r)).astype(BF16)
                 for k, r in enumerate(groups)], axis=0)

    def body(x_ref, o_ref, rows_of, mods, kvmods, stage_weights):
        if not stage_weights:
            return split_body(x_ref, o_ref, rows_of, mods, kvmods)
        x = x_ref[...]
        xn = _unit_rms(x)
        hb = ((xn * ng_ref[...]) * (1.0 + rows_of(mods[1])) + rows_of(mods[0])).astype(BF16)
        y = None
        n_slices = D_FF // FF_STAGE
        for c in range(min(2, n_slices)):
            for cp in weight_copies(c):
                cp.start()
        for c in range(n_slices):
            lo, hi = c * FF_STAGE, (c + 1) * FF_STAGE
            for cp in weight_copies(c):
                cp.wait()
            wg_s[:, lo:hi] = stage_in[c % 2, 0].astype(BF16)
            wu_s[:, lo:hi] = stage_in[c % 2, 1].astype(BF16)
            wd_s[lo:hi, :] = stage_out[c % 2].astype(BF16)
            if c + 2 < n_slices:
                for cp in weight_copies(c + 2):
                    cp.start()
            part = partial_ffn(hb, lo, hi)
            y = part if y is None else y + part
        x_new = x + (0.5 * rows_of(mods[2])) * y
        o_ref[...] = _unit_rms(x_new) * fg_ref[...] if final else x_new
        if with_kv:
            return ((xn * kvg_ref[...]) * (1.0 + rows_of(kvmods[1])) + rows_of(kvmods[0])).astype(BF16)

    def emit_kv(hk, tabs, kv_ref):
        kv_t = _dot_nt(wkvt_ref[...], hk)
        kv_ref[:KV_DIM, :] = _rope_t(kv_t[:KV_DIM, :], tabs[0][...], tabs[1][...])
        kv_ref[KV_DIM:, :] = kv_t[KV_DIM:, :]

    def prompt_step(stage_weights):
        hk = body(xp_ref, op_ref, prompt_rows, p_mods, p_kvmods, stage_weights)
        if with_kv:
            emit_kv(hk, p_tabs, kvp_ref)

    @pl.when(i == 0)
    def _():
        prompt_step(True)

    @pl.when((i > 0) & (i < n_prompt_tiles))
    def _():
        prompt_step(False)

    @pl.when(i >= n_prompt_tiles)
    def _():
        hk = body(xs_ref, os_ref, decode_rows, s_mods, s_kvmods, False)
        if with_kv:
            emit_kv(hk, s_tabs, kvs_ref)


def _resident(shape, index_map):
    return pl.BlockSpec(shape, index_map, pipeline_mode=pl.Buffered(1))


def _mod_spec(layer, col, group):
    if group.per_seq_rows is None:
        return pl.BlockSpec((None, SUBLANES, D_MODEL),
                            lambda b, t: (layer, group.prompt_mod_block, col))
    return pl.BlockSpec((None, group.tile // group.per_seq_rows, D_MODEL),
                        lambda b, t: (layer, t, col))


class _Group:
    def __init__(self, n_seq, seq_len, per_seq_rows, prompt_mod_block, tile):
        self.tile = tile
        self.rows = n_seq * seq_len
        self.per_seq_rows = per_seq_rows
        self.prompt_mod_block = prompt_mod_block
        if per_seq_rows is None:
            self.grid = (n_seq, seq_len // self.tile)
        else:
            self.grid = (1, self.rows // self.tile)
        self.tiles_per_seq = self.grid[1]

    def row_spec(self, width):
        nt = self.grid[1]
        return pl.BlockSpec((self.tile, width), lambda b, t: (b * nt + t, 0))

    def col_spec(self, height):
        nt = self.grid[1]
        return pl.BlockSpec((height, self.tile), lambda b, t: (0, b * nt + t))

    def table_t_spec(self):
        if self.per_seq_rows is None:
            return pl.BlockSpec((ROT_DIM // 2, self.tile), lambda b, t: (0, t))
        return pl.BlockSpec((ROT_DIM // 2, self.tile), lambda b, t: (0, 0))

    def table_spec(self):
        if self.per_seq_rows is None:
            return pl.BlockSpec((self.tile, LANES), lambda b, t: (t, 0))
        return pl.BlockSpec((self.tile, LANES), lambda b, t: (0, 0))


def _ffn(xp, xs, prompt, decode, mods, layer, sub, norm_g, wg, wu, wd, slot, kv=None, final_g=None):
    with_kv = kv is not None
    final = final_g is not None
    tile = prompt.tile
    assert decode.tile == tile
    n_p, n_s = prompt.rows // tile, decode.rows // tile
    tiles_per_seq = prompt.tiles_per_seq
    p_tile = lambda i: jnp.minimum(i, n_p - 1)
    s_tile = lambda i: jnp.maximum(i - n_p, 0)
    n_dec_seq = tile // decode.per_seq_rows

    def mod_specs(table_layer, cols):
        p = [pl.BlockSpec((None, SUBLANES, D_MODEL),
                          functools.partial(lambda i, c: (table_layer, prompt.prompt_mod_block, c), c=c))
             for c in cols]
        s = [pl.BlockSpec((None, n_dec_seq, D_MODEL),
                          functools.partial(lambda i, c: (table_layer, s_tile(i), c), c=c),
                          pipeline_mode=pl.Buffered(1))
             for c in cols]
        return p + s

    p_rows = lambda width: pl.BlockSpec((tile, width), lambda i: (p_tile(i), 0))
    s_rows = lambda width, **kw: pl.BlockSpec((tile, width), lambda i: (s_tile(i), 0), **kw)
    p_cols = lambda height: pl.BlockSpec((height, tile), lambda i: (0, p_tile(i)))
    s_cols = lambda height: pl.BlockSpec((height, tile), lambda i: (0, s_tile(i)))
    const = lambda shape: _resident(shape, lambda i: (0,) * len(shape))
    hbm = pl.BlockSpec(memory_space=pl.ANY)

    in_specs = [p_rows(D_MODEL), s_rows(D_MODEL, pipeline_mode=pl.Buffered(1))]
    in_specs += mod_specs(layer, [3 * sub + 0, 3 * sub + 1, 3 * sub + 2])
    in_specs += [_resident((None, 1, D_MODEL), lambda i: (layer * 3 + sub, 0, 0)), hbm, hbm, hbm]
    args = [xp, xs] + [mods] * 6 + [norm_g, wg, wu, wd]
    out_specs = [p_rows(D_MODEL), s_rows(D_MODEL)]
    out_shape = [jax.ShapeDtypeStruct((prompt.rows, D_MODEL), F32),
                 jax.ShapeDtypeStruct((decode.rows, D_MODEL), F32)]
    if with_kv:
        in_specs += [const((1, D_MODEL))] + mod_specs(0, [0, 1])
        in_specs += [const((2 * KV_DIM, D_MODEL))]
        in_specs += [pl.BlockSpec((ROT_DIM // 2, tile), lambda i: (0, p_tile(i) % tiles_per_seq))] * 2
        in_specs += [const((ROT_DIM // 2, tile))] * 2
        args += [kv["norm_g"]] + [kv["mods"]] * 4 + [kv["w_kv_t"]]
        args += [*kv["prompt_tables_t"], *kv["decode_tables_t"]]
        out_specs += [p_cols(2 * KV_DIM), s_cols(2 * KV_DIM)]
        out_shape += [jax.ShapeDtypeStruct((2 * KV_DIM, prompt.rows), F32),
                      jax.ShapeDtypeStruct((2 * KV_DIM, decode.rows), F32)]
    if final:
        in_specs.append(const((1, D_MODEL)))
        args.append(final_g)
    scratch = [
        pltpu.VMEM((D_MODEL, D_FF), BF16), pltpu.VMEM((D_MODEL, D_FF), BF16),
        pltpu.VMEM((D_FF, D_MODEL), BF16),
        pltpu.VMEM((2, 2, D_MODEL, FF_STAGE), F32),
        pltpu.VMEM((2, FF_STAGE, D_MODEL), F32),
        pltpu.SemaphoreType.DMA((2, 3)),
    ]
    return pl.pallas_call(
        functools.partial(_ffn_kernel, n_prompt_tiles=n_p, tiles_per_seq=tiles_per_seq,
                          dec_seq=decode.per_seq_rows, layer=layer, slot=slot,
                          with_kv=with_kv, final=final),
        grid=(n_p + n_s,),
        in_specs=in_specs,
        out_specs=out_specs,
        out_shape=out_shape,
        scratch_shapes=scratch,
        compiler_params=pltpu.CompilerParams(
            dimension_semantics=("arbitrary",),
            vmem_limit_bytes=VMEM_LIMIT_BYTES),
        name=f"ffn_l{layer}_s{sub}",
    )(*args)


def _conv_kernel(*refs, per_seq_rows):
    it = iter(refs)
    x_ref, sh_ref, sc_ref, gt_ref, ng_ref, win_ref, cw_ref, wout_ref = (next(it) for _ in range(8))
    if per_seq_rows is not None:
        st0_ref, st1_ref = next(it), next(it)
    o_ref, u_ref = next(it), next(it)
    if per_seq_rows is None:
        carry_ref = next(it)

    tile = x_ref.shape[0]
    prompt = per_seq_rows is None
    n_groups = tile // CONV_ROWS
    rows_of = lambda k: slice(k * CONV_ROWS, (k + 1) * CONV_ROWS)
    rows_in = lambda v, r: v if v.shape[0] == 1 else v[r]
    shift, scale, gate = (_mod_rows(ref, per_seq_rows) for ref in (sh_ref, sc_ref, gt_ref))
    row = lax.broadcasted_iota(jnp.int32, (CONV_ROWS, 1), 0)
    if prompt:
        @pl.when(pl.program_id(1) == 0)
        def _():
            carry_ref[...] = jnp.zeros_like(carry_ref)
        carried = (carry_ref[SUBLANES - 2:SUBLANES - 1, :], carry_ref[SUBLANES - 1:SUBLANES, :])
        t = row
    else:
        state2 = _repeat_rows(st0_ref, per_seq_rows)
        state1 = _repeat_rows(st1_ref, per_seq_rows)
        t = row & (per_seq_rows - 1)

    def project(k):
        x = x_ref[rows_of(k), :]
        h = ((_unit_rms(x) * ng_ref[...]) * (1.0 + rows_in(scale, rows_of(k)))
             + rows_in(shift, rows_of(k))).astype(BF16)
        bcv = _dot(h, win_ref[...])
        return x, bcv[:, :D_MODEL], bcv[:, D_MODEL:2 * D_MODEL] * bcv[:, 2 * D_MODEL:]

    def convolve(k, b_g, u, prev2, prev1):
        u1 = jnp.where(t == 0, prev1, pltpu.roll(u, 1, axis=0))
        u2 = jnp.where(t == 0, prev2, jnp.where(t == 1, prev1, pltpu.roll(u, 2, axis=0)))
        conv = u2 * cw_ref[0:1, :] + u1 * cw_ref[1:2, :] + u * cw_ref[2:3, :]
        return (b_g * conv).astype(BF16)

    projected, mixed = {}, {}
    for i in range(n_groups + 2):
        if i < n_groups:
            projected[i] = project(i)
        if 0 <= i - 1 < n_groups:
            k = i - 1
            x, b_g, u = projected.pop(k)
            if prompt:
                prev2, prev1 = carried
                carried = (u[CONV_ROWS - 2:CONV_ROWS - 1, :], u[CONV_ROWS - 1:, :])
                if k == n_groups - 1:
                    tail = u[CONV_ROWS - SUBLANES:, :]
                    carry_ref[...] = tail
                    u_ref[...] = tail
            else:
                prev2, prev1 = state2[rows_of(k)], state1[rows_of(k)]
                u_ref[rows_of(k), :] = u
            mixed[k] = (x, convolve(k, b_g, u, prev2, prev1))
        if 0 <= i - 2 < n_groups:
            k = i - 2
            x, act = mixed.pop(k)
            o_ref[rows_of(k), :] = x + rows_in(gate, rows_of(k)) * _dot(act, wout_ref[...])


def _conv_mixer(x, group, mods, norm_g, w_in, conv_w, w_out, state=None):
    prompt = group.per_seq_rows is None
    in_specs = [
        group.row_spec(D_MODEL),
        _mod_spec(0, 3, group), _mod_spec(0, 4, group), _mod_spec(0, 5, group),
        _resident((None, 1, D_MODEL), lambda b, t: (1, 0, 0)),
        _resident((D_MODEL, 3 * D_MODEL), lambda b, t: (0, 0)),
        _resident((CONV_WIDTH, D_MODEL), lambda b, t: (0, 0)),
        _resident((D_MODEL, D_MODEL), lambda b, t: (0, 0)),
    ]
    args = [x, mods, mods, mods, norm_g, w_in, conv_w, w_out]
    scratch = []
    if prompt:
        u_spec = pl.BlockSpec((SUBLANES, D_MODEL), lambda b, t: (b, 0))
        u_shape = jax.ShapeDtypeStruct((group.grid[0] * SUBLANES, D_MODEL), F32)
        scratch.append(pltpu.VMEM((SUBLANES, D_MODEL), F32))
    else:
        n_seq = group.tile // group.per_seq_rows
        st_spec = pl.BlockSpec((n_seq, D_MODEL), lambda b, t: (t, 0))
        in_specs += [st_spec, st_spec]
        args += [state[0], state[1]]
        u_spec = group.row_spec(D_MODEL)
        u_shape = jax.ShapeDtypeStruct((group.rows, D_MODEL), F32)
    return pl.pallas_call(
        functools.partial(_conv_kernel, per_seq_rows=group.per_seq_rows),
        grid=group.grid,
        in_specs=in_specs,
        out_specs=[group.row_spec(D_MODEL), u_spec],
        out_shape=[jax.ShapeDtypeStruct((group.rows, D_MODEL), F32), u_shape],
        scratch_shapes=scratch,
        compiler_params=pltpu.CompilerParams(
            dimension_semantics=("arbitrary", "arbitrary"),
            vmem_limit_bytes=VMEM_LIMIT_BYTES),
        name="conv_mixer",
    )(*args)


def _sink_column(sink_ref, kv_head, q_rows):
    grp = lax.broadcasted_iota(jnp.int32, (GROUP * q_rows, 1), 0) >> _log2(q_rows)
    col = jnp.full((GROUP * q_rows, 1), sink_ref[kv_head * GROUP + GROUP - 1], F32)
    for g in range(GROUP - 2, -1, -1):
        col = jnp.where(grp == g, sink_ref[kv_head * GROUP + g], col)
    return col


def _stack_heads(q_ref, rows, kv_head):
    return jnp.concatenate(
        [q_ref[rows, (kv_head * GROUP + g) * HEAD_DIM:(kv_head * GROUP + g + 1) * HEAD_DIM]
         for g in range(GROUP)], axis=0)


def _softmax_parts(scores, sink):
    m = sink
    for s in scores:
        m = jnp.maximum(m, jnp.max(s, axis=-1, keepdims=True))
    ps = [jnp.exp(s - m) for s in scores]
    denom = jnp.exp(sink - m)
    for p in ps:
        denom = denom + jnp.sum(p, axis=-1, keepdims=True)
    return ps, 1.0 / denom


def _project_q(x_ref, sh_ref, sc_ref, ng_ref, wq_ref, tables, per_seq_rows):
    h = ((_unit_rms(x_ref[...]) * ng_ref[...]) * (1.0 + _mod_rows(sc_ref, per_seq_rows))
         + _mod_rows(sh_ref, per_seq_rows)).astype(BF16)
    q = _rope(_dot(h, wq_ref[...]), *tables)
    return (q * (HEAD_DIM ** -0.5)).astype(BF16)


def _band_caps():
    shape = (WINDOW + Q_ROWS, GROUP * Q_ROWS)
    kj = lax.broadcasted_iota(jnp.int32, shape, 0)
    qi = lax.broadcasted_iota(jnp.int32, shape, 1) & (Q_ROWS - 1)
    in_band = (kj > qi) & (kj <= qi + WINDOW)
    return tuple(jnp.where(in_band & (kj >= n * Q_ROWS), FMAX, NEG) for n in range(WINDOW // Q_ROWS + 1))


def _attn_prompt_kernel(x_ref, sh_ref, sc_ref, gt_ref, ng_ref, wq_ref, wo_ref,
                        kp_ref, kc_ref, vp_ref, vc_ref, cos_ref, sin_ref, sink_ref,
                        o_ref, q_scr, k_scr, v_scr, v_odd_scr, a_scr, cap_scr, h_scr, at_scr):
    tile = x_ref.shape[0]

    @pl.when((pl.program_id(0) == 0) & (pl.program_id(1) == 0))
    def _():
        for n, cap in enumerate(_band_caps()):
            cap_scr[n] = cap

    k_scr[:WINDOW, :] = kp_ref[...].T.astype(BF16)
    k_scr[WINDOW:, :] = kc_ref[...].T.astype(BF16)
    v_scr[:, :WINDOW] = vp_ref[...].astype(BF16)
    v_scr[:, WINDOW:] = vc_ref[...].astype(BF16)
    v_odd_scr[:, :tile + WINDOW - Q_ROWS] = v_scr[:, Q_ROWS:]
    first_tile = pl.program_id(1) == 0
    norm_gain = ng_ref[...]
    scale1 = 1.0 + _mod_rows(sc_ref, None)
    shift = _mod_rows(sh_ref, None)
    gate = _mod_rows(gt_ref, None)

    group_dims = GROUP * HEAD_DIM
    blocks_per_chunk = ATTN_CHUNK // Q_ROWS
    kv_rows = lambda kvh: slice(kvh * HEAD_DIM, (kvh + 1) * HEAD_DIM)
    block_lanes = lambda blk: slice(blk * GROUP * Q_ROWS, (blk + 1) * GROUP * Q_ROWS)

    def chunk_rows(c):
        return slice(c * ATTN_CHUNK, (c + 1) * ATTN_CHUNK)

    def normalize(c):
        rows = chunk_rows(c)
        h_scr[rows, :] = ((_unit_rms(x_ref[rows, :]) * norm_gain) * scale1 + shift).astype(BF16)

    def project_piece(c, kvh):
        rows = chunk_rows(c)
        dims = slice(kvh * group_dims, (kvh + 1) * group_dims)
        qt = _rope_t(_dot(h_scr[rows, :], wq_ref[:, dims]).T, cos_ref[:, rows], sin_ref[:, rows])
        qt = (qt * (HEAD_DIM ** -0.5 * LOG2E)).astype(BF16)
        for j in range(blocks_per_chunk):
            blk = c * blocks_per_chunk + j
            q_scr[kv_rows(kvh), block_lanes(blk)] = jnp.concatenate(
                [qt[g * HEAD_DIM:(g + 1) * HEAD_DIM, j * Q_ROWS:(j + 1) * Q_ROWS] for g in range(GROUP)],
                axis=1)

    def key_window(blk):
        return slice(blk * Q_ROWS, blk * Q_ROWS + WINDOW + Q_ROWS)

    def score(blk, kvh):
        hidden = WINDOW // Q_ROWS - blk
        cap = cap_scr[jnp.where(first_tile, hidden, 0)] if hidden > 0 else cap_scr[0]
        dims = kv_rows(kvh)
        s = jnp.minimum(_dot(k_scr[key_window(blk), dims], q_scr[dims, block_lanes(blk)]), cap)
        sink_lane = lax.broadcasted_iota(jnp.int32, (1, GROUP * Q_ROWS), 1) >> _log2(Q_ROWS)
        sink = jnp.full((1, GROUP * Q_ROWS), sink_ref[kvh * GROUP + GROUP - 1], F32)
        for g in range(GROUP - 1):
            sink = jnp.where(sink_lane == g, sink_ref[kvh * GROUP + g], sink)
        sink = LOG2E * sink
        return s, jnp.maximum(jnp.max(s, axis=0, keepdims=True), sink), sink

    def exponentiate(s, m, sink):
        p = jnp.exp2(s - m)
        return p.astype(BF16), jnp.sum(p, axis=0, keepdims=True) + jnp.exp2(sink - m)

    def weigh(blk, kvh, p, denom):
        dims = kv_rows(kvh)
        if blk % 2 == 0:
            v_win = v_scr[dims, key_window(blk)]
        else:
            v_win = v_odd_scr[dims, key_window(blk - 1)]
        a_scr[dims, block_lanes(blk)] = _dot(v_win, p) * (1.0 / denom)

    def transpose_piece(c, kvh):
        slab = a_scr[kv_rows(kvh), c * GROUP * ATTN_CHUNK:(c + 1) * GROUP * ATTN_CHUNK]
        per_head = [
            jnp.concatenate([slab[:, block_lanes(j)][:, g * Q_ROWS:(g + 1) * Q_ROWS]
                             for j in range(blocks_per_chunk)], axis=1)
            for g in range(GROUP)]
        dims = slice(kvh * group_dims, (kvh + 1) * group_dims)
        at_scr[chunk_rows(c), dims] = jnp.concatenate(per_head, axis=0).T.astype(BF16)

    def emit_piece(c, n):
        rows = chunk_rows(c)
        cols = slice(n * group_dims, (n + 1) * group_dims)
        y = _dot(at_scr[rows, :], wo_ref[:, cols])
        o_ref[rows, cols] = x_ref[rows, cols] + gate[:, cols] * y

    n_chunks = tile // ATTN_CHUNK
    n_pieces = D_MODEL // group_dims
    steps_per_chunk = blocks_per_chunk * N_KV_HEADS
    stride = steps_per_chunk // N_KV_HEADS
    steps = [(blk, kvh) for blk in range(n_chunks * blocks_per_chunk) for kvh in range(N_KV_HEADS)]
    scored, weights = {}, {}
    normalize(0)
    for kvh in range(N_KV_HEADS):
        project_piece(0, kvh)
    for i in range(len(steps) + 2):
        c, local = divmod(i, steps_per_chunk)
        if i < len(steps):
            if local == 0 and c + 1 < n_chunks:
                normalize(c + 1)
            scored[i] = score(*steps[i])
        if 0 <= i - 1 < len(steps):
            weights[i - 1] = exponentiate(*scored.pop(i - 1))
        if 0 <= i - 2 < len(steps):
            blk, kvh = steps[i - 2]
            weigh(blk, kvh, *weights.pop(i - 2))
            if blk % blocks_per_chunk == blocks_per_chunk - 1:
                transpose_piece(blk // blocks_per_chunk, kvh)
        if i < len(steps):
            if local % stride == 0 and c + 1 < n_chunks:
                project_piece(c + 1, local // stride)
            if local % stride == stride // 2 and c > 0:
                emit_piece(c - 1, local // stride)
    for n in range(n_pieces):
        emit_piece(n_chunks - 1, n)


def _attn_prompt(x, group, mods, norm_g, w_q, w_o, kv_t, tables_t, sinks):
    tile = group.tile
    nt = group.grid[1]
    blocks_per_tile = tile // WINDOW

    def prev_block(b, t):
        return jnp.maximum((b * nt + t) * blocks_per_tile - 1, 0)

    kv_cur = lambda part: pl.BlockSpec((KV_DIM, tile), lambda b, t: (part, b * nt + t))

    in_specs = [
        group.row_spec(D_MODEL),
        _mod_spec(1, 3, group), _mod_spec(1, 4, group), _mod_spec(1, 5, group),
        _resident((None, 1, D_MODEL), lambda b, t: (4, 0, 0)),
        _resident((D_MODEL, D_MODEL), lambda b, t: (0, 0)),
        _resident((D_MODEL, D_MODEL), lambda b, t: (0, 0)),
        pl.BlockSpec((KV_DIM, WINDOW), lambda b, t: (0, prev_block(b, t))), kv_cur(0),
        pl.BlockSpec((KV_DIM, WINDOW), lambda b, t: (1, prev_block(b, t))), kv_cur(1),
        group.table_t_spec(), group.table_t_spec(),
        pl.BlockSpec(memory_space=pltpu.SMEM),
    ]
    return pl.pallas_call(
        _attn_prompt_kernel,
        grid=group.grid,
        in_specs=in_specs,
        out_specs=group.row_spec(D_MODEL),
        out_shape=jax.ShapeDtypeStruct((group.rows, D_MODEL), F32),
        scratch_shapes=[
            pltpu.VMEM((KV_DIM, GROUP * tile), BF16),
            pltpu.VMEM((tile + WINDOW, KV_DIM), BF16),
            pltpu.VMEM((KV_DIM, tile + WINDOW), BF16),
            pltpu.VMEM((KV_DIM, tile + WINDOW), BF16),
            pltpu.VMEM((KV_DIM, GROUP * tile), F32),
            pltpu.VMEM((WINDOW // Q_ROWS + 1, WINDOW + Q_ROWS, GROUP * Q_ROWS), F32),
            pltpu.VMEM((tile, D_MODEL), BF16),
            pltpu.VMEM((tile, D_MODEL), BF16),
        ],
        compiler_params=pltpu.CompilerParams(
            dimension_semantics=("arbitrary", "arbitrary"),
            vmem_limit_bytes=VMEM_LIMIT_BYTES),
        name="attn_prompt",
    )(x, mods, mods, mods, norm_g, w_q, w_o, kv_t, kv_t, kv_t, kv_t, *tables_t, sinks)


def _attn_sample_kernel(x_ref, sh_ref, sc_ref, gt_ref, ng_ref, wq_ref, wo_ref,
                        kc_ref, vc_ref, kn_ref, vn_ref, cos_ref, sup_ref, sdn_ref, sink_ref,
                        o_ref, kout_ref, vout_ref, q_scr, *, dec_seq):
    rows = x_ref.shape[0]
    n_seq = rows // dec_seq
    n_cache = n_seq * WINDOW
    q_scr[...] = _project_q(x_ref, sh_ref, sc_ref, ng_ref, wq_ref,
                            (cos_ref[...], sup_ref[...], sdn_ref[...]), dec_seq)

    pack = DEC_ATTN_PACK
    pack_rows = pack * dec_seq

    def windows(ref, first_seq, kvh):
        return jnp.concatenate(
            [ref[(s * N_KV_HEADS + kvh) * HEAD_DIM:(s * N_KV_HEADS + kvh + 1) * HEAD_DIM, :].astype(BF16)
             for s in range(first_seq, first_seq + pack)], axis=1)

    q_row = lax.broadcasted_iota(jnp.int32, (GROUP * pack_rows, 1), 0) & (pack_rows - 1)
    q_seq, q_i = q_row >> _log2(dec_seq), q_row & (dec_seq - 1)
    c_col = lax.broadcasted_iota(jnp.int32, (1, pack * WINDOW), 1)
    valid_c = (q_seq == (c_col >> _log2(WINDOW))) & ((c_col & (WINDOW - 1)) > q_i)
    n_col = lax.broadcasted_iota(jnp.int32, (1, rows), 1)
    n_seq_col, n_visible = n_col >> _log2(dec_seq), (n_col & (dec_seq - 1)) <= q_i
    kn = [kn_ref[kvh * HEAD_DIM:(kvh + 1) * HEAD_DIM, :].astype(BF16) for kvh in range(N_KV_HEADS)]
    vn = [vn_ref[kvh * HEAD_DIM:(kvh + 1) * HEAD_DIM, :].astype(BF16) for kvh in range(N_KV_HEADS)]

    keep = lax.broadcasted_iota(jnp.int32, (1, WINDOW), 1) < WINDOW - dec_seq

    def update_windows(first_seq):
        for src_ref, new_ref, dst_ref in ((kc_ref, kn_ref, kout_ref), (vc_ref, vn_ref, vout_ref)):
            new = new_ref[...]
            for s in range(first_seq, first_seq + pack):
                blk = slice(s * KV_DIM, (s + 1) * KV_DIM)
                old = pltpu.roll(src_ref[blk, :], WINDOW - dec_seq, axis=1)
                app = pltpu.roll(new, (WINDOW - dec_seq - s * dec_seq) % rows, axis=1)
                dst_ref[blk, :] = jnp.where(keep, old, app)

    def score(first_seq, kvh):
        q_rows = slice(first_seq * dec_seq, (first_seq + pack) * dec_seq)
        valid_n = ((q_seq + first_seq) == n_seq_col) & n_visible
        qg = _stack_heads(q_scr, q_rows, kvh)
        return [jnp.where(valid_c, _dot(qg, windows(kc_ref, first_seq, kvh)), NEG),
                jnp.where(valid_n, _dot(qg, kn[kvh]), NEG)]

    def exponentiate(kvh, scores):
        (p_c, p_n), inv = _softmax_parts(scores, _sink_column(sink_ref, kvh, pack_rows))
        return p_c.astype(BF16), p_n.astype(BF16), inv

    def weigh(first_seq, kvh, p_c, p_n, inv):
        o = (_dot_nt(p_c, windows(vc_ref, first_seq, kvh)) + _dot_nt(p_n, vn[kvh])) * inv
        return [o[g * pack_rows:(g + 1) * pack_rows, :] for g in range(GROUP)]

    steps = [(first_seq, kvh) for first_seq in range(0, n_seq, pack) for kvh in range(N_KV_HEADS)]
    scored, weights, heads = {}, {}, []
    for i in range(len(steps) + 2):
        if i < len(steps):
            if steps[i][1] == 0:
                update_windows(steps[i][0])
            scored[i] = score(*steps[i])
        if 0 <= i - 1 < len(steps):
            weights[i - 1] = exponentiate(steps[i - 1][1], scored.pop(i - 1))
        if 0 <= i - 2 < len(steps):
            heads += weigh(*steps[i - 2], *weights.pop(i - 2))
    per_pack = GROUP * N_KV_HEADS
    attn = jnp.concatenate(
        [jnp.concatenate(heads[n:n + per_pack], axis=1) for n in range(0, len(heads), per_pack)],
        axis=0).astype(BF16)
    o_ref[...] = x_ref[...] + _mod_rows(gt_ref, dec_seq) * _dot(attn, wo_ref[...])


def _attn_sample(x, n_seq_total, dec_seq, mods, norm_g, w_q, w_o, k_cache, v_cache,
                 kv_new, tables, sinks):
    n_seq = SAMPLE_ATTN_SEQS
    rows = n_seq * dec_seq
    assert rows == WINDOW == LANES
    grid = (n_seq_total // n_seq,)
    row_spec = lambda width: pl.BlockSpec((rows, width), lambda t: (t, 0))
    mod_spec = lambda col: pl.BlockSpec((None, n_seq, D_MODEL), lambda t: (1, t, col))
    cache_spec = pl.BlockSpec((n_seq * KV_DIM, WINDOW), lambda t: (t, 0))
    new_spec = lambda part: pl.BlockSpec((KV_DIM, rows), lambda t: (part, t))
    table_spec = pl.BlockSpec((rows, LANES), lambda t: (0, 0))
    in_specs = [
        row_spec(D_MODEL), mod_spec(3), mod_spec(4), mod_spec(5),
        _resident((None, 1, D_MODEL), lambda t: (4, 0, 0)),
        _resident((D_MODEL, D_MODEL), lambda t: (0, 0)),
        _resident((D_MODEL, D_MODEL), lambda t: (0, 0)),
        cache_spec, cache_spec, new_spec(0), new_spec(1),
        table_spec, table_spec, table_spec,
        pl.BlockSpec(memory_space=pltpu.SMEM),
    ]
    cache_shape = jax.ShapeDtypeStruct((n_seq_total * KV_DIM, WINDOW), F32)
    return pl.pallas_call(
        functools.partial(_attn_sample_kernel, dec_seq=dec_seq),
        grid=grid,
        in_specs=in_specs,
        out_specs=[row_spec(D_MODEL), cache_spec, cache_spec],
        out_shape=[jax.ShapeDtypeStruct((n_seq_total * dec_seq, D_MODEL), F32),
                   cache_shape, cache_shape],
        scratch_shapes=[pltpu.VMEM((rows, D_MODEL), BF16)],
        compiler_params=pltpu.CompilerParams(
            dimension_semantics=("arbitrary",),
            vmem_limit_bytes=VMEM_LIMIT_BYTES),
        name="attn_sample",
    )(x, mods, mods, mods, norm_g, w_q, w_o, k_cache, v_cache, kv_new, kv_new, *tables, sinks)


def _rope_angles(pos):
    inv_freq = ROPE_THETA ** (-jnp.arange(0, ROT_DIM, 2, dtype=F32) / ROT_DIM)
    ang = pos.astype(F32)[:, None] * inv_freq[None, :]
    return jnp.cos(ang), jnp.sin(ang)


def _rope_tables(pos):
    half = ROT_DIM // 2
    cos, sin = _rope_angles(pos)
    n = pos.shape[0]
    rest = HEAD_DIM - ROT_DIM
    zeros = jnp.zeros((n, half), F32)
    cos_t = jnp.concatenate([cos, cos, jnp.ones((n, rest), F32)], axis=1)
    sin_up = jnp.concatenate([-sin, zeros, jnp.zeros((n, rest), F32)], axis=1)
    sin_dn = jnp.concatenate([zeros, sin, jnp.zeros((n, rest), F32)], axis=1)
    reps = LANES // HEAD_DIM
    return tuple(jnp.tile(t, (1, reps)) for t in (cos_t, sin_up, sin_dn))


def kernel(x_prompt, x_sample, state_conv, cache_k_win, cache_v_win, c_prompt, c_sample, norm_g, w_ada, b_ada, w_ffn_gate, w_ffn_up, w_ffn_down, conv_w_in, conv_w, conv_w_out, kv_norm_g, w_ada_kv, b_ada_kv, w_k, w_v, attn_w_q, attn_sinks, attn_w_o, final_norm_g):
    batch, seq, d = x_prompt.shape
    dec_batch, dec_seq, _ = x_sample.shape
    w_buf = cache_k_win.shape[1]
    past_len = 16384
    assert d == D_MODEL and w_buf == WINDOW and dec_seq == SUBLANES
    assert w_ada.shape[0] == 2 and conv_w_in.shape[0] == 1 and attn_w_q.shape[0] == 1

    pad = (-(dec_batch + batch)) % SUBLANES
    c_all = jnp.concatenate([c_sample, c_prompt, jnp.zeros((pad, d), F32)], axis=0)
    mods = _ada(c_all, w_ada, b_ada)
    mods_kv = _ada(c_all, w_ada_kv[None], b_ada_kv[None])
    prompt_mod_block = dec_batch // SUBLANES

    make_groups = lambda tile: {
        "prompt": _Group(batch, seq, None, prompt_mod_block, tile),
        "sample": _Group(dec_batch, dec_seq, dec_seq, prompt_mod_block, tile),
    }
    ffn_groups, mixer_groups = make_groups(FFN_TILE), make_groups(MIXER_TILE)
    xp, xs = x_prompt.reshape(batch * seq, d), x_sample.reshape(dec_batch * dec_seq, d)

    norm_g3 = norm_g.reshape(-1, 1, d)
    w_in, w_out = conv_w_in[0].astype(BF16), conv_w_out[0].astype(BF16)
    w_q, w_o = attn_w_q[0].astype(BF16), attn_w_o[0].astype(BF16)
    sinks = attn_sinks[0]

    pos_dec = past_len + jnp.arange(dec_seq, dtype=jnp.int32)
    pos_prompt = jnp.arange(seq, dtype=jnp.int32)
    tab_dec = tuple(jnp.tile(t, (SAMPLE_ATTN_SEQS, 1)) for t in _rope_tables(pos_dec))
    tab_prompt_t = tuple(t.T for t in _rope_angles(pos_prompt))
    tab_dec_t = tuple(jnp.tile(t.T, (1, FFN_TILE // dec_seq)) for t in _rope_angles(pos_dec))
    kv_args = dict(norm_g=kv_norm_g.reshape(1, d), mods=mods_kv,
                   w_kv_t=jnp.concatenate([w_k.T, w_v.T], axis=0).astype(BF16),
                   prompt_tables_t=tab_prompt_t, decode_tables_t=tab_dec_t)
    to_rows = lambda c: jnp.transpose(c, (0, 2, 3, 1)).reshape(dec_batch * KV_DIM, w_buf)
    from_rows = lambda c: jnp.transpose(c.reshape(dec_batch, N_KV_HEADS, HEAD_DIM, w_buf), (0, 3, 1, 2))

    ffn = functools.partial(_ffn, prompt=ffn_groups["prompt"], decode=ffn_groups["sample"], mods=mods,
                            norm_g=norm_g3, wg=w_ffn_gate, wu=w_ffn_up, wd=w_ffn_down)
    xp, xs = ffn(xp, xs, layer=0, sub=0, slot=0)
    xp, u_p = _conv_mixer(xp, mixer_groups["prompt"], mods, norm_g3, w_in, conv_w[0], w_out)
    xs, u_s = _conv_mixer(xs, make_groups(DEC_MIXER_TILE)["sample"], mods, norm_g3, w_in, conv_w[0], w_out,
                          (state_conv[0, :, 0, :], state_conv[0, :, 1, :]))
    xp, xs = ffn(xp, xs, layer=0, sub=2, slot=1)
    xp, xs, kvt_p, kvt_s = ffn(xp, xs, layer=1, sub=0, slot=0, kv=kv_args)
    xp = _attn_prompt(xp, make_groups(ATTN_TILE)["prompt"], mods, norm_g3, w_q, w_o, kvt_p, tab_prompt_t, sinks)
    xs, k_win, v_win = _attn_sample(xs, dec_batch, dec_seq, mods, norm_g3, w_q, w_o,
                                    to_rows(cache_k_win), to_rows(cache_v_win), kvt_s, tab_dec, sinks)
    y_p, y_s = ffn(xp, xs, layer=1, sub=2, slot=1, final_g=final_norm_g.reshape(1, d))

    kv_state_p = jnp.stack([kvt_p[:, (b + 1) * seq - WINDOW:(b + 1) * seq] for b in range(batch)])
    kv_state_p = jnp.transpose(kv_state_p.reshape(batch, 2, N_KV_HEADS, HEAD_DIM, WINDOW), (1, 0, 4, 2, 3))
    k_state_p, v_state_p = kv_state_p[0], kv_state_p[1]
    k_state_s, v_state_s = from_rows(k_win), from_rows(v_win)
    tail = CONV_WIDTH - 1
    conv_p = u_p.reshape(batch, SUBLANES, d)[:, SUBLANES - tail:][None]
    conv_s = u_s.reshape(dec_batch, dec_seq, d)[:, dec_seq - tail:][None]
    return (y_p.reshape(batch, seq, d), y_s.reshape(dec_batch, dec_seq, d), conv_p, conv_s,
            k_state_p, v_state_p, k_state_s, v_state_s)
```

```python
import functools

import jax
import jax.numpy as jnp
from jax import lax
from jax.experimental import pallas as pl
from jax.experimental.pallas import tpu as pltpu

F32 = jnp.float32
BF16 = jnp.bfloat16

D_MODEL = 1024
D_FF = 2816
HEAD_DIM = 64
N_HEADS = 16
N_KV_HEADS = 4
GROUP = N_HEADS // N_KV_HEADS
KV_DIM = N_KV_HEADS * HEAD_DIM
WINDOW = 128
ROT_DIM = 16
ROPE_THETA = 500000.0
CONV_WIDTH = 3
EPS = 1e-6
NEG = -1e30
FMAX = float(jnp.finfo(jnp.float32).max)

SUBLANES = 8
LANES = 128
VMEM_LIMIT_BYTES = 60 * 1024 * 1024

ADA_COLS = 2304
FFN_TILE = 512
FFN_SUBTILES = 2
FF_STAGE = 256
CONV_ROWS = 256
MIXER_TILE = 1024
DEC_MIXER_TILE = 512
ATTN_TILE = 1024
Q_ROWS = 64
ATTN_CHUNK = 256
LOG2E = 1.4426950408889634
SAMPLE_ATTN_SEQS = 16
DEC_ATTN_PACK = 4
FF_CHUNKS = ((0, 1024), (1024, 2048), (2048, D_FF))


def _dot(a, b):
    return jnp.dot(a, b, preferred_element_type=F32)


def _dot_nt(a, b):
    return lax.dot_general(a, b, (((1,), (1,)), ((), ())), preferred_element_type=F32)


def _log2(n):
    assert n & (n - 1) == 0, n
    return n.bit_length() - 1


def _silu(x):
    return x * jax.nn.sigmoid(x)


def _unit_rms(x):
    return x * lax.rsqrt(jnp.mean(x * x, axis=-1, keepdims=True) + EPS)


def _repeat_rows(ref, reps):
    n = ref.shape[1]
    return jnp.concatenate(
        [jnp.broadcast_to(ref[i:i + 1, :], (reps, n)) for i in range(ref.shape[0])], axis=0)


def _mod_rows(ref, per_seq_rows):
    if per_seq_rows is None:
        return ref[pl.ds(pl.program_id(0), 1), :]
    return _repeat_rows(ref, per_seq_rows)


def _rope(x, cos_t, sin_up_t, sin_dn_t):
    n = x.shape[1]
    reps = n // LANES
    half = ROT_DIM // 2
    cos = jnp.concatenate([cos_t] * reps, axis=1)
    sin_up = jnp.concatenate([sin_up_t] * reps, axis=1)
    sin_dn = jnp.concatenate([sin_dn_t] * reps, axis=1)
    x_up = pltpu.roll(x, n - half, axis=1)
    x_dn = pltpu.roll(x, half, axis=1)
    return x * cos + x_up * sin_up + x_dn * sin_dn


def _ada_kernel(c_ref, w_ref, b_ref, o_ref):
    a = _silu(c_ref[...]).astype(BF16)
    o_ref[...] = _dot(a, w_ref[...].astype(BF16)) + b_ref[...]


def _ada(c_all, w, b):
    n_layers, _, n = w.shape
    m = c_all.shape[0]
    tn = ADA_COLS if n % ADA_COLS == 0 else D_MODEL
    return pl.pallas_call(
        _ada_kernel,
        grid=(n_layers, n // tn),
        in_specs=[
            pl.BlockSpec((m, D_MODEL), lambda l, j: (0, 0)),
            pl.BlockSpec((None, D_MODEL, tn), lambda l, j: (l, 0, j)),
            pl.BlockSpec((None, 1, tn), lambda l, j: (l, 0, j)),
        ],
        out_specs=pl.BlockSpec((None, m, tn), lambda l, j: (l, 0, j)),
        out_shape=jax.ShapeDtypeStruct((n_layers, m, n), F32),
        compiler_params=pltpu.CompilerParams(
            dimension_semantics=("arbitrary", "arbitrary"),
            vmem_limit_bytes=VMEM_LIMIT_BYTES),
        name="adaln_tables",
    )(c_all, w, b.reshape(n_layers, 1, n))


def _rope_t(xt, cos_t, sin_t):
    half = ROT_DIM // 2
    out = []
    for h in range(xt.shape[0] // HEAD_DIM):
        base = h * HEAD_DIM
        x1 = xt[base:base + half, :]
        x2 = xt[base + half:base + ROT_DIM, :]
        out += [x1 * cos_t - x2 * sin_t, x2 * cos_t + x1 * sin_t,
                xt[base + ROT_DIM:base + HEAD_DIM, :]]
    return jnp.concatenate(out, axis=0)


def _ffn_kernel(*refs, n_prompt_tiles, tiles_per_seq, dec_seq, layer, slot, with_kv, final):
    it = iter(refs)
    take = lambda n: [next(it) for _ in range(n)]
    xp_ref, xs_ref = take(2)
    p_mods, s_mods = take(3), take(3)
    ng_ref, wg_hbm, wu_hbm, wd_hbm = take(4)
    if with_kv:
        (kvg_ref,), p_kvmods, s_kvmods = take(1), take(2), take(2)
        (wkvt_ref,) = take(1)
        p_tabs, s_tabs = take(2), take(2)
    else:
        p_kvmods = s_kvmods = None
    if final:
        (fg_ref,) = take(1)
    op_ref, os_ref = take(2)
    if with_kv:
        kvp_ref, kvs_ref = take(2)
    wg_s, wu_s, wd_s, stage_in, stage_out, sem = take(6)

    i = pl.program_id(0)
    prompt_rows = lambda ref: ref[pl.ds(i // tiles_per_seq, 1), :]
    decode_rows = lambda ref: _repeat_rows(ref, dec_seq)

    def weight_copies(c):
        cols = pl.ds(c * FF_STAGE, FF_STAGE)
        buf = c % 2
        return (
            pltpu.make_async_copy(wg_hbm.at[layer, slot, :, cols], stage_in.at[buf, 0], sem.at[buf, 0]),
            pltpu.make_async_copy(wu_hbm.at[layer, slot, :, cols], stage_in.at[buf, 1], sem.at[buf, 1]),
            pltpu.make_async_copy(wd_hbm.at[layer, slot, cols, :], stage_out.at[buf], sem.at[buf, 2]),
        )

    def partial_ffn(hb, lo, hi):
        g = _dot(hb, wg_s[:, lo:hi])
        u = _dot(hb, wu_s[:, lo:hi])
        return _dot((_silu(g) * u).astype(BF16), wd_s[lo:hi, :])

    def split_body(x_ref, o_ref, rows_of, mods, kvmods):
        sub = x_ref.shape[0] // FFN_SUBTILES
        groups = [slice(k * sub, (k + 1) * sub) for k in range(FFN_SUBTILES)]
        rows_in = lambda v, r: v if v.shape[0] == 1 else v[r]
        shift, scale, gate = (rows_of(m) for m in mods)
        xs_, xns, hbs = [], [], []
        ys = [None] * FFN_SUBTILES
        for c, (lo, hi) in enumerate(FF_CHUNKS):
            for k, r in enumerate(groups):
                if c == 0:
                    x = x_ref[r, :]
                    xn = _unit_rms(x)
                    xs_.append(x)
                    xns.append(xn)
                    hbs.append(((xn * ng_ref[...]) * (1.0 + rows_in(scale, r))
                                + rows_in(shift, r)).astype(BF16))
                part = partial_ffn(hbs[k], lo, hi)
                ys[k] = part if ys[k] is None else ys[k] + part
                if c == len(FF_CHUNKS) - 1:
                    x_new = xs_[k] + (0.5 * rows_in(gate, r)) * ys[k]
                    o_ref[r, :] = _unit_rms(x_new) * fg_ref[...] if final else x_new
        if with_kv:
            kv_shift, kv_scale = rows_of(kvmods[0]), rows_of(kvmods[1])
            return jnp.concatenate(
                [((xns[k] * kvg_ref[...]) * (1.0 + rows_in(kv_scale, r)) + rows_in(kv_shift, r)).astype(BF16)
                 for k, r in enumerate(groups)], axis=0)

    def body(x_ref, o_ref, rows_of, mods, kvmods, stage_weights):
        if not stage_weights:
            return split_body(x_ref, o_ref, rows_of, mods, kvmods)
        x = x_ref[...]
        xn = _unit_rms(x)
        hb = ((xn * ng_ref[...]) * (1.0 + rows_of(mods[1])) + rows_of(mods[0])).astype(BF16)
        y = None
        n_slices = D_FF // FF_STAGE
        for c in range(min(2, n_slices)):
            for cp in weight_copies(c):
                cp.start()
        for c in range(n_slices):
            lo, hi = c * FF_STAGE, (c + 1) * FF_STAGE
            for cp in weight_copies(c):
                cp.wait()
            wg_s[:, lo:hi] = stage_in[c % 2, 0].astype(BF16)
            wu_s[:, lo:hi] = stage_in[c % 2, 1].astype(BF16)
            wd_s[lo:hi, :] = stage_out[c % 2].astype(BF16)
            if c + 2 < n_slices:
                for cp in weight_copies(c + 2):
                    cp.start()
            part = partial_ffn(hb, lo, hi)
            y = part if y is None else y + part
        x_new = x + (0.5 * rows_of(mods[2])) * y
        o_ref[...] = _unit_rms(x_new) * fg_ref[...] if final else x_new
        if with_kv:
            return ((xn * kvg_ref[...]) * (1.0 + rows_of(kvmods[1])) + rows_of(kvmods[0])).astype(BF16)

    def emit_kv(hk, tabs, kv_ref):
        kv_t = _dot_nt(wkvt_ref[...], hk)
        kv_ref[:KV_DIM, :] = _rope_t(kv_t[:KV_DIM, :], tabs[0][...], tabs[1][...])
        kv_ref[KV_DIM:, :] = kv_t[KV_DIM:, :]

    def prompt_step(stage_weights):
        hk = body(xp_ref, op_ref, prompt_rows, p_mods, p_kvmods, stage_weights)
        if with_kv:
            emit_kv(hk, p_tabs, kvp_ref)

    @pl.when(i == 0)
    def _():
        prompt_step(True)

    @pl.when((i > 0) & (i < n_prompt_tiles))
    def _():
        prompt_step(False)

    @pl.when(i >= n_prompt_tiles)
    def _():
        hk = body(xs_ref, os_ref, decode_rows, s_mods, s_kvmods, False)
        if with_kv:
            emit_kv(hk, s_tabs, kvs_ref)


def _resident(shape, index_map):
    return pl.BlockSpec(shape, index_map, pipeline_mode=pl.Buffered(1))


def _mod_spec(layer, col, group):
    if group.per_seq_rows is None:
        return pl.BlockSpec((None, SUBLANES, D_MODEL),
                            lambda b, t: (layer, group.prompt_mod_block, col))
    return pl.BlockSpec((None, group.tile // group.per_seq_rows, D_MODEL),
                        lambda b, t: (layer, t, col))


class _Group:
    def __init__(self, n_seq, seq_len, per_seq_rows, prompt_mod_block, tile):
        self.tile = tile
        self.rows = n_seq * seq_len
        self.per_seq_rows = per_seq_rows
        self.prompt_mod_block = prompt_mod_block
        if per_seq_rows is None:
            self.grid = (n_seq, seq_len // self.tile)
        else:
            self.grid = (1, self.rows // self.tile)
        self.tiles_per_seq = self.grid[1]

    def row_spec(self, width):
        nt = self.grid[1]
        return pl.BlockSpec((self.tile, width), lambda b, t: (b * nt + t, 0))

    def table_t_spec(self):
        if self.per_seq_rows is None:
            return pl.BlockSpec((ROT_DIM // 2, self.tile), lambda b, t: (0, t))
        return pl.BlockSpec((ROT_DIM // 2, self.tile), lambda b, t: (0, 0))


def _ffn(xp, xs, prompt, decode, mods, layer, sub, norm_g, wg, wu, wd, slot, kv=None, final_g=None):
    with_kv = kv is not None
    final = final_g is not None
    tile = prompt.tile
    assert decode.tile == tile
    n_p, n_s = prompt.rows // tile, decode.rows // tile
    tiles_per_seq = prompt.tiles_per_seq
    p_tile = lambda i: jnp.minimum(i, n_p - 1)
    s_tile = lambda i: jnp.maximum(i - n_p, 0)
    n_dec_seq = tile // decode.per_seq_rows

    def mod_specs(table_layer, cols):
        p = [pl.BlockSpec((None, SUBLANES, D_MODEL),
                          functools.partial(lambda i, c: (table_layer, prompt.prompt_mod_block, c), c=c))
             for c in cols]
        s = [pl.BlockSpec((None, n_dec_seq, D_MODEL),
                          functools.partial(lambda i, c: (table_layer, s_tile(i), c), c=c),
                          pipeline_mode=pl.Buffered(1))
             for c in cols]
        return p + s

    p_rows = lambda width: pl.BlockSpec((tile, width), lambda i: (p_tile(i), 0))
    s_rows = lambda width, **kw: pl.BlockSpec((tile, width), lambda i: (s_tile(i), 0), **kw)
    p_cols = lambda height: pl.BlockSpec((height, tile), lambda i: (0, p_tile(i)))
    s_cols = lambda height: pl.BlockSpec((height, tile), lambda i: (0, s_tile(i)))
    const = lambda shape: _resident(shape, lambda i: (0,) * len(shape))
    hbm = pl.BlockSpec(memory_space=pl.ANY)

    in_specs = [p_rows(D_MODEL), s_rows(D_MODEL, pipeline_mode=pl.Buffered(1))]
    in_specs += mod_specs(layer, [3 * sub + 0, 3 * sub + 1, 3 * sub + 2])
    in_specs += [_resident((None, 1, D_MODEL), lambda i: (layer * 3 + sub, 0, 0)), hbm, hbm, hbm]
    args = [xp, xs] + [mods] * 6 + [norm_g, wg, wu, wd]
    out_specs = [p_rows(D_MODEL), s_rows(D_MODEL)]
    out_shape = [jax.ShapeDtypeStruct((prompt.rows, D_MODEL), F32),
                 jax.ShapeDtypeStruct((decode.rows, D_MODEL), F32)]
    if with_kv:
        in_specs += [const((1, D_MODEL))] + mod_specs(0, [0, 1])
        in_specs += [const((2 * KV_DIM, D_MODEL))]
        in_specs += [pl.BlockSpec((ROT_DIM // 2, tile), lambda i: (0, p_tile(i) % tiles_per_seq))] * 2
        in_specs += [const((ROT_DIM // 2, tile))] * 2
        args += [kv["norm_g"]] + [kv["mods"]] * 4 + [kv["w_kv_t"]]
        args += [*kv["prompt_tables_t"], *kv["decode_tables_t"]]
        out_specs += [p_cols(2 * KV_DIM), s_cols(2 * KV_DIM)]
        out_shape += [jax.ShapeDtypeStruct((2 * KV_DIM, prompt.rows), F32),
                      jax.ShapeDtypeStruct((2 * KV_DIM, decode.rows), F32)]
    if final:
        in_specs.append(const((1, D_MODEL)))
        args.append(final_g)
    scratch = [
        pltpu.VMEM((D_MODEL, D_FF), BF16), pltpu.VMEM((D_MODEL, D_FF), BF16),
        pltpu.VMEM((D_FF, D_MODEL), BF16),
        pltpu.VMEM((2, 2, D_MODEL, FF_STAGE), F32),
        pltpu.VMEM((2, FF_STAGE, D_MODEL), F32),
        pltpu.SemaphoreType.DMA((2, 3)),
    ]
    return pl.pallas_call(
        functools.partial(_ffn_kernel, n_prompt_tiles=n_p, tiles_per_seq=tiles_per_seq,
                          dec_seq=decode.per_seq_rows, layer=layer, slot=slot,
                          with_kv=with_kv, final=final),
        grid=(n_p + n_s,),
        in_specs=in_specs,
        out_specs=out_specs,
        out_shape=out_shape,
        scratch_shapes=scratch,
        compiler_params=pltpu.CompilerParams(
            dimension_semantics=("arbitrary",),
            vmem_limit_bytes=VMEM_LIMIT_BYTES),
        name=f"ffn_l{layer}_s{sub}",
    )(*args)


def _conv_kernel(*refs, per_seq_rows):
    it = iter(refs)
    x_ref, sh_ref, sc_ref, gt_ref, ng_ref, win_ref, cw_ref, wout_ref = (next(it) for _ in range(8))
    if per_seq_rows is not None:
        st0_ref, st1_ref = next(it), next(it)
    o_ref, u_ref = next(it), next(it)
    if per_seq_rows is None:
        carry_ref = next(it)

    tile = x_ref.shape[0]
    prompt = per_seq_rows is None
    n_groups = tile // CONV_ROWS
    rows_of = lambda k: slice(k * CONV_ROWS, (k + 1) * CONV_ROWS)
    rows_in = lambda v, r: v if v.shape[0] == 1 else v[r]
    shift, scale, gate = (_mod_rows(ref, per_seq_rows) for ref in (sh_ref, sc_ref, gt_ref))
    row = lax.broadcasted_iota(jnp.int32, (CONV_ROWS, 1), 0)
    if prompt:
        @pl.when(pl.program_id(1) == 0)
        def _():
            carry_ref[...] = jnp.zeros_like(carry_ref)
        carried = (carry_ref[SUBLANES - 2:SUBLANES - 1, :], carry_ref[SUBLANES - 1:SUBLANES, :])
        t = row
    else:
        state2 = _repeat_rows(st0_ref, per_seq_rows)
        state1 = _repeat_rows(st1_ref, per_seq_rows)
        t = row & (per_seq_rows - 1)

    def project(k):
        x = x_ref[rows_of(k), :]
        h = ((_unit_rms(x) * ng_ref[...]) * (1.0 + rows_in(scale, rows_of(k)))
             + rows_in(shift, rows_of(k))).astype(BF16)
        bcv = _dot(h, win_ref[...])
        return x, bcv[:, :D_MODEL], bcv[:, D_MODEL:2 * D_MODEL] * bcv[:, 2 * D_MODEL:]

    def convolve(k, b_g, u, prev2, prev1):
        u1 = jnp.where(t == 0, prev1, pltpu.roll(u, 1, axis=0))
        u2 = jnp.where(t == 0, prev2, jnp.where(t == 1, prev1, pltpu.roll(u, 2, axis=0)))
        conv = u2 * cw_ref[0:1, :] + u1 * cw_ref[1:2, :] + u * cw_ref[2:3, :]
        return (b_g * conv).astype(BF16)

    projected, mixed = {}, {}
    for i in range(n_groups + 2):
        if i < n_groups:
            projected[i] = project(i)
        if 0 <= i - 1 < n_groups:
            k = i - 1
            x, b_g, u = projected.pop(k)
            if prompt:
                prev2, prev1 = carried
                carried = (u[CONV_ROWS - 2:CONV_ROWS - 1, :], u[CONV_ROWS - 1:, :])
                if k == n_groups - 1:
                    tail = u[CONV_ROWS - SUBLANES:, :]
                    carry_ref[...] = tail
                    u_ref[...] = tail
            else:
                prev2, prev1 = state2[rows_of(k)], state1[rows_of(k)]
                u_ref[rows_of(k), :] = u
            mixed[k] = (x, convolve(k, b_g, u, prev2, prev1))
        if 0 <= i - 2 < n_groups:
            k = i - 2
            x, act = mixed.pop(k)
            o_ref[rows_of(k), :] = x + rows_in(gate, rows_of(k)) * _dot(act, wout_ref[...])


def _conv_mixer(x, group, mods, norm_g, w_in, conv_w, w_out, state=None):
    prompt = group.per_seq_rows is None
    in_specs = [
        group.row_spec(D_MODEL),
        _mod_spec(0, 3, group), _mod_spec(0, 4, group), _mod_spec(0, 5, group),
        _resident((None, 1, D_MODEL), lambda b, t: (1, 0, 0)),
        _resident((D_MODEL, 3 * D_MODEL), lambda b, t: (0, 0)),
        _resident((CONV_WIDTH, D_MODEL), lambda b, t: (0, 0)),
        _resident((D_MODEL, D_MODEL), lambda b, t: (0, 0)),
    ]
    args = [x, mods, mods, mods, norm_g, w_in, conv_w, w_out]
    scratch = []
    if prompt:
        u_spec = pl.BlockSpec((SUBLANES, D_MODEL), lambda b, t: (b, 0))
        u_shape = jax.ShapeDtypeStruct((group.grid[0] * SUBLANES, D_MODEL), F32)
        scratch.append(pltpu.VMEM((SUBLANES, D_MODEL), F32))
    else:
        n_seq = group.tile // group.per_seq_rows
        st_spec = pl.BlockSpec((n_seq, D_MODEL), lambda b, t: (t, 0))
        in_specs += [st_spec, st_spec]
        args += [state[0], state[1]]
        u_spec = group.row_spec(D_MODEL)
        u_shape = jax.ShapeDtypeStruct((group.rows, D_MODEL), F32)
    return pl.pallas_call(
        functools.partial(_conv_kernel, per_seq_rows=group.per_seq_rows),
        grid=group.grid,
        in_specs=in_specs,
        out_specs=[group.row_spec(D_MODEL), u_spec],
        out_shape=[jax.ShapeDtypeStruct((group.rows, D_MODEL), F32), u_shape],
        scratch_shapes=scratch,
        compiler_params=pltpu.CompilerParams(
            dimension_semantics=("arbitrary", "arbitrary"),
            vmem_limit_bytes=VMEM_LIMIT_BYTES),
        name="conv_mixer",
    )(*args)


def _sink_column(sink_ref, kv_head, q_rows):
    grp = lax.broadcasted_iota(jnp.int32, (GROUP * q_rows, 1), 0) >> _log2(q_rows)
    col = jnp.full((GROUP * q_rows, 1), sink_ref[kv_head * GROUP + GROUP - 1], F32)
    for g in range(GROUP - 2, -1, -1):
        col = jnp.where(grp == g, sink_ref[kv_head * GROUP + g], col)
    return col


def _stack_heads(q_ref, rows, kv_head):
    return jnp.concatenate(
        [q_ref[rows, (kv_head * GROUP + g) * HEAD_DIM:(kv_head * GROUP + g + 1) * HEAD_DIM]
         for g in range(GROUP)], axis=0)


def _softmax_parts(scores, sink):
    m = sink
    for s in scores:
        m = jnp.maximum(m, jnp.max(s, axis=-1, keepdims=True))
    ps = [jnp.exp(s - m) for s in scores]
    denom = jnp.exp(sink - m)
    for p in ps:
        denom = denom + jnp.sum(p, axis=-1, keepdims=True)
    return ps, 1.0 / denom


def _project_q(x_ref, sh_ref, sc_ref, ng_ref, wq_ref, tables, per_seq_rows):
    h = ((_unit_rms(x_ref[...]) * ng_ref[...]) * (1.0 + _mod_rows(sc_ref, per_seq_rows))
         + _mod_rows(sh_ref, per_seq_rows)).astype(BF16)
    q = _rope(_dot(h, wq_ref[...]), *tables)
    return (q * (HEAD_DIM ** -0.5)).astype(BF16)


def _band_caps():
    shape = (WINDOW + Q_ROWS, GROUP * Q_ROWS)
    kj = lax.broadcasted_iota(jnp.int32, shape, 0)
    qi = lax.broadcasted_iota(jnp.int32, shape, 1) & (Q_ROWS - 1)
    in_band = (kj > qi) & (kj <= qi + WINDOW)
    return tuple(jnp.where(in_band & (kj >= n * Q_ROWS), FMAX, NEG) for n in range(WINDOW // Q_ROWS + 1))


def _attn_prompt_kernel(x_ref, sh_ref, sc_ref, gt_ref, ng_ref, wq_ref, wo_ref,
                        kp_ref, kc_ref, vp_ref, vc_ref, cos_ref, sin_ref, sink_ref,
                        o_ref, q_scr, k_scr, v_scr, v_odd_scr, a_scr, cap_scr, h_scr, at_scr):
    tile = x_ref.shape[0]

    @pl.when((pl.program_id(0) == 0) & (pl.program_id(1) == 0))
    def _():
        for n, cap in enumerate(_band_caps()):
            cap_scr[n] = cap

    k_scr[:WINDOW, :] = kp_ref[...].T.astype(BF16)
    k_scr[WINDOW:, :] = kc_ref[...].T.astype(BF16)
    v_scr[:, :WINDOW] = vp_ref[...].astype(BF16)
    v_scr[:, WINDOW:] = vc_ref[...].astype(BF16)
    v_odd_scr[:, :tile + WINDOW - Q_ROWS] = v_scr[:, Q_ROWS:]
    first_tile = pl.program_id(1) == 0
    norm_gain = ng_ref[...]
    scale1 = 1.0 + _mod_rows(sc_ref, None)
    shift = _mod_rows(sh_ref, None)
    gate = _mod_rows(gt_ref, None)

    group_dims = GROUP * HEAD_DIM
    blocks_per_chunk = ATTN_CHUNK // Q_ROWS
    kv_rows = lambda kvh: slice(kvh * HEAD_DIM, (kvh + 1) * HEAD_DIM)
    block_lanes = lambda blk: slice(blk * GROUP * Q_ROWS, (blk + 1) * GROUP * Q_ROWS)

    def chunk_rows(c):
        return slice(c * ATTN_CHUNK, (c + 1) * ATTN_CHUNK)

    def normalize(c):
        rows = chunk_rows(c)
        h_scr[rows, :] = ((_unit_rms(x_ref[rows, :]) * norm_gain) * scale1 + shift).astype(BF16)

    def project_piece(c, kvh):
        rows = chunk_rows(c)
        dims = slice(kvh * group_dims, (kvh + 1) * group_dims)
        qt = _rope_t(_dot(h_scr[rows, :], wq_ref[:, dims]).T, cos_ref[:, rows], sin_ref[:, rows])
        qt = (qt * (HEAD_DIM ** -0.5 * LOG2E)).astype(BF16)
        for j in range(blocks_per_chunk):
            blk = c * blocks_per_chunk + j
            q_scr[kv_rows(kvh), block_lanes(blk)] = jnp.concatenate(
                [qt[g * HEAD_DIM:(g + 1) * HEAD_DIM, j * Q_ROWS:(j + 1) * Q_ROWS] for g in range(GROUP)],
                axis=1)

    def key_window(blk):
        return slice(blk * Q_ROWS, blk * Q_ROWS + WINDOW + Q_ROWS)

    def score(blk, kvh):
        hidden = WINDOW // Q_ROWS - blk
        cap = cap_scr[jnp.where(first_tile, hidden, 0)] if hidden > 0 else cap_scr[0]
        dims = kv_rows(kvh)
        s = jnp.minimum(_dot(k_scr[key_window(blk), dims], q_scr[dims, block_lanes(blk)]), cap)
        sink_lane = lax.broadcasted_iota(jnp.int32, (1, GROUP * Q_ROWS), 1) >> _log2(Q_ROWS)
        sink = jnp.full((1, GROUP * Q_ROWS), sink_ref[kvh * GROUP + GROUP - 1], F32)
        for g in range(GROUP - 1):
            sink = jnp.where(sink_lane == g, sink_ref[kvh * GROUP + g], sink)
        sink = LOG2E * sink
        return s, jnp.maximum(jnp.max(s, axis=0, keepdims=True), sink), sink

    def exponentiate(s, m, sink):
        p = jnp.exp2(s - m)
        return p.astype(BF16), jnp.sum(p, axis=0, keepdims=True) + jnp.exp2(sink - m)

    def weigh(blk, kvh, p, denom):
        dims = kv_rows(kvh)
        if blk % 2 == 0:
            v_win = v_scr[dims, key_window(blk)]
        else:
            v_win = v_odd_scr[dims, key_window(blk - 1)]
        a_scr[dims, block_lanes(blk)] = _dot(v_win, p) * (1.0 / denom)

    def transpose_piece(c, kvh):
        slab = a_scr[kv_rows(kvh), c * GROUP * ATTN_CHUNK:(c + 1) * GROUP * ATTN_CHUNK]
        per_head = [
            jnp.concatenate([slab[:, block_lanes(j)][:, g * Q_ROWS:(g + 1) * Q_ROWS]
                             for j in range(blocks_per_chunk)], axis=1)
            for g in range(GROUP)]
        dims = slice(kvh * group_dims, (kvh + 1) * group_dims)
        at_scr[chunk_rows(c), dims] = jnp.concatenate(per_head, axis=0).T.astype(BF16)

    def emit_piece(c, n):
        rows = chunk_rows(c)
        cols = slice(n * group_dims, (n + 1) * group_dims)
        y = _dot(at_scr[rows, :], wo_ref[:, cols])
        o_ref[rows, cols] = x_ref[rows, cols] + gate[:, cols] * y

    n_chunks = tile // ATTN_CHUNK
    n_pieces = D_MODEL // group_dims
    steps_per_chunk = blocks_per_chunk * N_KV_HEADS
    stride = steps_per_chunk // N_KV_HEADS
    steps = [(blk, kvh) for blk in range(n_chunks * blocks_per_chunk) for kvh in range(N_KV_HEADS)]
    scored, weights = {}, {}
    normalize(0)
    for kvh in range(N_KV_HEADS):
        project_piece(0, kvh)
    for i in range(len(steps) + 2):
        c, local = divmod(i, steps_per_chunk)
        if i < len(steps):
            if local == 0 and c + 1 < n_chunks:
                normalize(c + 1)
            scored[i] = score(*steps[i])
        if 0 <= i - 1 < len(steps):
            weights[i - 1] = exponentiate(*scored.pop(i - 1))
        if 0 <= i - 2 < len(steps):
            blk, kvh = steps[i - 2]
            weigh(blk, kvh, *weights.pop(i - 2))
            if blk % blocks_per_chunk == blocks_per_chunk - 1:
                transpose_piece(blk // blocks_per_chunk, kvh)
        if i < len(steps):
            if local % stride == 0 and c + 1 < n_chunks:
                project_piece(c + 1, local // stride)
            if local % stride == stride // 2 and c > 0:
                emit_piece(c - 1, local // stride)
    for n in range(n_pieces):
        emit_piece(n_chunks - 1, n)


def _attn_prompt(x, group, mods, norm_g, w_q, w_o, kv_t, tables_t, sinks):
    tile = group.tile
    nt = group.grid[1]
    blocks_per_tile = tile // WINDOW

    def prev_block(b, t):
        return jnp.maximum((b * nt + t) * blocks_per_tile - 1, 0)

    kv_cur = lambda part: pl.BlockSpec((KV_DIM, tile), lambda b, t: (part, b * nt + t))

    in_specs = [
        group.row_spec(D_MODEL),
        _mod_spec(1, 3, group), _mod_spec(1, 4, group), _mod_spec(1, 5, group),
        _resident((None, 1, D_MODEL), lambda b, t: (4, 0, 0)),
        _resident((D_MODEL, D_MODEL), lambda b, t: (0, 0)),
        _resident((D_MODEL, D_MODEL), lambda b, t: (0, 0)),
        pl.BlockSpec((KV_DIM, WINDOW), lambda b, t: (0, prev_block(b, t))), kv_cur(0),
        pl.BlockSpec((KV_DIM, WINDOW), lambda b, t: (1, prev_block(b, t))), kv_cur(1),
        group.table_t_spec(), group.table_t_spec(),
        pl.BlockSpec(memory_space=pltpu.SMEM),
    ]
    return pl.pallas_call(
        _attn_prompt_kernel,
        grid=group.grid,
        in_specs=in_specs,
        out_specs=group.row_spec(D_MODEL),
        out_shape=jax.ShapeDtypeStruct((group.rows, D_MODEL), F32),
        scratch_shapes=[
            pltpu.VMEM((KV_DIM, GROUP * tile), BF16),
            pltpu.VMEM((tile + WINDOW, KV_DIM), BF16),
            pltpu.VMEM((KV_DIM, tile + WINDOW), BF16),
            pltpu.VMEM((KV_DIM, tile + WINDOW), BF16),
            pltpu.VMEM((KV_DIM, GROUP * tile), F32),
            pltpu.VMEM((WINDOW // Q_ROWS + 1, WINDOW + Q_ROWS, GROUP * Q_ROWS), F32),
            pltpu.VMEM((tile, D_MODEL), BF16),
            pltpu.VMEM((tile, D_MODEL), BF16),
        ],
        compiler_params=pltpu.CompilerParams(
            dimension_semantics=("arbitrary", "arbitrary"),
            vmem_limit_bytes=VMEM_LIMIT_BYTES),
        name="attn_prompt",
    )(x, mods, mods, mods, norm_g, w_q, w_o, kv_t, kv_t, kv_t, kv_t, *tables_t, sinks)


def _attn_sample_kernel(x_ref, sh_ref, sc_ref, gt_ref, ng_ref, wq_ref, wo_ref,
                        kc_ref, vc_ref, kn_ref, vn_ref, cos_ref, sup_ref, sdn_ref, sink_ref,
                        o_ref, kout_ref, vout_ref, q_scr, *, dec_seq):
    rows = x_ref.shape[0]
    n_seq = rows // dec_seq
    q_scr[...] = _project_q(x_ref, sh_ref, sc_ref, ng_ref, wq_ref,
                            (cos_ref[...], sup_ref[...], sdn_ref[...]), dec_seq)

    pack = DEC_ATTN_PACK
    pack_rows = pack * dec_seq

    def windows(ref, first_seq, kvh):
        return jnp.concatenate(
            [ref[(s * N_KV_HEADS + kvh) * HEAD_DIM:(s * N_KV_HEADS + kvh + 1) * HEAD_DIM, :].astype(BF16)
             for s in range(first_seq, first_seq + pack)], axis=1)

    q_row = lax.broadcasted_iota(jnp.int32, (GROUP * pack_rows, 1), 0) & (pack_rows - 1)
    q_seq, q_i = q_row >> _log2(dec_seq), q_row & (dec_seq - 1)
    c_col = lax.broadcasted_iota(jnp.int32, (1, pack * WINDOW), 1)
    valid_c = (q_seq == (c_col >> _log2(WINDOW))) & ((c_col & (WINDOW - 1)) > q_i)
    n_col = lax.broadcasted_iota(jnp.int32, (1, rows), 1)
    n_seq_col, n_visible = n_col >> _log2(dec_seq), (n_col & (dec_seq - 1)) <= q_i
    kn = [kn_ref[kvh * HEAD_DIM:(kvh + 1) * HEAD_DIM, :].astype(BF16) for kvh in range(N_KV_HEADS)]
    vn = [vn_ref[kvh * HEAD_DIM:(kvh + 1) * HEAD_DIM, :].astype(BF16) for kvh in range(N_KV_HEADS)]

    keep = lax.broadcasted_iota(jnp.int32, (1, WINDOW), 1) < WINDOW - dec_seq

    def update_windows(first_seq):
        for src_ref, new_ref, dst_ref in ((kc_ref, kn_ref, kout_ref), (vc_ref, vn_ref, vout_ref)):
            new = new_ref[...]
            for s in range(first_seq, first_seq + pack):
                blk = slice(s * KV_DIM, (s + 1) * KV_DIM)
                old = pltpu.roll(src_ref[blk, :], WINDOW - dec_seq, axis=1)
                app = pltpu.roll(new, (WINDOW - dec_seq - s * dec_seq) % rows, axis=1)
                dst_ref[blk, :] = jnp.where(keep, old, app)

    def score(first_seq, kvh):
        q_rows = slice(first_seq * dec_seq, (first_seq + pack) * dec_seq)
        valid_n = ((q_seq + first_seq) == n_seq_col) & n_visible
        qg = _stack_heads(q_scr, q_rows, kvh)
        return [jnp.where(valid_c, _dot(qg, windows(kc_ref, first_seq, kvh)), NEG),
                jnp.where(valid_n, _dot(qg, kn[kvh]), NEG)]

    def exponentiate(kvh, scores):
        (p_c, p_n), inv = _softmax_parts(scores, _sink_column(sink_ref, kvh, pack_rows))
        return p_c.astype(BF16), p_n.astype(BF16), inv

    def weigh(first_seq, kvh, p_c, p_n, inv):
        o = (_dot_nt(p_c, windows(vc_ref, first_seq, kvh)) + _dot_nt(p_n, vn[kvh])) * inv
        return [o[g * pack_rows:(g + 1) * pack_rows, :] for g in range(GROUP)]

    steps = [(first_seq, kvh) for first_seq in range(0, n_seq, pack) for kvh in range(N_KV_HEADS)]
    scored, weights, heads = {}, {}, []
    for i in range(len(steps) + 2):
        if i < len(steps):
            if steps[i][1] == 0:
                update_windows(steps[i][0])
            scored[i] = score(*steps[i])
        if 0 <= i - 1 < len(steps):
            weights[i - 1] = exponentiate(steps[i - 1][1], scored.pop(i - 1))
        if 0 <= i - 2 < len(steps):
            heads += weigh(*steps[i - 2], *weights.pop(i - 2))
    per_pack = GROUP * N_KV_HEADS
    attn = jnp.concatenate(
        [jnp.concatenate(heads[n:n + per_pack], axis=1) for n in range(0, len(heads), per_pack)],
        axis=0).astype(BF16)
    o_ref[...] = x_ref[...] + _mod_rows(gt_ref, dec_seq) * _dot(attn, wo_ref[...])


def _attn_sample(x, n_seq_total, dec_seq, mods, norm_g, w_q, w_o, k_cache, v_cache,
                 kv_new, tables, sinks):
    n_seq = SAMPLE_ATTN_SEQS
    rows = n_seq * dec_seq
    assert rows == WINDOW == LANES
    grid = (n_seq_total // n_seq,)
    row_spec = lambda width: pl.BlockSpec((rows, width), lambda t: (t, 0))
    mod_spec = lambda col: pl.BlockSpec((None, n_seq, D_MODEL), lambda t: (1, t, col))
    cache_spec = pl.BlockSpec((n_seq * KV_DIM, WINDOW), lambda t: (t, 0))
    new_spec = lambda part: pl.BlockSpec((KV_DIM, rows), lambda t: (part, t))
    table_spec = pl.BlockSpec((rows, LANES), lambda t: (0, 0))
    in_specs = [
        row_spec(D_MODEL), mod_spec(3), mod_spec(4), mod_spec(5),
        _resident((None, 1, D_MODEL), lambda t: (4, 0, 0)),
        _resident((D_MODEL, D_MODEL), lambda t: (0, 0)),
        _resident((D_MODEL, D_MODEL), lambda t: (0, 0)),
        cache_spec, cache_spec, new_spec(0), new_spec(1),
        table_spec, table_spec, table_spec,
        pl.BlockSpec(memory_space=pltpu.SMEM),
    ]
    cache_shape = jax.ShapeDtypeStruct((n_seq_total * KV_DIM, WINDOW), F32)
    return pl.pallas_call(
        functools.partial(_attn_sample_kernel, dec_seq=dec_seq),
        grid=grid,
        in_specs=in_specs,
        out_specs=[row_spec(D_MODEL), cache_spec, cache_spec],
        out_shape=[jax.ShapeDtypeStruct((n_seq_total * dec_seq, D_MODEL), F32),
                   cache_shape, cache_shape],
        scratch_shapes=[pltpu.VMEM((rows, D_MODEL), BF16)],
        compiler_params=pltpu.CompilerParams(
            dimension_semantics=("arbitrary",),
            vmem_limit_bytes=VMEM_LIMIT_BYTES),
        name="attn_sample",
    )(x, mods, mods, mods, norm_g, w_q, w_o, k_cache, v_cache, kv_new, kv_new, *tables, sinks)


def _rope_angles(pos):
    inv_freq = ROPE_THETA ** (-jnp.arange(0, ROT_DIM, 2, dtype=F32) / ROT_DIM)
    ang = pos.astype(F32)[:, None] * inv_freq[None, :]
    return jnp.cos(ang), jnp.sin(ang)


def _rope_tables(pos):
    half = ROT_DIM // 2
    cos, sin = _rope_angles(pos)
    n = pos.shape[0]
    rest = HEAD_DIM - ROT_DIM
    zeros = jnp.zeros((n, half), F32)
    cos_t = jnp.concatenate([cos, cos, jnp.ones((n, rest), F32)], axis=1)
    sin_up = jnp.concatenate([-sin, zeros, jnp.zeros((n, rest), F32)], axis=1)
    sin_dn = jnp.concatenate([zeros, sin, jnp.zeros((n, rest), F32)], axis=1)
    reps = LANES // HEAD_DIM
    return tuple(jnp.tile(t, (1, reps)) for t in (cos_t, sin_up, sin_dn))


def kernel(x_prompt, x_sample, state_conv, cache_k_win, cache_v_win, c_prompt, c_sample, norm_g, w_ada, b_ada, w_ffn_gate, w_ffn_up, w_ffn_down, conv_w_in, conv_w, conv_w_out, kv_norm_g, w_ada_kv, b_ada_kv, w_k, w_v, attn_w_q, attn_sinks, attn_w_o, final_norm_g):
    batch, seq, d = x_prompt.shape
    dec_batch, dec_seq, _ = x_sample.shape
    w_buf = cache_k_win.shape[1]
    past_len = 16384
    assert d == D_MODEL and w_buf == WINDOW and dec_seq == SUBLANES
    assert w_ada.shape[0] == 2 and conv_w_in.shape[0] == 1 and attn_w_q.shape[0] == 1

    pad = (-(dec_batch + batch)) % SUBLANES
    c_all = jnp.concatenate([c_sample, c_prompt, jnp.zeros((pad, d), F32)], axis=0)
    mods = _ada(c_all, w_ada, b_ada)
    mods_kv = _ada(c_all, w_ada_kv[None], b_ada_kv[None])
    prompt_mod_block = dec_batch // SUBLANES

    make_groups = lambda tile: {
        "prompt": _Group(batch, seq, None, prompt_mod_block, tile),
        "sample": _Group(dec_batch, dec_seq, dec_seq, prompt_mod_block, tile),
    }
    ffn_groups, mixer_groups = make_groups(FFN_TILE), make_groups(MIXER_TILE)
    xp, xs = x_prompt.reshape(batch * seq, d), x_sample.reshape(dec_batch * dec_seq, d)

    norm_g3 = norm_g.reshape(-1, 1, d)
    w_in, w_out = conv_w_in[0].astype(BF16), conv_w_out[0].astype(BF16)
    w_q, w_o = attn_w_q[0].astype(BF16), attn_w_o[0].astype(BF16)
    sinks = attn_sinks[0]

    pos_dec = past_len + jnp.arange(dec_seq, dtype=jnp.int32)
    pos_prompt = jnp.arange(seq, dtype=jnp.int32)
    tab_dec = tuple(jnp.tile(t, (SAMPLE_ATTN_SEQS, 1)) for t in _rope_tables(pos_dec))
    tab_prompt_t = tuple(t.T for t in _rope_angles(pos_prompt))
    tab_dec_t = tuple(jnp.tile(t.T, (1, FFN_TILE // dec_seq)) for t in _rope_angles(pos_dec))
    kv_args = dict(norm_g=kv_norm_g.reshape(1, d), mods=mods_kv,
                   w_kv_t=jnp.concatenate([w_k.T, w_v.T], axis=0).astype(BF16),
                   prompt_tables_t=tab_prompt_t, decode_tables_t=tab_dec_t)
    to_rows = lambda c: jnp.transpose(c, (0, 2, 3, 1)).reshape(dec_batch * KV_DIM, w_buf)
    from_rows = lambda c: jnp.transpose(c.reshape(dec_batch, N_KV_HEADS, HEAD_DIM, w_buf), (0, 3, 1, 2))

    ffn = functools.partial(_ffn, prompt=ffn_groups["prompt"], decode=ffn_groups["sample"], mods=mods,
                            norm_g=norm_g3, wg=w_ffn_gate, wu=w_ffn_up, wd=w_ffn_down)
    xp, xs = ffn(xp, xs, layer=0, sub=0, slot=0)
    xp, u_p = _conv_mixer(xp, mixer_groups["prompt"], mods, norm_g3, w_in, conv_w[0], w_out)
    xs, u_s = _conv_mixer(xs, make_groups(DEC_MIXER_TILE)["sample"], mods, norm_g3, w_in, conv_w[0], w_out,
                          (state_conv[0, :, 0, :], state_conv[0, :, 1, :]))
    xp, xs = ffn(xp, xs, layer=0, sub=2, slot=1)
    xp, xs, kvt_p, kvt_s = ffn(xp, xs, layer=1, sub=0, slot=0, kv=kv_args)
    xp = _attn_prompt(xp, make_groups(ATTN_TILE)["prompt"], mods, norm_g3, w_q, w_o, kvt_p, tab_prompt_t, sinks)
    xs, k_win, v_win = _attn_sample(xs, dec_batch, dec_seq, mods, norm_g3, w_q, w_o,
                                    to_rows(cache_k_win), to_rows(cache_v_win), kvt_s, tab_dec, sinks)
    y_p, y_s = ffn(xp, xs, layer=1, sub=2, slot=1, final_g=final_norm_g.reshape(1, d))

    kv_state_p = jnp.stack([kvt_p[:, (b + 1) * seq - WINDOW:(b + 1) * seq] for b in range(batch)])
    kv_state_p = jnp.transpose(kv_state_p.reshape(batch, 2, N_KV_HEADS, HEAD_DIM, WINDOW), (1, 0, 4, 2, 3))
    k_state_p, v_state_p = kv_state_p[0], kv_state_p[1]
    k_state_s, v_state_s = from_rows(k_win), from_rows(v_win)
    tail = CONV_WIDTH - 1
    conv_p = u_p.reshape(batch, SUBLANES, d)[:, SUBLANES - tail:][None]
    conv_s = u_s.reshape(dec_batch, dec_seq, d)[:, dec_seq - tail:][None]
    return (y_p.reshape(batch, seq, d), y_s.reshape(dec_batch, dec_seq, d), conv_p, conv_s,
            k_state_p, v_state_p, k_state_s, v_state_s)
```

```python
import functools

import jax
import jax.numpy as jnp
from jax import lax
from jax.experimental import pallas as pl
from jax.experimental.pallas import tpu as pltpu

F32 = jnp.float32
BF16 = jnp.bfloat16

D_MODEL = 1024
D_FF = 2816
HEAD_DIM = 64
N_HEADS = 16
N_KV_HEADS = 4
GROUP = N_HEADS // N_KV_HEADS
KV_DIM = N_KV_HEADS * HEAD_DIM
WINDOW = 128
ROT_DIM = 16
ROPE_THETA = 500000.0
CONV_WIDTH = 3
EPS = 1e-6
NEG = -1e30
FMAX = float(jnp.finfo(jnp.float32).max)

SUBLANES = 8
LANES = 128
VMEM_LIMIT_BYTES = 60 * 1024 * 1024
MIXER_VMEM_LIMIT_BYTES = 48 * 1024 * 1024

ADA_COLS = 2304
FFN_TILE = 512
FFN_SUBTILES = 2
FF_STAGE = 256
CONV_ROWS = 256
MIXER_TILE = 1024
DEC_MIXER_TILE = 512
ATTN_TILE = 1024
Q_ROWS = 64
ATTN_CHUNK = 256
LOG2E = 1.4426950408889634
SAMPLE_ATTN_SEQS = 16
DEC_ATTN_PACK = 4
FF_CHUNKS = ((0, 1024), (1024, 2048), (2048, D_FF))


def _dot(a, b):
    return jnp.dot(a, b, preferred_element_type=F32)


def _dot_nt(a, b):
    return lax.dot_general(a, b, (((1,), (1,)), ((), ())), preferred_element_type=F32)


def _log2(n):
    assert n & (n - 1) == 0, n
    return n.bit_length() - 1


def _silu(x):
    return x * jax.nn.sigmoid(x)


def _unit_rms(x):
    return x * lax.rsqrt(jnp.mean(x * x, axis=-1, keepdims=True) + EPS)


def _repeat_rows(ref, reps):
    n = ref.shape[1]
    return jnp.concatenate(
        [jnp.broadcast_to(ref[i:i + 1, :], (reps, n)) for i in range(ref.shape[0])], axis=0)


def _mod_rows(ref, per_seq_rows):
    if per_seq_rows is None:
        return ref[pl.ds(pl.program_id(0), 1), :]
    return _repeat_rows(ref, per_seq_rows)


def _rope(x, cos_t, sin_up_t, sin_dn_t):
    n = x.shape[1]
    reps = n // LANES
    half = ROT_DIM // 2
    cos = jnp.concatenate([cos_t] * reps, axis=1)
    sin_up = jnp.concatenate([sin_up_t] * reps, axis=1)
    sin_dn = jnp.concatenate([sin_dn_t] * reps, axis=1)
    x_up = pltpu.roll(x, n - half, axis=1)
    x_dn = pltpu.roll(x, half, axis=1)
    return x * cos + x_up * sin_up + x_dn * sin_dn


def _ada_kernel(c_ref, w_ref, b_ref, o_ref):
    a = _silu(c_ref[...]).astype(BF16)
    o_ref[...] = _dot(a, w_ref[...].astype(BF16)) + b_ref[...]


def _ada(c_all, w, b):
    n_layers, _, n = w.shape
    m = c_all.shape[0]
    tn = ADA_COLS if n % ADA_COLS == 0 else D_MODEL
    return pl.pallas_call(
        _ada_kernel,
        grid=(n_layers, n // tn),
        in_specs=[
            pl.BlockSpec((m, D_MODEL), lambda l, j: (0, 0)),
            pl.BlockSpec((None, D_MODEL, tn), lambda l, j: (l, 0, j)),
            pl.BlockSpec((None, 1, tn), lambda l, j: (l, 0, j)),
        ],
        out_specs=pl.BlockSpec((None, m, tn), lambda l, j: (l, 0, j)),
        out_shape=jax.ShapeDtypeStruct((n_layers, m, n), F32),
        compiler_params=pltpu.CompilerParams(
            dimension_semantics=("arbitrary", "arbitrary"),
            vmem_limit_bytes=VMEM_LIMIT_BYTES),
        name="adaln_tables",
    )(c_all, w, b.reshape(n_layers, 1, n))


def _rope_t(xt, cos_t, sin_t):
    half = ROT_DIM // 2
    out = []
    for h in range(xt.shape[0] // HEAD_DIM):
        base = h * HEAD_DIM
        x1 = xt[base:base + half, :]
        x2 = xt[base + half:base + ROT_DIM, :]
        out += [x1 * cos_t - x2 * sin_t, x2 * cos_t + x1 * sin_t,
                xt[base + ROT_DIM:base + HEAD_DIM, :]]
    return jnp.concatenate(out, axis=0)


def _ffn_kernel(*refs, n_prompt_tiles, tiles_per_seq, dec_seq, layer, slot, with_kv, final):
    it = iter(refs)
    take = lambda n: [next(it) for _ in range(n)]
    xp_ref, xs_ref = take(2)
    p_mods, s_mods = take(3), take(3)
    ng_ref, wg_hbm, wu_hbm, wd_hbm = take(4)
    if with_kv:
        (kvg_ref,), p_kvmods, s_kvmods = take(1), take(2), take(2)
        (wkvt_ref,) = take(1)
        p_tabs, s_tabs = take(2), take(2)
    else:
        p_kvmods = s_kvmods = None
    if final:
        (fg_ref,) = take(1)
    op_ref, os_ref = take(2)
    if with_kv:
        kvp_ref, kvs_ref = take(2)
    wg_s, wu_s, wd_s, stage_in, stage_out, sem = take(6)

    i = pl.program_id(0)
    prompt_rows = lambda ref: ref[pl.ds(i // tiles_per_seq, 1), :]
    decode_rows = lambda ref: _repeat_rows(ref, dec_seq)

    def weight_copies(c):
        cols = pl.ds(c * FF_STAGE, FF_STAGE)
        buf = c % 2
        return (
            pltpu.make_async_copy(wg_hbm.at[layer, slot, :, cols], stage_in.at[buf, 0], sem.at[buf, 0]),
            pltpu.make_async_copy(wu_hbm.at[layer, slot, :, cols], stage_in.at[buf, 1], sem.at[buf, 1]),
            pltpu.make_async_copy(wd_hbm.at[layer, slot, cols, :], stage_out.at[buf], sem.at[buf, 2]),
        )

    def partial_ffn(hb, lo, hi):
        g = _dot(hb, wg_s[:, lo:hi])
        u = _dot(hb, wu_s[:, lo:hi])
        return _dot((_silu(g) * u).astype(BF16), wd_s[lo:hi, :])

    def split_body(x_ref, o_ref, rows_of, mods, kvmods):
        sub = x_ref.shape[0] // FFN_SUBTILES
        groups = [slice(k * sub, (k + 1) * sub) for k in range(FFN_SUBTILES)]
        rows_in = lambda v, r: v if v.shape[0] == 1 else v[r]
        shift, scale, gate = (rows_of(m) for m in mods)
        xs_, xns, hbs = [], [], []
        ys = [None] * FFN_SUBTILES
        for c, (lo, hi) in enumerate(FF_CHUNKS):
            for k, r in enumerate(groups):
                if c == 0:
                    x = x_ref[r, :]
                    xn = _unit_rms(x)
                    xs_.append(x)
                    xns.append(xn)
                    hbs.append(((xn * ng_ref[...]) * (1.0 + rows_in(scale, r))
                                + rows_in(shift, r)).astype(BF16))
                part = partial_ffn(hbs[k], lo, hi)
                ys[k] = part if ys[k] is None else ys[k] + part
                if c == len(FF_CHUNKS) - 1:
                    x_new = xs_[k] + (0.5 * rows_in(gate, r)) * ys[k]
                    o_ref[r, :] = _unit_rms(x_new) * fg_ref[...] if final else x_new
        if with_kv:
            kv_shift, kv_scale = rows_of(kvmods[0]), rows_of(kvmods[1])
            return jnp.concatenate(
                [((xns[k] * kvg_ref[...]) * (1.0 + rows_in(kv_scale, r)) + rows_in(kv_shift, r)).astype(BF16)
                 for k, r in enumerate(groups)], axis=0)

    def body(x_ref, o_ref, rows_of, mods, kvmods, stage_weights):
        if not stage_weights:
            return split_body(x_ref, o_ref, rows_of, mods, kvmods)
        x = x_ref[...]
        xn = _unit_rms(x)
        hb = ((xn * ng_ref[...]) * (1.0 + rows_of(mods[1])) + rows_of(mods[0])).astype(BF16)
        y = None
        n_slices = D_FF // FF_STAGE
        for c in range(min(2, n_slices)):
            for cp in weight_copies(c):
                cp.start()
        for c in range(n_slices):
            lo, hi = c * FF_STAGE, (c + 1) * FF_STAGE
            for cp in weight_copies(c):
                cp.wait()
            wg_s[:, lo:hi] = stage_in[c % 2, 0].astype(BF16)
            wu_s[:, lo:hi] = stage_in[c % 2, 1].astype(BF16)
            wd_s[lo:hi, :] = stage_out[c % 2].astype(BF16)
            if c + 2 < n_slices:
                for cp in weight_copies(c + 2):
                    cp.start()
            part = partial_ffn(hb, lo, hi)
            y = part if y is None else y + part
        x_new = x + (0.5 * rows_of(mods[2])) * y
        o_ref[...] = _unit_rms(x_new) * fg_ref[...] if final else x_new
        if with_kv:
            return ((xn * kvg_ref[...]) * (1.0 + rows_of(kvmods[1])) + rows_of(kvmods[0])).astype(BF16)

    def emit_kv(hk, tabs, kv_ref):
        kv_t = _dot_nt(wkvt_ref[...], hk)
        kv_ref[:KV_DIM, :] = _rope_t(kv_t[:KV_DIM, :], tabs[0][...], tabs[1][...])
        kv_ref[KV_DIM:, :] = kv_t[KV_DIM:, :]

    def prompt_step(stage_weights):
        hk = body(xp_ref, op_ref, prompt_rows, p_mods, p_kvmods, stage_weights)
        if with_kv:
            emit_kv(hk, p_tabs, kvp_ref)

    @pl.when(i == 0)
    def _():
        prompt_step(True)

    @pl.when((i > 0) & (i < n_prompt_tiles))
    def _():
        prompt_step(False)

    @pl.when(i >= n_prompt_tiles)
    def _():
        hk = body(xs_ref, os_ref, decode_rows, s_mods, s_kvmods, False)
        if with_kv:
            emit_kv(hk, s_tabs, kvs_ref)


def _resident(shape, index_map):
    return pl.BlockSpec(shape, index_map, pipeline_mode=pl.Buffered(1))


def _mod_spec(layer, col, group):
    if group.per_seq_rows is None:
        return pl.BlockSpec((None, SUBLANES, D_MODEL),
                            lambda b, t: (layer, group.prompt_mod_block, col))
    return pl.BlockSpec((None, group.tile // group.per_seq_rows, D_MODEL),
                        lambda b, t: (layer, t, col))


class _Group:
    def __init__(self, n_seq, seq_len, per_seq_rows, prompt_mod_block, tile):
        self.tile = tile
        self.rows = n_seq * seq_len
        self.per_seq_rows = per_seq_rows
        self.prompt_mod_block = prompt_mod_block
        if per_seq_rows is None:
            self.grid = (n_seq, seq_len // self.tile)
        else:
            self.grid = (1, self.rows // self.tile)
        self.tiles_per_seq = self.grid[1]

    def row_spec(self, width):
        nt = self.grid[1]
        return pl.BlockSpec((self.tile, width), lambda b, t: (b * nt + t, 0))

    def table_t_spec(self):
        if self.per_seq_rows is None:
            return pl.BlockSpec((ROT_DIM // 2, self.tile), lambda b, t: (0, t))
        return pl.BlockSpec((ROT_DIM // 2, self.tile), lambda b, t: (0, 0))


def _ffn(xp, xs, prompt, decode, mods, layer, sub, norm_g, wg, wu, wd, slot, kv=None, final_g=None):
    with_kv = kv is not None
    final = final_g is not None
    tile = prompt.tile
    assert decode.tile == tile
    n_p, n_s = prompt.rows // tile, decode.rows // tile
    tiles_per_seq = prompt.tiles_per_seq
    p_tile = lambda i: jnp.minimum(i, n_p - 1)
    s_tile = lambda i: jnp.maximum(i - n_p, 0)
    n_dec_seq = tile // decode.per_seq_rows

    def mod_specs(table_layer, cols):
        p = [pl.BlockSpec((None, SUBLANES, D_MODEL),
                          functools.partial(lambda i, c: (table_layer, prompt.prompt_mod_block, c), c=c))
             for c in cols]
        s = [pl.BlockSpec((None, n_dec_seq, D_MODEL),
                          functools.partial(lambda i, c: (table_layer, s_tile(i), c), c=c),
                          pipeline_mode=pl.Buffered(1))
             for c in cols]
        return p + s

    p_rows = lambda width: pl.BlockSpec((tile, width), lambda i: (p_tile(i), 0))
    s_rows = lambda width, **kw: pl.BlockSpec((tile, width), lambda i: (s_tile(i), 0), **kw)
    p_cols = lambda height: pl.BlockSpec((height, tile), lambda i: (0, p_tile(i)))
    s_cols = lambda height: pl.BlockSpec((height, tile), lambda i: (0, s_tile(i)))
    const = lambda shape: _resident(shape, lambda i: (0,) * len(shape))
    hbm = pl.BlockSpec(memory_space=pl.ANY)

    in_specs = [p_rows(D_MODEL), s_rows(D_MODEL, pipeline_mode=pl.Buffered(1))]
    in_specs += mod_specs(layer, [3 * sub + 0, 3 * sub + 1, 3 * sub + 2])
    in_specs += [_resident((None, 1, D_MODEL), lambda i: (layer * 3 + sub, 0, 0)), hbm, hbm, hbm]
    args = [xp, xs] + [mods] * 6 + [norm_g, wg, wu, wd]
    out_specs = [p_rows(D_MODEL), s_rows(D_MODEL)]
    out_shape = [jax.ShapeDtypeStruct((prompt.rows, D_MODEL), F32),
                 jax.ShapeDtypeStruct((decode.rows, D_MODEL), F32)]
    if with_kv:
        in_specs += [const((1, D_MODEL))] + mod_specs(0, [0, 1])
        in_specs += [const((2 * KV_DIM, D_MODEL))]
        in_specs += [pl.BlockSpec((ROT_DIM // 2, tile), lambda i: (0, p_tile(i) % tiles_per_seq))] * 2
        in_specs += [const((ROT_DIM // 2, tile))] * 2
        args += [kv["norm_g"]] + [kv["mods"]] * 4 + [kv["w_kv_t"]]
        args += [*kv["prompt_tables_t"], *kv["decode_tables_t"]]
        out_specs += [p_cols(2 * KV_DIM), s_cols(2 * KV_DIM)]
        out_shape += [jax.ShapeDtypeStruct((2 * KV_DIM, prompt.rows), F32),
                      jax.ShapeDtypeStruct((2 * KV_DIM, decode.rows), F32)]
    if final:
        in_specs.append(const((1, D_MODEL)))
        args.append(final_g)
    scratch = [
        pltpu.VMEM((D_MODEL, D_FF), BF16), pltpu.VMEM((D_MODEL, D_FF), BF16),
        pltpu.VMEM((D_FF, D_MODEL), BF16),
        pltpu.VMEM((2, 2, D_MODEL, FF_STAGE), F32),
        pltpu.VMEM((2, FF_STAGE, D_MODEL), F32),
        pltpu.SemaphoreType.DMA((2, 3)),
    ]
    return pl.pallas_call(
        functools.partial(_ffn_kernel, n_prompt_tiles=n_p, tiles_per_seq=tiles_per_seq,
                          dec_seq=decode.per_seq_rows, layer=layer, slot=slot,
                          with_kv=with_kv, final=final),
        grid=(n_p + n_s,),
        in_specs=in_specs,
        out_specs=out_specs,
        out_shape=out_shape,
        scratch_shapes=scratch,
        compiler_params=pltpu.CompilerParams(
            dimension_semantics=("arbitrary",),
            vmem_limit_bytes=VMEM_LIMIT_BYTES),
        name=f"ffn_l{layer}_s{sub}",
    )(*args)


def _conv_kernel(*refs, per_seq_rows):
    it = iter(refs)
    x_ref, sh_ref, sc_ref, gt_ref, ng_ref, win_ref, cw_ref, wout_ref = (next(it) for _ in range(8))
    if per_seq_rows is not None:
        st0_ref, st1_ref = next(it), next(it)
    o_ref, u_ref = next(it), next(it)
    if per_seq_rows is None:
        carry_ref = next(it)

    tile = x_ref.shape[0]
    prompt = per_seq_rows is None
    n_groups = tile // CONV_ROWS
    rows_of = lambda k: slice(k * CONV_ROWS, (k + 1) * CONV_ROWS)
    rows_in = lambda v, r: v if v.shape[0] == 1 else v[r]
    shift, scale, gate = (_mod_rows(ref, per_seq_rows) for ref in (sh_ref, sc_ref, gt_ref))
    row = lax.broadcasted_iota(jnp.int32, (CONV_ROWS, 1), 0)
    if prompt:
        @pl.when(pl.program_id(1) == 0)
        def _():
            carry_ref[...] = jnp.zeros_like(carry_ref)
        carried = (carry_ref[SUBLANES - 2:SUBLANES - 1, :], carry_ref[SUBLANES - 1:SUBLANES, :])
        t = row
    else:
        state2 = _repeat_rows(st0_ref, per_seq_rows)
        state1 = _repeat_rows(st1_ref, per_seq_rows)
        t = row & (per_seq_rows - 1)

    def project(k):
        x = x_ref[rows_of(k), :]
        h = ((_unit_rms(x) * ng_ref[...]) * (1.0 + rows_in(scale, rows_of(k)))
             + rows_in(shift, rows_of(k))).astype(BF16)
        bcv = _dot(h, win_ref[...])
        return x, bcv[:, :D_MODEL], bcv[:, D_MODEL:2 * D_MODEL] * bcv[:, 2 * D_MODEL:]

    def convolve(k, b_g, u, prev2, prev1):
        u1 = jnp.where(t == 0, prev1, pltpu.roll(u, 1, axis=0))
        u2 = jnp.where(t == 0, prev2, jnp.where(t == 1, prev1, pltpu.roll(u, 2, axis=0)))
        conv = u2 * cw_ref[0:1, :] + u1 * cw_ref[1:2, :] + u * cw_ref[2:3, :]
        return (b_g * conv).astype(BF16)

    projected, mixed = {}, {}
    for i in range(n_groups + 2):
        if i < n_groups:
            projected[i] = project(i)
        if 0 <= i - 1 < n_groups:
            k = i - 1
            x, b_g, u = projected.pop(k)
            if prompt:
                prev2, prev1 = carried
                carried = (u[CONV_ROWS - 2:CONV_ROWS - 1, :], u[CONV_ROWS - 1:, :])
                if k == n_groups - 1:
                    tail = u[CONV_ROWS - SUBLANES:, :]
                    carry_ref[...] = tail
                    u_ref[...] = tail
            else:
                prev2, prev1 = state2[rows_of(k)], state1[rows_of(k)]
                u_ref[rows_of(k), :] = u
            mixed[k] = (x, convolve(k, b_g, u, prev2, prev1))
        if 0 <= i - 2 < n_groups:
            k = i - 2
            x, act = mixed.pop(k)
            o_ref[rows_of(k), :] = x + rows_in(gate, rows_of(k)) * _dot(act, wout_ref[...])


def _conv_mixer(x, group, mods, norm_g, w_in, conv_w, w_out, state=None):
    prompt = group.per_seq_rows is None
    in_specs = [
        group.row_spec(D_MODEL),
        _mod_spec(0, 3, group), _mod_spec(0, 4, group), _mod_spec(0, 5, group),
        _resident((None, 1, D_MODEL), lambda b, t: (1, 0, 0)),
        _resident((D_MODEL, 3 * D_MODEL), lambda b, t: (0, 0)),
        _resident((CONV_WIDTH, D_MODEL), lambda b, t: (0, 0)),
        _resident((D_MODEL, D_MODEL), lambda b, t: (0, 0)),
    ]
    args = [x, mods, mods, mods, norm_g, w_in, conv_w, w_out]
    scratch = []
    if prompt:
        u_spec = pl.BlockSpec((SUBLANES, D_MODEL), lambda b, t: (b, 0))
        u_shape = jax.ShapeDtypeStruct((group.grid[0] * SUBLANES, D_MODEL), F32)
        scratch.append(pltpu.VMEM((SUBLANES, D_MODEL), F32))
    else:
        n_seq = group.tile // group.per_seq_rows
        st_spec = pl.BlockSpec((n_seq, D_MODEL), lambda b, t: (t, 0))
        in_specs += [st_spec, st_spec]
        args += [state[0], state[1]]
        u_spec = group.row_spec(D_MODEL)
        u_shape = jax.ShapeDtypeStruct((group.rows, D_MODEL), F32)
    return pl.pallas_call(
        functools.partial(_conv_kernel, per_seq_rows=group.per_seq_rows),
        grid=group.grid,
        in_specs=in_specs,
        out_specs=[group.row_spec(D_MODEL), u_spec],
        out_shape=[jax.ShapeDtypeStruct((group.rows, D_MODEL), F32), u_shape],
        scratch_shapes=scratch,
        compiler_params=pltpu.CompilerParams(
            dimension_semantics=("arbitrary", "arbitrary"),
            allow_input_fusion=[i in (5, 7) for i in range(len(args))],
            vmem_limit_bytes=MIXER_VMEM_LIMIT_BYTES),
        name="conv_mixer",
    )(*args)


def _sink_column(sink_ref, kv_head, q_rows):
    grp = lax.broadcasted_iota(jnp.int32, (GROUP * q_rows, 1), 0) >> _log2(q_rows)
    col = jnp.full((GROUP * q_rows, 1), sink_ref[kv_head * GROUP + GROUP - 1], F32)
    for g in range(GROUP - 2, -1, -1):
        col = jnp.where(grp == g, sink_ref[kv_head * GROUP + g], col)
    return col


def _stack_heads(q_ref, rows, kv_head):
    return jnp.concatenate(
        [q_ref[rows, (kv_head * GROUP + g) * HEAD_DIM:(kv_head * GROUP + g + 1) * HEAD_DIM]
         for g in range(GROUP)], axis=0)


def _softmax_parts(scores, sink):
    m = sink
    for s in scores:
        m = jnp.maximum(m, jnp.max(s, axis=-1, keepdims=True))
    ps = [jnp.exp(s - m) for s in scores]
    denom = jnp.exp(sink - m)
    for p in ps:
        denom = denom + jnp.sum(p, axis=-1, keepdims=True)
    return ps, 1.0 / denom


def _project_q(x_ref, sh_ref, sc_ref, ng_ref, wq_ref, tables, per_seq_rows):
    h = ((_unit_rms(x_ref[...]) * ng_ref[...]) * (1.0 + _mod_rows(sc_ref, per_seq_rows))
         + _mod_rows(sh_ref, per_seq_rows)).astype(BF16)
    q = _rope(_dot(h, wq_ref[...]), *tables)
    return (q * (HEAD_DIM ** -0.5)).astype(BF16)


def _band_caps():
    shape = (WINDOW + Q_ROWS, GROUP * Q_ROWS)
    kj = lax.broadcasted_iota(jnp.int32, shape, 0)
    qi = lax.broadcasted_iota(jnp.int32, shape, 1) & (Q_ROWS - 1)
    in_band = (kj > qi) & (kj <= qi + WINDOW)
    return tuple(jnp.where(in_band & (kj >= n * Q_ROWS), FMAX, NEG) for n in range(WINDOW // Q_ROWS + 1))


def _attn_prompt_kernel(x_ref, sh_ref, sc_ref, gt_ref, ng_ref, wq_ref, wo_ref,
                        kp_ref, kc_ref, vp_ref, vc_ref, cos_ref, sin_ref, sink_ref,
                        o_ref, q_scr, k_scr, v_scr, v_odd_scr, a_scr, cap_scr, h_scr, at_scr):
    tile = x_ref.shape[0]

    @pl.when((pl.program_id(0) == 0) & (pl.program_id(1) == 0))
    def _():
        for n, cap in enumerate(_band_caps()):
            cap_scr[n] = cap

    k_scr[:WINDOW, :] = kp_ref[...].T.astype(BF16)
    k_scr[WINDOW:, :] = kc_ref[...].T.astype(BF16)
    v_scr[:, :WINDOW] = vp_ref[...].astype(BF16)
    v_scr[:, WINDOW:] = vc_ref[...].astype(BF16)
    v_odd_scr[:, :tile + WINDOW - Q_ROWS] = v_scr[:, Q_ROWS:]
    first_tile = pl.program_id(1) == 0
    norm_gain = ng_ref[...]
    scale1 = 1.0 + _mod_rows(sc_ref, None)
    shift = _mod_rows(sh_ref, None)
    gate = _mod_rows(gt_ref, None)

    group_dims = GROUP * HEAD_DIM
    blocks_per_chunk = ATTN_CHUNK // Q_ROWS
    kv_rows = lambda kvh: slice(kvh * HEAD_DIM, (kvh + 1) * HEAD_DIM)
    block_lanes = lambda blk: slice(blk * GROUP * Q_ROWS, (blk + 1) * GROUP * Q_ROWS)

    def chunk_rows(c):
        return slice(c * ATTN_CHUNK, (c + 1) * ATTN_CHUNK)

    def normalize(c):
        rows = chunk_rows(c)
        h_scr[rows, :] = ((_unit_rms(x_ref[rows, :]) * norm_gain) * scale1 + shift).astype(BF16)

    def project_piece(c, kvh):
        rows = chunk_rows(c)
        dims = slice(kvh * group_dims, (kvh + 1) * group_dims)
        qt = _rope_t(_dot(h_scr[rows, :], wq_ref[:, dims]).T, cos_ref[:, rows], sin_ref[:, rows])
        qt = (qt * (HEAD_DIM ** -0.5 * LOG2E)).astype(BF16)
        for j in range(blocks_per_chunk):
            blk = c * blocks_per_chunk + j
            q_scr[kv_rows(kvh), block_lanes(blk)] = jnp.concatenate(
                [qt[g * HEAD_DIM:(g + 1) * HEAD_DIM, j * Q_ROWS:(j + 1) * Q_ROWS] for g in range(GROUP)],
                axis=1)

    def key_window(blk):
        return slice(blk * Q_ROWS, blk * Q_ROWS + WINDOW + Q_ROWS)

    def score(blk, kvh):
        hidden = WINDOW // Q_ROWS - blk
        cap = cap_scr[jnp.where(first_tile, hidden, 0)] if hidden > 0 else cap_scr[0]
        dims = kv_rows(kvh)
        s = jnp.minimum(_dot(k_scr[key_window(blk), dims], q_scr[dims, block_lanes(blk)]), cap)
        sink_lane = lax.broadcasted_iota(jnp.int32, (1, GROUP * Q_ROWS), 1) >> _log2(Q_ROWS)
        sink = jnp.full((1, GROUP * Q_ROWS), sink_ref[kvh * GROUP + GROUP - 1], F32)
        for g in range(GROUP - 1):
            sink = jnp.where(sink_lane == g, sink_ref[kvh * GROUP + g], sink)
        sink = LOG2E * sink
        return s, jnp.maximum(jnp.max(s, axis=0, keepdims=True), sink), sink

    def exponentiate(s, m, sink):
        p = jnp.exp2(s - m)
        return p.astype(BF16), jnp.sum(p, axis=0, keepdims=True) + jnp.exp2(sink - m)

    def weigh(blk, kvh, p, denom):
        dims = kv_rows(kvh)
        if blk % 2 == 0:
            v_win = v_scr[dims, key_window(blk)]
        else:
            v_win = v_odd_scr[dims, key_window(blk - 1)]
        a_scr[dims, block_lanes(blk)] = _dot(v_win, p) * (1.0 / denom)

    def transpose_piece(c, kvh):
        slab = a_scr[kv_rows(kvh), c * GROUP * ATTN_CHUNK:(c + 1) * GROUP * ATTN_CHUNK]
        per_head = [
            jnp.concatenate([slab[:, block_lanes(j)][:, g * Q_ROWS:(g + 1) * Q_ROWS]
                             for j in range(blocks_per_chunk)], axis=1)
            for g in range(GROUP)]
        dims = slice(kvh * group_dims, (kvh + 1) * group_dims)
        at_scr[chunk_rows(c), dims] = jnp.concatenate(per_head, axis=0).T.astype(BF16)

    def emit_piece(c, n):
        rows = chunk_rows(c)
        cols = slice(n * group_dims, (n + 1) * group_dims)
        y = _dot(at_scr[rows, :], wo_ref[:, cols])
        o_ref[rows, cols] = x_ref[rows, cols] + gate[:, cols] * y

    n_chunks = tile // ATTN_CHUNK
    n_pieces = D_MODEL // group_dims
    steps_per_chunk = blocks_per_chunk * N_KV_HEADS
    stride = steps_per_chunk // N_KV_HEADS
    steps = [(blk, kvh) for blk in range(n_chunks * blocks_per_chunk) for kvh in range(N_KV_HEADS)]
    scored, weights = {}, {}
    normalize(0)
    for kvh in range(N_KV_HEADS):
        project_piece(0, kvh)
    for i in range(len(steps) + 2):
        c, local = divmod(i, steps_per_chunk)
        if i < len(steps):
            if local == 0 and c + 1 < n_chunks:
                normalize(c + 1)
            scored[i] = score(*steps[i])
        if 0 <= i - 1 < len(steps):
            weights[i - 1] = exponentiate(*scored.pop(i - 1))
        if 0 <= i - 2 < len(steps):
            blk, kvh = steps[i - 2]
            weigh(blk, kvh, *weights.pop(i - 2))
            if blk % blocks_per_chunk == blocks_per_chunk - 1:
                transpose_piece(blk // blocks_per_chunk, kvh)
        if i < len(steps):
            if local % stride == 0 and c + 1 < n_chunks:
                project_piece(c + 1, local // stride)
            if local % stride == stride // 2 and c > 0:
                emit_piece(c - 1, local // stride)
    for n in range(n_pieces):
        emit_piece(n_chunks - 1, n)


def _attn_prompt(x, group, mods, norm_g, w_q, w_o, kv_t, tables_t, sinks):
    tile = group.tile
    nt = group.grid[1]
    blocks_per_tile = tile // WINDOW

    def prev_block(b, t):
        return jnp.maximum((b * nt + t) * blocks_per_tile - 1, 0)

    kv_cur = lambda part: pl.BlockSpec((KV_DIM, tile), lambda b, t: (part, b * nt + t))

    in_specs = [
        group.row_spec(D_MODEL),
        _mod_spec(1, 3, group), _mod_spec(1, 4, group), _mod_spec(1, 5, group),
        _resident((None, 1, D_MODEL), lambda b, t: (4, 0, 0)),
        _resident((D_MODEL, D_MODEL), lambda b, t: (0, 0)),
        _resident((D_MODEL, D_MODEL), lambda b, t: (0, 0)),
        pl.BlockSpec((KV_DIM, WINDOW), lambda b, t: (0, prev_block(b, t))), kv_cur(0),
        pl.BlockSpec((KV_DIM, WINDOW), lambda b, t: (1, prev_block(b, t))), kv_cur(1),
        group.table_t_spec(), group.table_t_spec(),
        pl.BlockSpec(memory_space=pltpu.SMEM),
    ]
    return pl.pallas_call(
        _attn_prompt_kernel,
        grid=group.grid,
        in_specs=in_specs,
        out_specs=group.row_spec(D_MODEL),
        out_shape=jax.ShapeDtypeStruct((group.rows, D_MODEL), F32),
        scratch_shapes=[
            pltpu.VMEM((KV_DIM, GROUP * tile), BF16),
            pltpu.VMEM((tile + WINDOW, KV_DIM), BF16),
            pltpu.VMEM((KV_DIM, tile + WINDOW), BF16),
            pltpu.VMEM((KV_DIM, tile + WINDOW), BF16),
            pltpu.VMEM((KV_DIM, GROUP * tile), F32),
            pltpu.VMEM((WINDOW // Q_ROWS + 1, WINDOW + Q_ROWS, GROUP * Q_ROWS), F32),
            pltpu.VMEM((tile, D_MODEL), BF16),
            pltpu.VMEM((tile, D_MODEL), BF16),
        ],
        compiler_params=pltpu.CompilerParams(
            dimension_semantics=("arbitrary", "arbitrary"),
            allow_input_fusion=[i in (5, 6) for i in range(len(in_specs))],
            vmem_limit_bytes=MIXER_VMEM_LIMIT_BYTES),
        name="attn_prompt",
    )(x, mods, mods, mods, norm_g, w_q, w_o, kv_t, kv_t, kv_t, kv_t, *tables_t, sinks)


def _attn_sample_kernel(x_ref, sh_ref, sc_ref, gt_ref, ng_ref, wq_ref, wo_ref,
                        kc_ref, vc_ref, kn_ref, vn_ref, cos_ref, sup_ref, sdn_ref, sink_ref,
                        o_ref, kout_ref, vout_ref, q_scr, *, dec_seq):
    rows = x_ref.shape[0]
    n_seq = rows // dec_seq
    q_scr[...] = _project_q(x_ref, sh_ref, sc_ref, ng_ref, wq_ref,
                            (cos_ref[...], sup_ref[...], sdn_ref[...]), dec_seq)

    pack = DEC_ATTN_PACK
    pack_rows = pack * dec_seq

    def windows(ref, first_seq, kvh):
        return jnp.concatenate(
            [ref[(s * N_KV_HEADS + kvh) * HEAD_DIM:(s * N_KV_HEADS + kvh + 1) * HEAD_DIM, :].astype(BF16)
             for s in range(first_seq, first_seq + pack)], axis=1)

    q_row = lax.broadcasted_iota(jnp.int32, (GROUP * pack_rows, 1), 0) & (pack_rows - 1)
    q_seq, q_i = q_row >> _log2(dec_seq), q_row & (dec_seq - 1)
    c_col = lax.broadcasted_iota(jnp.int32, (1, pack * WINDOW), 1)
    valid_c = (q_seq == (c_col >> _log2(WINDOW))) & ((c_col & (WINDOW - 1)) > q_i)
    n_col = lax.broadcasted_iota(jnp.int32, (1, rows), 1)
    n_seq_col, n_visible = n_col >> _log2(dec_seq), (n_col & (dec_seq - 1)) <= q_i
    kn = [kn_ref[kvh * HEAD_DIM:(kvh + 1) * HEAD_DIM, :].astype(BF16) for kvh in range(N_KV_HEADS)]
    vn = [vn_ref[kvh * HEAD_DIM:(kvh + 1) * HEAD_DIM, :].astype(BF16) for kvh in range(N_KV_HEADS)]

    keep = lax.broadcasted_iota(jnp.int32, (1, WINDOW), 1) < WINDOW - dec_seq

    def update_windows(first_seq):
        for src_ref, new_ref, dst_ref in ((kc_ref, kn_ref, kout_ref), (vc_ref, vn_ref, vout_ref)):
            new = new_ref[...]
            for s in range(first_seq, first_seq + pack):
                blk = slice(s * KV_DIM, (s + 1) * KV_DIM)
                old = pltpu.roll(src_ref[blk, :], WINDOW - dec_seq, axis=1)
                app = pltpu.roll(new, (WINDOW - dec_seq - s * dec_seq) % rows, axis=1)
                dst_ref[blk, :] = jnp.where(keep, old, app)

    def score(first_seq, kvh):
        q_rows = slice(first_seq * dec_seq, (first_seq + pack) * dec_seq)
        valid_n = ((q_seq + first_seq) == n_seq_col) & n_visible
        qg = _stack_heads(q_scr, q_rows, kvh)
        return [jnp.where(valid_c, _dot(qg, windows(kc_ref, first_seq, kvh)), NEG),
                jnp.where(valid_n, _dot(qg, kn[kvh]), NEG)]

    def exponentiate(kvh, scores):
        (p_c, p_n), inv = _softmax_parts(scores, _sink_column(sink_ref, kvh, pack_rows))
        return p_c.astype(BF16), p_n.astype(BF16), inv

    def weigh(first_seq, kvh, p_c, p_n, inv):
        o = (_dot_nt(p_c, windows(vc_ref, first_seq, kvh)) + _dot_nt(p_n, vn[kvh])) * inv
        return [o[g * pack_rows:(g + 1) * pack_rows, :] for g in range(GROUP)]

    steps = [(first_seq, kvh) for first_seq in range(0, n_seq, pack) for kvh in range(N_KV_HEADS)]
    scored, weights, heads = {}, {}, []
    for i in range(len(steps) + 2):
        if i < len(steps):
            if steps[i][1] == 0:
                update_windows(steps[i][0])
            scored[i] = score(*steps[i])
        if 0 <= i - 1 < len(steps):
            weights[i - 1] = exponentiate(steps[i - 1][1], scored.pop(i - 1))
        if 0 <= i - 2 < len(steps):
            heads += weigh(*steps[i - 2], *weights.pop(i - 2))
    per_pack = GROUP * N_KV_HEADS
    attn = jnp.concatenate(
        [jnp.concatenate(heads[n:n + per_pack], axis=1) for n in range(0, len(heads), per_pack)],
        axis=0).astype(BF16)
    o_ref[...] = x_ref[...] + _mod_rows(gt_ref, dec_seq) * _dot(attn, wo_ref[...])


def _attn_sample(x, n_seq_total, dec_seq, mods, norm_g, w_q, w_o, k_cache, v_cache,
                 kv_new, tables, sinks):
    n_seq = SAMPLE_ATTN_SEQS
    rows = n_seq * dec_seq
    assert rows == WINDOW == LANES
    grid = (n_seq_total // n_seq,)
    row_spec = lambda width: pl.BlockSpec((rows, width), lambda t: (t, 0))
    mod_spec = lambda col: pl.BlockSpec((None, n_seq, D_MODEL), lambda t: (1, t, col))
    cache_spec = pl.BlockSpec((n_seq * KV_DIM, WINDOW), lambda t: (t, 0))
    new_spec = lambda part: pl.BlockSpec((KV_DIM, rows), lambda t: (part, t))
    table_spec = pl.BlockSpec((rows, LANES), lambda t: (0, 0))
    in_specs = [
        row_spec(D_MODEL), mod_spec(3), mod_spec(4), mod_spec(5),
        _resident((None, 1, D_MODEL), lambda t: (4, 0, 0)),
        _resident((D_MODEL, D_MODEL), lambda t: (0, 0)),
        _resident((D_MODEL, D_MODEL), lambda t: (0, 0)),
        cache_spec, cache_spec, new_spec(0), new_spec(1),
        table_spec, table_spec, table_spec,
        pl.BlockSpec(memory_space=pltpu.SMEM),
    ]
    cache_shape = jax.ShapeDtypeStruct((n_seq_total * KV_DIM, WINDOW), F32)
    return pl.pallas_call(
        functools.partial(_attn_sample_kernel, dec_seq=dec_seq),
        grid=grid,
        in_specs=in_specs,
        out_specs=[row_spec(D_MODEL), cache_spec, cache_spec],
        out_shape=[jax.ShapeDtypeStruct((n_seq_total * dec_seq, D_MODEL), F32),
                   cache_shape, cache_shape],
        scratch_shapes=[pltpu.VMEM((rows, D_MODEL), BF16)],
        compiler_params=pltpu.CompilerParams(
            dimension_semantics=("arbitrary",),
            allow_input_fusion=[i in (5, 6) for i in range(len(in_specs))],
            vmem_limit_bytes=MIXER_VMEM_LIMIT_BYTES),
        name="attn_sample",
    )(x, mods, mods, mods, norm_g, w_q, w_o, k_cache, v_cache, kv_new, kv_new, *tables, sinks)


def _rope_angles(pos):
    inv_freq = ROPE_THETA ** (-jnp.arange(0, ROT_DIM, 2, dtype=F32) / ROT_DIM)
    ang = pos.astype(F32)[:, None] * inv_freq[None, :]
    return jnp.cos(ang), jnp.sin(ang)


def _rope_tables(pos):
    half = ROT_DIM // 2
    cos, sin = _rope_angles(pos)
    n = pos.shape[0]
    rest = HEAD_DIM - ROT_DIM
    zeros = jnp.zeros((n, half), F32)
    cos_t = jnp.concatenate([cos, cos, jnp.ones((n, rest), F32)], axis=1)
    sin_up = jnp.concatenate([-sin, zeros, jnp.zeros((n, rest), F32)], axis=1)
    sin_dn = jnp.concatenate([zeros, sin, jnp.zeros((n, rest), F32)], axis=1)
    reps = LANES // HEAD_DIM
    return tuple(jnp.tile(t, (1, reps)) for t in (cos_t, sin_up, sin_dn))


def kernel(x_prompt, x_sample, state_conv, cache_k_win, cache_v_win, c_prompt, c_sample, norm_g, w_ada, b_ada, w_ffn_gate, w_ffn_up, w_ffn_down, conv_w_in, conv_w, conv_w_out, kv_norm_g, w_ada_kv, b_ada_kv, w_k, w_v, attn_w_q, attn_sinks, attn_w_o, final_norm_g):
    batch, seq, d = x_prompt.shape
    dec_batch, dec_seq, _ = x_sample.shape
    w_buf = cache_k_win.shape[1]
    past_len = 16384
    assert d == D_MODEL and w_buf == WINDOW and dec_seq == SUBLANES
    assert w_ada.shape[0] == 2 and conv_w_in.shape[0] == 1 and attn_w_q.shape[0] == 1

    pad = (-(dec_batch + batch)) % SUBLANES
    c_all = jnp.concatenate([c_sample, c_prompt, jnp.zeros((pad, d), F32)], axis=0)
    mods = _ada(c_all, w_ada, b_ada)
    mods_kv = _ada(c_all, w_ada_kv[None], b_ada_kv[None])
    prompt_mod_block = dec_batch // SUBLANES

    make_groups = lambda tile: {
        "prompt": _Group(batch, seq, None, prompt_mod_block, tile),
        "sample": _Group(dec_batch, dec_seq, dec_seq, prompt_mod_block, tile),
    }
    ffn_groups, mixer_groups = make_groups(FFN_TILE), make_groups(MIXER_TILE)
    xp, xs = x_prompt.reshape(batch * seq, d), x_sample.reshape(dec_batch * dec_seq, d)

    norm_g3 = norm_g.reshape(-1, 1, d)
    w_in, w_out = conv_w_in[0].astype(BF16), conv_w_out[0].astype(BF16)
    w_q, w_o = attn_w_q[0].astype(BF16), attn_w_o[0].astype(BF16)
    sinks = attn_sinks[0]

    pos_dec = past_len + jnp.arange(dec_seq, dtype=jnp.int32)
    pos_prompt = jnp.arange(seq, dtype=jnp.int32)
    tab_dec = tuple(jnp.tile(t, (SAMPLE_ATTN_SEQS, 1)) for t in _rope_tables(pos_dec))
    tab_prompt_t = tuple(t.T for t in _rope_angles(pos_prompt))
    tab_dec_t = tuple(jnp.tile(t.T, (1, FFN_TILE // dec_seq)) for t in _rope_angles(pos_dec))
    kv_args = dict(norm_g=kv_norm_g.reshape(1, d), mods=mods_kv,
                   w_kv_t=jnp.concatenate([w_k.T, w_v.T], axis=0).astype(BF16),
                   prompt_tables_t=tab_prompt_t, decode_tables_t=tab_dec_t)
    to_rows = lambda c: jnp.transpose(c, (0, 2, 3, 1)).reshape(dec_batch * KV_DIM, w_buf)
    from_rows = lambda c: jnp.transpose(c.reshape(dec_batch, N_KV_HEADS, HEAD_DIM, w_buf), (0, 3, 1, 2))

    ffn = functools.partial(_ffn, prompt=ffn_groups["prompt"], decode=ffn_groups["sample"], mods=mods,
                            norm_g=norm_g3, wg=w_ffn_gate, wu=w_ffn_up, wd=w_ffn_down)
    xp, xs = ffn(xp, xs, layer=0, sub=0, slot=0)
    xp, u_p = _conv_mixer(xp, mixer_groups["prompt"], mods, norm_g3, w_in, conv_w[0], w_out)
    xs, u_s = _conv_mixer(xs, make_groups(DEC_MIXER_TILE)["sample"], mods, norm_g3, w_in, conv_w[0], w_out,
                          (state_conv[0, :, 0, :], state_conv[0, :, 1, :]))
    xp, xs = ffn(xp, xs, layer=0, sub=2, slot=1)
    xp, xs, kvt_p, kvt_s = ffn(xp, xs, layer=1, sub=0, slot=0, kv=kv_args)
    xp = _attn_prompt(xp, make_groups(ATTN_TILE)["prompt"], mods, norm_g3, w_q, w_o, kvt_p, tab_prompt_t, sinks)
    xs, k_win, v_win = _attn_sample(xs, dec_batch, dec_seq, mods, norm_g3, w_q, w_o,
                                    to_rows(cache_k_win), to_rows(cache_v_win), kvt_s, tab_dec, sinks)
    y_p, y_s = ffn(xp, xs, layer=1, sub=2, slot=1, final_g=final_norm_g.reshape(1, d))

    kv_state_p = jnp.stack([kvt_p[:, (b + 1) * seq - WINDOW:(b + 1) * seq] for b in range(batch)])
    kv_state_p = jnp.transpose(kv_state_p.reshape(batch, 2, N_KV_HEADS, HEAD_DIM, WINDOW), (1, 0, 4, 2, 3))
    k_state_p, v_state_p = kv_state_p[0], kv_state_p[1]
    k_state_s, v_state_s = from_rows(k_win), from_rows(v_win)
    tail = CONV_WIDTH - 1
    conv_p = u_p.reshape(batch, SUBLANES, d)[:, SUBLANES - tail:][None]
    conv_s = u_s.reshape(dec_batch, dec_seq, d)[:, dec_seq - tail:][None]
    return (y_p.reshape(batch, seq, d), y_s.reshape(dec_batch, dec_seq, d), conv_p, conv_s,
            k_state_p, v_state_p, k_state_s, v_state_s)
```
